```python
import math
import jax, jax.numpy as jnp
from jax import lax
import numpy as np

D_MODEL = 1024
BATCH = 16
SEQ = 256
DEPTH = 2
DEC_BATCH = 2
DEC_SEQ = 4096
PAST_LEN = 512

GRID_W = 64
N_BRANCH = 4
D_BRANCH = 512
SC_WIDTH = 3
S5_GROUP = 16
S5_GROUPS = D_BRANCH // S5_GROUP
S5_STATE = 64
HG_HEADS = 4
HG_DK = D_BRANCH // HG_HEADS
HG_DV = D_BRANCH // HG_HEADS
HG_CHUNK = 16
POOL_WINDOWS = (2, 4, 8, 16)
POOL_GROUP = D_BRANCH // 4
D_FF = 2816
N_SUB = 3
N_IN_PARTS = 10
IN_WIDTH = N_IN_PARTS * D_BRANCH + N_BRANCH * D_MODEL
ALPHA = (2 * DEPTH) ** 0.25
BETA = (8 * DEPTH) ** -0.25
LN_EPS = 1e-5
POS_BASE = 10000.0

kernel_name = 'hybrid_prefix_flow_trunk_step'


def layer_norm(x, g, b):
    xf = x.astype(jnp.float32)
    xc = xf - xf.mean(-1, keepdims=True)
    var = (xc * xc).mean(-1, keepdims=True)
    return (xc * lax.rsqrt(var + LN_EPS) * g + b).astype(x.dtype)


def rms_norm(x, g):
    xf = x.astype(jnp.float32)
    ms = (xf * xf).mean(-1, keepdims=True)
    return (xf * lax.rsqrt(ms + LN_EPS) * g).astype(x.dtype)


def swiglu(h, w1, w3, w2):
    return (jax.nn.silu(h @ w1) * (h @ w3)) @ w2


def grid_pos_embedding(n_tok, dtype):
    rows = n_tok // GRID_W
    r = jnp.repeat(jnp.arange(rows, dtype=jnp.float32), GRID_W)
    col = jnp.tile(jnp.arange(GRID_W, dtype=jnp.float32), rows)
    quarter = D_MODEL // 4
    omega = POS_BASE ** (-jnp.arange(quarter, dtype=jnp.float32) / quarter)
    ar = r[:, None] * omega
    ac = col[:, None] * omega
    return jnp.concatenate([jnp.sin(ar), jnp.cos(ar), jnp.sin(ac), jnp.cos(ac)], -1).astype(dtype)


def to_lines(u, grid):
    if grid:
        bt, n, ch = u.shape
        return u.reshape(bt * (n // GRID_W), GRID_W, ch)
    return u


def short_conv(u, w, grid):
    bt, n, ch = u.shape
    v = to_lines(u, grid)
    y = lax.conv_general_dilated(v, w[:, None, :].astype(v.dtype), window_strides=(1,),
                                 padding=((SC_WIDTH // 2, SC_WIDTH // 2),),
                                 dimension_numbers=('NWC', 'WIO', 'NWC'), feature_group_count=ch)
    return y.reshape(bt, n, ch)


def multiscale_pool(u, w_pool, scale, grid):
    bt, n, ch = u.shape
    v = to_lines(u, grid).astype(jnp.float32)
    nl, ln, _ = v.shape
    csum = jnp.concatenate([jnp.zeros((nl, 1, ch), jnp.float32), jnp.cumsum(v, axis=1)], axis=1)
    t = np.arange(ln)
    outs = []
    for j, w in enumerate(POOL_WINDOWS):
        lo = np.clip(t - w // 2, 0, ln)
        hi = np.clip(t + w // 2, 0, ln)
        cnt = (hi - lo).astype(np.float32)[:, None]
        sl = slice(j * POOL_GROUP, (j + 1) * POOL_GROUP)
        seg = csum[..., sl]
        mean = (seg[:, hi] - seg[:, lo]) / cnt
        outs.append(mean - v[..., sl])
    pooled = jnp.stack(outs, axis=2).astype(u.dtype)
    y = jnp.einsum('nlgc,gcd->nlgd', pooled, w_pool).reshape(nl, ln, ch) * scale
    return y.reshape(bt, n, ch)


def cmul(ar, ai, br, bi):
    return ar * br - ai * bi, ar * bi + ai * br


def s5_scan(u, lam_re, lam_im, log_dt, b_re, b_im, h0_re, h0_im):
    lam_re = lam_re.astype(jnp.float32)
    lam_im = lam_im.astype(jnp.float32)
    dt = jnp.exp(log_dt.astype(jnp.float32))[:, None]
    mag = jnp.exp(lam_re * dt)
    a_re = mag * jnp.cos(lam_im * dt)
    a_im = mag * jnp.sin(lam_im * dt)
    den = lam_re * lam_re + lam_im * lam_im
    num_re = a_re - 1.0
    coef_re = (num_re * lam_re + a_im * lam_im) / den
    coef_im = (a_im * lam_re - num_re * lam_im) / den
    b_re = b_re.astype(jnp.float32)
    b_im = b_im.astype(jnp.float32)
    bb_re = coef_re[..., None] * b_re - coef_im[..., None] * b_im
    bb_im = coef_re[..., None] * b_im + coef_im[..., None] * b_re
    bu_re = jnp.einsum('blgh,gph->blgp', u, bb_re)
    bu_im = jnp.einsum('blgh,gph->blgp', u, bb_im)
    i0_re, i0_im = cmul(a_re, a_im, h0_re, h0_im)
    bu_re = bu_re.at[:, 0].add(i0_re)
    bu_im = bu_im.at[:, 0].add(i0_im)
    a_re_b = jnp.broadcast_to(a_re, bu_re.shape)
    a_im_b = jnp.broadcast_to(a_im, bu_im.shape)

    def combine(e1, e2):
        a1r, a1i, b1r, b1i = e1
        a2r, a2i, b2r, b2i = e2
        ar, ai = cmul(a2r, a2i, a1r, a1i)
        br, bi = cmul(a2r, a2i, b1r, b1i)
        return ar, ai, br + b2r, bi + b2i

    _, _, x_re, x_im = lax.associative_scan(combine, (a_re_b, a_im_b, bu_re, bu_im), axis=1)
    return x_re, x_im


def s5_mixer(u, lam_re, lam_im, log_dt, b_re, b_im, c_re, c_im, d_skip, glu_w, glu_b, h0_re, h0_im):
    bt, n, _ = u.shape
    uf = u.astype(jnp.float32).reshape(bt, n, S5_GROUPS, S5_GROUP)
    y = d_skip.astype(jnp.float32).reshape(S5_GROUPS, S5_GROUP) * uf
    fin_re, fin_im = [], []
    for d in range(2):
        ud = uf if d == 0 else jnp.flip(uf, 1)
        x_re, x_im = s5_scan(ud, lam_re[d], lam_im[d], log_dt[d], b_re[d], b_im[d],
                             h0_re[:, d].astype(jnp.float32), h0_im[:, d].astype(jnp.float32))
        fin_re.append(x_re[:, -1])
        fin_im.append(x_im[:, -1])
        yd = (jnp.einsum('blgp,ghp->blgh', x_re, c_re[d].astype(jnp.float32))
              - jnp.einsum('blgp,ghp->blgh', x_im, c_im[d].astype(jnp.float32)))
        y = y + (yd if d == 0 else jnp.flip(yd, 1))
    z = jax.nn.gelu(y.reshape(bt, n, D_BRANCH)).astype(u.dtype)
    out = z * jax.nn.sigmoid(z @ glu_w + glu_b)
    return out, jnp.stack(fin_re, 1), jnp.stack(fin_im, 1)


def hgrn2_chunk_scan(q, k, v, logf, h0):
    bt, n, nh, dk = q.shape
    dv = v.shape[-1]
    nc = n // HG_CHUNK
    q = q.reshape(bt, nc, HG_CHUNK, nh, dk)
    k = k.reshape(bt, nc, HG_CHUNK, nh, dk)
    v = v.reshape(bt, nc, HG_CHUNK, nh, dv)
    b = jnp.cumsum(logf.reshape(bt, nc, HG_CHUNK, nh, dk), axis=2)
    mask = np.tril(np.ones((HG_CHUNK, HG_CHUNK), dtype=bool))[None, None, :, :, None, None]
    diff = b[:, :, :, None] - b[:, :, None, :]
    decay = jnp.where(mask, jnp.exp(jnp.minimum(diff, 0.0)), 0.0)
    scores = jnp.einsum('bntshk,bnthk,bnshk->bnths', decay, q, k)
    intra = jnp.einsum('bnths,bnshv->bnthv', scores, v)
    g_last = b[:, :, -1]
    k_tail = k * jnp.exp(g_last[:, :, None] - b)
    upd = jnp.einsum('bnshk,bnshv->bnhkv', k_tail, v)

    def step(s, inp):
        dec, u = inp
        return jnp.exp(dec)[..., None] * s + u, s

    s_fin, s_prev = lax.scan(step, h0, (jnp.moveaxis(g_last, 1, 0), jnp.moveaxis(upd, 1, 0)))
    inter = jnp.einsum('bnthk,nbhkv->bnthv', q * jnp.exp(b), s_prev)
    return (intra + inter).reshape(bt, n, nh, dv), s_fin


def hgrn2_mixer(q, f_fwd, f_bwd, i_val, g, lb, norm_g, h0):
    bt, n, _ = q.shape
    qh = q.astype(jnp.float32).reshape(bt, n, HG_HEADS, HG_DK)
    vh = i_val.astype(jnp.float32).reshape(bt, n, HG_HEADS, HG_DV)
    o = jnp.zeros((bt, n, HG_HEADS, HG_DV), jnp.float32)
    fins = []
    for d, fz in enumerate((f_fwd, f_bwd)):
        lbd = lb[d].astype(jnp.float32).reshape(HG_HEADS, HG_DK)
        f = lbd + (1.0 - lbd) * jax.nn.sigmoid(fz.astype(jnp.float32).reshape(bt, n, HG_HEADS, HG_DK))
        logf = jnp.log(f)
        kh = 1.0 - f
        h0d = h0[:, d].astype(jnp.float32)
        if d == 0:
            od, fin = hgrn2_chunk_scan(qh, kh, vh, logf, h0d)
        else:
            od, fin = hgrn2_chunk_scan(jnp.flip(qh, 1), jnp.flip(kh, 1), jnp.flip(vh, 1), jnp.flip(logf, 1), h0d)
            od = jnp.flip(od, 1)
        o = o + od
        fins.append(fin)
    o = rms_norm(o, norm_g.astype(jnp.float32)).reshape(bt, n, D_BRANCH).astype(q.dtype) * jax.nn.silu(g)
    return o, jnp.stack(fins, 1)


def mixer_block(h, grid, s5_h0_re, s5_h0_im, hg_h0, lb_l, W, l):
    bt, n, _ = h.shape
    proj = h @ W['w_in'][l]
    parts = jnp.split(proj, [D_BRANCH * (i + 1) for i in range(N_IN_PARTS)], axis=-1)
    sc_b, sc_c, sc_h, s5_u, hg_q, hg_ff, hg_fb, hg_i, hg_g, pool_u, gate_logits = parts
    ya = sc_b * short_conv(sc_c * sc_h, W['sc_conv'][l], grid)
    yb, s5_re, s5_im = s5_mixer(s5_u, W['s5_lam_re'][l], W['s5_lam_im'][l], W['s5_log_dt'][l],
                                W['s5_b_re'][l], W['s5_b_im'][l], W['s5_c_re'][l], W['s5_c_im'][l],
                                W['s5_d'][l], W['s5_glu_w'][l], W['s5_glu_b'][l], s5_h0_re, s5_h0_im)
    yc, hg = hgrn2_mixer(hg_q, hg_ff, hg_fb, hg_i, hg_g, lb_l, W['hg_norm_g'][l], hg_h0)
    yd = multiscale_pool(pool_u, W['pool_w'][l], W['pool_scale'][l], grid)
    branches = jnp.stack([ya, yb, yc, yd], axis=2)
    proj_b = jnp.einsum('blkc,kcd->blkd', branches, W['w_branch'][l])
    gates = jax.nn.sigmoid(gate_logits.reshape(bt, n, N_BRANCH, D_MODEL))
    merged = jnp.sum(gates * proj_b, axis=2)
    return merged @ W['w_out'][l], s5_re, s5_im, hg


def trunk_layer(x, cond, grid, s5_h0_re, s5_h0_im, hg_h0, lb_l, W, l):
    mod = (jax.nn.silu(cond) @ W['w_ada'][l] + W['b_ada'][l]).reshape(cond.shape[0], 1, N_SUB, 3, D_MODEL)
    h = x * (1.0 + mod[:, :, 0, 1]) + mod[:, :, 0, 0]
    f = 0.5 * swiglu(h, W['ffn_w1'][l, 0], W['ffn_w3'][l, 0], W['ffn_w2'][l, 0])
    x = layer_norm(ALPHA * x + mod[:, :, 0, 2] * f, W['ln_g'][l, 0], W['ln_b'][l, 0])
    h = x * (1.0 + mod[:, :, 1, 1]) + mod[:, :, 1, 0]
    m, s5_re, s5_im, hg = mixer_block(h, grid, s5_h0_re, s5_h0_im, hg_h0, lb_l, W, l)
    x = layer_norm(ALPHA * x + mod[:, :, 1, 2] * m, W['ln_g'][l, 1], W['ln_b'][l, 1])
    h = x * (1.0 + mod[:, :, 2, 1]) + mod[:, :, 2, 0]
    f = 0.5 * swiglu(h, W['ffn_w1'][l, 1], W['ffn_w3'][l, 1], W['ffn_w2'][l, 1])
    x = layer_norm(ALPHA * x + mod[:, :, 2, 2] * f, W['ln_g'][l, 2], W['ln_b'][l, 2])
    return x, s5_re, s5_im, hg


def run_trunk(x, cond, grid, s5_h0_re, s5_h0_im, hg_h0, lbs, W):
    s5_re, s5_im, hg = [], [], []
    for l in range(DEPTH):
        x, a, b, s = trunk_layer(x, cond, grid, s5_h0_re[:, l], s5_h0_im[:, l], hg_h0[:, l], lbs[l], W, l)
        s5_re.append(a)
        s5_im.append(b)
        hg.append(s)
    return x, jnp.stack(s5_re, 1), jnp.stack(s5_im, 1), jnp.stack(hg, 1)


def setup_inputs(seed: int = 0) -> dict:
    key = jax.random.key(seed)
    ks = iter(jax.random.split(key, 40))

    def nrm(shape, s):
        return jax.random.normal(next(ks), shape, jnp.float32) * s

    lam_im = jnp.broadcast_to(jnp.pi * jnp.arange(S5_STATE, dtype=jnp.float32), (DEPTH, 2, S5_GROUPS, S5_STATE))
    return {
        'x_prompt': nrm((BATCH, SEQ, D_MODEL), 1.0),
        'x_sample': nrm((DEC_BATCH, DEC_SEQ, D_MODEL), 1.0),
        'state_s5_re': nrm((DEC_BATCH, DEPTH, 2, S5_GROUPS, S5_STATE), 0.3),
        'state_s5_im': nrm((DEC_BATCH, DEPTH, 2, S5_GROUPS, S5_STATE), 0.3),
        'state_hgrn': nrm((DEC_BATCH, DEPTH, 2, HG_HEADS, HG_DK, HG_DV), 0.5),
        'c': nrm((DEC_BATCH, D_MODEL), 1.0),
        'c_ctx': nrm((D_MODEL,), 1.0),
        'w_ada': nrm((DEPTH, D_MODEL, N_SUB * 3 * D_MODEL), D_MODEL ** -0.5),
        'b_ada': nrm((DEPTH, N_SUB * 3 * D_MODEL), 0.01),
        'ln_g': 1.0 + nrm((DEPTH, N_SUB, D_MODEL), 0.01),
        'ln_b': nrm((DEPTH, N_SUB, D_MODEL), 0.01),
        'ffn_w1': nrm((DEPTH, 2, D_MODEL, D_FF), D_MODEL ** -0.5),
        'ffn_w3': nrm((DEPTH, 2, D_MODEL, D_FF), D_MODEL ** -0.5),
        'ffn_w2': nrm((DEPTH, 2, D_FF, D_MODEL), BETA * D_FF ** -0.5),
        'w_in': nrm((DEPTH, D_MODEL, IN_WIDTH), D_MODEL ** -0.5),
        'sc_conv': nrm((DEPTH, SC_WIDTH, D_BRANCH), SC_WIDTH ** -0.5),
        's5_lam_re': -0.5 + nrm((DEPTH, 2, S5_GROUPS, S5_STATE), 0.01),
        's5_lam_im': lam_im + nrm((DEPTH, 2, S5_GROUPS, S5_STATE), 0.01),
        's5_log_dt': jax.random.uniform(next(ks), (DEPTH, 2, S5_GROUPS), jnp.float32, math.log(1e-3), math.log(1e-1)),
        's5_b_re': nrm((DEPTH, 2, S5_GROUPS, S5_STATE, S5_GROUP), (0.5 / S5_GROUP) ** 0.5),
        's5_b_im': nrm((DEPTH, 2, S5_GROUPS, S5_STATE, S5_GROUP), (0.5 / S5_GROUP) ** 0.5),
        's5_c_re': nrm((DEPTH, 2, S5_GROUPS, S5_GROUP, S5_STATE), 0.5 ** 0.5),
        's5_c_im': nrm((DEPTH, 2, S5_GROUPS, S5_GROUP, S5_STATE), 0.5 ** 0.5),
        's5_d': nrm((DEPTH, D_BRANCH), 1.0),
        's5_glu_w': nrm((DEPTH, D_BRANCH, D_BRANCH), D_BRANCH ** -0.5),
        's5_glu_b': nrm((DEPTH, D_BRANCH), 0.01),
        'hg_lb': nrm((DEPTH, 2, HG_HEADS * HG_DK), 1.0),
        'hg_norm_g': 1.0 + nrm((DEPTH, HG_DV), 0.01),
        'pool_w': nrm((DEPTH, len(POOL_WINDOWS), POOL_GROUP, POOL_GROUP), POOL_GROUP ** -0.5),
        'pool_scale': 1.0 + nrm((DEPTH, D_BRANCH), 0.01),
        'w_branch': nrm((DEPTH, N_BRANCH, D_BRANCH, D_MODEL), BETA * D_BRANCH ** -0.5),
        'w_out': nrm((DEPTH, D_MODEL, D_MODEL), BETA * D_MODEL ** -0.5),
    }


def reference(x_prompt, x_sample, state_s5_re, state_s5_im, state_hgrn, c, c_ctx,
              w_ada, b_ada, ln_g, ln_b, ffn_w1, ffn_w3, ffn_w2, w_in, sc_conv,
              s5_lam_re, s5_lam_im, s5_log_dt, s5_b_re, s5_b_im, s5_c_re, s5_c_im,
              s5_d, s5_glu_w, s5_glu_b, hg_lb, hg_norm_g, pool_w, pool_scale, w_branch, w_out):
    W = dict(w_ada=w_ada, b_ada=b_ada, ln_g=ln_g, ln_b=ln_b, ffn_w1=ffn_w1, ffn_w3=ffn_w3, ffn_w2=ffn_w2,
             w_in=w_in, sc_conv=sc_conv, s5_lam_re=s5_lam_re, s5_lam_im=s5_lam_im, s5_log_dt=s5_log_dt,
             s5_b_re=s5_b_re, s5_b_im=s5_b_im, s5_c_re=s5_c_re, s5_c_im=s5_c_im, s5_d=s5_d,
             s5_glu_w=s5_glu_w, s5_glu_b=s5_glu_b, hg_norm_g=hg_norm_g, pool_w=pool_w,
             pool_scale=pool_scale, w_branch=w_branch, w_out=w_out)
    lbs = jnp.cumsum(jax.nn.softmax(hg_lb.astype(jnp.float32), axis=0), axis=0)
    lbs = lbs - lbs[0:1]

    bp = x_prompt.shape[0]
    z_s5 = jnp.zeros((bp, DEPTH, 2, S5_GROUPS, S5_STATE), jnp.float32)
    z_hg = jnp.zeros((bp, DEPTH, 2, HG_HEADS, HG_DK, HG_DV), jnp.float32)
    y_prompt, new_s5_re, new_s5_im, new_hgrn = run_trunk(x_prompt, c_ctx[None, :], False, z_s5, z_s5, z_hg, lbs, W)

    xs = x_sample + grid_pos_embedding(x_sample.shape[1], x_sample.dtype)[None]
    y_sample, _, _, _ = run_trunk(xs, c, True, state_s5_re, state_s5_im, state_hgrn, lbs, W)

    return (y_prompt, y_sample, new_s5_re.astype(state_s5_re.dtype),
            new_s5_im.astype(state_s5_im.dtype), new_hgrn.astype(state_hgrn.dtype))
```

```python
import functools

import numpy as np
import jax
import jax.numpy as jnp
from jax import lax
from jax.experimental import pallas as pl
from jax.experimental.pallas import tpu as pltpu

F32 = jnp.float32
BF16 = jnp.bfloat16

D_MODEL = 1024
N_CTX_SEQ = 16
CTX_LEN = 256
DEPTH = 2
N_LAT_SEQ = 2
LAT_LEN = 4096
GRID_W = 64
D_BRANCH = 512
N_BRANCH = 4
S5_GROUPS = 32
S5_GROUP = 16
S5_STATE = 64
HG_HEADS = 4
HG_DK = 128
POOL_WINDOWS = (2, 4, 8, 16)
POOL_GROUP = 128
D_FF = 2816
N_SUB = 3
N_IN_PARTS = 10
IN_WIDTH = N_IN_PARTS * D_BRANCH + N_BRANCH * D_MODEL
ADA_WIDTH = N_SUB * 3 * D_MODEL
ALPHA = (2 * DEPTH) ** 0.25
LN_EPS = 1e-5
POS_BASE = 10000.0

N_CTX_TOK = N_CTX_SEQ * CTX_LEN
N_LAT_TOK = N_LAT_SEQ * LAT_LEN
N_TOK = N_CTX_TOK + N_LAT_TOK
N_COND = 1 + N_LAT_SEQ
COND_PAD = 8

LANES = 128
VMEM_LIMIT = 56 * 1024 * 1024

S5_T = 8
S5_BLK = 4
S5_GPB = S5_GROUPS // S5_BLK
S5_SW = S5_GPB * S5_STATE
S5_CW = S5_T * LANES
CTX_CHUNKS = CTX_LEN // S5_T
LAT_CHUNKS = LAT_LEN // S5_T
S5_ROWS_CTX = CTX_CHUNKS * N_CTX_SEQ
S5_ROWS_LAT = LAT_CHUNKS * N_LAT_SEQ
S5_ROWS = S5_ROWS_CTX + S5_ROWS_LAT
LAT_PER_GRP = 8 // N_LAT_SEQ

HG_C = 128
HG_LEVELS = 7
HG_TILES = N_TOK // HG_C
HG_CTX_TILES = N_CTX_TOK // HG_C
HG_TILES_PER_CTX = CTX_LEN // HG_C
HG_TILES_PER_LAT = LAT_LEN // HG_C
HG_E_ROWS = (HG_LEVELS + 1) * HG_C + 8


def _cparams(sem):
    return pltpu.CompilerParams(dimension_semantics=sem, vmem_limit_bytes=VMEM_LIMIT)


def _cond_row(tile, tile_tokens):
    tok = tile * tile_tokens
    return jnp.where(tok < N_CTX_TOK, 0, 1 + (tok - N_CTX_TOK) // LAT_LEN)


def _dot(a, b):
    return jnp.dot(a, b, preferred_element_type=F32)


def _dot_nt(a, b):
    return lax.dot_general(a, b, (((1,), (1,)), ((), ())), preferred_element_type=F32)


def _dot_tn(a, b):
    return lax.dot_general(a, b, (((0,), (0,)), ((), ())), preferred_element_type=F32)


def _split3(x):
    h1 = x.astype(BF16)
    r1 = x - h1.astype(F32)
    h2 = r1.astype(BF16)
    h3 = (r1 - h2.astype(F32)).astype(BF16)
    return h1, h2, h3


def _silu(x):
    return x * jax.nn.sigmoid(x)


def _layer_norm(y, g, b):
    mu = jnp.mean(y, axis=-1, keepdims=True)
    yc = y - mu
    var = jnp.mean(yc * yc, axis=-1, keepdims=True)
    return yc * lax.rsqrt(var + LN_EPS) * g + b


ADA_TN = 1152


def _ada_kernel(c_ref, w_ref, b_ref, o_ref):
    s = _silu(c_ref[...]).astype(BF16)
    o_ref[0] = _dot(s, w_ref[0].astype(BF16)) + b_ref[0]


def _ada(cond, w_ada, b_ada):
    return pl.pallas_call(
        _ada_kernel,
        out_shape=jax.ShapeDtypeStruct((DEPTH, COND_PAD, ADA_WIDTH), F32),
        grid=(DEPTH, ADA_WIDTH // ADA_TN),
        in_specs=[
            pl.BlockSpec((COND_PAD, D_MODEL), lambda l, j: (0, 0)),
            pl.BlockSpec((1, D_MODEL, ADA_TN), lambda l, j: (l, 0, j)),
            pl.BlockSpec((1, 1, ADA_TN), lambda l, j: (l, 0, j)),
        ],
        out_specs=pl.BlockSpec((1, COND_PAD, ADA_TN), lambda l, j: (l, 0, j)),
        compiler_params=_cparams(("parallel", "parallel")),
        name="ada",
    )(cond, w_ada, b_ada.reshape(DEPTH, 1, ADA_WIDTH))


FFN_TM = 512
FFN_TF = 1408


def _ffn_kernel(x_ref, mod_ref, w1_ref, w3_ref, w2_ref, g_ref, b_ref, o_ref, h_sc, acc_sc, *, sub):
    j = pl.program_id(1)

    @pl.when(j == 0)
    def _():
        shift = mod_ref[0, 3 * sub:3 * sub + 1, :]
        scale = mod_ref[0, 3 * sub + 1:3 * sub + 2, :]
        h_sc[...] = (x_ref[...] * (1.0 + scale) + shift).astype(BF16)
        acc_sc[...] = jnp.zeros_like(acc_sc)

    h = h_sc[...]
    a = _dot(h, w1_ref[...])
    b = _dot(h, w3_ref[...])
    acc_sc[...] += _dot((_silu(a) * b).astype(BF16), w2_ref[...])

    @pl.when(j == pl.num_programs(1) - 1)
    def _():
        gate = mod_ref[0, 3 * sub + 2:3 * sub + 3, :]
        y = ALPHA * x_ref[...] + gate * (0.5 * acc_sc[...])
        o_ref[...] = _layer_norm(y, g_ref[...], b_ref[...])


def _ffn(x, mod_l, w1, w3, w2, ln_g, ln_b, sub):
    return pl.pallas_call(
        functools.partial(_ffn_kernel, sub=sub),
        out_shape=jax.ShapeDtypeStruct((N_TOK, D_MODEL), F32),
        grid=(N_TOK // FFN_TM, D_FF // FFN_TF),
        in_specs=[
            pl.BlockSpec((FFN_TM, D_MODEL), lambda i, j: (i, 0)),
            pl.BlockSpec((1, N_SUB * 3, D_MODEL), lambda i, j: (_cond_row(i, FFN_TM), 0, 0)),
            pl.BlockSpec((D_MODEL, FFN_TF), lambda i, j: (0, j)),
            pl.BlockSpec((D_MODEL, FFN_TF), lambda i, j: (0, j)),
            pl.BlockSpec((FFN_TF, D_MODEL), lambda i, j: (j, 0)),
            pl.BlockSpec((1, D_MODEL), lambda i, j: (0, 0)),
            pl.BlockSpec((1, D_MODEL), lambda i, j: (0, 0)),
        ],
        out_specs=pl.BlockSpec((FFN_TM, D_MODEL), lambda i, j: (i, 0)),
        scratch_shapes=[pltpu.VMEM((FFN_TM, D_MODEL), BF16), pltpu.VMEM((FFN_TM, D_MODEL), F32)],
        compiler_params=_cparams(("parallel", "arbitrary")),
        name=f"ffn{sub}",
    )(x, mod_l, w1, w3, w2, ln_g.reshape(1, D_MODEL), ln_b.reshape(1, D_MODEL))


INP_TM = 512
INP_TN = 1536


def _inproj_kernel(x_ref, mod_ref, w_ref, o_ref, h_sc):
    @pl.when(pl.program_id(1) == 0)
    def _():
        shift = mod_ref[0, 3:4, :]
        scale = mod_ref[0, 4:5, :]
        h_sc[...] = (x_ref[...] * (1.0 + scale) + shift).astype(BF16)

    o_ref[...] = _dot(h_sc[...], w_ref[...])


def _inproj(x, mod_l, w_in):
    return pl.pallas_call(
        _inproj_kernel,
        out_shape=jax.ShapeDtypeStruct((N_TOK, IN_WIDTH), F32),
        grid=(N_TOK // INP_TM, IN_WIDTH // INP_TN),
        in_specs=[
            pl.BlockSpec((INP_TM, D_MODEL), lambda i, j: (i, 0)),
            pl.BlockSpec((1, N_SUB * 3, D_MODEL), lambda i, j: (_cond_row(i, INP_TM), 0, 0)),
            pl.BlockSpec((D_MODEL, INP_TN), lambda i, j: (0, j)),
        ],
        out_specs=pl.BlockSpec((INP_TM, INP_TN), lambda i, j: (i, j)),
        scratch_shapes=[pltpu.VMEM((INP_TM, D_MODEL), BF16)],
        compiler_params=_cparams(("parallel", "arbitrary")),
        name="inproj",
    )(x, mod_l, w_in)


def _s5_prep_kernel(v_ref, cre_ref, cim_ref, bre_ref, bim_ref,
                    pcre_ref, pcim_ref, pbre_ref, pbim_ref, kk_ref, are_ref, aim_ref):
    lam_re = v_ref[0, 0, 0, 0:1, :]
    lam_im = v_ref[0, 0, 0, 1:2, :]
    dt = jnp.exp(v_ref[0, 0, 0, 2:3, :])
    mag = jnp.exp(lam_re * dt)
    a_re = mag * jnp.cos(lam_im * dt)
    a_im = mag * jnp.sin(lam_im * dt)
    den = lam_re * lam_re + lam_im * lam_im
    num_re = a_re - 1.0
    coef_re = (num_re * lam_re + a_im * lam_im) / den
    coef_im = (a_im * lam_re - num_re * lam_im) / den
    c_re = cre_ref[0, 0, 0]
    c_im = cim_ref[0, 0, 0]
    bt_re = bre_ref[0, 0, 0]
    bt_im = bim_ref[0, 0, 0]
    bb_re = coef_re * bt_re - coef_im * bt_im
    bb_im = coef_re * bt_im + coef_im * bt_re
    bbr = _split3(bb_re)
    bbi = _split3(bb_im)
    for j in range(S5_T + 1):
        mj = jnp.exp(lam_re * dt * float(j))
        aj_re = mj * jnp.cos(lam_im * dt * float(j))
        aj_im = mj * jnp.sin(lam_im * dt * float(j))
        pc_re = c_re * aj_re - c_im * aj_im
        pc_im = c_re * aj_im + c_im * aj_re
        pcre_ref[0, 0, 0, j] = pc_re
        pcim_ref[0, 0, 0, j] = pc_im
        if j == S5_T:
            are_ref[0, 0, 0] = aj_re
            aim_ref[0, 0, 0] = aj_im
            continue
        pbre_ref[0, 0, 0, j] = bb_re * aj_re - bb_im * aj_im
        pbim_ref[0, 0, 0, j] = bb_re * aj_im + bb_im * aj_re
        pr = _split3(pc_re)
        pi = _split3(pc_im)
        acc = jnp.zeros((S5_GROUP, S5_GROUP), F32)
        for x in range(3):
            for y in range(3 - x):
                acc = acc + _dot_nt(pr[x], bbr[y]) - _dot_nt(pi[x], bbi[y])
        kk_ref[0, 0, 0, j] = acc


def _s5_prep(vec, c_re, c_im, bt_re, bt_im):
    lead = (DEPTH, 2, S5_GROUPS)

    def spec(*tail):
        return pl.BlockSpec((1, 1, 1) + tail, lambda l, d, g: (l, d, g) + (0,) * len(tail))

    def shape(*tail):
        return jax.ShapeDtypeStruct(lead + tail, F32)

    return pl.pallas_call(
        _s5_prep_kernel,
        out_shape=(
            shape(S5_T + 1, S5_GROUP, S5_STATE), shape(S5_T + 1, S5_GROUP, S5_STATE),
            shape(S5_T, S5_GROUP, S5_STATE), shape(S5_T, S5_GROUP, S5_STATE),
            shape(S5_T, S5_GROUP, S5_GROUP), shape(1, S5_STATE), shape(1, S5_STATE),
        ),
        grid=lead,
        in_specs=[spec(3, S5_STATE)] + [spec(S5_GROUP, S5_STATE)] * 4,
        out_specs=(
            spec(S5_T + 1, S5_GROUP, S5_STATE), spec(S5_T + 1, S5_GROUP, S5_STATE),
            spec(S5_T, S5_GROUP, S5_STATE), spec(S5_T, S5_GROUP, S5_STATE),
            spec(S5_T, S5_GROUP, S5_GROUP), spec(1, S5_STATE), spec(1, S5_STATE),
        ),
        compiler_params=_cparams(("parallel", "parallel", "parallel")),
        name="s5_prep",
    )(vec, c_re, c_im, bt_re, bt_im)


def _s5_tables(pc_re, pc_im, pb_re, pb_im, kk, a8_re, a8_im, d_skip):
    eye_g = jnp.eye(S5_GPB, dtype=F32)
    t = np.arange(S5_T)
    lag_f = t[None, :] - t[:, None]
    toep_f = jnp.where(jnp.asarray(lag_f >= 0)[None, :, :, None, None],
                       jnp.take(kk[0], np.clip(lag_f, 0, S5_T - 1), axis=1), 0.0)
    toep_b = jnp.where(jnp.asarray(lag_f <= 0)[None, :, :, None, None],
                       jnp.take(kk[1], np.clip(-lag_f, 0, S5_T - 1), axis=1), 0.0)
    skip = (jnp.eye(S5_T, dtype=F32)[None, :, :, None, None] * jnp.eye(S5_GROUP, dtype=F32)[None, None, None]
            * d_skip.reshape(S5_GROUPS, 1, 1, S5_GROUP, 1))
    toep = jnp.stack([toep_f + skip, toep_b])
    m = jnp.einsum('dbgsthk,gq->dbsgktqh', toep.reshape(2, S5_BLK, S5_GPB, S5_T, S5_T, S5_GROUP, S5_GROUP), eye_g)
    m = m.reshape(2, S5_BLK, S5_CW, S5_CW)

    def win_part(pb):
        by_s = jnp.stack([pb[0][:, ::-1], pb[1]])
        w = jnp.einsum('dbgshp,gq->dbsghqp', by_s.reshape(2, S5_BLK, S5_GPB, S5_T, S5_GROUP, S5_STATE), eye_g)
        return w.reshape(2, S5_BLK, S5_CW, S5_SW)

    def wout_part(pc):
        by_t = jnp.stack([pc[0][:, 1:], pc[1][:, :0:-1]])
        w = jnp.einsum('dbgthp,gq->dbqptgh', by_t.reshape(2, S5_BLK, S5_GPB, S5_T, S5_GROUP, S5_STATE), eye_g)
        return w.reshape(2, S5_BLK, S5_SW, S5_CW)

    w_in = jnp.concatenate([win_part(pb_re), win_part(pb_im)], axis=-1)
    w_out = jnp.concatenate([wout_part(pc_re), -wout_part(pc_im)], axis=-2)
    a8 = jnp.stack([a8_re.reshape(2, S5_BLK, 1, S5_SW), a8_im.reshape(2, S5_BLK, 1, S5_SW)], axis=2)
    return m.astype(BF16), w_in.astype(BF16), w_out.astype(BF16), a8


def _s5_kernel(u_ref, m_ref, win_ref, wout_ref, a8_ref, h0_ref, y_ref, fin_ref, xc_sc, xl_sc):
    d = pl.program_id(1)
    u = u_ref[0]
    xloc = _dot(u, win_ref[0, 0])
    xc_sc[...] = xloc[:S5_ROWS_CTX].reshape(CTX_CHUNKS, N_CTX_SEQ, 2 * S5_SW)
    xl_sc[...] = xloc[S5_ROWS_CTX:].reshape(S5_ROWS_LAT // 8, 8, 2 * S5_SW)
    a_re = a8_ref[0, 0, 0]
    a_im = a8_ref[0, 0, 1]

    def step(s_re, s_im, loc):
        n_re = a_re * s_re - a_im * s_im + loc[:, :S5_SW]
        n_im = a_re * s_im + a_im * s_re + loc[:, S5_SW:]
        return n_re, n_im

    def ctx_body(k, carry, reverse):
        c = (CTX_CHUNKS - 1 - k) if reverse else k
        loc = xc_sc[c]
        xc_sc[c, :, :S5_SW] = carry[0]
        xc_sc[c, :, S5_SW:] = carry[1]
        return step(carry[0], carry[1], loc)

    def lat_body(k, carry, reverse):
        g = (S5_ROWS_LAT // 8 - 1 - k) if reverse else k
        order = range(LAT_PER_GRP - 1, -1, -1) if reverse else range(LAT_PER_GRP)
        for j in order:
            rows = slice(j * N_LAT_SEQ, (j + 1) * N_LAT_SEQ)
            loc = xl_sc[g, rows, :]
            xl_sc[g, rows, :S5_SW] = carry[0]
            xl_sc[g, rows, S5_SW:] = carry[1]
            carry = step(carry[0], carry[1], loc)
        return carry

    zero = jnp.zeros((N_CTX_SEQ, S5_SW), F32)
    h0 = h0_ref[0, 0]

    def run(reverse):
        fin = lax.fori_loop(0, CTX_CHUNKS, functools.partial(ctx_body, reverse=reverse), (zero, zero))
        fin_ref[0, 0, :, :S5_SW] = fin[0]
        fin_ref[0, 0, :, S5_SW:] = fin[1]
        lax.fori_loop(0, S5_ROWS_LAT // 8, functools.partial(lat_body, reverse=reverse),
                      (h0[:, :S5_SW], h0[:, S5_SW:]))

    @pl.when(d == 0)
    def _():
        run(False)

    @pl.when(d == 1)
    def _():
        run(True)

    xin = jnp.concatenate([xc_sc[...].reshape(S5_ROWS_CTX, 2 * S5_SW),
                           xl_sc[...].reshape(S5_ROWS_LAT, 2 * S5_SW)], axis=0).astype(BF16)
    y = _dot(u, m_ref[0, 0]) + _dot(xin, wout_ref[0, 0])

    @pl.when(d == 0)
    def _():
        y_ref[0] = y

    @pl.when(d == 1)
    def _():
        y_ref[0] += y


def _s5_main(u, m, w_in, w_out, a8, h0):
    return pl.pallas_call(
        _s5_kernel,
        out_shape=(jax.ShapeDtypeStruct((S5_BLK, S5_ROWS, S5_CW), F32),
                   jax.ShapeDtypeStruct((2, S5_BLK, N_CTX_SEQ, 2 * S5_SW), F32)),
        grid=(S5_BLK, 2),
        in_specs=[
            pl.BlockSpec((1, S5_ROWS, S5_CW), lambda b, d: (b, 0, 0)),
            pl.BlockSpec((1, 1, S5_CW, S5_CW), lambda b, d: (d, b, 0, 0)),
            pl.BlockSpec((1, 1, S5_CW, 2 * S5_SW), lambda b, d: (d, b, 0, 0)),
            pl.BlockSpec((1, 1, 2 * S5_SW, S5_CW), lambda b, d: (d, b, 0, 0)),
            pl.BlockSpec((1, 1, 2, 1, S5_SW), lambda b, d: (d, b, 0, 0, 0)),
            pl.BlockSpec((1, 1, N_LAT_SEQ, 2 * S5_SW), lambda b, d: (d, b, 0, 0)),
        ],
        out_specs=(
            pl.BlockSpec((1, S5_ROWS, S5_CW), lambda b, d: (b, 0, 0)),
            pl.BlockSpec((1, 1, N_CTX_SEQ, 2 * S5_SW), lambda b, d: (d, b, 0, 0)),
        ),
        scratch_shapes=[pltpu.VMEM((CTX_CHUNKS, N_CTX_SEQ, 2 * S5_SW), F32),
                        pltpu.VMEM((S5_ROWS_LAT // 8, 8, 2 * S5_SW), F32)],
        compiler_params=_cparams(("parallel", "arbitrary")),
        name="s5",
    )(u, m, w_in, w_out, a8, h0)


def _s5_rows(s5_u):
    ctx = s5_u[:N_CTX_TOK].reshape(N_CTX_SEQ, CTX_CHUNKS, S5_T, S5_BLK, LANES)
    lat = s5_u[N_CTX_TOK:].reshape(N_LAT_SEQ, LAT_CHUNKS, S5_T, S5_BLK, LANES)
    ctx = ctx.transpose(3, 1, 0, 2, 4).reshape(S5_BLK, S5_ROWS_CTX, S5_CW)
    lat = lat.transpose(3, 1, 0, 2, 4).reshape(S5_BLK, S5_ROWS_LAT, S5_CW)
    return jnp.concatenate([ctx, lat], axis=1)


def _s5_unrows(y):
    ctx = y[:, :S5_ROWS_CTX].reshape(S5_BLK, CTX_CHUNKS, N_CTX_SEQ, S5_T, LANES)
    lat = y[:, S5_ROWS_CTX:].reshape(S5_BLK, LAT_CHUNKS, N_LAT_SEQ, S5_T, LANES)
    ctx = ctx.transpose(2, 1, 3, 0, 4).reshape(N_CTX_TOK, D_BRANCH)
    lat = lat.transpose(2, 1, 3, 0, 4).reshape(N_LAT_TOK, D_BRANCH)
    return jnp.concatenate([ctx, lat], axis=0)


def _hg_constants():
    t = np.arange(HG_C)
    ae = np.zeros((2, HG_E_ROWS, HG_C), np.float32)
    mask = np.zeros((2, HG_LEVELS, HG_C, HG_C), np.float32)
    for lvl in range(HG_LEVELS):
        half = 1 << lvl
        pos = t % (2 * half)
        mid = t - pos + half - 1
        upper = pos >= half
        u = t[None, :]
        rows_upper = (u > mid[:, None]) & (u <= t[:, None])
        rows_lower = (u > t[:, None]) & (u <= mid[:, None])
        ae[0, lvl * HG_C:(lvl + 1) * HG_C] = np.where(upper[:, None], rows_upper, rows_lower)
        same = (t[:, None] // (2 * half)) == (t[None, :] // (2 * half))
        mask[0, lvl] = same & upper[:, None] & (~upper)[None, :]
    ae[0, HG_LEVELS * HG_C:(HG_LEVELS + 1) * HG_C] = t[None, :] <= t[:, None]
    ae[0, (HG_LEVELS + 1) * HG_C:] = 1.0
    ae[1] = ae[0][:, ::-1]
    ae[1, :(HG_LEVELS + 1) * HG_C] = ae[1, :(HG_LEVELS + 1) * HG_C].reshape(HG_LEVELS + 1, HG_C, HG_C)[:, ::-1].reshape(-1, HG_C)
    mask[1] = mask[0][:, ::-1, ::-1]
    return ae, mask


def _hg_tile(d, i):
    return jnp.where(d == 0, i, HG_TILES - 1 - i)


def _hg_kernel(q_ref, f_ref, v_ref, lb_ref, ae_ref, mask_ref, h0_ref, o_ref, fin_ref, st_sc, *, layer):
    d = pl.program_id(0)
    tile = _hg_tile(d, pl.program_id(1))
    first_ctx = jnp.where(d == 0, 0, HG_TILES_PER_CTX - 1)
    first_lat = jnp.where(d == 0, 0, HG_TILES_PER_LAT - 1)
    is_start = jnp.where(tile < HG_CTX_TILES,
                         tile % HG_TILES_PER_CTX == first_ctx,
                         (tile - HG_CTX_TILES) % HG_TILES_PER_LAT == first_lat)

    @pl.when(is_start)
    def _():
        st_sc[...] = h0_ref[0, 0]

    x = lb_ref[:, 0, 0, :]
    e = jnp.exp(x - jnp.max(x, axis=0, keepdims=True))
    sm = e / jnp.sum(e, axis=0, keepdims=True)
    lb = jnp.sum(sm[1:layer + 1], axis=0, keepdims=True) if layer > 0 else jnp.zeros((1, D_BRANCH), F32)

    f = lb + (1.0 - lb) * jax.nn.sigmoid(f_ref[...])
    logf = jnp.log(f)
    kk = 1.0 - f
    ae = ae_ref[0]
    ex = sum(_dot(ae, piece) for piece in _split3(logf))
    dec = jnp.exp(ex[:(HG_LEVELS + 1) * HG_C])
    cum = ex[HG_LEVELS * HG_C:(HG_LEVELS + 1) * HG_C]
    tot = ex[(HG_LEVELS + 1) * HG_C:(HG_LEVELS + 1) * HG_C + 1]
    q = q_ref[...]
    v = v_ref[...]
    q_in = (q * dec[HG_LEVELS * HG_C:]).astype(BF16)
    k_tail = (kk * jnp.exp(tot - cum)).astype(BF16)
    dec_tot = jnp.exp(tot)
    qk = q * kk
    v_bf = v.astype(BF16)
    outs = []
    for h in range(HG_HEADS):
        sl = slice(h * HG_DK, (h + 1) * HG_DK)
        scores = jnp.zeros((HG_C, HG_C), F32)
        for lvl in range(HG_LEVELS):
            g = dec[lvl * HG_C:(lvl + 1) * HG_C, sl]
            scores = scores + mask_ref[0, lvl] * _dot_nt((q[:, sl] * g).astype(BF16), (kk[:, sl] * g).astype(BF16))
        diag = jnp.sum(qk[:, sl], axis=-1, keepdims=True)
        st = st_sc[h]
        o_h = _dot(scores.astype(BF16), v_bf[:, sl]) + diag * v[:, sl] + _dot_nt(q_in[:, sl], st.astype(BF16))
        outs.append(o_h)
        st_new = st * dec_tot[:, sl] + _dot_tn(v_bf[:, sl], k_tail[:, sl])
        st_sc[h] = st_new
        fin_ref[0, 0, h] = st_new
    o_ref[0] = jnp.concatenate(outs, axis=-1)


def _hgrn(proj, hg_lb, ae, mask, h0t, layer):
    q_col, f_col, v_col = 4, 5, 7

    def tok_spec(col):
        return pl.BlockSpec((HG_C, D_BRANCH), lambda d, i: (_hg_tile(d, i), col))

    def fin_slot(d, i):
        tile = _hg_tile(d, i)
        return jnp.where(tile < HG_CTX_TILES, tile // HG_TILES_PER_CTX, N_CTX_SEQ)

    return pl.pallas_call(
        functools.partial(_hg_kernel, layer=layer),
        out_shape=(jax.ShapeDtypeStruct((2, N_TOK, D_BRANCH), F32),
                   jax.ShapeDtypeStruct((N_CTX_SEQ + 1, 2, HG_HEADS, HG_DK, HG_DK), F32)),
        grid=(2, HG_TILES),
        in_specs=[
            tok_spec(q_col),
            pl.BlockSpec((HG_C, D_BRANCH), lambda d, i: (_hg_tile(d, i), f_col + d)),
            tok_spec(v_col),
            pl.BlockSpec((DEPTH, 1, 1, D_BRANCH), lambda d, i: (0, d, 0, 0)),
            pl.BlockSpec((1, HG_E_ROWS, HG_C), lambda d, i: (d, 0, 0)),
            pl.BlockSpec((1, HG_LEVELS, HG_C, HG_C), lambda d, i: (d, 0, 0, 0)),
            pl.BlockSpec((1, 1, HG_HEADS, HG_DK, HG_DK),
                         lambda d, i: (_cond_row(_hg_tile(d, i), HG_C), d, 0, 0, 0)),
        ],
        out_specs=(
            pl.BlockSpec((1, HG_C, D_BRANCH), lambda d, i: (d, _hg_tile(d, i), 0)),
            pl.BlockSpec((1, 1, HG_HEADS, HG_DK, HG_DK), lambda d, i: (fin_slot(d, i), d, 0, 0, 0)),
        ),
        scratch_shapes=[pltpu.VMEM((HG_HEADS, HG_DK, HG_DK), F32)],
        compiler_params=_cparams(("arbitrary", "arbitrary")),
        name="hgrn",
    )(proj, proj, proj, hg_lb.reshape(DEPTH, 2, 1, D_BRANCH), ae, mask, h0t)


MRG_TM = 256


def _gelu_tanh(x):
    return 0.5 * x * (1.0 + jnp.tanh(0.7978845608028654 * (x + 0.044715 * (x * x * x))))


def _merge_kernel(scb_ref, scc_ref, sch_ref, hgg_ref, pool_ref, g0_ref, g1_ref, g2_ref, g3_ref,
                  ys5_ref, of_ref, ob_ref, x_ref, mod_ref, conv_ref, gluw_ref, glub_ref, ng_ref,
                  poolw_ref, pscale_ref, wbr_ref, wout_ref, lng_ref, lnb_ref, o_ref):
    tile = pl.program_id(0)
    line = jnp.where(tile * MRG_TM < N_CTX_TOK, CTX_LEN, GRID_W)
    pos = lax.broadcasted_iota(jnp.int32, (MRG_TM, 1), 0) & (line - 1)

    def shifted(val, k):
        rolled = pltpu.roll(val, k % MRG_TM, axis=0)
        ok = (pos >= k) if k > 0 else (pos < line + k)
        return jnp.where(ok, rolled, 0.0)

    m = scc_ref[...] * sch_ref[...]
    conv = conv_ref[0:1, :] * shifted(m, 1) + conv_ref[1:2, :] * m + conv_ref[2:3, :] * shifted(m, -1)
    ya = scb_ref[...] * conv

    z = _gelu_tanh(ys5_ref[...])
    yb = z * jax.nn.sigmoid(_dot(z.astype(BF16), gluw_ref[...]) + glub_ref[...])

    o = of_ref[0] + ob_ref[0]
    normed = []
    for h in range(HG_HEADS):
        oh = o[:, h * HG_DK:(h + 1) * HG_DK]
        ms = jnp.mean(oh * oh, axis=-1, keepdims=True)
        normed.append(oh * lax.rsqrt(ms + LN_EPS) * ng_ref[...])
    yc = jnp.concatenate(normed, axis=-1) * _silu(hgg_ref[...])

    pu = pool_ref[...]
    posf = pos.astype(F32)
    linef = line.astype(F32)
    pooled = []
    for gi, w in enumerate(POOL_WINDOWS):
        vg = pu[:, gi * POOL_GROUP:(gi + 1) * POOL_GROUP]
        back, fwd, span = vg, vg, 1
        while 2 * span <= w // 2:
            back = back + shifted(back, span)
            fwd = fwd + shifted(fwd, -span)
            span *= 2
        s = shifted(back, 1) + fwd
        cnt = jnp.minimum(posf + w // 2, linef) - jnp.maximum(posf - w // 2, 0.0)
        pg = s / cnt - vg
        pooled.append(_dot(pg.astype(BF16), poolw_ref[gi]))
    yd = jnp.concatenate(pooled, axis=-1) * pscale_ref[...]

    merged = jnp.zeros((MRG_TM, D_MODEL), F32)
    for k, (br, gate_ref) in enumerate(((ya, g0_ref), (yb, g1_ref), (yc, g2_ref), (yd, g3_ref))):
        merged = merged + jax.nn.sigmoid(gate_ref[...]) * _dot(br.astype(BF16), wbr_ref[k])
    mix = _dot(merged.astype(BF16), wout_ref[...])
    y = ALPHA * x_ref[...] + mod_ref[0, 5:6, :] * mix
    o_ref[...] = _layer_norm(y, lng_ref[...], lnb_ref[...])


def _merge(proj, y_s5, o_hg, x, mod_l, conv_w, glu_w, glu_b, norm_g, pool_w, pool_scale, w_branch, w_out, ln_g, ln_b):
    def part(col):
        return pl.BlockSpec((MRG_TM, D_BRANCH), lambda i: (i, col))

    def gate(k):
        return pl.BlockSpec((MRG_TM, D_MODEL), lambda i: (i, N_IN_PARTS * D_BRANCH // D_MODEL + k))

    def full(*shape):
        return pl.BlockSpec(shape, lambda i: (0,) * len(shape))

    return pl.pallas_call(
        _merge_kernel,
        out_shape=jax.ShapeDtypeStruct((N_TOK, D_MODEL), F32),
        grid=(N_TOK // MRG_TM,),
        in_specs=[
            part(0), part(1), part(2), part(8), part(9), gate(0), gate(1), gate(2), gate(3),
            pl.BlockSpec((MRG_TM, D_BRANCH), lambda i: (i, 0)),
            pl.BlockSpec((1, MRG_TM, D_BRANCH), lambda i: (0, i, 0)),
            pl.BlockSpec((1, MRG_TM, D_BRANCH), lambda i: (1, i, 0)),
            pl.BlockSpec((MRG_TM, D_MODEL), lambda i: (i, 0)),
            pl.BlockSpec((1, N_SUB * 3, D_MODEL), lambda i: (_cond_row(i, MRG_TM), 0, 0)),
            full(3, D_BRANCH), full(D_BRANCH, D_BRANCH), full(1, D_BRANCH), full(1, HG_DK),
            full(len(POOL_WINDOWS), POOL_GROUP, POOL_GROUP), full(1, D_BRANCH),
            full(N_BRANCH, D_BRANCH, D_MODEL), full(D_MODEL, D_MODEL), full(1, D_MODEL), full(1, D_MODEL),
        ],
        out_specs=pl.BlockSpec((MRG_TM, D_MODEL), lambda i: (i, 0)),
        compiler_params=_cparams(("parallel",)),
        name="merge",
    )(proj, proj, proj, proj, proj, proj, proj, proj, proj, y_s5, o_hg, o_hg, x, mod_l,
      conv_w, glu_w, glu_b.reshape(1, D_BRANCH), norm_g.reshape(1, HG_DK), pool_w,
      pool_scale.reshape(1, D_BRANCH), w_branch, w_out, ln_g.reshape(1, D_MODEL), ln_b.reshape(1, D_MODEL))


def _grid_pos_embedding():
    rows = LAT_LEN // GRID_W
    r = jnp.repeat(jnp.arange(rows, dtype=F32), GRID_W)
    col = jnp.tile(jnp.arange(GRID_W, dtype=F32), rows)
    quarter = D_MODEL // 4
    omega = POS_BASE ** (-jnp.arange(quarter, dtype=F32) / quarter)
    ar = r[:, None] * omega
    ac = col[:, None] * omega
    return jnp.concatenate([jnp.sin(ar), jnp.cos(ar), jnp.sin(ac), jnp.cos(ac)], -1)


def kernel(x_prompt, x_sample, state_s5_re, state_s5_im, state_hgrn, c, c_ctx, w_ada, b_ada, ln_g, ln_b,
           ffn_w1, ffn_w3, ffn_w2, w_in, sc_conv, s5_lam_re, s5_lam_im, s5_log_dt, s5_b_re, s5_b_im,
           s5_c_re, s5_c_im, s5_d, s5_glu_w, s5_glu_b, hg_lb, hg_norm_g, pool_w, pool_scale, w_branch, w_out):
    xs = x_sample + _grid_pos_embedding()[None]
    x = jnp.concatenate([x_prompt.reshape(N_CTX_TOK, D_MODEL), xs.reshape(N_LAT_TOK, D_MODEL)], axis=0)

    cond = jnp.zeros((COND_PAD, D_MODEL), F32).at[0].set(c_ctx).at[1:N_COND].set(c)
    mod = _ada(cond, w_ada, b_ada).reshape(DEPTH, COND_PAD, N_SUB * 3, D_MODEL)

    vec = jnp.stack([s5_lam_re, s5_lam_im,
                     jnp.broadcast_to(s5_log_dt[..., None], s5_lam_re.shape)], axis=3)
    tabs = _s5_prep(vec, s5_c_re, s5_c_im, jnp.swapaxes(s5_b_re, -1, -2), jnp.swapaxes(s5_b_im, -1, -2))

    ae_np, mask_np = _hg_constants()
    ae = jnp.asarray(ae_np, BF16)
    mask = jnp.asarray(mask_np, F32)

    w1 = ffn_w1.astype(BF16)
    w3 = ffn_w3.astype(BF16)
    w2 = ffn_w2.astype(BF16)
    w_in_bf = w_in.astype(BF16)
    glu_w_bf = s5_glu_w.astype(BF16)
    pool_w_bf = pool_w.astype(BF16)
    w_branch_bf = w_branch.astype(BF16)
    w_out_bf = w_out.astype(BF16)

    fin_s5, fin_hg = [], []
    for l in range(DEPTH):
        x = _ffn(x, mod[l], w1[l, 0], w3[l, 0], w2[l, 0], ln_g[l, 0], ln_b[l, 0], 0)
        proj = _inproj(x, mod[l], w_in_bf[l])

        m, s5_win, s5_wout, a8 = _s5_tables(*(t[l] for t in tabs), s5_d[l])
        h0 = jnp.concatenate([
            state_s5_re[:, l].reshape(N_LAT_SEQ, 2, S5_BLK, S5_SW),
            state_s5_im[:, l].reshape(N_LAT_SEQ, 2, S5_BLK, S5_SW)], axis=-1).transpose(1, 2, 0, 3)
        u = _s5_rows(proj[:, 3 * D_BRANCH:4 * D_BRANCH]).astype(BF16)
        y_rows, fin = _s5_main(u, m, s5_win, s5_wout, a8, h0)
        y_s5 = _s5_unrows(y_rows)
        fin_s5.append(fin)

        h0t = jnp.concatenate([jnp.zeros((1, 2, HG_HEADS, HG_DK, HG_DK), F32),
                               jnp.swapaxes(state_hgrn[:, l], -1, -2)], axis=0)
        o_hg, fin_h = _hgrn(proj, hg_lb, ae, mask, h0t, l)
        fin_hg.append(fin_h[:N_CTX_SEQ])

        x = _merge(proj, y_s5, o_hg, x, mod[l], sc_conv[l], glu_w_bf[l], s5_glu_b[l], hg_norm_g[l],
                   pool_w_bf[l], pool_scale[l], w_branch_bf[l], w_out_bf[l], ln_g[l, 1], ln_b[l, 1])
        x = _ffn(x, mod[l], w1[l, 1], w3[l, 1], w2[l, 1], ln_g[l, 2], ln_b[l, 2], 2)

    y_prompt = x[:N_CTX_TOK].reshape(N_CTX_SEQ, CTX_LEN, D_MODEL)
    y_sample = x[N_CTX_TOK:].reshape(N_LAT_SEQ, LAT_LEN, D_MODEL)
    fin = jnp.stack(fin_s5)
    fin = fin.reshape(DEPTH, 2, S5_BLK, N_CTX_SEQ, 2, S5_GPB, S5_STATE)
    fin = fin.transpose(4, 3, 0, 1, 2, 5, 6).reshape(2, N_CTX_SEQ, DEPTH, 2, S5_GROUPS, S5_STATE)
    new_hgrn = jnp.swapaxes(jnp.stack(fin_hg, axis=1), -1, -2)
    return y_prompt, y_sample, fin[0], fin[1], new_hgrn
```

```python
import functools

import numpy as np
import jax
import jax.numpy as jnp
from jax import lax
from jax.experimental import pallas as pl
from jax.experimental.pallas import tpu as pltpu

F32 = jnp.float32
BF16 = jnp.bfloat16

D_MODEL = 1024
N_CTX_SEQ = 16
CTX_LEN = 256
DEPTH = 2
N_LAT_SEQ = 2
LAT_LEN = 4096
GRID_W = 64
D_BRANCH = 512
N_BRANCH = 4
S5_GROUPS = 32
S5_GROUP = 16
S5_STATE = 64
HG_HEADS = 4
HG_DK = 128
POOL_WINDOWS = (2, 4, 8, 16)
POOL_GROUP = 128
D_FF = 2816
N_SUB = 3
N_IN_PARTS = 10
IN_WIDTH = N_IN_PARTS * D_BRANCH + N_BRANCH * D_MODEL
ADA_WIDTH = N_SUB * 3 * D_MODEL
ALPHA = (2 * DEPTH) ** 0.25
LN_EPS = 1e-5
POS_BASE = 10000.0

N_CTX_TOK = N_CTX_SEQ * CTX_LEN
N_LAT_TOK = N_LAT_SEQ * LAT_LEN
N_TOK = N_CTX_TOK + N_LAT_TOK
N_COND = 1 + N_LAT_SEQ
COND_PAD = 8

LANES = 128
SUBLANES = 8
VMEM_LIMIT = 56 * 1024 * 1024

S5_T = 8
S5_BLK = 4
S5_GPB = S5_GROUPS // S5_BLK
S5_SW = S5_GPB * S5_STATE
S5_CW = S5_T * LANES
CTX_CHUNKS = CTX_LEN // S5_T
LAT_CHUNKS = LAT_LEN // S5_T
S5_ROWS_CTX = N_CTX_TOK // S5_T
S5_ROWS_LAT = N_LAT_TOK // S5_T
S5_ROWS = S5_ROWS_CTX + S5_ROWS_LAT
S5_LAT_GROUPS = LAT_CHUNKS // SUBLANES
S5_POW_ROWS = 24
S5_U_COL = 3

HG_C = 128
HG_LEVELS = 7
HG_TILES = N_TOK // HG_C
HG_CTX_TILES = N_CTX_TOK // HG_C
HG_TILES_PER_CTX = CTX_LEN // HG_C
HG_TILES_PER_LAT = LAT_LEN // HG_C
HG_E_ROWS = (HG_LEVELS + 1) * HG_C + 8


def _cparams(sem):
    return pltpu.CompilerParams(dimension_semantics=sem, vmem_limit_bytes=VMEM_LIMIT)


def _cond_row(tile, tile_tokens):
    tok = tile * tile_tokens
    return jnp.where(tok < N_CTX_TOK, 0, 1 + (tok - N_CTX_TOK) // LAT_LEN)


def _dot(a, b):
    return jnp.dot(a, b, preferred_element_type=F32)


def _dot_nt(a, b):
    return lax.dot_general(a, b, (((1,), (1,)), ((), ())), preferred_element_type=F32)


def _dot_tn(a, b):
    return lax.dot_general(a, b, (((0,), (0,)), ((), ())), preferred_element_type=F32)


def _split3(x):
    h1 = x.astype(BF16)
    r1 = x - h1.astype(F32)
    h2 = r1.astype(BF16)
    h3 = (r1 - h2.astype(F32)).astype(BF16)
    return h1, h2, h3


def _dot_nt_hi(a3, b3):
    acc = None
    for x in range(3):
        for y in range(3 - x):
            term = _dot_nt(a3[x], b3[y])
            acc = term if acc is None else acc + term
    return acc


def _silu(x):
    return x * jax.nn.sigmoid(x)


def _layer_norm(y, g, b):
    mu = jnp.mean(y, axis=-1, keepdims=True)
    yc = y - mu
    var = jnp.mean(yc * yc, axis=-1, keepdims=True)
    return yc * lax.rsqrt(var + LN_EPS) * g + b


def _mod_spec(layer, tile_tokens):
    return pl.BlockSpec((1, 1, N_SUB * 3, D_MODEL),
                        lambda i, *_: (layer, _cond_row(i, tile_tokens), 0, 0))


def _ln_spec(layer, sub):
    return pl.BlockSpec((1, 1, 1, D_MODEL), lambda *_: (layer, sub, 0, 0))


ADA_TN = 1152


def _ada_kernel(c_ref, w_ref, b_ref, o_ref):
    s = _silu(c_ref[...]).astype(BF16)
    o_ref[0] = _dot(s, w_ref[0].astype(BF16)) + b_ref[0]


def _ada(cond, w_ada, b_ada):
    return pl.pallas_call(
        _ada_kernel,
        out_shape=jax.ShapeDtypeStruct((DEPTH, COND_PAD, ADA_WIDTH), F32),
        grid=(DEPTH, ADA_WIDTH // ADA_TN),
        in_specs=[
            pl.BlockSpec((COND_PAD, D_MODEL), lambda l, j: (0, 0)),
            pl.BlockSpec((1, D_MODEL, ADA_TN), lambda l, j: (l, 0, j)),
            pl.BlockSpec((1, 1, ADA_TN), lambda l, j: (l, 0, j)),
        ],
        out_specs=pl.BlockSpec((1, COND_PAD, ADA_TN), lambda l, j: (l, 0, j)),
        compiler_params=_cparams(("parallel", "parallel")),
        name="ada",
    )(cond, w_ada, b_ada.reshape(DEPTH, 1, ADA_WIDTH))


FFN_TM = 2048
FFN_TF = 256


def _ffn_kernel(x_ref, mod_ref, w1_ref, w3_ref, w2_ref, g_ref, b_ref, o_ref, h_sc, *, sub):
    j = pl.program_id(1)

    @pl.when(j == 0)
    def _():
        shift = mod_ref[0, 0, 3 * sub:3 * sub + 1, :]
        scale = mod_ref[0, 0, 3 * sub + 1:3 * sub + 2, :]
        h_sc[...] = (x_ref[...] * (1.0 + scale) + shift).astype(BF16)

    h = h_sc[...]
    a = _dot(h, w1_ref[0, 0].astype(BF16))
    b = _dot(h, w3_ref[0, 0].astype(BF16))
    part = _dot((_silu(a) * b).astype(BF16), w2_ref[0, 0].astype(BF16))

    @pl.when(j == 0)
    def _():
        o_ref[...] = part

    @pl.when(j > 0)
    def _():
        o_ref[...] += part

    @pl.when(j == pl.num_programs(1) - 1)
    def _():
        gate = mod_ref[0, 0, 3 * sub + 2:3 * sub + 3, :]
        y = ALPHA * x_ref[...] + gate * (0.5 * o_ref[...])
        o_ref[...] = _layer_norm(y, g_ref[0, 0], b_ref[0, 0])


def _ffn(x, mod, w1, w3, w2, ln_g, ln_b, layer, sub):
    which = sub // 2
    return pl.pallas_call(
        functools.partial(_ffn_kernel, sub=sub),
        out_shape=jax.ShapeDtypeStruct((N_TOK, D_MODEL), F32),
        grid=(N_TOK // FFN_TM, D_FF // FFN_TF),
        in_specs=[
            pl.BlockSpec((FFN_TM, D_MODEL), lambda i, j: (i, 0), pipeline_mode=pl.Buffered(1)),
            _mod_spec(layer, FFN_TM),
            pl.BlockSpec((1, 1, D_MODEL, FFN_TF), lambda i, j: (layer, which, 0, j)),
            pl.BlockSpec((1, 1, D_MODEL, FFN_TF), lambda i, j: (layer, which, 0, j)),
            pl.BlockSpec((1, 1, FFN_TF, D_MODEL), lambda i, j: (layer, which, j, 0)),
            _ln_spec(layer, sub), _ln_spec(layer, sub),
        ],
        out_specs=pl.BlockSpec((FFN_TM, D_MODEL), lambda i, j: (i, 0)),
        scratch_shapes=[pltpu.VMEM((FFN_TM, D_MODEL), BF16)],
        compiler_params=_cparams(("parallel", "arbitrary")),
        name=f"ffn{sub}",
    )(x, mod, w1, w3, w2, ln_g, ln_b)


INP_TM = 2048
INP_TN = D_BRANCH


def _inproj_kernel(x_ref, mod_ref, w_ref, o_ref, u_ref, h_sc, blk_sc):
    j = pl.program_id(1)

    @pl.when(j == 0)
    def _():
        shift = mod_ref[0, 0, 3:4, :]
        scale = mod_ref[0, 0, 4:5, :]
        h_sc[...] = (x_ref[...] * (1.0 + scale) + shift).astype(BF16)

    o_ref[...] = _dot(h_sc[...], w_ref[0].astype(BF16))

    @pl.when(j == S5_U_COL)
    def _():
        for b in range(S5_BLK):
            blk_sc[b] = o_ref[:, b * LANES:(b + 1) * LANES]
            for t in range(S5_T):
                rows = blk_sc[b, pl.ds(t, INP_TM // S5_T, stride=S5_T), :]
                u_ref[b, :, t * LANES:(t + 1) * LANES] = rows.astype(BF16)


def _inproj(x, mod, w_in, layer):
    return pl.pallas_call(
        _inproj_kernel,
        out_shape=(jax.ShapeDtypeStruct((N_TOK, IN_WIDTH), F32),
                   jax.ShapeDtypeStruct((S5_BLK, S5_ROWS, S5_CW), BF16)),
        grid=(N_TOK // INP_TM, IN_WIDTH // INP_TN),
        in_specs=[
            pl.BlockSpec((INP_TM, D_MODEL), lambda i, j: (i, 0), pipeline_mode=pl.Buffered(1)),
            _mod_spec(layer, INP_TM),
            pl.BlockSpec((1, D_MODEL, INP_TN), lambda i, j: (layer, 0, j)),
        ],
        out_specs=(pl.BlockSpec((INP_TM, INP_TN), lambda i, j: (i, j)),
                   pl.BlockSpec((S5_BLK, INP_TM // S5_T, S5_CW), lambda i, j: (0, i, 0))),
        scratch_shapes=[pltpu.VMEM((INP_TM, D_MODEL), BF16), pltpu.VMEM((S5_BLK, INP_TM, LANES), F32)],
        compiler_params=_cparams(("parallel", "arbitrary")),
        name="inproj",
    )(x, mod, w_in)


def _s5_tab_kernel(lam_ref, bre_ref, bim_ref, cre_ref, cim_ref, dsk_ref, m_ref, win_ref, wout_ref, pow_ref):
    d = pl.program_id(1)
    fwd = d == 0
    lam_re = lam_ref[0, 0, 0, 0:1, :]
    lam_im = lam_ref[0, 0, 0, 1:2, :]
    dt = jnp.exp(lam_ref[0, 0, 0, 2:3, :])

    def apow(j):
        mag = jnp.exp(lam_re * dt * float(j))
        ang = lam_im * dt * float(j)
        return mag * jnp.cos(ang), mag * jnp.sin(ang)

    a_re, a_im = apow(1)
    den = lam_re * lam_re + lam_im * lam_im
    num_re = a_re - 1.0
    coef_re = (num_re * lam_re + a_im * lam_im) / den
    coef_im = (a_im * lam_re - num_re * lam_im) / den
    b_re = bre_ref[0, 0, 0]
    b_im = bim_ref[0, 0, 0]
    bb_re = coef_re * b_re - coef_im * b_im
    bb_im = coef_re * b_im + coef_im * b_re
    bbr = _split3(bb_re)
    bbi = _split3(bb_im)
    c_re = cre_ref[0, 0, 0]
    c_im = cim_ref[0, 0, 0]

    kmat = []
    for j in range(S5_T + 1):
        aj_re, aj_im = apow(j)
        pc_re = c_re * aj_re - c_im * aj_im
        pc_im = c_re * aj_im + c_im * aj_re
        if j >= 1:
            r = pl.multiple_of(jnp.where(fwd, j - 1, S5_T - j) * LANES, LANES)
            wout_ref[0, 0, 0, pl.ds(r, LANES), :] = jnp.concatenate([pc_re, -pc_im], axis=1).astype(BF16)
        if j < S5_T:
            pb_re = bb_re * aj_re - bb_im * aj_im
            pb_im = bb_re * aj_im + bb_im * aj_re
            r = pl.multiple_of(jnp.where(fwd, S5_T - 1 - j, j) * LANES, LANES)
            win_ref[0, 0, 0, pl.ds(r, LANES), :] = jnp.concatenate([pb_re, pb_im], axis=1).astype(BF16)
            kmat.append(_dot_nt_hi(bbr, _split3(pc_re)) - _dot_nt_hi(bbi, _split3(pc_im)))

    ri = lax.broadcasted_iota(jnp.int32, (LANES, LANES), 0)
    ci = lax.broadcasted_iota(jnp.int32, (LANES, LANES), 1)
    keep_upper = jnp.where(fwd, 1.0, 0.0)
    kmat[0] = kmat[0] + jnp.where(ri == ci, dsk_ref[0, 0], 0.0) * keep_upper
    for s in range(S5_T):
        for t in range(S5_T):
            tile = kmat[abs(t - s)]
            if t > s:
                tile = tile * keep_upper
            elif t < s:
                tile = tile * (1.0 - keep_upper)
            m_ref[0, 0, 0, s * LANES:(s + 1) * LANES, t * LANES:(t + 1) * LANES] = tile.astype(BF16)

    pow_ref[...] = jnp.zeros_like(pow_ref)
    for i in range(SUBLANES):
        for row, j in ((i, S5_T * (i + 1)), (SUBLANES + i, S5_T * (SUBLANES - i))):
            p_re, p_im = apow(j)
            pow_ref[0, 0, 0, 0, row:row + 1, :] = p_re
            pow_ref[0, 0, 0, 1, row:row + 1, :] = p_im
    p_re, p_im = apow(S5_T * 2 * SUBLANES)
    pow_ref[0, 0, 0, 0, 2 * SUBLANES:2 * SUBLANES + 1, :] = p_re
    pow_ref[0, 0, 0, 1, 2 * SUBLANES:2 * SUBLANES + 1, :] = p_im


def _s5_tables(lam, b_re, b_im, c_re, c_im, d_skip):
    lead = (DEPTH, 2, S5_BLK)

    def spec(*tail):
        return pl.BlockSpec((1, 1, 1) + tail, lambda l, d, b: (l, d, b) + (0,) * len(tail))

    return pl.pallas_call(
        _s5_tab_kernel,
        out_shape=(
            jax.ShapeDtypeStruct(lead + (S5_CW, S5_CW), BF16),
            jax.ShapeDtypeStruct(lead + (S5_CW, 2 * S5_SW), BF16),
            jax.ShapeDtypeStruct(lead + (S5_CW, 2 * S5_SW), BF16),
            jax.ShapeDtypeStruct(lead + (2, S5_POW_ROWS, S5_SW), F32),
        ),
        grid=lead,
        in_specs=[spec(3, S5_SW)] + [spec(LANES, S5_SW)] * 4
        + [pl.BlockSpec((1, 1, 1, LANES), lambda l, d, b: (l, b, 0, 0))],
        out_specs=(spec(S5_CW, S5_CW), spec(S5_CW, 2 * S5_SW), spec(S5_CW, 2 * S5_SW),
                   spec(2, S5_POW_ROWS, S5_SW)),
        compiler_params=_cparams(("parallel", "parallel", "parallel")),
        name="s5_tables",
    )(lam, b_re, b_im, c_re, c_im, d_skip)


def _s5_param_layout(s5_lam_re, s5_lam_im, s5_log_dt, s5_b_re, s5_b_im, s5_c_re, s5_c_im, s5_d):
    eye = jnp.eye(S5_GPB, dtype=F32)
    lam = jnp.stack([s5_lam_re, s5_lam_im, jnp.broadcast_to(s5_log_dt[..., None], s5_lam_re.shape)], axis=2)
    lam = lam.reshape(DEPTH, 2, 3, S5_BLK, S5_SW).transpose(0, 1, 3, 2, 4)

    def bdiag_b(b):
        b = b.reshape(DEPTH, 2, S5_BLK, S5_GPB, S5_STATE, S5_GROUP)
        return jnp.einsum('ldbgph,gq->ldbghqp', b, eye).reshape(DEPTH, 2, S5_BLK, LANES, S5_SW)

    def bdiag_c(c):
        c = c.reshape(DEPTH, 2, S5_BLK, S5_GPB, S5_GROUP, S5_STATE)
        return jnp.einsum('ldbghp,gq->ldbghqp', c, eye).reshape(DEPTH, 2, S5_BLK, LANES, S5_SW)

    return (lam, bdiag_b(s5_b_re), bdiag_b(s5_b_im), bdiag_c(s5_c_re), bdiag_c(s5_c_im),
            s5_d.reshape(DEPTH, S5_BLK, 1, LANES))


def _s5_kernel(u_ref, m_ref, win_ref, wout_ref, pow_ref, h0_ref, y_ref, fin_ref, x_sc):
    d = pl.program_id(1)
    u = u_ref[0]
    x_sc[...] = _dot(u, win_ref[0, 0, 0])
    p_re = pow_ref[0, 0, 0, 0]
    p_im = pow_ref[0, 0, 0, 1]
    re = slice(0, S5_SW)
    im = slice(S5_SW, 2 * S5_SW)

    def madd(t_re, t_im, w_re, w_im, s_re, s_im):
        return t_re + w_re * s_re - w_im * s_im, t_im + w_re * s_im + w_im * s_re

    def shift_rows(val, k, n, idx, reverse):
        if reverse:
            return jnp.where(idx < n - k, pltpu.roll(val, val.shape[0] - k, axis=0), 0.0)
        return jnp.where(idx >= k, pltpu.roll(val, k, axis=0), 0.0)

    def pow_row(k):
        row = k - 1 if k <= SUBLANES else 2 * SUBLANES
        return p_re[row:row + 1], p_im[row:row + 1]

    def ctx_scan(reverse):
        s_re = x_sc[0:S5_ROWS_CTX, re]
        s_im = x_sc[0:S5_ROWS_CTX, im]
        c = lax.broadcasted_iota(jnp.int32, (S5_ROWS_CTX, 1), 0) & (CTX_CHUNKS - 1)
        k = 1
        while k < CTX_CHUNKS:
            w_re, w_im = pow_row(k)
            s_re, s_im = madd(s_re, s_im, w_re, w_im,
                              shift_rows(s_re, k, CTX_CHUNKS, c, reverse), shift_rows(s_im, k, CTX_CHUNKS, c, reverse))
            k *= 2
        last = 0 if reverse else CTX_CHUNKS - 1
        pick = (lax.broadcasted_iota(jnp.int32, (N_CTX_SEQ, S5_ROWS_CTX), 1)
                == lax.broadcasted_iota(jnp.int32, (N_CTX_SEQ, S5_ROWS_CTX), 0) * CTX_CHUNKS + last)
        pick = jnp.where(pick, 1.0, 0.0).astype(BF16)
        fin_ref[0, 0, :, re] = sum(_dot(pick, piece) for piece in _split3(s_re))
        fin_ref[0, 0, :, im] = sum(_dot(pick, piece) for piece in _split3(s_im))
        x_sc[0:S5_ROWS_CTX, re] = shift_rows(s_re, 1, CTX_CHUNKS, c, reverse)
        x_sc[0:S5_ROWS_CTX, im] = shift_rows(s_im, 1, CTX_CHUNKS, c, reverse)

    def lat_group(k, carry, reverse):
        g = (S5_LAT_GROUPS - 1 - k) if reverse else k
        r8 = lax.broadcasted_iota(jnp.int32, (SUBLANES, 1), 0)
        tab = slice(SUBLANES, 2 * SUBLANES) if reverse else slice(0, SUBLANES)
        out = []
        for q in range(N_LAT_SEQ):
            base = pl.multiple_of(S5_ROWS_CTX + q * LAT_CHUNKS + g * SUBLANES, SUBLANES)
            s_re = x_sc[pl.ds(base, SUBLANES), re]
            s_im = x_sc[pl.ds(base, SUBLANES), im]
            j = 1
            while j < SUBLANES:
                w_re, w_im = pow_row(j)
                s_re, s_im = madd(s_re, s_im, w_re, w_im,
                                  shift_rows(s_re, j, SUBLANES, r8, reverse), shift_rows(s_im, j, SUBLANES, r8, reverse))
                j *= 2
            c_re, c_im = carry[2 * q], carry[2 * q + 1]
            s_re, s_im = madd(s_re, s_im, p_re[tab], p_im[tab], c_re, c_im)
            edge = (r8 == SUBLANES - 1) if reverse else (r8 == 0)
            x_sc[pl.ds(base, SUBLANES), re] = jnp.where(edge, c_re, shift_rows(s_re, 1, SUBLANES, r8, reverse))
            x_sc[pl.ds(base, SUBLANES), im] = jnp.where(edge, c_im, shift_rows(s_im, 1, SUBLANES, r8, reverse))
            end = slice(0, 1) if reverse else slice(SUBLANES - 1, SUBLANES)
            out += [s_re[end], s_im[end]]
        return tuple(out)

    def run(reverse):
        ctx_scan(reverse)
        h0 = h0_ref[0, 0]
        init = []
        for q in range(N_LAT_SEQ):
            init += [h0[q:q + 1, re], h0[q:q + 1, im]]
        lax.fori_loop(0, S5_LAT_GROUPS, functools.partial(lat_group, reverse=reverse), tuple(init))

    @pl.when(d == 0)
    def _():
        run(False)

    @pl.when(d == 1)
    def _():
        run(True)

    y = _dot(u, m_ref[0, 0, 0]) + _dot_nt(x_sc[...].astype(BF16), wout_ref[0, 0, 0])

    @pl.when(d == 0)
    def _():
        y_ref[0] = y

    @pl.when(d == 1)
    def _():
        y_ref[0] += y


def _s5_main(u, m, w_in, w_out, pows, h0, layer):
    def tab(*tail):
        return pl.BlockSpec((1, 1, 1) + tail, lambda b, d: (layer, d, b) + (0,) * len(tail))

    return pl.pallas_call(
        _s5_kernel,
        out_shape=(jax.ShapeDtypeStruct((S5_BLK, S5_ROWS, S5_CW), F32),
                   jax.ShapeDtypeStruct((2, S5_BLK, N_CTX_SEQ, 2 * S5_SW), F32)),
        grid=(S5_BLK, 2),
        in_specs=[
            pl.BlockSpec((1, S5_ROWS, S5_CW), lambda b, d: (b, 0, 0)),
            tab(S5_CW, S5_CW), tab(S5_CW, 2 * S5_SW), tab(S5_CW, 2 * S5_SW), tab(2, S5_POW_ROWS, S5_SW),
            pl.BlockSpec((1, 1, N_LAT_SEQ, 2 * S5_SW), lambda b, d: (d, b, 0, 0)),
        ],
        out_specs=(
            pl.BlockSpec((1, S5_ROWS, S5_CW), lambda b, d: (b, 0, 0)),
            pl.BlockSpec((1, 1, N_CTX_SEQ, 2 * S5_SW), lambda b, d: (d, b, 0, 0)),
        ),
        scratch_shapes=[pltpu.VMEM((S5_ROWS, 2 * S5_SW), F32)],
        compiler_params=_cparams(("parallel", "arbitrary")),
        name="s5",
    )(u, m, w_in, w_out, pows, h0)


def _hg_constants():
    t = np.arange(HG_C)
    ae = np.zeros((2, HG_E_ROWS, HG_C), np.float32)
    mask = np.zeros((2, HG_LEVELS, HG_C, HG_C), np.float32)
    for lvl in range(HG_LEVELS):
        half = 1 << lvl
        pos = t % (2 * half)
        mid = t - pos + half - 1
        upper = pos >= half
        u = t[None, :]
        rows_upper = (u > mid[:, None]) & (u <= t[:, None])
        rows_lower = (u > t[:, None]) & (u <= mid[:, None])
        ae[0, lvl * HG_C:(lvl + 1) * HG_C] = np.where(upper[:, None], rows_upper, rows_lower)
        same = (t[:, None] // (2 * half)) == (t[None, :] // (2 * half))
        mask[0, lvl] = same & upper[:, None] & (~upper)[None, :]
    ae[0, HG_LEVELS * HG_C:(HG_LEVELS + 1) * HG_C] = t[None, :] <= t[:, None]
    ae[0, (HG_LEVELS + 1) * HG_C:] = 1.0
    ae[1] = ae[0][:, ::-1]
    ae[1, :(HG_LEVELS + 1) * HG_C] = ae[1, :(HG_LEVELS + 1) * HG_C].reshape(HG_LEVELS + 1, HG_C, HG_C)[:, ::-1].reshape(-1, HG_C)
    mask[1] = mask[0][:, ::-1, ::-1]
    return ae, mask


def _hg_tile(direction, i):
    return i if direction == 0 else HG_TILES - 1 - i


def _hg_direction(direction, tile, q_ref, f_ref, v_ref, lb, ae_ref, mask_ref, h0_ref, o_ref, fin_ref, st_sc):
    first_ctx = 0 if direction == 0 else HG_TILES_PER_CTX - 1
    first_lat = 0 if direction == 0 else HG_TILES_PER_LAT - 1
    is_start = jnp.where(tile < HG_CTX_TILES,
                         tile % HG_TILES_PER_CTX == first_ctx,
                         (tile - HG_CTX_TILES) % HG_TILES_PER_LAT == first_lat)

    @pl.when(is_start)
    def _():
        st_sc[direction] = h0_ref[0, 0]

    f = lb + (1.0 - lb) * jax.nn.sigmoid(f_ref[...])
    logf = jnp.log(f)
    kk = 1.0 - f
    ae = ae_ref[direction]
    ex = sum(_dot(ae, piece) for piece in _split3(logf))
    dec = jnp.exp(ex[:(HG_LEVELS + 1) * HG_C])
    cum = ex[HG_LEVELS * HG_C:(HG_LEVELS + 1) * HG_C]
    tot = ex[(HG_LEVELS + 1) * HG_C:(HG_LEVELS + 1) * HG_C + 1]
    q = q_ref[...]
    v = v_ref[...]
    q_in = (q * dec[HG_LEVELS * HG_C:]).astype(BF16)
    k_tail = (kk * jnp.exp(tot - cum)).astype(BF16)
    dec_tot = jnp.exp(tot)
    qk = q * kk
    v_bf = v.astype(BF16)
    outs = []
    for h in range(HG_HEADS):
        sl = slice(h * HG_DK, (h + 1) * HG_DK)
        scores = jnp.zeros((HG_C, HG_C), F32)
        for lvl in range(HG_LEVELS):
            g = dec[lvl * HG_C:(lvl + 1) * HG_C, sl]
            scores = scores + mask_ref[direction, lvl] * _dot_nt((q[:, sl] * g).astype(BF16), (kk[:, sl] * g).astype(BF16))
        diag = jnp.sum(qk[:, sl], axis=-1, keepdims=True)
        st = st_sc[direction, h]
        o_h = _dot(scores.astype(BF16), v_bf[:, sl]) + diag * v[:, sl] + _dot_nt(q_in[:, sl], st.astype(BF16))
        outs.append(o_h)
        st_new = st * dec_tot[:, sl] + _dot_tn(v_bf[:, sl], k_tail[:, sl])
        st_sc[direction, h] = st_new
        fin_ref[0, h] = st_new
    o_ref[...] = jnp.concatenate(outs, axis=-1)


def _hg_kernel(qf_ref, ff_ref, vf_ref, qb_ref, fb_ref, vb_ref, lb_ref, ae_ref, mask_ref, h0f_ref, h0b_ref,
               of_ref, ob_ref, finf_ref, finb_ref, st_sc, *, layer):
    i = pl.program_id(0)
    x = lb_ref[:, :, 0, :]
    e = jnp.exp(x - jnp.max(x, axis=0, keepdims=True))
    sm = e / jnp.sum(e, axis=0, keepdims=True)
    lb = jnp.sum(sm[1:layer + 1], axis=0) if layer > 0 else jnp.zeros((2, D_BRANCH), F32)
    _hg_direction(0, _hg_tile(0, i), qf_ref, ff_ref, vf_ref, lb[0:1], ae_ref, mask_ref, h0f_ref, of_ref, finf_ref, st_sc)
    _hg_direction(1, _hg_tile(1, i), qb_ref, fb_ref, vb_ref, lb[1:2], ae_ref, mask_ref, h0b_ref, ob_ref, finb_ref, st_sc)


def _hgrn(proj, hg_lb, ae, mask, h0t, layer):
    q_col, f_col, v_col = 4, 5, 7

    def tok_spec(direction, col):
        return pl.BlockSpec((HG_C, D_BRANCH), lambda i: (_hg_tile(direction, i), col))

    def fin_spec(direction):
        def index(i):
            tile = _hg_tile(direction, i)
            return (jnp.where(tile < HG_CTX_TILES, tile // HG_TILES_PER_CTX, N_CTX_SEQ), 0, 0, 0)
        return pl.BlockSpec((1, HG_HEADS, HG_DK, HG_DK), index)

    def h0_spec(direction):
        return pl.BlockSpec((1, 1, HG_HEADS, HG_DK, HG_DK),
                            lambda i: (_cond_row(_hg_tile(direction, i), HG_C), direction, 0, 0, 0))

    fin_shape = jax.ShapeDtypeStruct((N_CTX_SEQ + 1, HG_HEADS, HG_DK, HG_DK), F32)
    return pl.pallas_call(
        functools.partial(_hg_kernel, layer=layer),
        out_shape=(jax.ShapeDtypeStruct((N_TOK, D_BRANCH), F32), jax.ShapeDtypeStruct((N_TOK, D_BRANCH), F32),
                   fin_shape, fin_shape),
        grid=(HG_TILES,),
        in_specs=[
            tok_spec(0, q_col), tok_spec(0, f_col), tok_spec(0, v_col),
            tok_spec(1, q_col), tok_spec(1, f_col + 1), tok_spec(1, v_col),
            pl.BlockSpec((DEPTH, 2, 1, D_BRANCH), lambda i: (0, 0, 0, 0)),
            pl.BlockSpec((2, HG_E_ROWS, HG_C), lambda i: (0, 0, 0)),
            pl.BlockSpec((2, HG_LEVELS, HG_C, HG_C), lambda i: (0, 0, 0, 0)),
            h0_spec(0), h0_spec(1),
        ],
        out_specs=(
            pl.BlockSpec((HG_C, D_BRANCH), lambda i: (_hg_tile(0, i), 0)),
            pl.BlockSpec((HG_C, D_BRANCH), lambda i: (_hg_tile(1, i), 0)),
            fin_spec(0), fin_spec(1),
        ),
        scratch_shapes=[pltpu.VMEM((2, HG_HEADS, HG_DK, HG_DK), F32)],
        compiler_params=_cparams(("arbitrary",)),
        name="hgrn",
    )(proj, proj, proj, proj, proj, proj, hg_lb.reshape(DEPTH, 2, 1, D_BRANCH), ae, mask, h0t, h0t)


MRG_TM = 256


def _gelu_tanh(x):
    return 0.5 * x * (1.0 + jnp.tanh(0.7978845608028654 * (x + 0.044715 * (x * x * x))))


def _merge_kernel(scb_ref, scc_ref, sch_ref, hgg_ref, pool_ref, g0_ref, g1_ref, g2_ref, g3_ref,
                  yrow_ref, of_ref, ob_ref, x_ref, mod_ref, conv_ref, gluw_ref, glub_ref, ng_ref,
                  poolw_ref, pscale_ref, wbr_ref, wout_ref, lng_ref, lnb_ref, o_ref,
                  ys5_sc, gluw_sc, poolw_sc, wbr_sc, wout_sc):
    tile = pl.program_id(0)

    @pl.when(tile == 0)
    def _():
        gluw_sc[...] = gluw_ref[0].astype(BF16)
        poolw_sc[...] = poolw_ref[0].astype(BF16)
        wbr_sc[...] = wbr_ref[0].astype(BF16)
        wout_sc[...] = wout_ref[0].astype(BF16)

    line = jnp.where(tile * MRG_TM < N_CTX_TOK, CTX_LEN, GRID_W)
    pos = lax.broadcasted_iota(jnp.int32, (MRG_TM, 1), 0) & (line - 1)

    def shifted(val, k):
        rolled = pltpu.roll(val, k % MRG_TM, axis=0)
        ok = (pos >= k) if k > 0 else (pos < line + k)
        return jnp.where(ok, rolled, 0.0)

    m = scc_ref[...] * sch_ref[...]
    conv = conv_ref[0, 0:1, :] * shifted(m, 1) + conv_ref[0, 1:2, :] * m + conv_ref[0, 2:3, :] * shifted(m, -1)
    ya = scb_ref[...] * conv

    for b in range(S5_BLK):
        for t in range(S5_T):
            ys5_sc[b, pl.ds(t, MRG_TM // S5_T, stride=S5_T), :] = yrow_ref[b, :, t * LANES:(t + 1) * LANES]
    z = _gelu_tanh(jnp.concatenate([ys5_sc[b] for b in range(S5_BLK)], axis=-1))
    yb = z * jax.nn.sigmoid(_dot(z.astype(BF16), gluw_sc[...]) + glub_ref[0])

    o = of_ref[...] + ob_ref[...]
    normed = []
    for h in range(HG_HEADS):
        oh = o[:, h * HG_DK:(h + 1) * HG_DK]
        ms = jnp.mean(oh * oh, axis=-1, keepdims=True)
        normed.append(oh * lax.rsqrt(ms + LN_EPS) * ng_ref[0])
    yc = jnp.concatenate(normed, axis=-1) * _silu(hgg_ref[...])

    pu = pool_ref[...]
    posf = pos.astype(F32)
    linef = line.astype(F32)
    pooled = []
    for gi, w in enumerate(POOL_WINDOWS):
        vg = pu[:, gi * POOL_GROUP:(gi + 1) * POOL_GROUP]
        back, fwd, span = vg, vg, 1
        while 2 * span <= w // 2:
            back = back + shifted(back, span)
            fwd = fwd + shifted(fwd, -span)
            span *= 2
        s = shifted(back, 1) + fwd
        cnt = jnp.minimum(posf + w // 2, linef) - jnp.maximum(posf - w // 2, 0.0)
        pg = s / cnt - vg
        pooled.append(_dot(pg.astype(BF16), poolw_sc[gi]))
    yd = jnp.concatenate(pooled, axis=-1) * pscale_ref[0]

    merged = jnp.zeros((MRG_TM, D_MODEL), F32)
    for k, (br, gate_ref) in enumerate(((ya, g0_ref), (yb, g1_ref), (yc, g2_ref), (yd, g3_ref))):
        merged = merged + jax.nn.sigmoid(gate_ref[...]) * _dot(br.astype(BF16), wbr_sc[k])
    mix = _dot(merged.astype(BF16), wout_sc[...])
    y = ALPHA * x_ref[...] + mod_ref[0, 0, 5:6, :] * mix
    o_ref[...] = _layer_norm(y, lng_ref[0, 0], lnb_ref[0, 0])


def _merge(proj, y_rows, o_f, o_b, x, mod, conv_w, glu_w, glu_b, norm_g, pool_w, pool_scale, w_branch, w_out,
           ln_g, ln_b, layer):
    def part(col):
        return pl.BlockSpec((MRG_TM, D_BRANCH), lambda i: (i, col))

    def gate(k):
        return pl.BlockSpec((MRG_TM, D_MODEL), lambda i: (i, N_IN_PARTS * D_BRANCH // D_MODEL + k))

    def per_layer(*shape, single=False):
        mode = dict(pipeline_mode=pl.Buffered(1)) if single else {}
        return pl.BlockSpec((1,) + shape, lambda i: (layer,) + (0,) * len(shape), **mode)

    return pl.pallas_call(
        _merge_kernel,
        out_shape=jax.ShapeDtypeStruct((N_TOK, D_MODEL), F32),
        grid=(N_TOK // MRG_TM,),
        in_specs=[
            part(0), part(1), part(2), part(8), part(9), gate(0), gate(1), gate(2), gate(3),
            pl.BlockSpec((S5_BLK, MRG_TM // S5_T, S5_CW), lambda i: (0, i, 0)),
            pl.BlockSpec((MRG_TM, D_BRANCH), lambda i: (i, 0)),
            pl.BlockSpec((MRG_TM, D_BRANCH), lambda i: (i, 0)),
            pl.BlockSpec((MRG_TM, D_MODEL), lambda i: (i, 0)),
            _mod_spec(layer, MRG_TM),
            per_layer(3, D_BRANCH), per_layer(D_BRANCH, D_BRANCH, single=True), per_layer(1, D_BRANCH),
            per_layer(1, HG_DK), per_layer(len(POOL_WINDOWS), POOL_GROUP, POOL_GROUP, single=True),
            per_layer(1, D_BRANCH), per_layer(N_BRANCH, D_BRANCH, D_MODEL, single=True),
            per_layer(D_MODEL, D_MODEL, single=True), _ln_spec(layer, 1), _ln_spec(layer, 1),
        ],
        out_specs=pl.BlockSpec((MRG_TM, D_MODEL), lambda i: (i, 0)),
        scratch_shapes=[
            pltpu.VMEM((S5_BLK, MRG_TM, LANES), F32),
            pltpu.VMEM((D_BRANCH, D_BRANCH), BF16),
            pltpu.VMEM((len(POOL_WINDOWS), POOL_GROUP, POOL_GROUP), BF16),
            pltpu.VMEM((N_BRANCH, D_BRANCH, D_MODEL), BF16),
            pltpu.VMEM((D_MODEL, D_MODEL), BF16),
        ],
        compiler_params=_cparams(("arbitrary",)),
        name="merge",
    )(proj, proj, proj, proj, proj, proj, proj, proj, proj, y_rows, o_f, o_b, x, mod,
      conv_w, glu_w, glu_b.reshape(DEPTH, 1, D_BRANCH), norm_g.reshape(DEPTH, 1, HG_DK), pool_w,
      pool_scale.reshape(DEPTH, 1, D_BRANCH), w_branch, w_out, ln_g, ln_b)


def _grid_pos_embedding():
    rows = LAT_LEN // GRID_W
    r = jnp.repeat(jnp.arange(rows, dtype=F32), GRID_W)
    col = jnp.tile(jnp.arange(GRID_W, dtype=F32), rows)
    quarter = D_MODEL // 4
    omega = POS_BASE ** (-jnp.arange(quarter, dtype=F32) / quarter)
    ar = r[:, None] * omega
    ac = col[:, None] * omega
    return jnp.concatenate([jnp.sin(ar), jnp.cos(ar), jnp.sin(ac), jnp.cos(ac)], -1)


def kernel(x_prompt, x_sample, state_s5_re, state_s5_im, state_hgrn, c, c_ctx, w_ada, b_ada, ln_g, ln_b,
           ffn_w1, ffn_w3, ffn_w2, w_in, sc_conv, s5_lam_re, s5_lam_im, s5_log_dt, s5_b_re, s5_b_im,
           s5_c_re, s5_c_im, s5_d, s5_glu_w, s5_glu_b, hg_lb, hg_norm_g, pool_w, pool_scale, w_branch, w_out):
    xs = x_sample + _grid_pos_embedding()[None]
    x = jnp.concatenate([x_prompt.reshape(N_CTX_TOK, D_MODEL), xs.reshape(N_LAT_TOK, D_MODEL)], axis=0)

    cond = jnp.zeros((COND_PAD, D_MODEL), F32).at[0].set(c_ctx).at[1:N_COND].set(c)
    mod = _ada(cond, w_ada, b_ada).reshape(DEPTH, COND_PAD, N_SUB * 3, D_MODEL)
    ln_g4 = ln_g.reshape(DEPTH, N_SUB, 1, D_MODEL)
    ln_b4 = ln_b.reshape(DEPTH, N_SUB, 1, D_MODEL)

    s5_m, s5_win, s5_wout, s5_pow = _s5_tables(*_s5_param_layout(
        s5_lam_re, s5_lam_im, s5_log_dt, s5_b_re, s5_b_im, s5_c_re, s5_c_im, s5_d))

    ae_np, mask_np = _hg_constants()
    ae = jnp.asarray(ae_np, BF16)
    mask = jnp.asarray(mask_np, F32)

    fin_s5, fin_hg = [], []
    for l in range(DEPTH):
        x = _ffn(x, mod, ffn_w1, ffn_w3, ffn_w2, ln_g4, ln_b4, l, 0)
        proj, u_rows = _inproj(x, mod, w_in, l)

        h0 = jnp.concatenate([
            state_s5_re[:, l].reshape(N_LAT_SEQ, 2, S5_BLK, S5_SW),
            state_s5_im[:, l].reshape(N_LAT_SEQ, 2, S5_BLK, S5_SW)], axis=-1).transpose(1, 2, 0, 3)
        y_rows, fin = _s5_main(u_rows, s5_m, s5_win, s5_wout, s5_pow, h0, l)
        fin_s5.append(fin)

        h0t = jnp.concatenate([jnp.zeros((1, 2, HG_HEADS, HG_DK, HG_DK), F32),
                               jnp.swapaxes(state_hgrn[:, l], -1, -2)], axis=0)
        o_f, o_b, fin_f, fin_b = _hgrn(proj, hg_lb, ae, mask, h0t, l)
        fin_hg.append(jnp.stack([fin_f[:N_CTX_SEQ], fin_b[:N_CTX_SEQ]], axis=1))

        x = _merge(proj, y_rows, o_f, o_b, x, mod, sc_conv, s5_glu_w, s5_glu_b, hg_norm_g,
                   pool_w, pool_scale, w_branch, w_out, ln_g4, ln_b4, l)
        x = _ffn(x, mod, ffn_w1, ffn_w3, ffn_w2, ln_g4, ln_b4, l, 2)

    y_prompt = x[:N_CTX_TOK].reshape(N_CTX_SEQ, CTX_LEN, D_MODEL)
    y_sample = x[N_CTX_TOK:].reshape(N_LAT_SEQ, LAT_LEN, D_MODEL)
    fin = jnp.stack(fin_s5)
    fin = fin.reshape(DEPTH, 2, S5_BLK, N_CTX_SEQ, 2, S5_GPB, S5_STATE)
    fin = fin.transpose(4, 3, 0, 1, 2, 5, 6).reshape(2, N_CTX_SEQ, DEPTH, 2, S5_GROUPS, S5_STATE)
    new_hgrn = jnp.swapaxes(jnp.stack(fin_hg, axis=1), -1, -2)
    return y_prompt, y_sample, fin[0], fin[1], new_hgrn
```

```python
import functools

import numpy as np
import jax
import jax.numpy as jnp
from jax import lax
from jax.experimental import pallas as pl
from jax.experimental.pallas import tpu as pltpu

F32 = jnp.float32
BF16 = jnp.bfloat16

D_MODEL = 1024
N_CTX_SEQ = 16
CTX_LEN = 256
DEPTH = 2
N_LAT_SEQ = 2
LAT_LEN = 4096
GRID_W = 64
D_BRANCH = 512
N_BRANCH = 4
S5_GROUPS = 32
S5_GROUP = 16
S5_STATE = 64
HG_HEADS = 4
HG_DK = 128
POOL_WINDOWS = (2, 4, 8, 16)
POOL_GROUP = 128
D_FF = 2816
N_SUB = 3
N_IN_PARTS = 10
IN_WIDTH = N_IN_PARTS * D_BRANCH + N_BRANCH * D_MODEL
ADA_WIDTH = N_SUB * 3 * D_MODEL
ALPHA = (2 * DEPTH) ** 0.25
LN_EPS = 1e-5
POS_BASE = 10000.0

N_CTX_TOK = N_CTX_SEQ * CTX_LEN
N_LAT_TOK = N_LAT_SEQ * LAT_LEN
N_TOK = N_CTX_TOK + N_LAT_TOK
N_COND = 1 + N_LAT_SEQ
COND_PAD = 8

LANES = 128
SUBLANES = 8
VMEM_LIMIT = 56 * 1024 * 1024

S5_T = 8
S5_BLK = 4
S5_GPB = S5_GROUPS // S5_BLK
S5_SW = S5_GPB * S5_STATE
S5_CW = S5_T * LANES
CTX_CHUNKS = CTX_LEN // S5_T
LAT_CHUNKS = LAT_LEN // S5_T
S5_ROWS_CTX = N_CTX_TOK // S5_T
S5_ROWS_LAT = N_LAT_TOK // S5_T
S5_ROWS = S5_ROWS_CTX + S5_ROWS_LAT
S5_LAT_GROUPS = LAT_CHUNKS // SUBLANES
S5_POW_ROWS = 24
S5_U_COL = 3

HG_C = 128
HG_LEVELS = 7
HG_TILES = N_TOK // HG_C
HG_CTX_TILES = N_CTX_TOK // HG_C
HG_TILES_PER_CTX = CTX_LEN // HG_C
HG_TILES_PER_LAT = LAT_LEN // HG_C
HG_E_ROWS = (HG_LEVELS + 1) * HG_C + 8


def _cparams(sem):
    return pltpu.CompilerParams(dimension_semantics=sem, vmem_limit_bytes=VMEM_LIMIT)


def _cond_row(tile, tile_tokens):
    tok = tile * tile_tokens
    return jnp.where(tok < N_CTX_TOK, 0, 1 + (tok - N_CTX_TOK) // LAT_LEN)


def _dot(a, b):
    return jnp.dot(a, b, preferred_element_type=F32)


def _dot_nt(a, b):
    return lax.dot_general(a, b, (((1,), (1,)), ((), ())), preferred_element_type=F32)


def _dot_tn(a, b):
    return lax.dot_general(a, b, (((0,), (0,)), ((), ())), preferred_element_type=F32)


def _split3(x):
    h1 = x.astype(BF16)
    r1 = x - h1.astype(F32)
    h2 = r1.astype(BF16)
    h3 = (r1 - h2.astype(F32)).astype(BF16)
    return h1, h2, h3


def _dot_nt_hi(a3, b3):
    acc = None
    for x in range(3):
        for y in range(3 - x):
            term = _dot_nt(a3[x], b3[y])
            acc = term if acc is None else acc + term
    return acc


def _silu(x):
    return x * jax.nn.sigmoid(x)


def _layer_norm(y, g, b):
    mu = jnp.mean(y, axis=-1, keepdims=True)
    yc = y - mu
    var = jnp.mean(yc * yc, axis=-1, keepdims=True)
    return yc * lax.rsqrt(var + LN_EPS) * g + b


def _mod_spec(layer, tile_tokens):
    return pl.BlockSpec((1, 1, N_SUB * 3, D_MODEL),
                        lambda i, *_: (layer, _cond_row(i, tile_tokens), 0, 0))


def _ln_spec(layer, sub):
    return pl.BlockSpec((1, 1, 1, D_MODEL), lambda *_: (layer, sub, 0, 0))


ADA_TN = 1152


def _ada_kernel(c_ref, w_ref, b_ref, o_ref):
    s = _silu(c_ref[...]).astype(BF16)
    o_ref[0] = _dot(s, w_ref[0].astype(BF16)) + b_ref[0]


def _ada(cond, w_ada, b_ada):
    return pl.pallas_call(
        _ada_kernel,
        out_shape=jax.ShapeDtypeStruct((DEPTH, COND_PAD, ADA_WIDTH), F32),
        grid=(DEPTH, ADA_WIDTH // ADA_TN),
        in_specs=[
            pl.BlockSpec((COND_PAD, D_MODEL), lambda l, j: (0, 0)),
            pl.BlockSpec((1, D_MODEL, ADA_TN), lambda l, j: (l, 0, j)),
            pl.BlockSpec((1, 1, ADA_TN), lambda l, j: (l, 0, j)),
        ],
        out_specs=pl.BlockSpec((1, COND_PAD, ADA_TN), lambda l, j: (l, 0, j)),
        compiler_params=_cparams(("parallel", "parallel")),
        name="ada",
    )(cond, w_ada, b_ada.reshape(DEPTH, 1, ADA_WIDTH))


CAST_ROWS = 512


def _cast_kernel(*refs):
    n = len(refs) // 2
    for src, dst in zip(refs[:n], refs[n:]):
        dst[...] = src[...].astype(BF16)


def _cast_bf16(*ws):
    shape = ws[0].shape
    rows, cols = int(np.prod(shape[:-1])), shape[-1]
    spec = pl.BlockSpec((CAST_ROWS, cols), lambda i: (i, 0))
    outs = pl.pallas_call(
        _cast_kernel,
        out_shape=tuple(jax.ShapeDtypeStruct((rows, cols), BF16) for _ in ws),
        grid=(rows // CAST_ROWS,),
        in_specs=[spec] * len(ws),
        out_specs=tuple([spec] * len(ws)),
        compiler_params=_cparams(("parallel",)),
        name="cast_bf16",
    )(*(w.reshape(rows, cols) for w in ws))
    return tuple(o.reshape(shape) for o in outs)


FFN_TM = 512
FFN_TC = 1408


def _ffn_kernel(x_ref, mod_ref, w1_ref, w3_ref, w2_ref, g_ref, b_ref, o_ref, act_sc, *, sub):
    x = x_ref[...]
    shift = mod_ref[0, 0, 3 * sub:3 * sub + 1, :]
    scale = mod_ref[0, 0, 3 * sub + 1:3 * sub + 2, :]
    gate = mod_ref[0, 0, 3 * sub + 2:3 * sub + 3, :]
    h = (x * (1.0 + scale) + shift).astype(BF16)
    for c in range(D_FF // FFN_TC):
        cols = slice(c * FFN_TC, (c + 1) * FFN_TC)
        a = _dot(h, w1_ref[0, 0, :, cols])
        b = _dot(h, w3_ref[0, 0, :, cols])
        act_sc[:, cols] = (_silu(a) * b).astype(BF16)
    f = _dot(act_sc[...], w2_ref[0, 0])
    y = ALPHA * x + gate * (0.5 * f)
    o_ref[...] = _layer_norm(y, g_ref[0, 0], b_ref[0, 0])


def _ffn(x, mod, w1, w3, w2, ln_g, ln_b, layer, sub):
    which = sub // 2

    def resident(*shape):
        return pl.BlockSpec((1, 1) + shape, lambda i: (layer, which, 0, 0), pipeline_mode=pl.Buffered(1))

    return pl.pallas_call(
        functools.partial(_ffn_kernel, sub=sub),
        out_shape=jax.ShapeDtypeStruct((N_TOK, D_MODEL), F32),
        grid=(N_TOK // FFN_TM,),
        in_specs=[
            pl.BlockSpec((FFN_TM, D_MODEL), lambda i: (i, 0)),
            _mod_spec(layer, FFN_TM),
            resident(D_MODEL, D_FF), resident(D_MODEL, D_FF), resident(D_FF, D_MODEL),
            _ln_spec(layer, sub), _ln_spec(layer, sub),
        ],
        out_specs=pl.BlockSpec((FFN_TM, D_MODEL), lambda i: (i, 0)),
        scratch_shapes=[pltpu.VMEM((FFN_TM, D_FF), BF16)],
        compiler_params=_cparams(("parallel",)),
        name=f"ffn{sub}",
    )(x, mod, w1, w3, w2, ln_g, ln_b)


INP_TM = 2048
INP_TN = D_BRANCH
HG_Q_COL, HG_F_COL, HG_V_COL = 4, 5, 7


def _inproj_kernel(x_ref, mod_ref, w_ref, o_ref, fz_ref, u_ref, h_sc, blk_sc):
    j = pl.program_id(1)

    @pl.when(j == 0)
    def _():
        shift = mod_ref[0, 0, 3:4, :]
        scale = mod_ref[0, 0, 4:5, :]
        h_sc[...] = (x_ref[...] * (1.0 + scale) + shift).astype(BF16)

    res = _dot(h_sc[...], w_ref[0].astype(BF16))
    o_ref[...] = res.astype(BF16)

    @pl.when((j == HG_F_COL) | (j == HG_F_COL + 1))
    def _():
        fz_ref[...] = res

    @pl.when(j == S5_U_COL)
    def _():
        for b in range(S5_BLK):
            blk_sc[b] = res[:, b * LANES:(b + 1) * LANES]
            for t in range(S5_T):
                rows = blk_sc[b, pl.ds(t, INP_TM // S5_T, stride=S5_T), :]
                u_ref[b, :, t * LANES:(t + 1) * LANES] = rows.astype(BF16)


def _inproj(x, mod, w_in, layer):
    return pl.pallas_call(
        _inproj_kernel,
        out_shape=(jax.ShapeDtypeStruct((N_TOK, IN_WIDTH), BF16),
                   jax.ShapeDtypeStruct((N_TOK, 2 * D_BRANCH), F32),
                   jax.ShapeDtypeStruct((S5_BLK, S5_ROWS, S5_CW), BF16)),
        grid=(N_TOK // INP_TM, IN_WIDTH // INP_TN),
        in_specs=[
            pl.BlockSpec((INP_TM, D_MODEL), lambda i, j: (i, 0), pipeline_mode=pl.Buffered(1)),
            _mod_spec(layer, INP_TM),
            pl.BlockSpec((1, D_MODEL, INP_TN), lambda i, j: (layer, 0, j)),
        ],
        out_specs=(pl.BlockSpec((INP_TM, INP_TN), lambda i, j: (i, j)),
                   pl.BlockSpec((INP_TM, D_BRANCH), lambda i, j: (i, jnp.clip(j - HG_F_COL, 0, 1))),
                   pl.BlockSpec((S5_BLK, INP_TM // S5_T, S5_CW), lambda i, j: (0, i, 0))),
        scratch_shapes=[pltpu.VMEM((INP_TM, D_MODEL), BF16), pltpu.VMEM((S5_BLK, INP_TM, LANES), F32)],
        compiler_params=_cparams(("parallel", "arbitrary")),
        name="inproj",
    )(x, mod, w_in)


def _s5_tab_kernel(lam_ref, bre_ref, bim_ref, cre_ref, cim_ref, dsk_ref, m_ref, win_ref, wout_ref, pow_ref):
    d = pl.program_id(1)
    fwd = d == 0
    lam_re = lam_ref[0, 0, 0, 0:1, :]
    lam_im = lam_ref[0, 0, 0, 1:2, :]
    dt = jnp.exp(lam_ref[0, 0, 0, 2:3, :])

    def apow(j):
        mag = jnp.exp(lam_re * dt * float(j))
        ang = lam_im * dt * float(j)
        return mag * jnp.cos(ang), mag * jnp.sin(ang)

    a_re, a_im = apow(1)
    den = lam_re * lam_re + lam_im * lam_im
    num_re = a_re - 1.0
    coef_re = (num_re * lam_re + a_im * lam_im) / den
    coef_im = (a_im * lam_re - num_re * lam_im) / den
    b_re = bre_ref[0, 0, 0]
    b_im = bim_ref[0, 0, 0]
    bb_re = coef_re * b_re - coef_im * b_im
    bb_im = coef_re * b_im + coef_im * b_re
    bbr = _split3(bb_re)
    bbi = _split3(bb_im)
    c_re = cre_ref[0, 0, 0]
    c_im = cim_ref[0, 0, 0]

    kmat = []
    for j in range(S5_T + 1):
        aj_re, aj_im = apow(j)
        pc_re = c_re * aj_re - c_im * aj_im
        pc_im = c_re * aj_im + c_im * aj_re
        if j >= 1:
            r = pl.multiple_of(jnp.where(fwd, j - 1, S5_T - j) * LANES, LANES)
            wout_ref[0, 0, 0, pl.ds(r, LANES), :] = jnp.concatenate([pc_re, -pc_im], axis=1).astype(BF16)
        if j < S5_T:
            pb_re = bb_re * aj_re - bb_im * aj_im
            pb_im = bb_re * aj_im + bb_im * aj_re
            r = pl.multiple_of(jnp.where(fwd, S5_T - 1 - j, j) * LANES, LANES)
            win_ref[0, 0, 0, pl.ds(r, LANES), :] = jnp.concatenate([pb_re, pb_im], axis=1).astype(BF16)
            kmat.append(_dot_nt_hi(bbr, _split3(pc_re)) - _dot_nt_hi(bbi, _split3(pc_im)))

    ri = lax.broadcasted_iota(jnp.int32, (LANES, LANES), 0)
    ci = lax.broadcasted_iota(jnp.int32, (LANES, LANES), 1)
    keep_upper = jnp.where(fwd, 1.0, 0.0)
    kmat[0] = kmat[0] + jnp.where(ri == ci, dsk_ref[0, 0], 0.0) * keep_upper
    for s in range(S5_T):
        for t in range(S5_T):
            tile = kmat[abs(t - s)]
            if t > s:
                tile = tile * keep_upper
            elif t < s:
                tile = tile * (1.0 - keep_upper)
            m_ref[0, 0, 0, s * LANES:(s + 1) * LANES, t * LANES:(t + 1) * LANES] = tile.astype(BF16)

    pow_ref[...] = jnp.zeros_like(pow_ref)
    for i in range(SUBLANES):
        for row, j in ((i, S5_T * (i + 1)), (SUBLANES + i, S5_T * (SUBLANES - i))):
            p_re, p_im = apow(j)
            pow_ref[0, 0, 0, 0, row:row + 1, :] = p_re
            pow_ref[0, 0, 0, 1, row:row + 1, :] = p_im
    p_re, p_im = apow(S5_T * 2 * SUBLANES)
    pow_ref[0, 0, 0, 0, 2 * SUBLANES:2 * SUBLANES + 1, :] = p_re
    pow_ref[0, 0, 0, 1, 2 * SUBLANES:2 * SUBLANES + 1, :] = p_im


def _s5_tables(lam, b_re, b_im, c_re, c_im, d_skip):
    lead = (DEPTH, 2, S5_BLK)

    def spec(*tail):
        return pl.BlockSpec((1, 1, 1) + tail, lambda l, d, b: (l, d, b) + (0,) * len(tail))

    return pl.pallas_call(
        _s5_tab_kernel,
        out_shape=(
            jax.ShapeDtypeStruct(lead + (S5_CW, S5_CW), BF16),
            jax.ShapeDtypeStruct(lead + (S5_CW, 2 * S5_SW), BF16),
            jax.ShapeDtypeStruct(lead + (S5_CW, 2 * S5_SW), BF16),
            jax.ShapeDtypeStruct(lead + (2, S5_POW_ROWS, S5_SW), F32),
        ),
        grid=lead,
        in_specs=[spec(3, S5_SW)] + [spec(LANES, S5_SW)] * 4
        + [pl.BlockSpec((1, 1, 1, LANES), lambda l, d, b: (l, b, 0, 0))],
        out_specs=(spec(S5_CW, S5_CW), spec(S5_CW, 2 * S5_SW), spec(S5_CW, 2 * S5_SW),
                   spec(2, S5_POW_ROWS, S5_SW)),
        compiler_params=_cparams(("parallel", "parallel", "parallel")),
        name="s5_tables",
    )(lam, b_re, b_im, c_re, c_im, d_skip)


def _s5_param_layout(s5_lam_re, s5_lam_im, s5_log_dt, s5_b_re, s5_b_im, s5_c_re, s5_c_im, s5_d):
    eye = jnp.eye(S5_GPB, dtype=F32)
    lam = jnp.stack([s5_lam_re, s5_lam_im, jnp.broadcast_to(s5_log_dt[..., None], s5_lam_re.shape)], axis=2)
    lam = lam.reshape(DEPTH, 2, 3, S5_BLK, S5_SW).transpose(0, 1, 3, 2, 4)

    def bdiag_b(b):
        b = b.reshape(DEPTH, 2, S5_BLK, S5_GPB, S5_STATE, S5_GROUP)
        return jnp.einsum('ldbgph,gq->ldbghqp', b, eye).reshape(DEPTH, 2, S5_BLK, LANES, S5_SW)

    def bdiag_c(c):
        c = c.reshape(DEPTH, 2, S5_BLK, S5_GPB, S5_GROUP, S5_STATE)
        return jnp.einsum('ldbghp,gq->ldbghqp', c, eye).reshape(DEPTH, 2, S5_BLK, LANES, S5_SW)

    return (lam, bdiag_b(s5_b_re), bdiag_b(s5_b_im), bdiag_c(s5_c_re), bdiag_c(s5_c_im),
            s5_d.reshape(DEPTH, S5_BLK, 1, LANES))


def _s5_kernel(u_ref, m_ref, win_ref, wout_ref, pow_ref, h0_ref, y_ref, fin_ref, x_sc):
    d = pl.program_id(1)
    u = u_ref[0]
    x_sc[...] = _dot(u, win_ref[0, 0, 0])
    p_re = pow_ref[0, 0, 0, 0]
    p_im = pow_ref[0, 0, 0, 1]
    re = slice(0, S5_SW)
    im = slice(S5_SW, 2 * S5_SW)

    def madd(t_re, t_im, w_re, w_im, s_re, s_im):
        return t_re + w_re * s_re - w_im * s_im, t_im + w_re * s_im + w_im * s_re

    def shift_rows(val, k, n, idx, reverse):
        if reverse:
            return jnp.where(idx < n - k, pltpu.roll(val, val.shape[0] - k, axis=0), 0.0)
        return jnp.where(idx >= k, pltpu.roll(val, k, axis=0), 0.0)

    def pow_row(k):
        row = k - 1 if k <= SUBLANES else 2 * SUBLANES
        return p_re[row:row + 1], p_im[row:row + 1]

    def ctx_scan(reverse):
        s_re = x_sc[0:S5_ROWS_CTX, re]
        s_im = x_sc[0:S5_ROWS_CTX, im]
        c = lax.broadcasted_iota(jnp.int32, (S5_ROWS_CTX, 1), 0) & (CTX_CHUNKS - 1)
        k = 1
        while k < CTX_CHUNKS:
            w_re, w_im = pow_row(k)
            s_re, s_im = madd(s_re, s_im, w_re, w_im,
                              shift_rows(s_re, k, CTX_CHUNKS, c, reverse), shift_rows(s_im, k, CTX_CHUNKS, c, reverse))
            k *= 2
        last = 0 if reverse else CTX_CHUNKS - 1
        pick = (lax.broadcasted_iota(jnp.int32, (N_CTX_SEQ, S5_ROWS_CTX), 1)
                == lax.broadcasted_iota(jnp.int32, (N_CTX_SEQ, S5_ROWS_CTX), 0) * CTX_CHUNKS + last)
        pick = jnp.where(pick, 1.0, 0.0).astype(BF16)
        fin_ref[0, 0, :, re] = sum(_dot(pick, piece) for piece in _split3(s_re))
        fin_ref[0, 0, :, im] = sum(_dot(pick, piece) for piece in _split3(s_im))
        x_sc[0:S5_ROWS_CTX, re] = shift_rows(s_re, 1, CTX_CHUNKS, c, reverse)
        x_sc[0:S5_ROWS_CTX, im] = shift_rows(s_im, 1, CTX_CHUNKS, c, reverse)

    def lat_group(k, carry, reverse):
        g = (S5_LAT_GROUPS - 1 - k) if reverse else k
        r8 = lax.broadcasted_iota(jnp.int32, (SUBLANES, 1), 0)
        tab = slice(SUBLANES, 2 * SUBLANES) if reverse else slice(0, SUBLANES)
        out = []
        for q in range(N_LAT_SEQ):
            base = pl.multiple_of(S5_ROWS_CTX + q * LAT_CHUNKS + g * SUBLANES, SUBLANES)
            s_re = x_sc[pl.ds(base, SUBLANES), re]
            s_im = x_sc[pl.ds(base, SUBLANES), im]
            j = 1
            while j < SUBLANES:
                w_re, w_im = pow_row(j)
                s_re, s_im = madd(s_re, s_im, w_re, w_im,
                                  shift_rows(s_re, j, SUBLANES, r8, reverse), shift_rows(s_im, j, SUBLANES, r8, reverse))
                j *= 2
            c_re, c_im = carry[2 * q], carry[2 * q + 1]
            s_re, s_im = madd(s_re, s_im, p_re[tab], p_im[tab], c_re, c_im)
            edge = (r8 == SUBLANES - 1) if reverse else (r8 == 0)
            x_sc[pl.ds(base, SUBLANES), re] = jnp.where(edge, c_re, shift_rows(s_re, 1, SUBLANES, r8, reverse))
            x_sc[pl.ds(base, SUBLANES), im] = jnp.where(edge, c_im, shift_rows(s_im, 1, SUBLANES, r8, reverse))
            end = slice(0, 1) if reverse else slice(SUBLANES - 1, SUBLANES)
            out += [s_re[end], s_im[end]]
        return tuple(out)

    def run(reverse):
        ctx_scan(reverse)
        h0 = h0_ref[0, 0]
        init = []
        for q in range(N_LAT_SEQ):
            init += [h0[q:q + 1, re], h0[q:q + 1, im]]
        lax.fori_loop(0, S5_LAT_GROUPS, functools.partial(lat_group, reverse=reverse), tuple(init))

    @pl.when(d == 0)
    def _():
        run(False)

    @pl.when(d == 1)
    def _():
        run(True)

    y = _dot(u, m_ref[0, 0, 0]) + _dot_nt(x_sc[...].astype(BF16), wout_ref[0, 0, 0])

    @pl.when(d == 0)
    def _():
        y_ref[0] = y

    @pl.when(d == 1)
    def _():
        y_ref[0] += y


def _s5_main(u, m, w_in, w_out, pows, h0, layer):
    def tab(*tail):
        return pl.BlockSpec((1, 1, 1) + tail, lambda b, d: (layer, d, b) + (0,) * len(tail))

    return pl.pallas_call(
        _s5_kernel,
        out_shape=(jax.ShapeDtypeStruct((S5_BLK, S5_ROWS, S5_CW), F32),
                   jax.ShapeDtypeStruct((2, S5_BLK, N_CTX_SEQ, 2 * S5_SW), F32)),
        grid=(S5_BLK, 2),
        in_specs=[
            pl.BlockSpec((1, S5_ROWS, S5_CW), lambda b, d: (b, 0, 0)),
            tab(S5_CW, S5_CW), tab(S5_CW, 2 * S5_SW), tab(S5_CW, 2 * S5_SW), tab(2, S5_POW_ROWS, S5_SW),
            pl.BlockSpec((1, 1, N_LAT_SEQ, 2 * S5_SW), lambda b, d: (d, b, 0, 0)),
        ],
        out_specs=(
            pl.BlockSpec((1, S5_ROWS, S5_CW), lambda b, d: (b, 0, 0)),
            pl.BlockSpec((1, 1, N_CTX_SEQ, 2 * S5_SW), lambda b, d: (d, b, 0, 0)),
        ),
        scratch_shapes=[pltpu.VMEM((S5_ROWS, 2 * S5_SW), F32)],
        compiler_params=_cparams(("parallel", "arbitrary")),
        name="s5",
    )(u, m, w_in, w_out, pows, h0)


def _hg_constants():
    t = np.arange(HG_C)
    ae = np.zeros((2, HG_E_ROWS, HG_C), np.float32)
    mask = np.zeros((2, HG_LEVELS, HG_C, HG_C), np.float32)
    for lvl in range(HG_LEVELS):
        half = 1 << lvl
        pos = t % (2 * half)
        mid = t - pos + half - 1
        upper = pos >= half
        u = t[None, :]
        rows_upper = (u > mid[:, None]) & (u <= t[:, None])
        rows_lower = (u > t[:, None]) & (u <= mid[:, None])
        ae[0, lvl * HG_C:(lvl + 1) * HG_C] = np.where(upper[:, None], rows_upper, rows_lower)
        same = (t[:, None] // (2 * half)) == (t[None, :] // (2 * half))
        mask[0, lvl] = same & upper[:, None] & (~upper)[None, :]
    ae[0, HG_LEVELS * HG_C:(HG_LEVELS + 1) * HG_C] = t[None, :] <= t[:, None]
    ae[0, (HG_LEVELS + 1) * HG_C:] = 1.0
    ae[1] = ae[0][:, ::-1]
    ae[1, :(HG_LEVELS + 1) * HG_C] = ae[1, :(HG_LEVELS + 1) * HG_C].reshape(HG_LEVELS + 1, HG_C, HG_C)[:, ::-1].reshape(-1, HG_C)
    mask[1] = mask[0][:, ::-1, ::-1]
    return ae, mask


def _hg_tile(direction, i):
    return i if direction == 0 else HG_TILES - 1 - i


def _hg_direction(direction, q_ref, f_ref, v_ref, lb, ae_ref, mask_ref, o_ref, fin_ref, st_sc):
    f = lb + (1.0 - lb) * jax.nn.sigmoid(f_ref[...])
    logf = jnp.log(f)
    kk = 1.0 - f
    ae = ae_ref[direction]
    ex = sum(_dot(ae, piece) for piece in _split3(logf))
    dec = jnp.exp(ex[:(HG_LEVELS + 1) * HG_C])
    cum = ex[HG_LEVELS * HG_C:(HG_LEVELS + 1) * HG_C]
    tot = ex[(HG_LEVELS + 1) * HG_C:(HG_LEVELS + 1) * HG_C + 1]
    q = q_ref[...].astype(F32)
    v_bf = v_ref[...]
    v = v_bf.astype(F32)
    q_in = (q * dec[HG_LEVELS * HG_C:]).astype(BF16)
    k_tail = (kk * jnp.exp(tot - cum)).astype(BF16)
    dec_tot = jnp.exp(tot)
    qk = q * kk
    outs = []
    for h in range(HG_HEADS):
        sl = slice(h * HG_DK, (h + 1) * HG_DK)
        scores = jnp.zeros((HG_C, HG_C), F32)
        for lvl in range(HG_LEVELS):
            g = dec[lvl * HG_C:(lvl + 1) * HG_C, sl]
            scores = scores + mask_ref[direction, lvl] * _dot_nt((q[:, sl] * g).astype(BF16), (kk[:, sl] * g).astype(BF16))
        diag = jnp.sum(qk[:, sl], axis=-1, keepdims=True)
        st = st_sc[direction, h]
        o_h = _dot(scores.astype(BF16), v_bf[:, sl]) + diag * v[:, sl] + _dot_nt(q_in[:, sl], st.astype(BF16))
        outs.append(o_h)
        st_new = st * dec_tot[:, sl] + _dot_tn(v_bf[:, sl], k_tail[:, sl])
        st_sc[direction, h] = st_new
        fin_ref[0, h] = st_new
    o_ref[...] = jnp.concatenate(outs, axis=-1)


def _hg_kernel(qf_ref, ff_ref, vf_ref, qb_ref, fb_ref, vb_ref, lb_ref, ae_ref, mask_ref, h0f_ref, h0b_ref,
               of_ref, ob_ref, finf_ref, finb_ref, st_sc, *, layer):
    i = pl.program_id(0)
    x = lb_ref[:, :, 0, :]
    e = jnp.exp(x - jnp.max(x, axis=0, keepdims=True))
    sm = e / jnp.sum(e, axis=0, keepdims=True)
    lb = jnp.sum(sm[1:layer + 1], axis=0) if layer > 0 else jnp.zeros((2, D_BRANCH), F32)
    for direction, h0_ref in ((0, h0f_ref), (1, h0b_ref)):
        tile = _hg_tile(direction, i)
        first_ctx = 0 if direction == 0 else HG_TILES_PER_CTX - 1
        first_lat = 0 if direction == 0 else HG_TILES_PER_LAT - 1
        is_start = jnp.where(tile < HG_CTX_TILES,
                             tile % HG_TILES_PER_CTX == first_ctx,
                             (tile - HG_CTX_TILES) % HG_TILES_PER_LAT == first_lat)

        @pl.when(is_start)
        def _(direction=direction, h0_ref=h0_ref):
            st_sc[direction] = h0_ref[0, 0]

    _hg_direction(0, qf_ref, ff_ref, vf_ref, lb[0:1], ae_ref, mask_ref, of_ref, finf_ref, st_sc)
    _hg_direction(1, qb_ref, fb_ref, vb_ref, lb[1:2], ae_ref, mask_ref, ob_ref, finb_ref, st_sc)


def _hgrn(proj, fz, hg_lb, ae, mask, h0t, layer):
    def tok_spec(direction, col):
        return pl.BlockSpec((HG_C, D_BRANCH), lambda i: (_hg_tile(direction, i), col))

    def fin_spec(direction):
        def index(i):
            tile = _hg_tile(direction, i)
            return (jnp.where(tile < HG_CTX_TILES, tile // HG_TILES_PER_CTX, N_CTX_SEQ), 0, 0, 0)
        return pl.BlockSpec((1, HG_HEADS, HG_DK, HG_DK), index)

    def h0_spec(direction):
        return pl.BlockSpec((1, 1, HG_HEADS, HG_DK, HG_DK),
                            lambda i: (_cond_row(_hg_tile(direction, i), HG_C), direction, 0, 0, 0))

    fin_shape = jax.ShapeDtypeStruct((N_CTX_SEQ + 1, HG_HEADS, HG_DK, HG_DK), F32)
    return pl.pallas_call(
        functools.partial(_hg_kernel, layer=layer),
        out_shape=(jax.ShapeDtypeStruct((N_TOK, D_BRANCH), F32), jax.ShapeDtypeStruct((N_TOK, D_BRANCH), F32),
                   fin_shape, fin_shape),
        grid=(HG_TILES,),
        in_specs=[
            tok_spec(0, HG_Q_COL), tok_spec(0, 0), tok_spec(0, HG_V_COL),
            tok_spec(1, HG_Q_COL), tok_spec(1, 1), tok_spec(1, HG_V_COL),
            pl.BlockSpec((DEPTH, 2, 1, D_BRANCH), lambda i: (0, 0, 0, 0)),
            pl.BlockSpec((2, HG_E_ROWS, HG_C), lambda i: (0, 0, 0)),
            pl.BlockSpec((2, HG_LEVELS, HG_C, HG_C), lambda i: (0, 0, 0, 0)),
            h0_spec(0), h0_spec(1),
        ],
        out_specs=(
            pl.BlockSpec((HG_C, D_BRANCH), lambda i: (_hg_tile(0, i), 0)),
            pl.BlockSpec((HG_C, D_BRANCH), lambda i: (_hg_tile(1, i), 0)),
            fin_spec(0), fin_spec(1),
        ),
        scratch_shapes=[pltpu.VMEM((2, HG_HEADS, HG_DK, HG_DK), F32)],
        compiler_params=_cparams(("arbitrary",)),
        name="hgrn",
    )(proj, fz, proj, proj, fz, proj, hg_lb.reshape(DEPTH, 2, 1, D_BRANCH), ae, mask, h0t, h0t)


MRG_TM = 256


def _gelu_tanh(x):
    return 0.5 * x * (1.0 + jnp.tanh(0.7978845608028654 * (x + 0.044715 * (x * x * x))))


def _merge_kernel(scb_ref, scc_ref, sch_ref, hgg_ref, pool_ref, g0_ref, g1_ref, g2_ref, g3_ref,
                  yrow_ref, of_ref, ob_ref, x_ref, mod_ref, conv_ref, gluw_ref, glub_ref, ng_ref,
                  poolw_ref, pscale_ref, wbr_ref, wout_ref, lng_ref, lnb_ref, o_ref,
                  ys5_sc, gluw_sc, poolw_sc, wbr_sc, wout_sc):
    tile = pl.program_id(0)

    @pl.when(tile == 0)
    def _():
        gluw_sc[...] = gluw_ref[0].astype(BF16)
        poolw_sc[...] = poolw_ref[0].astype(BF16)
        wbr_sc[...] = wbr_ref[0].astype(BF16)
        wout_sc[...] = wout_ref[0].astype(BF16)

    line = jnp.where(tile * MRG_TM < N_CTX_TOK, CTX_LEN, GRID_W)
    pos = lax.broadcasted_iota(jnp.int32, (MRG_TM, 1), 0) & (line - 1)

    def shifted(val, k):
        rolled = pltpu.roll(val, k % MRG_TM, axis=0)
        ok = (pos >= k) if k > 0 else (pos < line + k)
        return jnp.where(ok, rolled, 0.0)

    m = scc_ref[...].astype(F32) * sch_ref[...].astype(F32)
    conv = conv_ref[0, 0:1, :] * shifted(m, 1) + conv_ref[0, 1:2, :] * m + conv_ref[0, 2:3, :] * shifted(m, -1)
    ya = scb_ref[...].astype(F32) * conv

    for b in range(S5_BLK):
        for t in range(S5_T):
            ys5_sc[b, pl.ds(t, MRG_TM // S5_T, stride=S5_T), :] = yrow_ref[b, :, t * LANES:(t + 1) * LANES]
    z = _gelu_tanh(jnp.concatenate([ys5_sc[b] for b in range(S5_BLK)], axis=-1))
    yb = z * jax.nn.sigmoid(_dot(z.astype(BF16), gluw_sc[...]) + glub_ref[0])

    o = of_ref[...] + ob_ref[...]
    normed = []
    for h in range(HG_HEADS):
        oh = o[:, h * HG_DK:(h + 1) * HG_DK]
        ms = jnp.mean(oh * oh, axis=-1, keepdims=True)
        normed.append(oh * lax.rsqrt(ms + LN_EPS) * ng_ref[0])
    yc = jnp.concatenate(normed, axis=-1) * _silu(hgg_ref[...].astype(F32))

    pu = pool_ref[...].astype(F32)
    posf = pos.astype(F32)
    linef = line.astype(F32)
    pooled = []
    for gi, w in enumerate(POOL_WINDOWS):
        vg = pu[:, gi * POOL_GROUP:(gi + 1) * POOL_GROUP]
        back, fwd, span = vg, vg, 1
        while 2 * span <= w // 2:
            back = back + shifted(back, span)
            fwd = fwd + shifted(fwd, -span)
            span *= 2
        s = shifted(back, 1) + fwd
        cnt = jnp.minimum(posf + w // 2, linef) - jnp.maximum(posf - w // 2, 0.0)
        pg = s / cnt - vg
        pooled.append(_dot(pg.astype(BF16), poolw_sc[gi]))
    yd = jnp.concatenate(pooled, axis=-1) * pscale_ref[0]

    merged = jnp.zeros((MRG_TM, D_MODEL), F32)
    for k, (br, gate_ref) in enumerate(((ya, g0_ref), (yb, g1_ref), (yc, g2_ref), (yd, g3_ref))):
        merged = merged + jax.nn.sigmoid(gate_ref[...].astype(F32)) * _dot(br.astype(BF16), wbr_sc[k])
    mix = _dot(merged.astype(BF16), wout_sc[...])
    y = ALPHA * x_ref[...] + mod_ref[0, 0, 5:6, :] * mix
    o_ref[...] = _layer_norm(y, lng_ref[0, 0], lnb_ref[0, 0])


def _merge(proj, y_rows, o_f, o_b, x, mod, conv_w, glu_w, glu_b, norm_g, pool_w, pool_scale, w_branch, w_out,
           ln_g, ln_b, layer):
    def part(col):
        return pl.BlockSpec((MRG_TM, D_BRANCH), lambda i: (i, col))

    def gate(k):
        return pl.BlockSpec((MRG_TM, D_MODEL), lambda i: (i, N_IN_PARTS * D_BRANCH // D_MODEL + k))

    def per_layer(*shape, single=False):
        mode = dict(pipeline_mode=pl.Buffered(1)) if single else {}
        return pl.BlockSpec((1,) + shape, lambda i: (layer,) + (0,) * len(shape), **mode)

    return pl.pallas_call(
        _merge_kernel,
        out_shape=jax.ShapeDtypeStruct((N_TOK, D_MODEL), F32),
        grid=(N_TOK // MRG_TM,),
        in_specs=[
            part(0), part(1), part(2), part(8), part(9), gate(0), gate(1), gate(2), gate(3),
            pl.BlockSpec((S5_BLK, MRG_TM // S5_T, S5_CW), lambda i: (0, i, 0)),
            pl.BlockSpec((MRG_TM, D_BRANCH), lambda i: (i, 0)),
            pl.BlockSpec((MRG_TM, D_BRANCH), lambda i: (i, 0)),
            pl.BlockSpec((MRG_TM, D_MODEL), lambda i: (i, 0)),
            _mod_spec(layer, MRG_TM),
            per_layer(3, D_BRANCH), per_layer(D_BRANCH, D_BRANCH, single=True), per_layer(1, D_BRANCH),
            per_layer(1, HG_DK), per_layer(len(POOL_WINDOWS), POOL_GROUP, POOL_GROUP, single=True),
            per_layer(1, D_BRANCH), per_layer(N_BRANCH, D_BRANCH, D_MODEL, single=True),
            per_layer(D_MODEL, D_MODEL, single=True), _ln_spec(layer, 1), _ln_spec(layer, 1),
        ],
        out_specs=pl.BlockSpec((MRG_TM, D_MODEL), lambda i: (i, 0)),
        scratch_shapes=[
            pltpu.VMEM((S5_BLK, MRG_TM, LANES), F32),
            pltpu.VMEM((D_BRANCH, D_BRANCH), BF16),
            pltpu.VMEM((len(POOL_WINDOWS), POOL_GROUP, POOL_GROUP), BF16),
            pltpu.VMEM((N_BRANCH, D_BRANCH, D_MODEL), BF16),
            pltpu.VMEM((D_MODEL, D_MODEL), BF16),
        ],
        compiler_params=_cparams(("arbitrary",)),
        name="merge",
    )(proj, proj, proj, proj, proj, proj, proj, proj, proj, y_rows, o_f, o_b, x, mod,
      conv_w, glu_w, glu_b.reshape(DEPTH, 1, D_BRANCH), norm_g.reshape(DEPTH, 1, HG_DK), pool_w,
      pool_scale.reshape(DEPTH, 1, D_BRANCH), w_branch, w_out, ln_g, ln_b)


def _grid_pos_embedding():
    rows = LAT_LEN // GRID_W
    r = jnp.repeat(jnp.arange(rows, dtype=F32), GRID_W)
    col = jnp.tile(jnp.arange(GRID_W, dtype=F32), rows)
    quarter = D_MODEL // 4
    omega = POS_BASE ** (-jnp.arange(quarter, dtype=F32) / quarter)
    ar = r[:, None] * omega
    ac = col[:, None] * omega
    return jnp.concatenate([jnp.sin(ar), jnp.cos(ar), jnp.sin(ac), jnp.cos(ac)], -1)


def kernel(x_prompt, x_sample, state_s5_re, state_s5_im, state_hgrn, c, c_ctx, w_ada, b_ada, ln_g, ln_b,
           ffn_w1, ffn_w3, ffn_w2, w_in, sc_conv, s5_lam_re, s5_lam_im, s5_log_dt, s5_b_re, s5_b_im,
           s5_c_re, s5_c_im, s5_d, s5_glu_w, s5_glu_b, hg_lb, hg_norm_g, pool_w, pool_scale, w_branch, w_out):
    xs = x_sample + _grid_pos_embedding()[None]
    x = jnp.concatenate([x_prompt.reshape(N_CTX_TOK, D_MODEL), xs.reshape(N_LAT_TOK, D_MODEL)], axis=0)

    cond = jnp.zeros((COND_PAD, D_MODEL), F32).at[0].set(c_ctx).at[1:N_COND].set(c)
    mod = _ada(cond, w_ada, b_ada).reshape(DEPTH, COND_PAD, N_SUB * 3, D_MODEL)
    ln_g4 = ln_g.reshape(DEPTH, N_SUB, 1, D_MODEL)
    ln_b4 = ln_b.reshape(DEPTH, N_SUB, 1, D_MODEL)

    s5_m, s5_win, s5_wout, s5_pow = _s5_tables(*_s5_param_layout(
        s5_lam_re, s5_lam_im, s5_log_dt, s5_b_re, s5_b_im, s5_c_re, s5_c_im, s5_d))

    ffn_w1, ffn_w3 = _cast_bf16(ffn_w1, ffn_w3)
    (ffn_w2,) = _cast_bf16(ffn_w2)

    ae_np, mask_np = _hg_constants()
    ae = jnp.asarray(ae_np, BF16)
    mask = jnp.asarray(mask_np, F32)

    fin_s5, fin_hg = [], []
    for l in range(DEPTH):
        x = _ffn(x, mod, ffn_w1, ffn_w3, ffn_w2, ln_g4, ln_b4, l, 0)
        proj, fz, u_rows = _inproj(x, mod, w_in, l)

        h0 = jnp.concatenate([
            state_s5_re[:, l].reshape(N_LAT_SEQ, 2, S5_BLK, S5_SW),
            state_s5_im[:, l].reshape(N_LAT_SEQ, 2, S5_BLK, S5_SW)], axis=-1).transpose(1, 2, 0, 3)
        y_rows, fin = _s5_main(u_rows, s5_m, s5_win, s5_wout, s5_pow, h0, l)
        fin_s5.append(fin)

        h0t = jnp.concatenate([jnp.zeros((1, 2, HG_HEADS, HG_DK, HG_DK), F32),
                               jnp.swapaxes(state_hgrn[:, l], -1, -2)], axis=0)
        o_f, o_b, fin_f, fin_b = _hgrn(proj, fz, hg_lb, ae, mask, h0t, l)
        fin_hg.append(jnp.stack([fin_f[:N_CTX_SEQ], fin_b[:N_CTX_SEQ]], axis=1))

        x = _merge(proj, y_rows, o_f, o_b, x, mod, sc_conv, s5_glu_w, s5_glu_b, hg_norm_g,
                   pool_w, pool_scale, w_branch, w_out, ln_g4, ln_b4, l)
        x = _ffn(x, mod, ffn_w1, ffn_w3, ffn_w2, ln_g4, ln_b4, l, 2)

    y_prompt = x[:N_CTX_TOK].reshape(N_CTX_SEQ, CTX_LEN, D_MODEL)
    y_sample = x[N_CTX_TOK:].reshape(N_LAT_SEQ, LAT_LEN, D_MODEL)
    fin = jnp.stack(fin_s5)
    fin = fin.reshape(DEPTH, 2, S5_BLK, N_CTX_SEQ, 2, S5_GPB, S5_STATE)
    fin = fin.transpose(4, 3, 0, 1, 2, 5, 6).reshape(2, N_CTX_SEQ, DEPTH, 2, S5_GROUPS, S5_STATE)
    new_hgrn = jnp.swapaxes(jnp.stack(fin_hg, axis=1), -1, -2)
    return y_prompt, y_sample, fin[0], fin[1], new_hgrn
```

```python
import functools

import numpy as np
import jax
import jax.numpy as jnp
from jax import lax
from jax.experimental import pallas as pl
from jax.experimental.pallas import tpu as pltpu

F32 = jnp.float32
BF16 = jnp.bfloat16

D_MODEL = 1024
N_CTX_SEQ = 16
CTX_LEN = 256
DEPTH = 2
N_LAT_SEQ = 2
LAT_LEN = 4096
GRID_W = 64
D_BRANCH = 512
N_BRANCH = 4
S5_GROUPS = 32
S5_GROUP = 16
S5_STATE = 64
HG_HEADS = 4
HG_DK = 128
POOL_WINDOWS = (2, 4, 8, 16)
POOL_GROUP = 128
D_FF = 2816
N_SUB = 3
N_IN_PARTS = 10
IN_WIDTH = N_IN_PARTS * D_BRANCH + N_BRANCH * D_MODEL
ADA_WIDTH = N_SUB * 3 * D_MODEL
ALPHA = (2 * DEPTH) ** 0.25
LN_EPS = 1e-5
POS_BASE = 10000.0

N_CTX_TOK = N_CTX_SEQ * CTX_LEN
N_LAT_TOK = N_LAT_SEQ * LAT_LEN
N_TOK = N_CTX_TOK + N_LAT_TOK
N_COND = 1 + N_LAT_SEQ
COND_PAD = 8

LANES = 128
SUBLANES = 8
VMEM_LIMIT = 56 * 1024 * 1024

S5_T = 8
S5_BLK = 4
S5_GPB = S5_GROUPS // S5_BLK
S5_SW = S5_GPB * S5_STATE
S5_CW = S5_T * LANES
CTX_CHUNKS = CTX_LEN // S5_T
LAT_CHUNKS = LAT_LEN // S5_T
S5_ROWS_CTX = N_CTX_TOK // S5_T
S5_ROWS_LAT = N_LAT_TOK // S5_T
S5_ROWS = S5_ROWS_CTX + S5_ROWS_LAT
S5_LAT_GROUPS = LAT_CHUNKS // SUBLANES
S5_POW_ROWS = 24
S5_U_COL = 3

HG_C = 128
HG_LEVELS = 7
HG_TILES = N_TOK // HG_C
HG_CTX_TILES = N_CTX_TOK // HG_C
HG_TILES_PER_CTX = CTX_LEN // HG_C
HG_TILES_PER_LAT = LAT_LEN // HG_C
HG_E_ROWS = (HG_LEVELS + 1) * HG_C + 8


def _cparams(sem):
    return pltpu.CompilerParams(dimension_semantics=sem, vmem_limit_bytes=VMEM_LIMIT)


def _cond_row(tile, tile_tokens):
    tok = tile * tile_tokens
    return jnp.where(tok < N_CTX_TOK, 0, 1 + (tok - N_CTX_TOK) // LAT_LEN)


def _dot(a, b):
    return jnp.dot(a, b, preferred_element_type=F32)


def _dot_nt(a, b):
    return lax.dot_general(a, b, (((1,), (1,)), ((), ())), preferred_element_type=F32)


def _dot_tn(a, b):
    return lax.dot_general(a, b, (((0,), (0,)), ((), ())), preferred_element_type=F32)


def _split3(x):
    h1 = x.astype(BF16)
    r1 = x - h1.astype(F32)
    h2 = r1.astype(BF16)
    h3 = (r1 - h2.astype(F32)).astype(BF16)
    return h1, h2, h3


def _dot_nt_hi(a3, b3):
    acc = None
    for x in range(3):
        for y in range(3 - x):
            term = _dot_nt(a3[x], b3[y])
            acc = term if acc is None else acc + term
    return acc


def _silu(x):
    return x * jax.nn.sigmoid(x)


def _layer_norm(y, g, b):
    mu = jnp.mean(y, axis=-1, keepdims=True)
    yc = y - mu
    var = jnp.mean(yc * yc, axis=-1, keepdims=True)
    return yc * lax.rsqrt(var + LN_EPS) * g + b


def _mod_spec(layer, tile_tokens):
    return pl.BlockSpec((1, 1, N_SUB * 3, D_MODEL),
                        lambda i, *_: (layer, _cond_row(i, tile_tokens), 0, 0))


def _ln_spec(layer, sub):
    return pl.BlockSpec((1, 1, 1, D_MODEL), lambda *_: (layer, sub, 0, 0))


ADA_TN = 1152


def _ada_kernel(c_ref, w_ref, b_ref, o_ref):
    s = _silu(c_ref[...]).astype(BF16)
    o_ref[0] = _dot(s, w_ref[0].astype(BF16)) + b_ref[0]


def _ada(cond, w_ada, b_ada):
    return pl.pallas_call(
        _ada_kernel,
        out_shape=jax.ShapeDtypeStruct((DEPTH, COND_PAD, ADA_WIDTH), F32),
        grid=(DEPTH, ADA_WIDTH // ADA_TN),
        in_specs=[
            pl.BlockSpec((COND_PAD, D_MODEL), lambda l, j: (0, 0)),
            pl.BlockSpec((1, D_MODEL, ADA_TN), lambda l, j: (l, 0, j)),
            pl.BlockSpec((1, 1, ADA_TN), lambda l, j: (l, 0, j)),
        ],
        out_specs=pl.BlockSpec((1, COND_PAD, ADA_TN), lambda l, j: (l, 0, j)),
        compiler_params=_cparams(("parallel", "parallel")),
        name="ada",
    )(cond, w_ada, b_ada.reshape(DEPTH, 1, ADA_WIDTH))


CAST_ROWS = 512


def _cast_kernel(*refs):
    n = len(refs) // 2
    for src, dst in zip(refs[:n], refs[n:]):
        dst[...] = src[...].astype(BF16)


def _cast_bf16(*ws):
    shape = ws[0].shape
    rows, cols = int(np.prod(shape[:-1])), shape[-1]
    spec = pl.BlockSpec((CAST_ROWS, cols), lambda i: (i, 0))
    outs = pl.pallas_call(
        _cast_kernel,
        out_shape=tuple(jax.ShapeDtypeStruct((rows, cols), BF16) for _ in ws),
        grid=(rows // CAST_ROWS,),
        in_specs=[spec] * len(ws),
        out_specs=tuple([spec] * len(ws)),
        compiler_params=_cparams(("parallel",)),
        name="cast_bf16",
    )(*(w.reshape(rows, cols) for w in ws))
    return tuple(o.reshape(shape) for o in outs)


FFN_TM = 512
FFN_TC = 1408


def _ffn_kernel(x_ref, mod_ref, w1_ref, w3_ref, w2_ref, g_ref, b_ref, o_ref, act_sc, *, sub):
    x = x_ref[...]
    shift = mod_ref[0, 0, 3 * sub:3 * sub + 1, :]
    scale = mod_ref[0, 0, 3 * sub + 1:3 * sub + 2, :]
    gate = mod_ref[0, 0, 3 * sub + 2:3 * sub + 3, :]
    h = (x * (1.0 + scale) + shift).astype(BF16)
    for c in range(D_FF // FFN_TC):
        cols = slice(c * FFN_TC, (c + 1) * FFN_TC)
        a = _dot(h, w1_ref[0, 0, :, cols])
        b = _dot(h, w3_ref[0, 0, :, cols])
        act_sc[:, cols] = (_silu(a) * b).astype(BF16)
    f = _dot(act_sc[...], w2_ref[0, 0])
    y = ALPHA * x + gate * (0.5 * f)
    o_ref[...] = _layer_norm(y, g_ref[0, 0], b_ref[0, 0])


def _ffn(x, mod, w1, w3, w2, ln_g, ln_b, layer, sub):
    which = sub // 2

    def resident(*shape):
        return pl.BlockSpec((1, 1) + shape, lambda i: (layer, which, 0, 0), pipeline_mode=pl.Buffered(1))

    return pl.pallas_call(
        functools.partial(_ffn_kernel, sub=sub),
        out_shape=jax.ShapeDtypeStruct((N_TOK, D_MODEL), F32),
        grid=(N_TOK // FFN_TM,),
        in_specs=[
            pl.BlockSpec((FFN_TM, D_MODEL), lambda i: (i, 0)),
            _mod_spec(layer, FFN_TM),
            resident(D_MODEL, D_FF), resident(D_MODEL, D_FF), resident(D_FF, D_MODEL),
            _ln_spec(layer, sub), _ln_spec(layer, sub),
        ],
        out_specs=pl.BlockSpec((FFN_TM, D_MODEL), lambda i: (i, 0)),
        scratch_shapes=[pltpu.VMEM((FFN_TM, D_FF), BF16)],
        compiler_params=_cparams(("parallel",)),
        name=f"ffn{sub}",
    )(x, mod, w1, w3, w2, ln_g, ln_b)


INP_TM = 2048
INP_TN = D_BRANCH
HG_Q_COL, HG_F_COL, HG_V_COL = 4, 5, 7


def _inproj_kernel(x_ref, mod_ref, w_ref, o_ref, fz_ref, u_ref, h_sc, blk_sc):
    j = pl.program_id(1)

    @pl.when(j == 0)
    def _():
        shift = mod_ref[0, 0, 3:4, :]
        scale = mod_ref[0, 0, 4:5, :]
        h_sc[...] = (x_ref[...] * (1.0 + scale) + shift).astype(BF16)

    res = _dot(h_sc[...], w_ref[0].astype(BF16))
    o_ref[...] = res.astype(BF16)

    @pl.when((j == HG_F_COL) | (j == HG_F_COL + 1))
    def _():
        fz_ref[...] = res

    @pl.when(j == S5_U_COL)
    def _():
        for b in range(S5_BLK):
            blk_sc[b] = res[:, b * LANES:(b + 1) * LANES]
            for t in range(S5_T):
                rows = blk_sc[b, pl.ds(t, INP_TM // S5_T, stride=S5_T), :]
                u_ref[b, :, t * LANES:(t + 1) * LANES] = rows.astype(BF16)


def _inproj(x, mod, w_in, layer):
    return pl.pallas_call(
        _inproj_kernel,
        out_shape=(jax.ShapeDtypeStruct((N_TOK, IN_WIDTH), BF16),
                   jax.ShapeDtypeStruct((N_TOK, 2 * D_BRANCH), F32),
                   jax.ShapeDtypeStruct((S5_BLK, S5_ROWS, S5_CW), BF16)),
        grid=(N_TOK // INP_TM, IN_WIDTH // INP_TN),
        in_specs=[
            pl.BlockSpec((INP_TM, D_MODEL), lambda i, j: (i, 0), pipeline_mode=pl.Buffered(1)),
            _mod_spec(layer, INP_TM),
            pl.BlockSpec((1, D_MODEL, INP_TN), lambda i, j: (layer, 0, j)),
        ],
        out_specs=(pl.BlockSpec((INP_TM, INP_TN), lambda i, j: (i, j)),
                   pl.BlockSpec((INP_TM, D_BRANCH), lambda i, j: (i, jnp.clip(j - HG_F_COL, 0, 1))),
                   pl.BlockSpec((S5_BLK, INP_TM // S5_T, S5_CW), lambda i, j: (0, i, 0))),
        scratch_shapes=[pltpu.VMEM((INP_TM, D_MODEL), BF16), pltpu.VMEM((S5_BLK, INP_TM, LANES), F32)],
        compiler_params=_cparams(("parallel", "arbitrary")),
        name="inproj",
    )(x, mod, w_in)


def _s5_tab_kernel(lam_ref, bre_ref, bim_ref, cre_ref, cim_ref, dsk_ref, m_ref, win_ref, wout_ref, pow_ref):
    d = pl.program_id(1)
    fwd = d == 0
    lam_re = lam_ref[0, 0, 0, 0:1, :]
    lam_im = lam_ref[0, 0, 0, 1:2, :]
    dt = jnp.exp(lam_ref[0, 0, 0, 2:3, :])

    def apow(j):
        mag = jnp.exp(lam_re * dt * float(j))
        ang = lam_im * dt * float(j)
        return mag * jnp.cos(ang), mag * jnp.sin(ang)

    a_re, a_im = apow(1)
    den = lam_re * lam_re + lam_im * lam_im
    num_re = a_re - 1.0
    coef_re = (num_re * lam_re + a_im * lam_im) / den
    coef_im = (a_im * lam_re - num_re * lam_im) / den
    b_re = bre_ref[0, 0, 0]
    b_im = bim_ref[0, 0, 0]
    bb_re = coef_re * b_re - coef_im * b_im
    bb_im = coef_re * b_im + coef_im * b_re
    bbr = _split3(bb_re)
    bbi = _split3(bb_im)
    c_re = cre_ref[0, 0, 0]
    c_im = cim_ref[0, 0, 0]

    kmat = []
    for j in range(S5_T + 1):
        aj_re, aj_im = apow(j)
        pc_re = c_re * aj_re - c_im * aj_im
        pc_im = c_re * aj_im + c_im * aj_re
        if j >= 1:
            r = pl.multiple_of(jnp.where(fwd, j - 1, S5_T - j) * LANES, LANES)
            wout_ref[0, 0, 0, pl.ds(r, LANES), :] = jnp.concatenate([pc_re, -pc_im], axis=1).astype(BF16)
        if j < S5_T:
            pb_re = bb_re * aj_re - bb_im * aj_im
            pb_im = bb_re * aj_im + bb_im * aj_re
            r = pl.multiple_of(jnp.where(fwd, S5_T - 1 - j, j) * LANES, LANES)
            win_ref[0, 0, 0, pl.ds(r, LANES), :] = jnp.concatenate([pb_re, pb_im], axis=1).astype(BF16)
            kmat.append(_dot_nt_hi(bbr, _split3(pc_re)) - _dot_nt_hi(bbi, _split3(pc_im)))

    ri = lax.broadcasted_iota(jnp.int32, (LANES, LANES), 0)
    ci = lax.broadcasted_iota(jnp.int32, (LANES, LANES), 1)
    keep_upper = jnp.where(fwd, 1.0, 0.0)
    kmat[0] = kmat[0] + jnp.where(ri == ci, dsk_ref[0, 0], 0.0) * keep_upper
    for s in range(S5_T):
        for t in range(S5_T):
            tile = kmat[abs(t - s)]
            if t > s:
                tile = tile * keep_upper
            elif t < s:
                tile = tile * (1.0 - keep_upper)
            m_ref[0, 0, 0, s * LANES:(s + 1) * LANES, t * LANES:(t + 1) * LANES] = tile.astype(BF16)

    pow_ref[...] = jnp.zeros_like(pow_ref)
    for i in range(SUBLANES):
        for row, j in ((i, S5_T * (i + 1)), (SUBLANES + i, S5_T * (SUBLANES - i))):
            p_re, p_im = apow(j)
            pow_ref[0, 0, 0, 0, row:row + 1, :] = p_re
            pow_ref[0, 0, 0, 1, row:row + 1, :] = p_im
    p_re, p_im = apow(S5_T * 2 * SUBLANES)
    pow_ref[0, 0, 0, 0, 2 * SUBLANES:2 * SUBLANES + 1, :] = p_re
    pow_ref[0, 0, 0, 1, 2 * SUBLANES:2 * SUBLANES + 1, :] = p_im


def _s5_tables(lam, b_re, b_im, c_re, c_im, d_skip):
    lead = (DEPTH, 2, S5_BLK)

    def spec(*tail):
        return pl.BlockSpec((1, 1, 1) + tail, lambda l, d, b: (l, d, b) + (0,) * len(tail))

    return pl.pallas_call(
        _s5_tab_kernel,
        out_shape=(
            jax.ShapeDtypeStruct(lead + (S5_CW, S5_CW), BF16),
            jax.ShapeDtypeStruct(lead + (S5_CW, 2 * S5_SW), BF16),
            jax.ShapeDtypeStruct(lead + (S5_CW, 2 * S5_SW), BF16),
            jax.ShapeDtypeStruct(lead + (2, S5_POW_ROWS, S5_SW), F32),
        ),
        grid=lead,
        in_specs=[spec(3, S5_SW)] + [spec(LANES, S5_SW)] * 4
        + [pl.BlockSpec((1, 1, 1, LANES), lambda l, d, b: (l, b, 0, 0))],
        out_specs=(spec(S5_CW, S5_CW), spec(S5_CW, 2 * S5_SW), spec(S5_CW, 2 * S5_SW),
                   spec(2, S5_POW_ROWS, S5_SW)),
        compiler_params=_cparams(("parallel", "parallel", "parallel")),
        name="s5_tables",
    )(lam, b_re, b_im, c_re, c_im, d_skip)


def _s5_param_layout(s5_lam_re, s5_lam_im, s5_log_dt, s5_b_re, s5_b_im, s5_c_re, s5_c_im, s5_d):
    eye = jnp.eye(S5_GPB, dtype=F32)
    lam = jnp.stack([s5_lam_re, s5_lam_im, jnp.broadcast_to(s5_log_dt[..., None], s5_lam_re.shape)], axis=2)
    lam = lam.reshape(DEPTH, 2, 3, S5_BLK, S5_SW).transpose(0, 1, 3, 2, 4)

    def bdiag_b(b):
        b = b.reshape(DEPTH, 2, S5_BLK, S5_GPB, S5_STATE, S5_GROUP)
        return jnp.einsum('ldbgph,gq->ldbghqp', b, eye).reshape(DEPTH, 2, S5_BLK, LANES, S5_SW)

    def bdiag_c(c):
        c = c.reshape(DEPTH, 2, S5_BLK, S5_GPB, S5_GROUP, S5_STATE)
        return jnp.einsum('ldbghp,gq->ldbghqp', c, eye).reshape(DEPTH, 2, S5_BLK, LANES, S5_SW)

    return (lam, bdiag_b(s5_b_re), bdiag_b(s5_b_im), bdiag_c(s5_c_re), bdiag_c(s5_c_im),
            s5_d.reshape(DEPTH, S5_BLK, 1, LANES))


def _s5_kernel(u_ref, m_ref, win_ref, wout_ref, pow_ref, h0_ref, y_ref, fin_ref, x_sc):
    d = pl.program_id(1)
    u = u_ref[0]
    x_sc[...] = _dot(u, win_ref[0, 0, 0])
    p_re = pow_ref[0, 0, 0, 0]
    p_im = pow_ref[0, 0, 0, 1]
    re = slice(0, S5_SW)
    im = slice(S5_SW, 2 * S5_SW)

    def madd(t_re, t_im, w_re, w_im, s_re, s_im):
        return t_re + w_re * s_re - w_im * s_im, t_im + w_re * s_im + w_im * s_re

    def shift_rows(val, k, n, idx, reverse):
        if reverse:
            return jnp.where(idx < n - k, pltpu.roll(val, val.shape[0] - k, axis=0), 0.0)
        return jnp.where(idx >= k, pltpu.roll(val, k, axis=0), 0.0)

    def pow_row(k):
        row = k - 1 if k <= SUBLANES else 2 * SUBLANES
        return p_re[row:row + 1], p_im[row:row + 1]

    def ctx_scan(reverse):
        s_re = x_sc[0:S5_ROWS_CTX, re]
        s_im = x_sc[0:S5_ROWS_CTX, im]
        c = lax.broadcasted_iota(jnp.int32, (S5_ROWS_CTX, 1), 0) & (CTX_CHUNKS - 1)
        k = 1
        while k < CTX_CHUNKS:
            w_re, w_im = pow_row(k)
            s_re, s_im = madd(s_re, s_im, w_re, w_im,
                              shift_rows(s_re, k, CTX_CHUNKS, c, reverse), shift_rows(s_im, k, CTX_CHUNKS, c, reverse))
            k *= 2
        last = 0 if reverse else CTX_CHUNKS - 1
        pick = (lax.broadcasted_iota(jnp.int32, (N_CTX_SEQ, S5_ROWS_CTX), 1)
                == lax.broadcasted_iota(jnp.int32, (N_CTX_SEQ, S5_ROWS_CTX), 0) * CTX_CHUNKS + last)
        pick = jnp.where(pick, 1.0, 0.0).astype(BF16)
        fin_ref[0, 0, :, re] = sum(_dot(pick, piece) for piece in _split3(s_re))
        fin_ref[0, 0, :, im] = sum(_dot(pick, piece) for piece in _split3(s_im))
        x_sc[0:S5_ROWS_CTX, re] = shift_rows(s_re, 1, CTX_CHUNKS, c, reverse)
        x_sc[0:S5_ROWS_CTX, im] = shift_rows(s_im, 1, CTX_CHUNKS, c, reverse)

    def lat_group(k, carry, reverse):
        g = (S5_LAT_GROUPS - 1 - k) if reverse else k
        r8 = lax.broadcasted_iota(jnp.int32, (SUBLANES, 1), 0)
        tab = slice(SUBLANES, 2 * SUBLANES) if reverse else slice(0, SUBLANES)
        out = []
        for q in range(N_LAT_SEQ):
            base = pl.multiple_of(S5_ROWS_CTX + q * LAT_CHUNKS + g * SUBLANES, SUBLANES)
            s_re = x_sc[pl.ds(base, SUBLANES), re]
            s_im = x_sc[pl.ds(base, SUBLANES), im]
            j = 1
            while j < SUBLANES:
                w_re, w_im = pow_row(j)
                s_re, s_im = madd(s_re, s_im, w_re, w_im,
                                  shift_rows(s_re, j, SUBLANES, r8, reverse), shift_rows(s_im, j, SUBLANES, r8, reverse))
                j *= 2
            c_re, c_im = carry[2 * q], carry[2 * q + 1]
            s_re, s_im = madd(s_re, s_im, p_re[tab], p_im[tab], c_re, c_im)
            edge = (r8 == SUBLANES - 1) if reverse else (r8 == 0)
            x_sc[pl.ds(base, SUBLANES), re] = jnp.where(edge, c_re, shift_rows(s_re, 1, SUBLANES, r8, reverse))
            x_sc[pl.ds(base, SUBLANES), im] = jnp.where(edge, c_im, shift_rows(s_im, 1, SUBLANES, r8, reverse))
            end = slice(0, 1) if reverse else slice(SUBLANES - 1, SUBLANES)
            out += [s_re[end], s_im[end]]
        return tuple(out)

    def run(reverse):
        ctx_scan(reverse)
        h0 = h0_ref[0, 0]
        init = []
        for q in range(N_LAT_SEQ):
            init += [h0[q:q + 1, re], h0[q:q + 1, im]]
        lax.fori_loop(0, S5_LAT_GROUPS, functools.partial(lat_group, reverse=reverse), tuple(init))

    @pl.when(d == 0)
    def _():
        run(False)

    @pl.when(d == 1)
    def _():
        run(True)

    y = _dot(u, m_ref[0, 0, 0]) + _dot_nt(x_sc[...].astype(BF16), wout_ref[0, 0, 0])

    @pl.when(d == 0)
    def _():
        y_ref[0] = y

    @pl.when(d == 1)
    def _():
        y_ref[0] += y


def _s5_main(u, m, w_in, w_out, pows, h0, layer):
    def tab(*tail):
        return pl.BlockSpec((1, 1, 1) + tail, lambda b, d: (layer, d, b) + (0,) * len(tail))

    return pl.pallas_call(
        _s5_kernel,
        out_shape=(jax.ShapeDtypeStruct((S5_BLK, S5_ROWS, S5_CW), F32),
                   jax.ShapeDtypeStruct((2, S5_BLK, N_CTX_SEQ, 2 * S5_SW), F32)),
        grid=(S5_BLK, 2),
        in_specs=[
            pl.BlockSpec((1, S5_ROWS, S5_CW), lambda b, d: (b, 0, 0)),
            tab(S5_CW, S5_CW), tab(S5_CW, 2 * S5_SW), tab(S5_CW, 2 * S5_SW), tab(2, S5_POW_ROWS, S5_SW),
            pl.BlockSpec((1, 1, N_LAT_SEQ, 2 * S5_SW), lambda b, d: (d, b, 0, 0)),
        ],
        out_specs=(
            pl.BlockSpec((1, S5_ROWS, S5_CW), lambda b, d: (b, 0, 0)),
            pl.BlockSpec((1, 1, N_CTX_SEQ, 2 * S5_SW), lambda b, d: (d, b, 0, 0)),
        ),
        scratch_shapes=[pltpu.VMEM((S5_ROWS, 2 * S5_SW), F32)],
        compiler_params=_cparams(("parallel", "arbitrary")),
        name="s5",
    )(u, m, w_in, w_out, pows, h0)


def _hg_constants():
    t = np.arange(HG_C)
    ae = np.zeros((2, HG_E_ROWS, HG_C), np.float32)
    mask = np.zeros((2, HG_LEVELS, HG_C, HG_C), np.float32)
    for lvl in range(HG_LEVELS):
        half = 1 << lvl
        pos = t % (2 * half)
        mid = t - pos + half - 1
        upper = pos >= half
        u = t[None, :]
        rows_upper = (u > mid[:, None]) & (u <= t[:, None])
        rows_lower = (u > t[:, None]) & (u <= mid[:, None])
        ae[0, lvl * HG_C:(lvl + 1) * HG_C] = np.where(upper[:, None], rows_upper, rows_lower)
        same = (t[:, None] // (2 * half)) == (t[None, :] // (2 * half))
        mask[0, lvl] = same & upper[:, None] & (~upper)[None, :]
    ae[0, HG_LEVELS * HG_C:(HG_LEVELS + 1) * HG_C] = t[None, :] <= t[:, None]
    ae[0, (HG_LEVELS + 1) * HG_C:] = 1.0
    ae[1] = ae[0][:, ::-1]
    ae[1, :(HG_LEVELS + 1) * HG_C] = ae[1, :(HG_LEVELS + 1) * HG_C].reshape(HG_LEVELS + 1, HG_C, HG_C)[:, ::-1].reshape(-1, HG_C)
    mask[1] = mask[0][:, ::-1, ::-1]
    ae2 = np.concatenate([ae, ae], axis=-1)
    mask2 = np.concatenate([mask, mask], axis=-1)
    pair_diag = np.kron(np.eye(2, dtype=np.float32), np.ones((HG_DK, HG_DK), np.float32))
    return ae2, mask2, pair_diag


def _hg_tile(direction, i):
    return i if direction == 0 else HG_TILES - 1 - i


HG_PAIR = 2 * HG_DK


def _pair_diag(x):
    zero = jnp.zeros((x.shape[0], HG_DK), x.dtype)
    return jnp.concatenate([jnp.concatenate([x[:, :HG_DK], zero], axis=1),
                            jnp.concatenate([zero, x[:, HG_DK:]], axis=1)], axis=0)


def _hg_direction(direction, q_ref, f_ref, v_ref, lb, ae_ref, mask_ref, pd_ref, o_ref, fin_ref, st_sc):
    f = lb + (1.0 - lb) * jax.nn.sigmoid(f_ref[...])
    logf = jnp.log(f)
    kk = 1.0 - f
    hi = logf.astype(BF16)
    lo = (logf - hi.astype(F32)).astype(BF16)
    ex = _dot(ae_ref[direction], jnp.concatenate([hi, lo], axis=0))
    dec = jnp.exp(ex[:(HG_LEVELS + 1) * HG_C])
    cum = ex[HG_LEVELS * HG_C:(HG_LEVELS + 1) * HG_C]
    tot = ex[(HG_LEVELS + 1) * HG_C:(HG_LEVELS + 1) * HG_C + 1]
    q = q_ref[...].astype(F32)
    v_bf = v_ref[...]
    v = v_bf.astype(F32)
    q_in = (q * dec[HG_LEVELS * HG_C:]).astype(BF16)
    k_tail = (kk * jnp.exp(tot - cum)).astype(BF16)
    dec_tot = jnp.exp(tot)
    qk = q * kk
    outs = []
    for p in range(HG_HEADS // 2):
        sl = slice(p * HG_PAIR, (p + 1) * HG_PAIR)
        scores = jnp.zeros((HG_C, HG_PAIR), F32)
        for lvl in range(HG_LEVELS):
            g = dec[lvl * HG_C:(lvl + 1) * HG_C, sl]
            scores = scores + mask_ref[direction, lvl] * _dot_nt(
                (q[:, sl] * g).astype(BF16), _pair_diag((kk[:, sl] * g).astype(BF16)))
        diag = jnp.concatenate(
            [jnp.broadcast_to(jnp.sum(qk[:, (2 * p + j) * HG_DK:(2 * p + j + 1) * HG_DK], axis=-1, keepdims=True),
                              (HG_C, HG_DK)) for j in range(2)], axis=-1)
        st = st_sc[direction, p]
        outs.append(_dot(scores.astype(BF16), _pair_diag(v_bf[:, sl])) + diag * v[:, sl]
                    + _dot_nt(q_in[:, sl], st.astype(BF16)))
        st_new = st * dec_tot[:, sl] + pd_ref[...] * _dot_tn(v_bf[:, sl], k_tail[:, sl])
        st_sc[direction, p] = st_new
        for j in range(2):
            fin_ref[0, 2 * p + j] = st_new[j * HG_DK:(j + 1) * HG_DK, j * HG_DK:(j + 1) * HG_DK]
    o_ref[...] = jnp.concatenate(outs, axis=-1)


def _hg_kernel(qf_ref, ff_ref, vf_ref, qb_ref, fb_ref, vb_ref, lb_ref, ae_ref, mask_ref, pd_ref, h0f_ref, h0b_ref,
               of_ref, ob_ref, finf_ref, finb_ref, st_sc, *, layer):
    i = pl.program_id(0)
    x = lb_ref[:, :, 0, :]
    e = jnp.exp(x - jnp.max(x, axis=0, keepdims=True))
    sm = e / jnp.sum(e, axis=0, keepdims=True)
    lb = jnp.sum(sm[1:layer + 1], axis=0) if layer > 0 else jnp.zeros((2, D_BRANCH), F32)
    for direction, h0_ref in ((0, h0f_ref), (1, h0b_ref)):
        tile = _hg_tile(direction, i)
        first_ctx = 0 if direction == 0 else HG_TILES_PER_CTX - 1
        first_lat = 0 if direction == 0 else HG_TILES_PER_LAT - 1
        is_start = jnp.where(tile < HG_CTX_TILES,
                             tile % HG_TILES_PER_CTX == first_ctx,
                             (tile - HG_CTX_TILES) % HG_TILES_PER_LAT == first_lat)

        @pl.when(is_start)
        def _(direction=direction, h0_ref=h0_ref):
            for p in range(HG_HEADS // 2):
                st_sc[direction, p] = _pair_diag(
                    jnp.concatenate([h0_ref[0, 0, 2 * p], h0_ref[0, 0, 2 * p + 1]], axis=1))

    _hg_direction(0, qf_ref, ff_ref, vf_ref, lb[0:1], ae_ref, mask_ref, pd_ref, of_ref, finf_ref, st_sc)
    _hg_direction(1, qb_ref, fb_ref, vb_ref, lb[1:2], ae_ref, mask_ref, pd_ref, ob_ref, finb_ref, st_sc)


def _hgrn(proj, fz, hg_lb, ae, mask, pair_diag, h0t, layer):
    def tok_spec(direction, col):
        return pl.BlockSpec((HG_C, D_BRANCH), lambda i: (_hg_tile(direction, i), col))

    def fin_spec(direction):
        def index(i):
            tile = _hg_tile(direction, i)
            return (jnp.where(tile < HG_CTX_TILES, tile // HG_TILES_PER_CTX, N_CTX_SEQ), 0, 0, 0)
        return pl.BlockSpec((1, HG_HEADS, HG_DK, HG_DK), index)

    def h0_spec(direction):
        return pl.BlockSpec((1, 1, HG_HEADS, HG_DK, HG_DK),
                            lambda i: (_cond_row(_hg_tile(direction, i), HG_C), direction, 0, 0, 0))

    fin_shape = jax.ShapeDtypeStruct((N_CTX_SEQ + 1, HG_HEADS, HG_DK, HG_DK), F32)
    return pl.pallas_call(
        functools.partial(_hg_kernel, layer=layer),
        out_shape=(jax.ShapeDtypeStruct((N_TOK, D_BRANCH), F32), jax.ShapeDtypeStruct((N_TOK, D_BRANCH), F32),
                   fin_shape, fin_shape),
        grid=(HG_TILES,),
        in_specs=[
            tok_spec(0, HG_Q_COL), tok_spec(0, 0), tok_spec(0, HG_V_COL),
            tok_spec(1, HG_Q_COL), tok_spec(1, 1), tok_spec(1, HG_V_COL),
            pl.BlockSpec((DEPTH, 2, 1, D_BRANCH), lambda i: (0, 0, 0, 0)),
            pl.BlockSpec((2, HG_E_ROWS, 2 * HG_C), lambda i: (0, 0, 0)),
            pl.BlockSpec((2, HG_LEVELS, HG_C, HG_PAIR), lambda i: (0, 0, 0, 0)),
            pl.BlockSpec((HG_PAIR, HG_PAIR), lambda i: (0, 0)),
            h0_spec(0), h0_spec(1),
        ],
        out_specs=(
            pl.BlockSpec((HG_C, D_BRANCH), lambda i: (_hg_tile(0, i), 0)),
            pl.BlockSpec((HG_C, D_BRANCH), lambda i: (_hg_tile(1, i), 0)),
            fin_spec(0), fin_spec(1),
        ),
        scratch_shapes=[pltpu.VMEM((2, HG_HEADS // 2, HG_PAIR, HG_PAIR), F32)],
        compiler_params=_cparams(("arbitrary",)),
        name="hgrn",
    )(proj, fz, proj, proj, fz, proj, hg_lb.reshape(DEPTH, 2, 1, D_BRANCH), ae, mask, pair_diag, h0t, h0t)


MRG_TM = 256


def _gelu_tanh(x):
    return 0.5 * x * (1.0 + jnp.tanh(0.7978845608028654 * (x + 0.044715 * (x * x * x))))


def _merge_kernel(scb_ref, scc_ref, sch_ref, hgg_ref, pool_ref, g0_ref, g1_ref, g2_ref, g3_ref,
                  yrow_ref, of_ref, ob_ref, x_ref, mod_ref, conv_ref, gluw_ref, glub_ref, ng_ref,
                  poolw_ref, pscale_ref, wbr_ref, wout_ref, lng_ref, lnb_ref, o_ref,
                  ys5_sc, gluw_sc, poolw_sc, wbr_sc, wout_sc):
    tile = pl.program_id(0)

    @pl.when(tile == 0)
    def _():
        gluw_sc[...] = gluw_ref[0].astype(BF16)
        poolw_sc[...] = poolw_ref[0].astype(BF16)
        wbr_sc[...] = wbr_ref[0].astype(BF16)
        wout_sc[...] = wout_ref[0].astype(BF16)

    line = jnp.where(tile * MRG_TM < N_CTX_TOK, CTX_LEN, GRID_W)
    pos = lax.broadcasted_iota(jnp.int32, (MRG_TM, 1), 0) & (line - 1)

    def shifted(val, k):
        rolled = pltpu.roll(val, k % MRG_TM, axis=0)
        ok = (pos >= k) if k > 0 else (pos < line + k)
        return jnp.where(ok, rolled, 0.0)

    m = scc_ref[...].astype(F32) * sch_ref[...].astype(F32)
    conv = conv_ref[0, 0:1, :] * shifted(m, 1) + conv_ref[0, 1:2, :] * m + conv_ref[0, 2:3, :] * shifted(m, -1)
    ya = scb_ref[...].astype(F32) * conv

    for b in range(S5_BLK):
        for t in range(S5_T):
            ys5_sc[b, pl.ds(t, MRG_TM // S5_T, stride=S5_T), :] = yrow_ref[b, :, t * LANES:(t + 1) * LANES]
    z = _gelu_tanh(jnp.concatenate([ys5_sc[b] for b in range(S5_BLK)], axis=-1))
    yb = z * jax.nn.sigmoid(_dot(z.astype(BF16), gluw_sc[...]) + glub_ref[0])

    o = of_ref[...] + ob_ref[...]
    normed = []
    for h in range(HG_HEADS):
        oh = o[:, h * HG_DK:(h + 1) * HG_DK]
        ms = jnp.mean(oh * oh, axis=-1, keepdims=True)
        normed.append(oh * lax.rsqrt(ms + LN_EPS) * ng_ref[0])
    yc = jnp.concatenate(normed, axis=-1) * _silu(hgg_ref[...].astype(F32))

    pu = pool_ref[...].astype(F32)
    posf = pos.astype(F32)
    linef = line.astype(F32)
    pooled = []
    for gi, w in enumerate(POOL_WINDOWS):
        vg = pu[:, gi * POOL_GROUP:(gi + 1) * POOL_GROUP]
        back, fwd, span = vg, vg, 1
        while 2 * span <= w // 2:
            back = back + shifted(back, span)
            fwd = fwd + shifted(fwd, -span)
            span *= 2
        s = shifted(back, 1) + fwd
        cnt = jnp.minimum(posf + w // 2, linef) - jnp.maximum(posf - w // 2, 0.0)
        pg = s / cnt - vg
        pooled.append(_dot(pg.astype(BF16), poolw_sc[gi]))
    yd = jnp.concatenate(pooled, axis=-1) * pscale_ref[0]

    merged = jnp.zeros((MRG_TM, D_MODEL), F32)
    for k, (br, gate_ref) in enumerate(((ya, g0_ref), (yb, g1_ref), (yc, g2_ref), (yd, g3_ref))):
        merged = merged + jax.nn.sigmoid(gate_ref[...].astype(F32)) * _dot(br.astype(BF16), wbr_sc[k])
    mix = _dot(merged.astype(BF16), wout_sc[...])
    y = ALPHA * x_ref[...] + mod_ref[0, 0, 5:6, :] * mix
    o_ref[...] = _layer_norm(y, lng_ref[0, 0], lnb_ref[0, 0])


def _merge(proj, y_rows, o_f, o_b, x, mod, conv_w, glu_w, glu_b, norm_g, pool_w, pool_scale, w_branch, w_out,
           ln_g, ln_b, layer):
    def part(col):
        return pl.BlockSpec((MRG_TM, D_BRANCH), lambda i: (i, col))

    def gate(k):
        return pl.BlockSpec((MRG_TM, D_MODEL), lambda i: (i, N_IN_PARTS * D_BRANCH // D_MODEL + k))

    def per_layer(*shape, single=False):
        mode = dict(pipeline_mode=pl.Buffered(1)) if single else {}
        return pl.BlockSpec((1,) + shape, lambda i: (layer,) + (0,) * len(shape), **mode)

    return pl.pallas_call(
        _merge_kernel,
        out_shape=jax.ShapeDtypeStruct((N_TOK, D_MODEL), F32),
        grid=(N_TOK // MRG_TM,),
        in_specs=[
            part(0), part(1), part(2), part(8), part(9), gate(0), gate(1), gate(2), gate(3),
            pl.BlockSpec((S5_BLK, MRG_TM // S5_T, S5_CW), lambda i: (0, i, 0)),
            pl.BlockSpec((MRG_TM, D_BRANCH), lambda i: (i, 0)),
            pl.BlockSpec((MRG_TM, D_BRANCH), lambda i: (i, 0)),
            pl.BlockSpec((MRG_TM, D_MODEL), lambda i: (i, 0)),
            _mod_spec(layer, MRG_TM),
            per_layer(3, D_BRANCH), per_layer(D_BRANCH, D_BRANCH, single=True), per_layer(1, D_BRANCH),
            per_layer(1, HG_DK), per_layer(len(POOL_WINDOWS), POOL_GROUP, POOL_GROUP, single=True),
            per_layer(1, D_BRANCH), per_layer(N_BRANCH, D_BRANCH, D_MODEL, single=True),
            per_layer(D_MODEL, D_MODEL, single=True), _ln_spec(layer, 1), _ln_spec(layer, 1),
        ],
        out_specs=pl.BlockSpec((MRG_TM, D_MODEL), lambda i: (i, 0)),
        scratch_shapes=[
            pltpu.VMEM((S5_BLK, MRG_TM, LANES), F32),
            pltpu.VMEM((D_BRANCH, D_BRANCH), BF16),
            pltpu.VMEM((len(POOL_WINDOWS), POOL_GROUP, POOL_GROUP), BF16),
            pltpu.VMEM((N_BRANCH, D_BRANCH, D_MODEL), BF16),
            pltpu.VMEM((D_MODEL, D_MODEL), BF16),
        ],
        compiler_params=_cparams(("arbitrary",)),
        name="merge",
    )(proj, proj, proj, proj, proj, proj, proj, proj, proj, y_rows, o_f, o_b, x, mod,
      conv_w, glu_w, glu_b.reshape(DEPTH, 1, D_BRANCH), norm_g.reshape(DEPTH, 1, HG_DK), pool_w,
      pool_scale.reshape(DEPTH, 1, D_BRANCH), w_branch, w_out, ln_g, ln_b)


def _grid_pos_embedding():
    rows = LAT_LEN // GRID_W
    quarter = D_MODEL // 4
    omega = POS_BASE ** (-jnp.arange(quarter, dtype=F32) / quarter)
    ar = jnp.arange(rows, dtype=F32)[:, None] * omega
    ac = jnp.arange(GRID_W, dtype=F32)[:, None] * omega
    row_tab = jnp.concatenate([jnp.sin(ar), jnp.cos(ar)], -1)[:, None, :]
    col_tab = jnp.concatenate([jnp.sin(ac), jnp.cos(ac)], -1)[None, :, :]
    shape = (rows, GRID_W, 2 * quarter)
    pos = jnp.concatenate([jnp.broadcast_to(row_tab, shape), jnp.broadcast_to(col_tab, shape)], -1)
    return pos.reshape(LAT_LEN, D_MODEL)


def kernel(x_prompt, x_sample, state_s5_re, state_s5_im, state_hgrn, c, c_ctx, w_ada, b_ada, ln_g, ln_b,
           ffn_w1, ffn_w3, ffn_w2, w_in, sc_conv, s5_lam_re, s5_lam_im, s5_log_dt, s5_b_re, s5_b_im,
           s5_c_re, s5_c_im, s5_d, s5_glu_w, s5_glu_b, hg_lb, hg_norm_g, pool_w, pool_scale, w_branch, w_out):
    xs = x_sample + _grid_pos_embedding()[None]
    x = jnp.concatenate([x_prompt.reshape(N_CTX_TOK, D_MODEL), xs.reshape(N_LAT_TOK, D_MODEL)], axis=0)

    cond = jnp.zeros((COND_PAD, D_MODEL), F32).at[0].set(c_ctx).at[1:N_COND].set(c)
    mod = _ada(cond, w_ada, b_ada).reshape(DEPTH, COND_PAD, N_SUB * 3, D_MODEL)
    ln_g4 = ln_g.reshape(DEPTH, N_SUB, 1, D_MODEL)
    ln_b4 = ln_b.reshape(DEPTH, N_SUB, 1, D_MODEL)

    s5_m, s5_win, s5_wout, s5_pow = _s5_tables(*_s5_param_layout(
        s5_lam_re, s5_lam_im, s5_log_dt, s5_b_re, s5_b_im, s5_c_re, s5_c_im, s5_d))

    ffn_w1, ffn_w3 = _cast_bf16(ffn_w1, ffn_w3)
    (ffn_w2,) = _cast_bf16(ffn_w2)

    ae_np, mask_np, pair_np = _hg_constants()
    ae = jnp.asarray(ae_np, BF16)
    mask = jnp.asarray(mask_np, F32)
    pair_diag = jnp.asarray(pair_np, F32)

    fin_s5, fin_hg = [], []
    for l in range(DEPTH):
        x = _ffn(x, mod, ffn_w1, ffn_w3, ffn_w2, ln_g4, ln_b4, l, 0)
        proj, fz, u_rows = _inproj(x, mod, w_in, l)

        h0 = jnp.concatenate([
            state_s5_re[:, l].reshape(N_LAT_SEQ, 2, S5_BLK, S5_SW),
            state_s5_im[:, l].reshape(N_LAT_SEQ, 2, S5_BLK, S5_SW)], axis=-1).transpose(1, 2, 0, 3)
        y_rows, fin = _s5_main(u_rows, s5_m, s5_win, s5_wout, s5_pow, h0, l)
        fin_s5.append(fin)

        h0t = jnp.concatenate([jnp.zeros((1, 2, HG_HEADS, HG_DK, HG_DK), F32),
                               jnp.swapaxes(state_hgrn[:, l], -1, -2)], axis=0)
        o_f, o_b, fin_f, fin_b = _hgrn(proj, fz, hg_lb, ae, mask, pair_diag, h0t, l)
        fin_hg.append(jnp.stack([fin_f[:N_CTX_SEQ], fin_b[:N_CTX_SEQ]], axis=1))

        x = _merge(proj, y_rows, o_f, o_b, x, mod, sc_conv, s5_glu_w, s5_glu_b, hg_norm_g,
                   pool_w, pool_scale, w_branch, w_out, ln_g4, ln_b4, l)
        x = _ffn(x, mod, ffn_w1, ffn_w3, ffn_w2, ln_g4, ln_b4, l, 2)

    y_prompt = x[:N_CTX_TOK].reshape(N_CTX_SEQ, CTX_LEN, D_MODEL)
    y_sample = x[N_CTX_TOK:].reshape(N_LAT_SEQ, LAT_LEN, D_MODEL)
    fin = jnp.stack(fin_s5)
    fin = fin.reshape(DEPTH, 2, S5_BLK, N_CTX_SEQ, 2, S5_GPB, S5_STATE)
    fin = fin.transpose(4, 3, 0, 1, 2, 5, 6).reshape(2, N_CTX_SEQ, DEPTH, 2, S5_GROUPS, S5_STATE)
    new_hgrn = jnp.swapaxes(jnp.stack(fin_hg, axis=1), -1, -2)
    return y_prompt, y_sample, fin[0], fin[1], new_hgrn
```

```python
import functools

import numpy as np
import jax
import jax.numpy as jnp
from jax import lax
from jax.experimental import pallas as pl
from jax.experimental.pallas import tpu as pltpu

F32 = jnp.float32
BF16 = jnp.bfloat16

D_MODEL = 1024
N_CTX_SEQ = 16
CTX_LEN = 256
DEPTH = 2
N_LAT_SEQ = 2
LAT_LEN = 4096
GRID_W = 64
D_BRANCH = 512
N_BRANCH = 4
S5_GROUPS = 32
S5_GROUP = 16
S5_STATE = 64
HG_HEADS = 4
HG_DK = 128
POOL_WINDOWS = (2, 4, 8, 16)
POOL_GROUP = 128
D_FF = 2816
N_SUB = 3
N_IN_PARTS = 10
IN_WIDTH = N_IN_PARTS * D_BRANCH + N_BRANCH * D_MODEL
ADA_WIDTH = N_SUB * 3 * D_MODEL
ALPHA = (2 * DEPTH) ** 0.25
LN_EPS = 1e-5
POS_BASE = 10000.0

N_CTX_TOK = N_CTX_SEQ * CTX_LEN
N_LAT_TOK = N_LAT_SEQ * LAT_LEN
N_TOK = N_CTX_TOK + N_LAT_TOK
N_COND = 1 + N_LAT_SEQ
COND_PAD = 8

LANES = 128
SUBLANES = 8
VMEM_LIMIT = 56 * 1024 * 1024

S5_T = 8
S5_BLK = 4
S5_GPB = S5_GROUPS // S5_BLK
S5_SW = S5_GPB * S5_STATE
S5_CW = S5_T * LANES
CTX_CHUNKS = CTX_LEN // S5_T
LAT_CHUNKS = LAT_LEN // S5_T
S5_ROWS_CTX = N_CTX_TOK // S5_T
S5_ROWS_LAT = N_LAT_TOK // S5_T
S5_ROWS = S5_ROWS_CTX + S5_ROWS_LAT
S5_LAT_GROUPS = LAT_CHUNKS // SUBLANES
S5_POW_ROWS = 24
S5_U_COL = 3

HG_C = 128
HG_LEVELS = 7
HG_TILES = N_TOK // HG_C
HG_CTX_TILES = N_CTX_TOK // HG_C
HG_TILES_PER_CTX = CTX_LEN // HG_C
HG_TILES_PER_LAT = LAT_LEN // HG_C
HG_E_ROWS = (HG_LEVELS + 1) * HG_C + 8


def _cparams(sem):
    return pltpu.CompilerParams(dimension_semantics=sem, vmem_limit_bytes=VMEM_LIMIT)


def _cond_row(tile, tile_tokens):
    tok = tile * tile_tokens
    return jnp.where(tok < N_CTX_TOK, 0, 1 + (tok - N_CTX_TOK) // LAT_LEN)


def _dot(a, b):
    return jnp.dot(a, b, preferred_element_type=F32)


def _dot_nt(a, b):
    return lax.dot_general(a, b, (((1,), (1,)), ((), ())), preferred_element_type=F32)


def _dot_tn(a, b):
    return lax.dot_general(a, b, (((0,), (0,)), ((), ())), preferred_element_type=F32)


def _split3(x):
    h1 = x.astype(BF16)
    r1 = x - h1.astype(F32)
    h2 = r1.astype(BF16)
    h3 = (r1 - h2.astype(F32)).astype(BF16)
    return h1, h2, h3


def _dot_nt_hi(a3, b3):
    acc = None
    for x in range(3):
        for y in range(3 - x):
            term = _dot_nt(a3[x], b3[y])
            acc = term if acc is None else acc + term
    return acc


def _silu(x):
    return x * jax.nn.sigmoid(x)


def _sigmoid_tanh(x):
    return 0.5 * jnp.tanh(0.5 * x) + 0.5


def _layer_norm(y, g, b):
    mu = jnp.mean(y, axis=-1, keepdims=True)
    yc = y - mu
    var = jnp.mean(yc * yc, axis=-1, keepdims=True)
    return yc * lax.rsqrt(var + LN_EPS) * g + b


def _mod_spec(layer, tile_tokens):
    return pl.BlockSpec((1, 1, N_SUB * 3, D_MODEL),
                        lambda i, *_: (layer, _cond_row(i, tile_tokens), 0, 0))


def _ln_spec(layer, sub):
    return pl.BlockSpec((1, 1, 1, D_MODEL), lambda *_: (layer, sub, 0, 0))


ADA_TN = 1152


def _ada_kernel(c_ref, w_ref, b_ref, o_ref):
    s = _silu(c_ref[...]).astype(BF16)
    o_ref[0] = _dot(s, w_ref[0].astype(BF16)) + b_ref[0]


def _ada(cond, w_ada, b_ada):
    return pl.pallas_call(
        _ada_kernel,
        out_shape=jax.ShapeDtypeStruct((DEPTH, COND_PAD, ADA_WIDTH), F32),
        grid=(DEPTH, ADA_WIDTH // ADA_TN),
        in_specs=[
            pl.BlockSpec((COND_PAD, D_MODEL), lambda l, j: (0, 0)),
            pl.BlockSpec((1, D_MODEL, ADA_TN), lambda l, j: (l, 0, j)),
            pl.BlockSpec((1, 1, ADA_TN), lambda l, j: (l, 0, j)),
        ],
        out_specs=pl.BlockSpec((1, COND_PAD, ADA_TN), lambda l, j: (l, 0, j)),
        compiler_params=_cparams(("parallel", "parallel")),
        name="ada",
    )(cond, w_ada, b_ada.reshape(DEPTH, 1, ADA_WIDTH))


CAST_ROWS = 512


def _cast_kernel(*refs):
    n = len(refs) // 2
    for src, dst in zip(refs[:n], refs[n:]):
        dst[...] = src[...].astype(BF16)


def _cast_bf16(*ws):
    shape = ws[0].shape
    rows, cols = int(np.prod(shape[:-1])), shape[-1]
    spec = pl.BlockSpec((CAST_ROWS, cols), lambda i: (i, 0))
    outs = pl.pallas_call(
        _cast_kernel,
        out_shape=tuple(jax.ShapeDtypeStruct((rows, cols), BF16) for _ in ws),
        grid=(rows // CAST_ROWS,),
        in_specs=[spec] * len(ws),
        out_specs=tuple([spec] * len(ws)),
        compiler_params=_cparams(("parallel",)),
        name="cast_bf16",
    )(*(w.reshape(rows, cols) for w in ws))
    return tuple(o.reshape(shape) for o in outs)


FFN_TM = 512
FFN_TC = 1408


def _ffn_kernel(*refs, sub, split_input):
    if split_input:
        ctx_ref, lat_ref, pos_ref = refs[:3]
        refs = refs[3:]
        x = jnp.where(pl.program_id(0) < N_CTX_TOK // FFN_TM, ctx_ref[...], lat_ref[...] + pos_ref[...])
    else:
        x = refs[0][...]
        refs = refs[1:]
    mod_ref, w1_ref, w3_ref, w2_ref, g_ref, b_ref, o_ref, act_sc = refs
    shift = mod_ref[0, 0, 3 * sub:3 * sub + 1, :]
    scale = mod_ref[0, 0, 3 * sub + 1:3 * sub + 2, :]
    gate = mod_ref[0, 0, 3 * sub + 2:3 * sub + 3, :]
    h = (x * (1.0 + scale) + shift).astype(BF16)
    for c in range(D_FF // FFN_TC):
        cols = slice(c * FFN_TC, (c + 1) * FFN_TC)
        a = _dot(h, w1_ref[0, 0, :, cols])
        b = _dot(h, w3_ref[0, 0, :, cols])
        act_sc[:, cols] = (_silu(a) * b).astype(BF16)
    f = _dot(act_sc[...], w2_ref[0, 0])
    y = ALPHA * x + gate * (0.5 * f)
    o_ref[...] = _layer_norm(y, g_ref[0, 0], b_ref[0, 0])


def _ffn(x, mod, w1, w3, w2, ln_g, ln_b, layer, sub):
    which = sub // 2
    split_input = isinstance(x, tuple)

    def resident(*shape):
        return pl.BlockSpec((1, 1) + shape, lambda i: (layer, which, 0, 0), pipeline_mode=pl.Buffered(1))

    if split_input:
        ctx_tiles = N_CTX_TOK // FFN_TM
        pos_tiles = LAT_LEN // FFN_TM
        x_specs = [
            pl.BlockSpec((FFN_TM, D_MODEL), lambda i: (jnp.minimum(i, ctx_tiles - 1), 0)),
            pl.BlockSpec((FFN_TM, D_MODEL), lambda i: (jnp.maximum(i - ctx_tiles, 0), 0)),
            pl.BlockSpec((FFN_TM, D_MODEL), lambda i: (jnp.maximum(i - ctx_tiles, 0) % pos_tiles, 0)),
        ]
    else:
        x_specs = [pl.BlockSpec((FFN_TM, D_MODEL), lambda i: (i, 0))]
        x = (x,)

    return pl.pallas_call(
        functools.partial(_ffn_kernel, sub=sub, split_input=split_input),
        out_shape=jax.ShapeDtypeStruct((N_TOK, D_MODEL), F32),
        grid=(N_TOK // FFN_TM,),
        in_specs=x_specs + [
            _mod_spec(layer, FFN_TM),
            resident(D_MODEL, D_FF), resident(D_MODEL, D_FF), resident(D_FF, D_MODEL),
            _ln_spec(layer, sub), _ln_spec(layer, sub),
        ],
        out_specs=pl.BlockSpec((FFN_TM, D_MODEL), lambda i: (i, 0)),
        scratch_shapes=[pltpu.VMEM((FFN_TM, D_FF), BF16)],
        compiler_params=_cparams(("parallel",)),
        name=f"ffn{sub}",
    )(*x, mod, w1, w3, w2, ln_g, ln_b)


INP_TM = 1024
INP_PARTS = 2
INP_TN = INP_PARTS * D_BRANCH
HG_Q_COL, HG_F_COL, HG_V_COL = 4, 5, 7


def _inproj_kernel(x_ref, mod_ref, w_ref, o_ref, fz_ref, u_ref, h_sc, blk_sc):
    j = pl.program_id(1)

    @pl.when(j == 0)
    def _():
        shift = mod_ref[0, 0, 3:4, :]
        scale = mod_ref[0, 0, 4:5, :]
        h_sc[...] = (x_ref[...] * (1.0 + scale) + shift).astype(BF16)

    res = _dot(h_sc[...], w_ref[0].astype(BF16))
    is_gate = j >= N_IN_PARTS // INP_PARTS
    o_ref[...] = jnp.where(is_gate, _sigmoid_tanh(res), res).astype(BF16)

    def part(p):
        k = p % INP_PARTS
        return res[:, k * D_BRANCH:(k + 1) * D_BRANCH]

    for p in (HG_F_COL, HG_F_COL + 1):
        @pl.when(j == p // INP_PARTS)
        def _(p=p):
            fz_ref[...] = part(p)

    @pl.when(j == S5_U_COL // INP_PARTS)
    def _():
        s5_u = part(S5_U_COL)
        for b in range(S5_BLK):
            blk_sc[b] = s5_u[:, b * LANES:(b + 1) * LANES]
            for t in range(S5_T):
                rows = blk_sc[b, pl.ds(t, INP_TM // S5_T, stride=S5_T), :]
                u_ref[b, :, t * LANES:(t + 1) * LANES] = rows.astype(BF16)


def _inproj(x, mod, w_in, layer):
    return pl.pallas_call(
        _inproj_kernel,
        out_shape=(jax.ShapeDtypeStruct((N_TOK, IN_WIDTH), BF16),
                   jax.ShapeDtypeStruct((N_TOK, 2 * D_BRANCH), F32),
                   jax.ShapeDtypeStruct((S5_BLK, S5_ROWS, S5_CW), BF16)),
        grid=(N_TOK // INP_TM, IN_WIDTH // INP_TN),
        in_specs=[
            pl.BlockSpec((INP_TM, D_MODEL), lambda i, j: (i, 0), pipeline_mode=pl.Buffered(1)),
            _mod_spec(layer, INP_TM),
            pl.BlockSpec((1, D_MODEL, INP_TN), lambda i, j: (layer, 0, j)),
        ],
        out_specs=(pl.BlockSpec((INP_TM, INP_TN), lambda i, j: (i, j)),
                   pl.BlockSpec((INP_TM, D_BRANCH),
                                lambda i, j: (i, jnp.where(j >= (HG_F_COL + 1) // INP_PARTS, 1, 0))),
                   pl.BlockSpec((S5_BLK, INP_TM // S5_T, S5_CW), lambda i, j: (0, i, 0))),
        scratch_shapes=[pltpu.VMEM((INP_TM, D_MODEL), BF16), pltpu.VMEM((S5_BLK, INP_TM, LANES), F32)],
        compiler_params=_cparams(("parallel", "arbitrary")),
        name="inproj",
    )(x, mod, w_in)


def _s5_tab_kernel(lam_ref, bre_ref, bim_ref, cre_ref, cim_ref, dsk_ref, m_ref, win_ref, wout_ref, pow_ref):
    d = pl.program_id(1)
    fwd = d == 0
    lam_re = lam_ref[0, 0, 0, 0:1, :]
    lam_im = lam_ref[0, 0, 0, 1:2, :]
    dt = jnp.exp(lam_ref[0, 0, 0, 2:3, :])

    def apow(j):
        mag = jnp.exp(lam_re * dt * float(j))
        ang = lam_im * dt * float(j)
        return mag * jnp.cos(ang), mag * jnp.sin(ang)

    a_re, a_im = apow(1)
    den = lam_re * lam_re + lam_im * lam_im
    num_re = a_re - 1.0
    coef_re = (num_re * lam_re + a_im * lam_im) / den
    coef_im = (a_im * lam_re - num_re * lam_im) / den
    b_re = bre_ref[0, 0, 0]
    b_im = bim_ref[0, 0, 0]
    bb_re = coef_re * b_re - coef_im * b_im
    bb_im = coef_re * b_im + coef_im * b_re
    bbr = _split3(bb_re)
    bbi = _split3(bb_im)
    c_re = cre_ref[0, 0, 0]
    c_im = cim_ref[0, 0, 0]

    kmat = []
    for j in range(S5_T + 1):
        aj_re, aj_im = apow(j)
        pc_re = c_re * aj_re - c_im * aj_im
        pc_im = c_re * aj_im + c_im * aj_re
        if j >= 1:
            r = pl.multiple_of(jnp.where(fwd, j - 1, S5_T - j) * LANES, LANES)
            wout_ref[0, 0, 0, pl.ds(r, LANES), :] = jnp.concatenate([pc_re, -pc_im], axis=1).astype(BF16)
        if j < S5_T:
            pb_re = bb_re * aj_re - bb_im * aj_im
            pb_im = bb_re * aj_im + bb_im * aj_re
            r = pl.multiple_of(jnp.where(fwd, S5_T - 1 - j, j) * LANES, LANES)
            win_ref[0, 0, 0, pl.ds(r, LANES), :] = jnp.concatenate([pb_re, pb_im], axis=1).astype(BF16)
            kmat.append(_dot_nt_hi(bbr, _split3(pc_re)) - _dot_nt_hi(bbi, _split3(pc_im)))

    ri = lax.broadcasted_iota(jnp.int32, (LANES, LANES), 0)
    ci = lax.broadcasted_iota(jnp.int32, (LANES, LANES), 1)
    keep_upper = jnp.where(fwd, 1.0, 0.0)
    kmat[0] = kmat[0] + jnp.where(ri == ci, dsk_ref[0, 0], 0.0) * keep_upper
    for s in range(S5_T):
        for t in range(S5_T):
            tile = kmat[abs(t - s)]
            if t > s:
                tile = tile * keep_upper
            elif t < s:
                tile = tile * (1.0 - keep_upper)
            m_ref[0, 0, 0, s * LANES:(s + 1) * LANES, t * LANES:(t + 1) * LANES] = tile.astype(BF16)

    pow_ref[...] = jnp.zeros_like(pow_ref)
    for i in range(SUBLANES):
        for row, j in ((i, S5_T * (i + 1)), (SUBLANES + i, S5_T * (SUBLANES - i))):
            p_re, p_im = apow(j)
            pow_ref[0, 0, 0, 0, row:row + 1, :] = p_re
            pow_ref[0, 0, 0, 1, row:row + 1, :] = p_im
    p_re, p_im = apow(S5_T * 2 * SUBLANES)
    pow_ref[0, 0, 0, 0, 2 * SUBLANES:2 * SUBLANES + 1, :] = p_re
    pow_ref[0, 0, 0, 1, 2 * SUBLANES:2 * SUBLANES + 1, :] = p_im


def _s5_tables(lam, b_re, b_im, c_re, c_im, d_skip):
    lead = (DEPTH, 2, S5_BLK)

    def spec(*tail):
        return pl.BlockSpec((1, 1, 1) + tail, lambda l, d, b: (l, d, b) + (0,) * len(tail))

    return pl.pallas_call(
        _s5_tab_kernel,
        out_shape=(
            jax.ShapeDtypeStruct(lead + (S5_CW, S5_CW), BF16),
            jax.ShapeDtypeStruct(lead + (S5_CW, 2 * S5_SW), BF16),
            jax.ShapeDtypeStruct(lead + (S5_CW, 2 * S5_SW), BF16),
            jax.ShapeDtypeStruct(lead + (2, S5_POW_ROWS, S5_SW), F32),
        ),
        grid=lead,
        in_specs=[spec(3, S5_SW)] + [spec(LANES, S5_SW)] * 4
        + [pl.BlockSpec((1, 1, 1, LANES), lambda l, d, b: (l, b, 0, 0))],
        out_specs=(spec(S5_CW, S5_CW), spec(S5_CW, 2 * S5_SW), spec(S5_CW, 2 * S5_SW),
                   spec(2, S5_POW_ROWS, S5_SW)),
        compiler_params=_cparams(("parallel", "parallel", "parallel")),
        name="s5_tables",
    )(lam, b_re, b_im, c_re, c_im, d_skip)


def _s5_param_layout(s5_lam_re, s5_lam_im, s5_log_dt, s5_b_re, s5_b_im, s5_c_re, s5_c_im, s5_d):
    eye = jnp.eye(S5_GPB, dtype=F32)
    lam = jnp.stack([s5_lam_re, s5_lam_im, jnp.broadcast_to(s5_log_dt[..., None], s5_lam_re.shape)], axis=2)
    lam = lam.reshape(DEPTH, 2, 3, S5_BLK, S5_SW).transpose(0, 1, 3, 2, 4)

    def bdiag_b(b):
        b = b.reshape(DEPTH, 2, S5_BLK, S5_GPB, S5_STATE, S5_GROUP)
        return jnp.einsum('ldbgph,gq->ldbghqp', b, eye).reshape(DEPTH, 2, S5_BLK, LANES, S5_SW)

    def bdiag_c(c):
        c = c.reshape(DEPTH, 2, S5_BLK, S5_GPB, S5_GROUP, S5_STATE)
        return jnp.einsum('ldbghp,gq->ldbghqp', c, eye).reshape(DEPTH, 2, S5_BLK, LANES, S5_SW)

    return (lam, bdiag_b(s5_b_re), bdiag_b(s5_b_im), bdiag_c(s5_c_re), bdiag_c(s5_c_im),
            s5_d.reshape(DEPTH, S5_BLK, 1, LANES))


def _s5_kernel(u_ref, m_ref, win_ref, wout_ref, pow_ref, h0_ref, y_ref, fin_ref, x_sc):
    d = pl.program_id(1)
    u = u_ref[0]
    x_sc[...] = _dot(u, win_ref[0, 0, 0])
    p_re = pow_ref[0, 0, 0, 0]
    p_im = pow_ref[0, 0, 0, 1]
    re = slice(0, S5_SW)
    im = slice(S5_SW, 2 * S5_SW)

    def madd(t_re, t_im, w_re, w_im, s_re, s_im):
        return t_re + w_re * s_re - w_im * s_im, t_im + w_re * s_im + w_im * s_re

    def shift_rows(val, k, n, idx, reverse):
        if reverse:
            return jnp.where(idx < n - k, pltpu.roll(val, val.shape[0] - k, axis=0), 0.0)
        return jnp.where(idx >= k, pltpu.roll(val, k, axis=0), 0.0)

    def pow_row(k):
        row = k - 1 if k <= SUBLANES else 2 * SUBLANES
        return p_re[row:row + 1], p_im[row:row + 1]

    def ctx_scan(reverse):
        s_re = x_sc[0:S5_ROWS_CTX, re]
        s_im = x_sc[0:S5_ROWS_CTX, im]
        c = lax.broadcasted_iota(jnp.int32, (S5_ROWS_CTX, 1), 0) & (CTX_CHUNKS - 1)
        k = 1
        while k < CTX_CHUNKS:
            w_re, w_im = pow_row(k)
            s_re, s_im = madd(s_re, s_im, w_re, w_im,
                              shift_rows(s_re, k, CTX_CHUNKS, c, reverse), shift_rows(s_im, k, CTX_CHUNKS, c, reverse))
            k *= 2
        last = 0 if reverse else CTX_CHUNKS - 1
        pick = (lax.broadcasted_iota(jnp.int32, (N_CTX_SEQ, S5_ROWS_CTX), 1)
                == lax.broadcasted_iota(jnp.int32, (N_CTX_SEQ, S5_ROWS_CTX), 0) * CTX_CHUNKS + last)
        pick = jnp.where(pick, 1.0, 0.0).astype(BF16)
        fin_ref[0, 0, :, re] = sum(_dot(pick, piece) for piece in _split3(s_re))
        fin_ref[0, 0, :, im] = sum(_dot(pick, piece) for piece in _split3(s_im))
        x_sc[0:S5_ROWS_CTX, re] = shift_rows(s_re, 1, CTX_CHUNKS, c, reverse)
        x_sc[0:S5_ROWS_CTX, im] = shift_rows(s_im, 1, CTX_CHUNKS, c, reverse)

    def lat_group(k, carry, reverse):
        g = (S5_LAT_GROUPS - 1 - k) if reverse else k
        r8 = lax.broadcasted_iota(jnp.int32, (SUBLANES, 1), 0)
        tab = slice(SUBLANES, 2 * SUBLANES) if reverse else slice(0, SUBLANES)
        out = []
        for q in range(N_LAT_SEQ):
            base = pl.multiple_of(S5_ROWS_CTX + q * LAT_CHUNKS + g * SUBLANES, SUBLANES)
            s_re = x_sc[pl.ds(base, SUBLANES), re]
            s_im = x_sc[pl.ds(base, SUBLANES), im]
            j = 1
            while j < SUBLANES:
                w_re, w_im = pow_row(j)
                s_re, s_im = madd(s_re, s_im, w_re, w_im,
                                  shift_rows(s_re, j, SUBLANES, r8, reverse), shift_rows(s_im, j, SUBLANES, r8, reverse))
                j *= 2
            c_re, c_im = carry[2 * q], carry[2 * q + 1]
            s_re, s_im = madd(s_re, s_im, p_re[tab], p_im[tab], c_re, c_im)
            edge = (r8 == SUBLANES - 1) if reverse else (r8 == 0)
            x_sc[pl.ds(base, SUBLANES), re] = jnp.where(edge, c_re, shift_rows(s_re, 1, SUBLANES, r8, reverse))
            x_sc[pl.ds(base, SUBLANES), im] = jnp.where(edge, c_im, shift_rows(s_im, 1, SUBLANES, r8, reverse))
            end = slice(0, 1) if reverse else slice(SUBLANES - 1, SUBLANES)
            out += [s_re[end], s_im[end]]
        return tuple(out)

    def run(reverse):
        ctx_scan(reverse)
        h0 = h0_ref[0, 0]
        init = []
        for q in range(N_LAT_SEQ):
            init += [h0[q:q + 1, re], h0[q:q + 1, im]]
        lax.fori_loop(0, S5_LAT_GROUPS, functools.partial(lat_group, reverse=reverse), tuple(init))

    @pl.when(d == 0)
    def _():
        run(False)

    @pl.when(d == 1)
    def _():
        run(True)

    y = _dot(u, m_ref[0, 0, 0]) + _dot_nt(x_sc[...].astype(BF16), wout_ref[0, 0, 0])

    @pl.when(d == 0)
    def _():
        y_ref[0] = y

    @pl.when(d == 1)
    def _():
        y_ref[0] += y


def _s5_main(u, m, w_in, w_out, pows, h0, layer):
    def tab(*tail):
        return pl.BlockSpec((1, 1, 1) + tail, lambda b, d: (layer, d, b) + (0,) * len(tail))

    return pl.pallas_call(
        _s5_kernel,
        out_shape=(jax.ShapeDtypeStruct((S5_BLK, S5_ROWS, S5_CW), F32),
                   jax.ShapeDtypeStruct((2, S5_BLK, N_CTX_SEQ, 2 * S5_SW), F32)),
        grid=(S5_BLK, 2),
        in_specs=[
            pl.BlockSpec((1, S5_ROWS, S5_CW), lambda b, d: (b, 0, 0)),
            tab(S5_CW, S5_CW), tab(S5_CW, 2 * S5_SW), tab(S5_CW, 2 * S5_SW), tab(2, S5_POW_ROWS, S5_SW),
            pl.BlockSpec((1, 1, N_LAT_SEQ, 2 * S5_SW), lambda b, d: (d, b, 0, 0)),
        ],
        out_specs=(
            pl.BlockSpec((1, S5_ROWS, S5_CW), lambda b, d: (b, 0, 0)),
            pl.BlockSpec((1, 1, N_CTX_SEQ, 2 * S5_SW), lambda b, d: (d, b, 0, 0)),
        ),
        scratch_shapes=[pltpu.VMEM((S5_ROWS, 2 * S5_SW), F32)],
        compiler_params=_cparams(("parallel", "arbitrary")),
        name="s5",
    )(u, m, w_in, w_out, pows, h0)


def _hg_constants():
    t = np.arange(HG_C)
    ae = np.zeros((2, HG_E_ROWS, HG_C), np.float32)
    mask = np.zeros((2, HG_LEVELS, HG_C, HG_C), np.float32)
    for lvl in range(HG_LEVELS):
        half = 1 << lvl
        pos = t % (2 * half)
        mid = t - pos + half - 1
        upper = pos >= half
        u = t[None, :]
        rows_upper = (u > mid[:, None]) & (u <= t[:, None])
        rows_lower = (u > t[:, None]) & (u <= mid[:, None])
        ae[0, lvl * HG_C:(lvl + 1) * HG_C] = np.where(upper[:, None], rows_upper, rows_lower)
        same = (t[:, None] // (2 * half)) == (t[None, :] // (2 * half))
        mask[0, lvl] = same & upper[:, None] & (~upper)[None, :]
    ae[0, HG_LEVELS * HG_C:(HG_LEVELS + 1) * HG_C] = t[None, :] <= t[:, None]
    ae[0, (HG_LEVELS + 1) * HG_C:] = 1.0
    ae[1] = ae[0][:, ::-1]
    ae[1, :(HG_LEVELS + 1) * HG_C] = ae[1, :(HG_LEVELS + 1) * HG_C].reshape(HG_LEVELS + 1, HG_C, HG_C)[:, ::-1].reshape(-1, HG_C)
    mask[1] = mask[0][:, ::-1, ::-1]
    ae2 = np.concatenate([ae, ae], axis=-1)
    mask2 = np.concatenate([mask, mask], axis=-1)
    pair_diag = np.kron(np.eye(2, dtype=np.float32), np.ones((HG_DK, HG_DK), np.float32))
    return ae2, mask2, pair_diag


def _hg_tile(direction, i):
    return i if direction == 0 else HG_TILES - 1 - i


HG_PAIR = 2 * HG_DK


def _pair_diag(x):
    zero = jnp.zeros((x.shape[0], HG_DK), x.dtype)
    return jnp.concatenate([jnp.concatenate([x[:, :HG_DK], zero], axis=1),
                            jnp.concatenate([zero, x[:, HG_DK:]], axis=1)], axis=0)


def _hg_direction(direction, q_ref, f_ref, v_ref, lb, ae_ref, mask_ref, pd_ref, o_ref, st_sc):
    f = lb + (1.0 - lb) * jax.nn.sigmoid(f_ref[...])
    logf = jnp.log2(f)
    kk = 1.0 - f
    hi = logf.astype(BF16)
    lo = (logf - hi.astype(F32)).astype(BF16)
    ex = _dot(ae_ref[direction], jnp.concatenate([hi, lo], axis=0))
    dec = jnp.exp2(ex[:(HG_LEVELS + 1) * HG_C])
    cum = ex[HG_LEVELS * HG_C:(HG_LEVELS + 1) * HG_C]
    tot = ex[(HG_LEVELS + 1) * HG_C:(HG_LEVELS + 1) * HG_C + 1]
    q = q_ref[...].astype(F32)
    v_bf = v_ref[...]
    v = v_bf.astype(F32)
    q_in = (q * dec[HG_LEVELS * HG_C:]).astype(BF16)
    k_tail = (kk * jnp.exp2(tot - cum)).astype(BF16)
    dec_tot = jnp.exp2(tot)
    qk = q * kk
    outs = []
    for p in range(HG_HEADS // 2):
        sl = slice(p * HG_PAIR, (p + 1) * HG_PAIR)
        scores = jnp.zeros((HG_C, HG_PAIR), F32)
        for lvl in range(HG_LEVELS):
            g = dec[lvl * HG_C:(lvl + 1) * HG_C, sl]
            scores = scores + mask_ref[direction, lvl] * _dot_nt(
                (q[:, sl] * g).astype(BF16), _pair_diag((kk[:, sl] * g).astype(BF16)))
        diag = jnp.concatenate(
            [jnp.broadcast_to(jnp.sum(qk[:, (2 * p + j) * HG_DK:(2 * p + j + 1) * HG_DK], axis=-1, keepdims=True),
                              (HG_C, HG_DK)) for j in range(2)], axis=-1)
        st = st_sc[direction, p]
        outs.append(_dot(scores.astype(BF16), _pair_diag(v_bf[:, sl])) + diag * v[:, sl]
                    + _dot_nt(q_in[:, sl], st.astype(BF16)))
        st_new = st * dec_tot[:, sl] + pd_ref[...] * _dot_tn(v_bf[:, sl], k_tail[:, sl])
        st_sc[direction, p] = st_new
    o_ref[...] = jnp.concatenate(outs, axis=-1)


def _hg_kernel(qf_ref, ff_ref, vf_ref, qb_ref, fb_ref, vb_ref, lb_ref, ae_ref, mask_ref, pd_ref, h0f_ref, h0b_ref,
               of_ref, ob_ref, finf_ref, finb_ref, st_sc, *, layer):
    i = pl.program_id(0)
    x = lb_ref[:, :, 0, :]
    e = jnp.exp(x - jnp.max(x, axis=0, keepdims=True))
    sm = e / jnp.sum(e, axis=0, keepdims=True)
    lb = jnp.sum(sm[1:layer + 1], axis=0) if layer > 0 else jnp.zeros((2, D_BRANCH), F32)
    for direction, h0_ref in ((0, h0f_ref), (1, h0b_ref)):
        tile = _hg_tile(direction, i)
        first_ctx = 0 if direction == 0 else HG_TILES_PER_CTX - 1
        first_lat = 0 if direction == 0 else HG_TILES_PER_LAT - 1
        is_start = jnp.where(tile < HG_CTX_TILES,
                             tile % HG_TILES_PER_CTX == first_ctx,
                             (tile - HG_CTX_TILES) % HG_TILES_PER_LAT == first_lat)

        @pl.when(is_start)
        def _(direction=direction, h0_ref=h0_ref):
            for p in range(HG_HEADS // 2):
                st_sc[direction, p] = _pair_diag(
                    jnp.concatenate([h0_ref[0, 0, 2 * p], h0_ref[0, 0, 2 * p + 1]], axis=1))

    _hg_direction(0, qf_ref, ff_ref, vf_ref, lb[0:1], ae_ref, mask_ref, pd_ref, of_ref, st_sc)
    _hg_direction(1, qb_ref, fb_ref, vb_ref, lb[1:2], ae_ref, mask_ref, pd_ref, ob_ref, st_sc)

    for direction, fin_ref in ((0, finf_ref), (1, finb_ref)):
        @pl.when(_hg_tile(direction, i) < HG_CTX_TILES)
        def _(direction=direction, fin_ref=fin_ref):
            for h in range(HG_HEADS):
                p, j = divmod(h, 2)
                fin_ref[0, h] = st_sc[direction, p, j * HG_DK:(j + 1) * HG_DK, j * HG_DK:(j + 1) * HG_DK]


def _hgrn(proj, fz, hg_lb, ae, mask, pair_diag, h0t, layer):
    def tok_spec(direction, col):
        return pl.BlockSpec((HG_C, D_BRANCH), lambda i: (_hg_tile(direction, i), col))

    def fin_spec(direction):
        return pl.BlockSpec((1, HG_HEADS, HG_DK, HG_DK),
                            lambda i: (jnp.minimum(_hg_tile(direction, i) // HG_TILES_PER_CTX, N_CTX_SEQ - 1), 0, 0, 0))

    def h0_spec(direction):
        return pl.BlockSpec((1, 1, HG_HEADS, HG_DK, HG_DK),
                            lambda i: (_cond_row(_hg_tile(direction, i), HG_C), direction, 0, 0, 0))

    fin_shape = jax.ShapeDtypeStruct((N_CTX_SEQ, HG_HEADS, HG_DK, HG_DK), F32)
    return pl.pallas_call(
        functools.partial(_hg_kernel, layer=layer),
        out_shape=(jax.ShapeDtypeStruct((N_TOK, D_BRANCH), F32), jax.ShapeDtypeStruct((N_TOK, D_BRANCH), F32),
                   fin_shape, fin_shape),
        grid=(HG_TILES,),
        in_specs=[
            tok_spec(0, HG_Q_COL), tok_spec(0, 0), tok_spec(0, HG_V_COL),
            tok_spec(1, HG_Q_COL), tok_spec(1, 1), tok_spec(1, HG_V_COL),
            pl.BlockSpec((DEPTH, 2, 1, D_BRANCH), lambda i: (0, 0, 0, 0)),
            pl.BlockSpec((2, HG_E_ROWS, 2 * HG_C), lambda i: (0, 0, 0)),
            pl.BlockSpec((2, HG_LEVELS, HG_C, HG_PAIR), lambda i: (0, 0, 0, 0)),
            pl.BlockSpec((HG_PAIR, HG_PAIR), lambda i: (0, 0)),
            h0_spec(0), h0_spec(1),
        ],
        out_specs=(
            pl.BlockSpec((HG_C, D_BRANCH), lambda i: (_hg_tile(0, i), 0)),
            pl.BlockSpec((HG_C, D_BRANCH), lambda i: (_hg_tile(1, i), 0)),
            fin_spec(0), fin_spec(1),
        ),
        scratch_shapes=[pltpu.VMEM((2, HG_HEADS // 2, HG_PAIR, HG_PAIR), F32)],
        compiler_params=_cparams(("arbitrary",)),
        name="hgrn",
    )(proj, fz, proj, proj, fz, proj, hg_lb.reshape(DEPTH, 2, 1, D_BRANCH), ae, mask, pair_diag, h0t, h0t)


MRG_TM = 256


def _gelu_tanh(x):
    return 0.5 * x * (1.0 + jnp.tanh(0.7978845608028654 * (x + 0.044715 * (x * x * x))))


def _merge_kernel(scb_ref, scc_ref, sch_ref, hgg_ref, pool_ref, g0_ref, g1_ref, g2_ref, g3_ref,
                  yrow_ref, of_ref, ob_ref, x_ref, mod_ref, conv_ref, gluw_ref, glub_ref, ng_ref,
                  poolw_ref, pscale_ref, wbr_ref, wout_ref, lng_ref, lnb_ref, o_ref,
                  ys5_sc, gluw_sc, poolw_sc, wbr_sc, wout_sc):
    tile = pl.program_id(0)

    @pl.when(tile == 0)
    def _():
        gluw_sc[...] = gluw_ref[0].astype(BF16)
        poolw_sc[...] = poolw_ref[0].astype(BF16)
        wbr_sc[...] = wbr_ref[0].astype(BF16)
        wout_sc[...] = wout_ref[0].astype(BF16)

    line = jnp.where(tile * MRG_TM < N_CTX_TOK, CTX_LEN, GRID_W)
    pos = lax.broadcasted_iota(jnp.int32, (MRG_TM, 1), 0) & (line - 1)

    def shifted(val, k):
        rolled = pltpu.roll(val, k % MRG_TM, axis=0)
        ok = (pos >= k) if k > 0 else (pos < line + k)
        return jnp.where(ok, rolled, 0.0)

    m = scc_ref[...].astype(F32) * sch_ref[...].astype(F32)
    conv = conv_ref[0, 0:1, :] * shifted(m, 1) + conv_ref[0, 1:2, :] * m + conv_ref[0, 2:3, :] * shifted(m, -1)
    ya = scb_ref[...].astype(F32) * conv

    for b in range(S5_BLK):
        for t in range(S5_T):
            ys5_sc[b, pl.ds(t, MRG_TM // S5_T, stride=S5_T), :] = yrow_ref[b, :, t * LANES:(t + 1) * LANES]
    z = _gelu_tanh(jnp.concatenate([ys5_sc[b] for b in range(S5_BLK)], axis=-1))
    yb = z * _sigmoid_tanh(_dot(z.astype(BF16), gluw_sc[...]) + glub_ref[0])

    o = of_ref[...] + ob_ref[...]
    normed = []
    for h in range(HG_HEADS):
        oh = o[:, h * HG_DK:(h + 1) * HG_DK]
        ms = jnp.mean(oh * oh, axis=-1, keepdims=True)
        normed.append(oh * lax.rsqrt(ms + LN_EPS) * ng_ref[0])
    gate_c = hgg_ref[...].astype(F32)
    yc = jnp.concatenate(normed, axis=-1) * (gate_c * _sigmoid_tanh(gate_c))

    pu = pool_ref[...].astype(F32)
    posf = pos.astype(F32)
    linef = line.astype(F32)
    pooled = []
    for gi, w in enumerate(POOL_WINDOWS):
        vg = pu[:, gi * POOL_GROUP:(gi + 1) * POOL_GROUP]
        back, fwd, span = vg, vg, 1
        while 2 * span <= w // 2:
            back = back + shifted(back, span)
            fwd = fwd + shifted(fwd, -span)
            span *= 2
        s = shifted(back, 1) + fwd
        cnt = jnp.minimum(posf + w // 2, linef) - jnp.maximum(posf - w // 2, 0.0)
        pg = s / cnt - vg
        pooled.append(_dot(pg.astype(BF16), poolw_sc[gi]))
    yd = jnp.concatenate(pooled, axis=-1) * pscale_ref[0]

    merged = jnp.zeros((MRG_TM, D_MODEL), F32)
    for k, (br, gate_ref) in enumerate(((ya, g0_ref), (yb, g1_ref), (yc, g2_ref), (yd, g3_ref))):
        merged = merged + gate_ref[...].astype(F32) * _dot(br.astype(BF16), wbr_sc[k])
    mix = _dot(merged.astype(BF16), wout_sc[...])
    y = ALPHA * x_ref[...] + mod_ref[0, 0, 5:6, :] * mix
    o_ref[...] = _layer_norm(y, lng_ref[0, 0], lnb_ref[0, 0])


def _merge(proj, y_rows, o_f, o_b, x, mod, conv_w, glu_w, glu_b, norm_g, pool_w, pool_scale, w_branch, w_out,
           ln_g, ln_b, layer):
    def part(col):
        return pl.BlockSpec((MRG_TM, D_BRANCH), lambda i: (i, col))

    def gate(k):
        return pl.BlockSpec((MRG_TM, D_MODEL), lambda i: (i, N_IN_PARTS * D_BRANCH // D_MODEL + k))

    def per_layer(*shape, single=False):
        mode = dict(pipeline_mode=pl.Buffered(1)) if single else {}
        return pl.BlockSpec((1,) + shape, lambda i: (layer,) + (0,) * len(shape), **mode)

    return pl.pallas_call(
        _merge_kernel,
        out_shape=jax.ShapeDtypeStruct((N_TOK, D_MODEL), F32),
        grid=(N_TOK // MRG_TM,),
        in_specs=[
            part(0), part(1), part(2), part(8), part(9), gate(0), gate(1), gate(2), gate(3),
            pl.BlockSpec((S5_BLK, MRG_TM // S5_T, S5_CW), lambda i: (0, i, 0)),
            pl.BlockSpec((MRG_TM, D_BRANCH), lambda i: (i, 0)),
            pl.BlockSpec((MRG_TM, D_BRANCH), lambda i: (i, 0)),
            pl.BlockSpec((MRG_TM, D_MODEL), lambda i: (i, 0)),
            _mod_spec(layer, MRG_TM),
            per_layer(3, D_BRANCH), per_layer(D_BRANCH, D_BRANCH, single=True), per_layer(1, D_BRANCH),
            per_layer(1, HG_DK), per_layer(len(POOL_WINDOWS), POOL_GROUP, POOL_GROUP, single=True),
            per_layer(1, D_BRANCH), per_layer(N_BRANCH, D_BRANCH, D_MODEL, single=True),
            per_layer(D_MODEL, D_MODEL, single=True), _ln_spec(layer, 1), _ln_spec(layer, 1),
        ],
        out_specs=pl.BlockSpec((MRG_TM, D_MODEL), lambda i: (i, 0)),
        scratch_shapes=[
            pltpu.VMEM((S5_BLK, MRG_TM, LANES), F32),
            pltpu.VMEM((D_BRANCH, D_BRANCH), BF16),
            pltpu.VMEM((len(POOL_WINDOWS), POOL_GROUP, POOL_GROUP), BF16),
            pltpu.VMEM((N_BRANCH, D_BRANCH, D_MODEL), BF16),
            pltpu.VMEM((D_MODEL, D_MODEL), BF16),
        ],
        compiler_params=_cparams(("arbitrary",)),
        name="merge",
    )(proj, proj, proj, proj, proj, proj, proj, proj, proj, y_rows, o_f, o_b, x, mod,
      conv_w, glu_w, glu_b.reshape(DEPTH, 1, D_BRANCH), norm_g.reshape(DEPTH, 1, HG_DK), pool_w,
      pool_scale.reshape(DEPTH, 1, D_BRANCH), w_branch, w_out, ln_g, ln_b)


def _grid_pos_embedding():
    rows = LAT_LEN // GRID_W
    quarter = D_MODEL // 4
    omega = POS_BASE ** (-jnp.arange(quarter, dtype=F32) / quarter)
    ar = jnp.arange(rows, dtype=F32)[:, None] * omega
    ac = jnp.arange(GRID_W, dtype=F32)[:, None] * omega
    row_tab = jnp.concatenate([jnp.sin(ar), jnp.cos(ar)], -1)[:, None, :]
    col_tab = jnp.concatenate([jnp.sin(ac), jnp.cos(ac)], -1)[None, :, :]
    shape = (rows, GRID_W, 2 * quarter)
    pos = jnp.concatenate([jnp.broadcast_to(row_tab, shape), jnp.broadcast_to(col_tab, shape)], -1)
    return pos.reshape(LAT_LEN, D_MODEL)


def kernel(x_prompt, x_sample, state_s5_re, state_s5_im, state_hgrn, c, c_ctx, w_ada, b_ada, ln_g, ln_b,
           ffn_w1, ffn_w3, ffn_w2, w_in, sc_conv, s5_lam_re, s5_lam_im, s5_log_dt, s5_b_re, s5_b_im,
           s5_c_re, s5_c_im, s5_d, s5_glu_w, s5_glu_b, hg_lb, hg_norm_g, pool_w, pool_scale, w_branch, w_out):
    x = (x_prompt.reshape(N_CTX_TOK, D_MODEL), x_sample.reshape(N_LAT_TOK, D_MODEL), _grid_pos_embedding())

    cond = jnp.zeros((COND_PAD, D_MODEL), F32).at[0].set(c_ctx).at[1:N_COND].set(c)
    mod = _ada(cond, w_ada, b_ada).reshape(DEPTH, COND_PAD, N_SUB * 3, D_MODEL)
    ln_g4 = ln_g.reshape(DEPTH, N_SUB, 1, D_MODEL)
    ln_b4 = ln_b.reshape(DEPTH, N_SUB, 1, D_MODEL)

    s5_m, s5_win, s5_wout, s5_pow = _s5_tables(*_s5_param_layout(
        s5_lam_re, s5_lam_im, s5_log_dt, s5_b_re, s5_b_im, s5_c_re, s5_c_im, s5_d))

    ffn_w1, ffn_w3 = _cast_bf16(ffn_w1, ffn_w3)
    (ffn_w2,) = _cast_bf16(ffn_w2)

    ae_np, mask_np, pair_np = _hg_constants()
    ae = jnp.asarray(ae_np, BF16)
    mask = jnp.asarray(mask_np, F32)
    pair_diag = jnp.asarray(pair_np, F32)

    fin_s5, fin_hg = [], []
    for l in range(DEPTH):
        x = _ffn(x, mod, ffn_w1, ffn_w3, ffn_w2, ln_g4, ln_b4, l, 0)
        proj, fz, u_rows = _inproj(x, mod, w_in, l)

        h0 = jnp.concatenate([
            state_s5_re[:, l].reshape(N_LAT_SEQ, 2, S5_BLK, S5_SW),
            state_s5_im[:, l].reshape(N_LAT_SEQ, 2, S5_BLK, S5_SW)], axis=-1).transpose(1, 2, 0, 3)
        y_rows, fin = _s5_main(u_rows, s5_m, s5_win, s5_wout, s5_pow, h0, l)
        fin_s5.append(fin)

        h0t = jnp.concatenate([jnp.zeros((1, 2, HG_HEADS, HG_DK, HG_DK), F32),
                               jnp.swapaxes(state_hgrn[:, l], -1, -2)], axis=0)
        o_f, o_b, fin_f, fin_b = _hgrn(proj, fz, hg_lb, ae, mask, pair_diag, h0t, l)
        fin_hg.append(jnp.stack([fin_f, fin_b], axis=1))

        x = _merge(proj, y_rows, o_f, o_b, x, mod, sc_conv, s5_glu_w, s5_glu_b, hg_norm_g,
                   pool_w, pool_scale, w_branch, w_out, ln_g4, ln_b4, l)
        x = _ffn(x, mod, ffn_w1, ffn_w3, ffn_w2, ln_g4, ln_b4, l, 2)

    y_prompt = x[:N_CTX_TOK].reshape(N_CTX_SEQ, CTX_LEN, D_MODEL)
    y_sample = x[N_CTX_TOK:].reshape(N_LAT_SEQ, LAT_LEN, D_MODEL)
    fin = jnp.stack(fin_s5)
    fin = fin.reshape(DEPTH, 2, S5_BLK, N_CTX_SEQ, 2, S5_GPB, S5_STATE)
    fin = fin.transpose(4, 3, 0, 1, 2, 5, 6).reshape(2, N_CTX_SEQ, DEPTH, 2, S5_GROUPS, S5_STATE)
    new_hgrn = jnp.swapaxes(jnp.stack(fin_hg, axis=1), -1, -2)
    return y_prompt, y_sample, fin[0], fin[1], new_hgrn
```

```python
import functools

import numpy as np
import jax
import jax.numpy as jnp
from jax import lax
from jax.experimental import pallas as pl
from jax.experimental.pallas import tpu as pltpu

F32 = jnp.float32
BF16 = jnp.bfloat16

D_MODEL = 1024
N_CTX_SEQ = 16
CTX_LEN = 256
DEPTH = 2
N_LAT_SEQ = 2
LAT_LEN = 4096
GRID_W = 64
D_BRANCH = 512
N_BRANCH = 4
S5_GROUPS = 32
S5_GROUP = 16
S5_STATE = 64
HG_HEADS = 4
HG_DK = 128
POOL_WINDOWS = (2, 4, 8, 16)
POOL_GROUP = 128
D_FF = 2816
N_SUB = 3
N_IN_PARTS = 10
IN_WIDTH = N_IN_PARTS * D_BRANCH + N_BRANCH * D_MODEL
ADA_WIDTH = N_SUB * 3 * D_MODEL
ALPHA = (2 * DEPTH) ** 0.25
LN_EPS = 1e-5
POS_BASE = 10000.0

N_CTX_TOK = N_CTX_SEQ * CTX_LEN
N_LAT_TOK = N_LAT_SEQ * LAT_LEN
N_TOK = N_CTX_TOK + N_LAT_TOK
N_COND = 1 + N_LAT_SEQ
COND_PAD = 8

LANES = 128
SUBLANES = 8
VMEM_LIMIT = 56 * 1024 * 1024

S5_T = 8
S5_BLK = 4
S5_GPB = S5_GROUPS // S5_BLK
S5_SW = S5_GPB * S5_STATE
S5_CW = S5_T * LANES
CTX_CHUNKS = CTX_LEN // S5_T
LAT_CHUNKS = LAT_LEN // S5_T
S5_ROWS_CTX = N_CTX_TOK // S5_T
S5_ROWS_LAT = N_LAT_TOK // S5_T
S5_ROWS = S5_ROWS_CTX + S5_ROWS_LAT
S5_LAT_GROUPS = LAT_CHUNKS // SUBLANES
S5_POW_ROWS = 24
S5_U_COL = 3

HG_C = 128
HG_LEVELS = 7
HG_TILES = N_TOK // HG_C
HG_CTX_TILES = N_CTX_TOK // HG_C
HG_TILES_PER_CTX = CTX_LEN // HG_C
HG_TILES_PER_LAT = LAT_LEN // HG_C
HG_E_ROWS = (HG_LEVELS + 1) * HG_C + 8


def _cparams(sem):
    return pltpu.CompilerParams(dimension_semantics=sem, vmem_limit_bytes=VMEM_LIMIT)


def _cond_row(tile, tile_tokens):
    tok = tile * tile_tokens
    return jnp.where(tok < N_CTX_TOK, 0, 1 + (tok - N_CTX_TOK) // LAT_LEN)


def _dot(a, b):
    return jnp.dot(a, b, preferred_element_type=F32)


def _dot_nt(a, b):
    return lax.dot_general(a, b, (((1,), (1,)), ((), ())), preferred_element_type=F32)


def _dot_tn(a, b):
    return lax.dot_general(a, b, (((0,), (0,)), ((), ())), preferred_element_type=F32)


def _split3(x):
    h1 = x.astype(BF16)
    r1 = x - h1.astype(F32)
    h2 = r1.astype(BF16)
    h3 = (r1 - h2.astype(F32)).astype(BF16)
    return h1, h2, h3


def _dot_nt_hi(a3, b3):
    acc = None
    for x in range(2):
        for y in range(2 - x):
            term = _dot_nt(a3[x], b3[y])
            acc = term if acc is None else acc + term
    return acc


def _silu(x):
    return x * jax.nn.sigmoid(x)


def _sigmoid_tanh(x):
    return 0.5 * jnp.tanh(0.5 * x) + 0.5


def _layer_norm(y, g, b):
    mu = jnp.mean(y, axis=-1, keepdims=True)
    yc = y - mu
    var = jnp.mean(yc * yc, axis=-1, keepdims=True)
    return yc * lax.rsqrt(var + LN_EPS) * g + b


def _mod_spec(layer, tile_tokens):
    return pl.BlockSpec((1, 1, N_SUB * 3, D_MODEL),
                        lambda i, *_: (layer, _cond_row(i, tile_tokens), 0, 0))


def _ln_spec(layer, sub):
    return pl.BlockSpec((1, 1, 1, D_MODEL), lambda *_: (layer, sub, 0, 0))


ADA_TN = 1152


def _ada_kernel(c_ref, w_ref, b_ref, o_ref):
    s = _silu(c_ref[...]).astype(BF16)
    o_ref[0] = _dot(s, w_ref[0].astype(BF16)) + b_ref[0]


def _ada(cond, w_ada, b_ada):
    return pl.pallas_call(
        _ada_kernel,
        out_shape=jax.ShapeDtypeStruct((DEPTH, COND_PAD, ADA_WIDTH), F32),
        grid=(DEPTH, ADA_WIDTH // ADA_TN),
        in_specs=[
            pl.BlockSpec((COND_PAD, D_MODEL), lambda l, j: (0, 0)),
            pl.BlockSpec((1, D_MODEL, ADA_TN), lambda l, j: (l, 0, j)),
            pl.BlockSpec((1, 1, ADA_TN), lambda l, j: (l, 0, j)),
        ],
        out_specs=pl.BlockSpec((1, COND_PAD, ADA_TN), lambda l, j: (l, 0, j)),
        compiler_params=_cparams(("parallel", "parallel")),
        name="ada",
    )(cond, w_ada, b_ada.reshape(DEPTH, 1, ADA_WIDTH))


CAST_ROWS = 512


def _cast_kernel(*refs):
    n = len(refs) // 2
    for src, dst in zip(refs[:n], refs[n:]):
        dst[...] = src[...].astype(BF16)


def _cast_bf16(*ws):
    shape = ws[0].shape
    rows, cols = int(np.prod(shape[:-1])), shape[-1]
    spec = pl.BlockSpec((CAST_ROWS, cols), lambda i: (i, 0))
    outs = pl.pallas_call(
        _cast_kernel,
        out_shape=tuple(jax.ShapeDtypeStruct((rows, cols), BF16) for _ in ws),
        grid=(rows // CAST_ROWS,),
        in_specs=[spec] * len(ws),
        out_specs=tuple([spec] * len(ws)),
        compiler_params=_cparams(("parallel",)),
        name="cast_bf16",
    )(*(w.reshape(rows, cols) for w in ws))
    return tuple(o.reshape(shape) for o in outs)


FFN_TM = 1024
FFN_TM_SPLIT = 512
FFN_TC = 256


def _ffn_kernel(*refs, sub, tm, split_input, split_output):
    ctx_tiles = N_CTX_TOK // tm
    is_ctx = pl.program_id(0) < ctx_tiles
    if split_input:
        ctx_ref, lat_ref, pos_ref = refs[:3]
        refs = refs[3:]
        x = jnp.where(is_ctx, ctx_ref[...], lat_ref[...] + pos_ref[...])
    else:
        x = refs[0][...]
        refs = refs[1:]
    mod_ref, w1_ref, w3_ref, w2_ref, g_ref, b_ref = refs[:6]
    out_refs, act_sc = refs[6:-1], refs[-1]
    shift = mod_ref[0, 0, 3 * sub:3 * sub + 1, :]
    scale = mod_ref[0, 0, 3 * sub + 1:3 * sub + 2, :]
    gate = mod_ref[0, 0, 3 * sub + 2:3 * sub + 3, :]
    h = (x * (1.0 + scale) + shift).astype(BF16)
    for c in range(D_FF // FFN_TC):
        cols = slice(c * FFN_TC, (c + 1) * FFN_TC)
        a = _dot(h, w1_ref[0, 0, :, cols])
        b = _dot(h, w3_ref[0, 0, :, cols])
        act_sc[:, cols] = (_silu(a) * b).astype(BF16)
    f = _dot(act_sc[...], w2_ref[0, 0])
    y = _layer_norm(ALPHA * x + gate * (0.5 * f), g_ref[0, 0], b_ref[0, 0])
    if split_output:
        @pl.when(is_ctx)
        def _():
            out_refs[0][...] = y

        @pl.when(jnp.logical_not(is_ctx))
        def _():
            out_refs[1][...] = y
    else:
        out_refs[0][...] = y


def _ffn(x, mod, w1, w3, w2, ln_g, ln_b, layer, sub, split_output=False):
    which = sub // 2
    split_input = isinstance(x, tuple)
    tm = FFN_TM_SPLIT if split_input else FFN_TM
    ctx_tiles = N_CTX_TOK // tm

    def resident(*shape):
        return pl.BlockSpec((1, 1) + shape, lambda i: (layer, which, 0, 0), pipeline_mode=pl.Buffered(1))

    def ctx_tile(i):
        return (jnp.minimum(i, ctx_tiles - 1), 0)

    def lat_tile(i):
        return (jnp.maximum(i - ctx_tiles, 0), 0)

    if split_input:
        pos_tiles = LAT_LEN // tm
        x_specs = [
            pl.BlockSpec((tm, D_MODEL), ctx_tile), pl.BlockSpec((tm, D_MODEL), lat_tile),
            pl.BlockSpec((tm, D_MODEL), lambda i: (jnp.maximum(i - ctx_tiles, 0) % pos_tiles, 0)),
        ]
    else:
        x_specs = [pl.BlockSpec((tm, D_MODEL), lambda i: (i, 0))]
        x = (x,)
    if split_output:
        out_shape = (jax.ShapeDtypeStruct((N_CTX_TOK, D_MODEL), F32), jax.ShapeDtypeStruct((N_LAT_TOK, D_MODEL), F32))
        out_specs = (pl.BlockSpec((tm, D_MODEL), ctx_tile), pl.BlockSpec((tm, D_MODEL), lat_tile))
    else:
        out_shape = jax.ShapeDtypeStruct((N_TOK, D_MODEL), F32)
        out_specs = pl.BlockSpec((tm, D_MODEL), lambda i: (i, 0))

    return pl.pallas_call(
        functools.partial(_ffn_kernel, sub=sub, tm=tm, split_input=split_input, split_output=split_output),
        out_shape=out_shape,
        grid=(N_TOK // tm,),
        in_specs=x_specs + [
            _mod_spec(layer, tm),
            resident(D_MODEL, D_FF), resident(D_MODEL, D_FF), resident(D_FF, D_MODEL),
            _ln_spec(layer, sub), _ln_spec(layer, sub),
        ],
        out_specs=out_specs,
        scratch_shapes=[pltpu.VMEM((tm, D_FF), BF16)],
        compiler_params=_cparams(("arbitrary",) if split_output else ("parallel",)),
        name=f"ffn{sub}",
    )(*x, mod, w1, w3, w2, ln_g, ln_b)


INP_TM = 2048
INP_TN = D_BRANCH
HG_Q_COL, HG_F_COL, HG_V_COL = 4, 5, 7


def _inproj_kernel(x_ref, mod_ref, w_ref, o_ref, fz_ref, u_ref, h_sc, blk_sc):
    j = pl.program_id(1)

    @pl.when(j == 0)
    def _():
        shift = mod_ref[0, 0, 3:4, :]
        scale = mod_ref[0, 0, 4:5, :]
        h_sc[...] = (x_ref[...] * (1.0 + scale) + shift).astype(BF16)

    def project():
        res = _dot(h_sc[...], w_ref[0].astype(BF16))
        o_ref[...] = res.astype(BF16)
        return res

    is_f = (j == HG_F_COL) | (j == HG_F_COL + 1)
    is_u = j == S5_U_COL

    @pl.when(jnp.logical_not(is_f | is_u))
    def _():
        project()

    @pl.when(is_f)
    def _():
        fz_ref[...] = project()

    @pl.when(is_u)
    def _():
        s5_u = project()
        for b in range(S5_BLK):
            blk_sc[b] = s5_u[:, b * LANES:(b + 1) * LANES]
            for t in range(S5_T):
                rows = blk_sc[b, pl.ds(t, INP_TM // S5_T, stride=S5_T), :]
                u_ref[b, :, t * LANES:(t + 1) * LANES] = rows.astype(BF16)


def _inproj(x, mod, w_in, layer):
    return pl.pallas_call(
        _inproj_kernel,
        out_shape=(jax.ShapeDtypeStruct((N_TOK, IN_WIDTH), BF16),
                   jax.ShapeDtypeStruct((N_TOK, 2 * D_BRANCH), F32),
                   jax.ShapeDtypeStruct((S5_BLK, S5_ROWS, S5_CW), BF16)),
        grid=(N_TOK // INP_TM, IN_WIDTH // INP_TN),
        in_specs=[
            pl.BlockSpec((INP_TM, D_MODEL), lambda i, j: (i, 0), pipeline_mode=pl.Buffered(1)),
            _mod_spec(layer, INP_TM),
            pl.BlockSpec((1, D_MODEL, INP_TN), lambda i, j: (layer, 0, j)),
        ],
        out_specs=(pl.BlockSpec((INP_TM, INP_TN), lambda i, j: (i, j)),
                   pl.BlockSpec((INP_TM, D_BRANCH),
                                lambda i, j: (i, jnp.where(j > HG_F_COL, 1, 0))),
                   pl.BlockSpec((S5_BLK, INP_TM // S5_T, S5_CW), lambda i, j: (0, i, 0))),
        scratch_shapes=[pltpu.VMEM((INP_TM, D_MODEL), BF16), pltpu.VMEM((S5_BLK, INP_TM, LANES), F32)],
        compiler_params=_cparams(("parallel", "arbitrary")),
        name="inproj",
    )(x, mod, w_in)


def _s5_tab_kernel(lam_ref, bre_ref, bim_ref, cre_ref, cim_ref, dsk_ref, m_ref, win_ref, wout_ref, pow_ref):
    d = pl.program_id(1)
    fwd = d == 0
    lam_re = lam_ref[0, 0, 0, 0:1, :]
    lam_im = lam_ref[0, 0, 0, 1:2, :]
    dt = jnp.exp(lam_ref[0, 0, 0, 2:3, :])

    def apow(j):
        mag = jnp.exp(lam_re * dt * float(j))
        ang = lam_im * dt * float(j)
        return mag * jnp.cos(ang), mag * jnp.sin(ang)

    a_re, a_im = apow(1)
    den = lam_re * lam_re + lam_im * lam_im
    num_re = a_re - 1.0
    coef_re = (num_re * lam_re + a_im * lam_im) / den
    coef_im = (a_im * lam_re - num_re * lam_im) / den
    b_re = bre_ref[0, 0, 0]
    b_im = bim_ref[0, 0, 0]
    bb_re = coef_re * b_re - coef_im * b_im
    bb_im = coef_re * b_im + coef_im * b_re
    bbr = _split3(bb_re)
    bbi = _split3(bb_im)
    c_re = cre_ref[0, 0, 0]
    c_im = cim_ref[0, 0, 0]

    kmat = []
    for j in range(S5_T + 1):
        aj_re, aj_im = apow(j)
        pc_re = c_re * aj_re - c_im * aj_im
        pc_im = c_re * aj_im + c_im * aj_re
        if j >= 1:
            r = pl.multiple_of(jnp.where(fwd, j - 1, S5_T - j) * LANES, LANES)
            wout_ref[0, 0, 0, pl.ds(r, LANES), :] = jnp.concatenate([pc_re, -pc_im], axis=1).astype(BF16)
        if j < S5_T:
            pb_re = bb_re * aj_re - bb_im * aj_im
            pb_im = bb_re * aj_im + bb_im * aj_re
            r = pl.multiple_of(jnp.where(fwd, S5_T - 1 - j, j) * LANES, LANES)
            win_ref[0, 0, 0, pl.ds(r, LANES), :] = jnp.concatenate([pb_re, pb_im], axis=1).astype(BF16)
            kmat.append(_dot_nt_hi(bbr, _split3(pc_re)) - _dot_nt_hi(bbi, _split3(pc_im)))

    ri = lax.broadcasted_iota(jnp.int32, (LANES, LANES), 0)
    ci = lax.broadcasted_iota(jnp.int32, (LANES, LANES), 1)
    keep_upper = jnp.where(fwd, 1.0, 0.0)
    kmat[0] = kmat[0] + jnp.where(ri == ci, dsk_ref[0, 0], 0.0) * keep_upper
    for s in range(S5_T):
        for t in range(S5_T):
            tile = kmat[abs(t - s)]
            if t > s:
                tile = tile * keep_upper
            elif t < s:
                tile = tile * (1.0 - keep_upper)
            m_ref[0, 0, 0, s * LANES:(s + 1) * LANES, t * LANES:(t + 1) * LANES] = tile.astype(BF16)

    pow_ref[...] = jnp.zeros_like(pow_ref)
    for i in range(SUBLANES):
        for row, j in ((i, S5_T * (i + 1)), (SUBLANES + i, S5_T * (SUBLANES - i))):
            p_re, p_im = apow(j)
            pow_ref[0, 0, 0, 0, row:row + 1, :] = p_re
            pow_ref[0, 0, 0, 1, row:row + 1, :] = p_im
    p_re, p_im = apow(S5_T * 2 * SUBLANES)
    pow_ref[0, 0, 0, 0, 2 * SUBLANES:2 * SUBLANES + 1, :] = p_re
    pow_ref[0, 0, 0, 1, 2 * SUBLANES:2 * SUBLANES + 1, :] = p_im


def _s5_tables(lam, b_re, b_im, c_re, c_im, d_skip):
    lead = (DEPTH, 2, S5_BLK)

    def spec(*tail):
        return pl.BlockSpec((1, 1, 1) + tail, lambda l, d, b: (l, d, b) + (0,) * len(tail))

    return pl.pallas_call(
        _s5_tab_kernel,
        out_shape=(
            jax.ShapeDtypeStruct(lead + (S5_CW, S5_CW), BF16),
            jax.ShapeDtypeStruct(lead + (S5_CW, 2 * S5_SW), BF16),
            jax.ShapeDtypeStruct(lead + (S5_CW, 2 * S5_SW), BF16),
            jax.ShapeDtypeStruct(lead + (2, S5_POW_ROWS, S5_SW), F32),
        ),
        grid=lead,
        in_specs=[spec(3, S5_SW)] + [spec(LANES, S5_SW)] * 4
        + [pl.BlockSpec((1, 1, 1, LANES), lambda l, d, b: (l, b, 0, 0))],
        out_specs=(spec(S5_CW, S5_CW), spec(S5_CW, 2 * S5_SW), spec(S5_CW, 2 * S5_SW),
                   spec(2, S5_POW_ROWS, S5_SW)),
        compiler_params=_cparams(("parallel", "parallel", "parallel")),
        name="s5_tables",
    )(lam, b_re, b_im, c_re, c_im, d_skip)


def _s5_param_layout(s5_lam_re, s5_lam_im, s5_log_dt, s5_b_re, s5_b_im, s5_c_re, s5_c_im, s5_d):
    eye = jnp.eye(S5_GPB, dtype=F32)
    lam = jnp.stack([s5_lam_re, s5_lam_im, jnp.broadcast_to(s5_log_dt[..., None], s5_lam_re.shape)], axis=2)
    lam = lam.reshape(DEPTH, 2, 3, S5_BLK, S5_SW).transpose(0, 1, 3, 2, 4)

    def bdiag_b(b):
        b = b.reshape(DEPTH, 2, S5_BLK, S5_GPB, S5_STATE, S5_GROUP)
        return jnp.einsum('ldbgph,gq->ldbghqp', b, eye).reshape(DEPTH, 2, S5_BLK, LANES, S5_SW)

    def bdiag_c(c):
        c = c.reshape(DEPTH, 2, S5_BLK, S5_GPB, S5_GROUP, S5_STATE)
        return jnp.einsum('ldbghp,gq->ldbghqp', c, eye).reshape(DEPTH, 2, S5_BLK, LANES, S5_SW)

    return (lam, bdiag_b(s5_b_re), bdiag_b(s5_b_im), bdiag_c(s5_c_re), bdiag_c(s5_c_im),
            s5_d.reshape(DEPTH, S5_BLK, 1, LANES))


def _s5_kernel(u_ref, m_ref, win_ref, wout_ref, pow_ref, h0_ref, y_ref, fin_ref, x_sc):
    d = pl.program_id(1)
    u = u_ref[0]
    x_sc[...] = _dot(u, win_ref[0, 0, 0])
    p_re = pow_ref[0, 0, 0, 0]
    p_im = pow_ref[0, 0, 0, 1]
    re = slice(0, S5_SW)
    im = slice(S5_SW, 2 * S5_SW)

    def madd(t_re, t_im, w_re, w_im, s_re, s_im):
        return t_re + w_re * s_re - w_im * s_im, t_im + w_re * s_im + w_im * s_re

    def shift_rows(val, k, n, idx, reverse):
        if reverse:
            return jnp.where(idx < n - k, pltpu.roll(val, val.shape[0] - k, axis=0), 0.0)
        return jnp.where(idx >= k, pltpu.roll(val, k, axis=0), 0.0)

    def pow_row(k):
        row = k - 1 if k <= SUBLANES else 2 * SUBLANES
        return p_re[row:row + 1], p_im[row:row + 1]

    def ctx_scan(reverse):
        s_re = x_sc[0:S5_ROWS_CTX, re]
        s_im = x_sc[0:S5_ROWS_CTX, im]
        c = lax.broadcasted_iota(jnp.int32, (S5_ROWS_CTX, 1), 0) & (CTX_CHUNKS - 1)
        def shift_ctx(val, k):
            if k % SUBLANES:
                return shift_rows(val, k, CTX_CHUNKS, c, reverse)
            v3 = val.reshape(N_CTX_SEQ, CTX_CHUNKS, S5_SW)
            pad = jnp.zeros((N_CTX_SEQ, k, S5_SW), F32)
            moved = (jnp.concatenate([v3[:, k:], pad], axis=1) if reverse
                     else jnp.concatenate([pad, v3[:, :CTX_CHUNKS - k]], axis=1))
            return moved.reshape(S5_ROWS_CTX, S5_SW)

        k = 1
        while k < CTX_CHUNKS:
            w_re, w_im = pow_row(k)
            s_re, s_im = madd(s_re, s_im, w_re, w_im, shift_ctx(s_re, k), shift_ctx(s_im, k))
            k *= 2
        last = 0 if reverse else CTX_CHUNKS - 1
        pick = (lax.broadcasted_iota(jnp.int32, (N_CTX_SEQ, S5_ROWS_CTX), 1)
                == lax.broadcasted_iota(jnp.int32, (N_CTX_SEQ, S5_ROWS_CTX), 0) * CTX_CHUNKS + last)
        pick = jnp.where(pick, 1.0, 0.0).astype(BF16)
        fin_ref[0, 0, :, re] = sum(_dot(pick, piece) for piece in _split3(s_re))
        fin_ref[0, 0, :, im] = sum(_dot(pick, piece) for piece in _split3(s_im))
        x_sc[0:S5_ROWS_CTX, re] = shift_rows(s_re, 1, CTX_CHUNKS, c, reverse)
        x_sc[0:S5_ROWS_CTX, im] = shift_rows(s_im, 1, CTX_CHUNKS, c, reverse)

    def lat_group(k, carry, reverse):
        g = (S5_LAT_GROUPS - 1 - k) if reverse else k
        r8 = lax.broadcasted_iota(jnp.int32, (SUBLANES, 1), 0)
        tab = slice(SUBLANES, 2 * SUBLANES) if reverse else slice(0, SUBLANES)
        out = []
        for q in range(N_LAT_SEQ):
            base = pl.multiple_of(S5_ROWS_CTX + q * LAT_CHUNKS + g * SUBLANES, SUBLANES)
            s_re = x_sc[pl.ds(base, SUBLANES), re]
            s_im = x_sc[pl.ds(base, SUBLANES), im]
            j = 1
            while j < SUBLANES:
                w_re, w_im = pow_row(j)
                s_re, s_im = madd(s_re, s_im, w_re, w_im,
                                  shift_rows(s_re, j, SUBLANES, r8, reverse), shift_rows(s_im, j, SUBLANES, r8, reverse))
                j *= 2
            c_re, c_im = carry[2 * q], carry[2 * q + 1]
            s_re, s_im = madd(s_re, s_im, p_re[tab], p_im[tab], c_re, c_im)
            edge = (r8 == SUBLANES - 1) if reverse else (r8 == 0)
            x_sc[pl.ds(base, SUBLANES), re] = jnp.where(edge, c_re, shift_rows(s_re, 1, SUBLANES, r8, reverse))
            x_sc[pl.ds(base, SUBLANES), im] = jnp.where(edge, c_im, shift_rows(s_im, 1, SUBLANES, r8, reverse))
            end = slice(0, 1) if reverse else slice(SUBLANES - 1, SUBLANES)
            out += [s_re[end], s_im[end]]
        return tuple(out)

    def run(reverse):
        ctx_scan(reverse)
        h0 = h0_ref[0, 0]
        init = []
        for q in range(N_LAT_SEQ):
            init += [h0[q:q + 1, re], h0[q:q + 1, im]]
        lax.fori_loop(0, S5_LAT_GROUPS, functools.partial(lat_group, reverse=reverse), tuple(init))

    @pl.when(d == 0)
    def _():
        run(False)

    @pl.when(d == 1)
    def _():
        run(True)

    y = _dot(u, m_ref[0, 0, 0]) + _dot_nt(x_sc[...].astype(BF16), wout_ref[0, 0, 0])

    @pl.when(d == 0)
    def _():
        y_ref[0] = y

    @pl.when(d == 1)
    def _():
        y_ref[0] += y


def _s5_main(u, m, w_in, w_out, pows, h0, layer):
    def tab(*tail):
        return pl.BlockSpec((1, 1, 1) + tail, lambda b, d: (layer, d, b) + (0,) * len(tail))

    return pl.pallas_call(
        _s5_kernel,
        out_shape=(jax.ShapeDtypeStruct((S5_BLK, S5_ROWS, S5_CW), F32),
                   jax.ShapeDtypeStruct((2, S5_BLK, N_CTX_SEQ, 2 * S5_SW), F32)),
        grid=(S5_BLK, 2),
        in_specs=[
            pl.BlockSpec((1, S5_ROWS, S5_CW), lambda b, d: (b, 0, 0)),
            tab(S5_CW, S5_CW), tab(S5_CW, 2 * S5_SW), tab(S5_CW, 2 * S5_SW), tab(2, S5_POW_ROWS, S5_SW),
            pl.BlockSpec((1, 1, N_LAT_SEQ, 2 * S5_SW), lambda b, d: (d, b, 0, 0)),
        ],
        out_specs=(
            pl.BlockSpec((1, S5_ROWS, S5_CW), lambda b, d: (b, 0, 0)),
            pl.BlockSpec((1, 1, N_CTX_SEQ, 2 * S5_SW), lambda b, d: (d, b, 0, 0)),
        ),
        scratch_shapes=[pltpu.VMEM((S5_ROWS, 2 * S5_SW), F32)],
        compiler_params=_cparams(("parallel", "arbitrary")),
        name="s5",
    )(u, m, w_in, w_out, pows, h0)


def _hg_constants():
    t = np.arange(HG_C)
    ae = np.zeros((2, HG_E_ROWS, HG_C), np.float32)
    mask = np.zeros((2, HG_LEVELS, HG_C, HG_C), np.float32)
    for lvl in range(HG_LEVELS):
        half = 1 << lvl
        pos = t % (2 * half)
        mid = t - pos + half - 1
        upper = pos >= half
        u = t[None, :]
        rows_upper = (u > mid[:, None]) & (u <= t[:, None])
        rows_lower = (u > t[:, None]) & (u <= mid[:, None])
        ae[0, lvl * HG_C:(lvl + 1) * HG_C] = np.where(upper[:, None], rows_upper, rows_lower)
        same = (t[:, None] // (2 * half)) == (t[None, :] // (2 * half))
        mask[0, lvl] = same & upper[:, None] & (~upper)[None, :]
    ae[0, HG_LEVELS * HG_C:(HG_LEVELS + 1) * HG_C] = t[None, :] <= t[:, None]
    ae[0, (HG_LEVELS + 1) * HG_C:] = 1.0
    ae[1] = ae[0][:, ::-1]
    ae[1, :(HG_LEVELS + 1) * HG_C] = ae[1, :(HG_LEVELS + 1) * HG_C].reshape(HG_LEVELS + 1, HG_C, HG_C)[:, ::-1].reshape(-1, HG_C)
    mask[1] = mask[0][:, ::-1, ::-1]
    ae2 = np.concatenate([ae, ae], axis=-1)
    mask2 = np.concatenate([mask, mask], axis=-1)
    pair_diag = np.kron(np.eye(2, dtype=np.float32), np.ones((HG_DK, HG_DK), np.float32))
    return ae2, mask2, pair_diag


def _hg_tile(direction, i):
    return i if direction == 0 else HG_TILES - 1 - i


HG_PAIR = 2 * HG_DK


def _pair_diag(x):
    zero = jnp.zeros((x.shape[0], HG_DK), x.dtype)
    return jnp.concatenate([jnp.concatenate([x[:, :HG_DK], zero], axis=1),
                            jnp.concatenate([zero, x[:, HG_DK:]], axis=1)], axis=0)


def _hg_direction(direction, q_ref, f_ref, v_ref, lb, ae_ref, mask_ref, pd_ref, o_ref, st_sc):
    f = lb + (1.0 - lb) * jax.nn.sigmoid(f_ref[...])
    logf = jnp.log2(f)
    kk = 1.0 - f
    hi = logf.astype(BF16)
    lo = (logf - hi.astype(F32)).astype(BF16)
    ex = _dot(ae_ref[direction], jnp.concatenate([hi, lo], axis=0))
    dec = jnp.exp2(ex[:(HG_LEVELS + 1) * HG_C])
    cum = ex[HG_LEVELS * HG_C:(HG_LEVELS + 1) * HG_C]
    tot = ex[(HG_LEVELS + 1) * HG_C:(HG_LEVELS + 1) * HG_C + 1]
    q = q_ref[...].astype(F32)
    v_bf = v_ref[...]
    v = v_bf.astype(F32)
    q_in = (q * dec[HG_LEVELS * HG_C:]).astype(BF16)
    k_tail = (kk * jnp.exp2(tot - cum)).astype(BF16)
    dec_tot = jnp.exp2(tot)
    qk = q * kk
    outs = []
    for p in range(HG_HEADS // 2):
        sl = slice(p * HG_PAIR, (p + 1) * HG_PAIR)
        scores = jnp.zeros((HG_C, HG_PAIR), F32)
        for lvl in range(HG_LEVELS):
            g = dec[lvl * HG_C:(lvl + 1) * HG_C, sl]
            scores = scores + mask_ref[direction, lvl] * _dot_nt(
                (q[:, sl] * g).astype(BF16), _pair_diag((kk[:, sl] * g).astype(BF16)))
        diag = jnp.concatenate(
            [jnp.broadcast_to(jnp.sum(qk[:, (2 * p + j) * HG_DK:(2 * p + j + 1) * HG_DK], axis=-1, keepdims=True),
                              (HG_C, HG_DK)) for j in range(2)], axis=-1)
        st = st_sc[direction, p]
        outs.append(_dot(scores.astype(BF16), _pair_diag(v_bf[:, sl])) + diag * v[:, sl]
                    + _dot_nt(q_in[:, sl], st.astype(BF16)))
        st_new = st * dec_tot[:, sl] + pd_ref[...] * _dot_tn(v_bf[:, sl], k_tail[:, sl])
        st_sc[direction, p] = st_new
    o_ref[...] = jnp.concatenate(outs, axis=-1)


def _hg_kernel(qf_ref, ff_ref, vf_ref, qb_ref, fb_ref, vb_ref, lb_ref, ae_ref, mask_ref, pd_ref, h0f_ref, h0b_ref,
               of_ref, ob_ref, finf_ref, finb_ref, st_sc, *, layer):
    i = pl.program_id(0)
    x = lb_ref[:, :, 0, :]
    e = jnp.exp(x - jnp.max(x, axis=0, keepdims=True))
    sm = e / jnp.sum(e, axis=0, keepdims=True)
    lb = jnp.sum(sm[1:layer + 1], axis=0) if layer > 0 else jnp.zeros((2, D_BRANCH), F32)
    for direction, h0_ref in ((0, h0f_ref), (1, h0b_ref)):
        tile = _hg_tile(direction, i)
        first_ctx = 0 if direction == 0 else HG_TILES_PER_CTX - 1
        first_lat = 0 if direction == 0 else HG_TILES_PER_LAT - 1
        is_start = jnp.where(tile < HG_CTX_TILES,
                             tile % HG_TILES_PER_CTX == first_ctx,
                             (tile - HG_CTX_TILES) % HG_TILES_PER_LAT == first_lat)

        @pl.when(is_start)
        def _(direction=direction, h0_ref=h0_ref):
            for p in range(HG_HEADS // 2):
                st_sc[direction, p] = _pair_diag(
                    jnp.concatenate([h0_ref[0, 0, 2 * p], h0_ref[0, 0, 2 * p + 1]], axis=1))

    _hg_direction(0, qf_ref, ff_ref, vf_ref, lb[0:1], ae_ref, mask_ref, pd_ref, of_ref, st_sc)
    _hg_direction(1, qb_ref, fb_ref, vb_ref, lb[1:2], ae_ref, mask_ref, pd_ref, ob_ref, st_sc)

    for direction, fin_ref in ((0, finf_ref), (1, finb_ref)):
        @pl.when(_hg_tile(direction, i) < HG_CTX_TILES)
        def _(direction=direction, fin_ref=fin_ref):
            for h in range(HG_HEADS):
                p, j = divmod(h, 2)
                fin_ref[0, h] = st_sc[direction, p, j * HG_DK:(j + 1) * HG_DK, j * HG_DK:(j + 1) * HG_DK]


def _hgrn(proj, fz, hg_lb, ae, mask, pair_diag, h0t, layer):
    def tok_spec(direction, col):
        return pl.BlockSpec((HG_C, D_BRANCH), lambda i: (_hg_tile(direction, i), col))

    def fin_spec(direction):
        return pl.BlockSpec((1, HG_HEADS, HG_DK, HG_DK),
                            lambda i: (jnp.minimum(_hg_tile(direction, i) // HG_TILES_PER_CTX, N_CTX_SEQ - 1), 0, 0, 0))

    def h0_spec(direction):
        return pl.BlockSpec((1, 1, HG_HEADS, HG_DK, HG_DK),
                            lambda i: (_cond_row(_hg_tile(direction, i), HG_C), direction, 0, 0, 0))

    fin_shape = jax.ShapeDtypeStruct((N_CTX_SEQ, HG_HEADS, HG_DK, HG_DK), F32)
    return pl.pallas_call(
        functools.partial(_hg_kernel, layer=layer),
        out_shape=(jax.ShapeDtypeStruct((N_TOK, D_BRANCH), F32), jax.ShapeDtypeStruct((N_TOK, D_BRANCH), F32),
                   fin_shape, fin_shape),
        grid=(HG_TILES,),
        in_specs=[
            tok_spec(0, HG_Q_COL), tok_spec(0, 0), tok_spec(0, HG_V_COL),
            tok_spec(1, HG_Q_COL), tok_spec(1, 1), tok_spec(1, HG_V_COL),
            pl.BlockSpec((DEPTH, 2, 1, D_BRANCH), lambda i: (0, 0, 0, 0)),
            pl.BlockSpec((2, HG_E_ROWS, 2 * HG_C), lambda i: (0, 0, 0)),
            pl.BlockSpec((2, HG_LEVELS, HG_C, HG_PAIR), lambda i: (0, 0, 0, 0)),
            pl.BlockSpec((HG_PAIR, HG_PAIR), lambda i: (0, 0)),
            h0_spec(0), h0_spec(1),
        ],
        out_specs=(
            pl.BlockSpec((HG_C, D_BRANCH), lambda i: (_hg_tile(0, i), 0)),
            pl.BlockSpec((HG_C, D_BRANCH), lambda i: (_hg_tile(1, i), 0)),
            fin_spec(0), fin_spec(1),
        ),
        scratch_shapes=[pltpu.VMEM((2, HG_HEADS // 2, HG_PAIR, HG_PAIR), F32)],
        compiler_params=_cparams(("arbitrary",)),
        name="hgrn",
    )(proj, fz, proj, proj, fz, proj, hg_lb.reshape(DEPTH, 2, 1, D_BRANCH), ae, mask, pair_diag, h0t, h0t)


MRG_TM = 256


def _gelu_tanh(x):
    return 0.5 * x * (1.0 + jnp.tanh(0.7978845608028654 * (x + 0.044715 * (x * x * x))))


def _merge_kernel(scb_ref, scc_ref, sch_ref, hgg_ref, pool_ref, g0_ref, g1_ref, g2_ref, g3_ref,
                  yrow_ref, of_ref, ob_ref, x_ref, mod_ref, conv_ref, gluw_ref, glub_ref, ng_ref,
                  poolw_ref, pscale_ref, wbr_ref, wout_ref, lng_ref, lnb_ref, o_ref,
                  ys5_sc, gluw_sc, poolw_sc, wbr_sc, wout_sc):
    tile = pl.program_id(0)

    @pl.when(tile == 0)
    def _():
        gluw_sc[...] = gluw_ref[0].astype(BF16)
        poolw_sc[...] = poolw_ref[0].astype(BF16)
        wbr_sc[...] = wbr_ref[0].astype(BF16)
        wout_sc[...] = wout_ref[0].astype(BF16)

    line = jnp.where(tile * MRG_TM < N_CTX_TOK, CTX_LEN, GRID_W)
    pos = lax.broadcasted_iota(jnp.int32, (MRG_TM, 1), 0) & (line - 1)

    def shifted(val, k):
        rolled = pltpu.roll(val, k % MRG_TM, axis=0)
        ok = (pos >= k) if k > 0 else (pos < line + k)
        return jnp.where(ok, rolled, 0.0)

    m = scc_ref[...].astype(F32) * sch_ref[...].astype(F32)
    conv = conv_ref[0, 0:1, :] * shifted(m, 1) + conv_ref[0, 1:2, :] * m + conv_ref[0, 2:3, :] * shifted(m, -1)
    ya = scb_ref[...].astype(F32) * conv

    for b in range(S5_BLK):
        for t in range(S5_T):
            ys5_sc[b, pl.ds(t, MRG_TM // S5_T, stride=S5_T), :] = yrow_ref[b, :, t * LANES:(t + 1) * LANES]
    z = _gelu_tanh(jnp.concatenate([ys5_sc[b] for b in range(S5_BLK)], axis=-1))
    yb = z * _sigmoid_tanh(_dot(z.astype(BF16), gluw_sc[...]) + glub_ref[0])

    o = of_ref[...] + ob_ref[...]
    normed = []
    for h in range(HG_HEADS):
        oh = o[:, h * HG_DK:(h + 1) * HG_DK]
        ms = jnp.mean(oh * oh, axis=-1, keepdims=True)
        normed.append(oh * lax.rsqrt(ms + LN_EPS) * ng_ref[0])
    gate_c = hgg_ref[...].astype(F32)
    yc = jnp.concatenate(normed, axis=-1) * (gate_c * _sigmoid_tanh(gate_c))

    pu = pool_ref[...].astype(F32)
    posf = pos.astype(F32)
    linef = line.astype(F32)
    pooled = []
    for gi, w in enumerate(POOL_WINDOWS):
        vg = pu[:, gi * POOL_GROUP:(gi + 1) * POOL_GROUP]
        back, fwd, span = vg, vg, 1
        while 2 * span <= w // 2:
            back = back + shifted(back, span)
            fwd = fwd + shifted(fwd, -span)
            span *= 2
        s = shifted(back, 1) + fwd
        cnt = jnp.minimum(posf + w // 2, linef) - jnp.maximum(posf - w // 2, 0.0)
        pg = s / cnt - vg
        pooled.append(_dot(pg.astype(BF16), poolw_sc[gi]))
    yd = jnp.concatenate(pooled, axis=-1) * pscale_ref[0]

    merged = jnp.zeros((MRG_TM, D_MODEL), F32)
    for k, (br, gate_ref) in enumerate(((ya, g0_ref), (yb, g1_ref), (yc, g2_ref), (yd, g3_ref))):
        merged = merged + _sigmoid_tanh(gate_ref[...].astype(F32)) * _dot(br.astype(BF16), wbr_sc[k])
    mix = _dot(merged.astype(BF16), wout_sc[...])
    y = ALPHA * x_ref[...] + mod_ref[0, 0, 5:6, :] * mix
    o_ref[...] = _layer_norm(y, lng_ref[0, 0], lnb_ref[0, 0])


def _merge(proj, y_rows, o_f, o_b, x, mod, conv_w, glu_w, glu_b, norm_g, pool_w, pool_scale, w_branch, w_out,
           ln_g, ln_b, layer):
    def part(col):
        return pl.BlockSpec((MRG_TM, D_BRANCH), lambda i: (i, col))

    def gate(k):
        return pl.BlockSpec((MRG_TM, D_MODEL), lambda i: (i, N_IN_PARTS * D_BRANCH // D_MODEL + k))

    def per_layer(*shape, single=False):
        mode = dict(pipeline_mode=pl.Buffered(1)) if single else {}
        return pl.BlockSpec((1,) + shape, lambda i: (layer,) + (0,) * len(shape), **mode)

    return pl.pallas_call(
        _merge_kernel,
        out_shape=jax.ShapeDtypeStruct((N_TOK, D_MODEL), F32),
        grid=(N_TOK // MRG_TM,),
        in_specs=[
            part(0), part(1), part(2), part(8), part(9), gate(0), gate(1), gate(2), gate(3),
            pl.BlockSpec((S5_BLK, MRG_TM // S5_T, S5_CW), lambda i: (0, i, 0)),
            pl.BlockSpec((MRG_TM, D_BRANCH), lambda i: (i, 0)),
            pl.BlockSpec((MRG_TM, D_BRANCH), lambda i: (i, 0)),
            pl.BlockSpec((MRG_TM, D_MODEL), lambda i: (i, 0)),
            _mod_spec(layer, MRG_TM),
            per_layer(3, D_BRANCH), per_layer(D_BRANCH, D_BRANCH, single=True), per_layer(1, D_BRANCH),
            per_layer(1, HG_DK), per_layer(len(POOL_WINDOWS), POOL_GROUP, POOL_GROUP, single=True),
            per_layer(1, D_BRANCH), per_layer(N_BRANCH, D_BRANCH, D_MODEL, single=True),
            per_layer(D_MODEL, D_MODEL, single=True), _ln_spec(layer, 1), _ln_spec(layer, 1),
        ],
        out_specs=pl.BlockSpec((MRG_TM, D_MODEL), lambda i: (i, 0)),
        scratch_shapes=[
            pltpu.VMEM((S5_BLK, MRG_TM, LANES), F32),
            pltpu.VMEM((D_BRANCH, D_BRANCH), BF16),
            pltpu.VMEM((len(POOL_WINDOWS), POOL_GROUP, POOL_GROUP), BF16),
            pltpu.VMEM((N_BRANCH, D_BRANCH, D_MODEL), BF16),
            pltpu.VMEM((D_MODEL, D_MODEL), BF16),
        ],
        compiler_params=_cparams(("arbitrary",)),
        name="merge",
    )(proj, proj, proj, proj, proj, proj, proj, proj, proj, y_rows, o_f, o_b, x, mod,
      conv_w, glu_w, glu_b.reshape(DEPTH, 1, D_BRANCH), norm_g.reshape(DEPTH, 1, HG_DK), pool_w,
      pool_scale.reshape(DEPTH, 1, D_BRANCH), w_branch, w_out, ln_g, ln_b)


def _grid_pos_embedding():
    rows = LAT_LEN // GRID_W
    quarter = D_MODEL // 4
    omega = POS_BASE ** (-jnp.arange(quarter, dtype=F32) / quarter)
    ar = jnp.arange(rows, dtype=F32)[:, None] * omega
    ac = jnp.arange(GRID_W, dtype=F32)[:, None] * omega
    row_tab = jnp.concatenate([jnp.sin(ar), jnp.cos(ar)], -1)[:, None, :]
    col_tab = jnp.concatenate([jnp.sin(ac), jnp.cos(ac)], -1)[None, :, :]
    shape = (rows, GRID_W, 2 * quarter)
    pos = jnp.concatenate([jnp.broadcast_to(row_tab, shape), jnp.broadcast_to(col_tab, shape)], -1)
    return pos.reshape(LAT_LEN, D_MODEL)


def kernel(x_prompt, x_sample, state_s5_re, state_s5_im, state_hgrn, c, c_ctx, w_ada, b_ada, ln_g, ln_b,
           ffn_w1, ffn_w3, ffn_w2, w_in, sc_conv, s5_lam_re, s5_lam_im, s5_log_dt, s5_b_re, s5_b_im,
           s5_c_re, s5_c_im, s5_d, s5_glu_w, s5_glu_b, hg_lb, hg_norm_g, pool_w, pool_scale, w_branch, w_out):
    x = (x_prompt.reshape(N_CTX_TOK, D_MODEL), x_sample.reshape(N_LAT_TOK, D_MODEL), _grid_pos_embedding())

    cond = jnp.zeros((COND_PAD, D_MODEL), F32).at[0].set(c_ctx).at[1:N_COND].set(c)
    mod = _ada(cond, w_ada, b_ada).reshape(DEPTH, COND_PAD, N_SUB * 3, D_MODEL)
    ln_g4 = ln_g.reshape(DEPTH, N_SUB, 1, D_MODEL)
    ln_b4 = ln_b.reshape(DEPTH, N_SUB, 1, D_MODEL)

    s5_m, s5_win, s5_wout, s5_pow = _s5_tables(*_s5_param_layout(
        s5_lam_re, s5_lam_im, s5_log_dt, s5_b_re, s5_b_im, s5_c_re, s5_c_im, s5_d))

    ffn_w1, ffn_w3 = _cast_bf16(ffn_w1, ffn_w3)
    (ffn_w2,) = _cast_bf16(ffn_w2)

    ae_np, mask_np, pair_np = _hg_constants()
    ae = jnp.asarray(ae_np, BF16)
    mask = jnp.asarray(mask_np, F32)
    pair_diag = jnp.asarray(pair_np, F32)

    fin_s5, fin_hg = [], []
    for l in range(DEPTH):
        x = _ffn(x, mod, ffn_w1, ffn_w3, ffn_w2, ln_g4, ln_b4, l, 0)
        proj, fz, u_rows = _inproj(x, mod, w_in, l)

        h0 = jnp.concatenate([
            state_s5_re[:, l].reshape(N_LAT_SEQ, 2, S5_BLK, S5_SW),
            state_s5_im[:, l].reshape(N_LAT_SEQ, 2, S5_BLK, S5_SW)], axis=-1).transpose(1, 2, 0, 3)
        y_rows, fin = _s5_main(u_rows, s5_m, s5_win, s5_wout, s5_pow, h0, l)
        fin_s5.append(fin)

        h0t = jnp.concatenate([jnp.zeros((1, 2, HG_HEADS, HG_DK, HG_DK), F32),
                               jnp.swapaxes(state_hgrn[:, l], -1, -2)], axis=0)
        o_f, o_b, fin_f, fin_b = _hgrn(proj, fz, hg_lb, ae, mask, pair_diag, h0t, l)
        fin_hg.append(jnp.stack([fin_f, fin_b], axis=1))

        x = _merge(proj, y_rows, o_f, o_b, x, mod, sc_conv, s5_glu_w, s5_glu_b, hg_norm_g,
                   pool_w, pool_scale, w_branch, w_out, ln_g4, ln_b4, l)
        x = _ffn(x, mod, ffn_w1, ffn_w3, ffn_w2, ln_g4, ln_b4, l, 2, split_output=(l == DEPTH - 1))

    y_prompt = x[0].reshape(N_CTX_SEQ, CTX_LEN, D_MODEL)
    y_sample = x[1].reshape(N_LAT_SEQ, LAT_LEN, D_MODEL)
    fin = jnp.stack(fin_s5)
    fin = fin.reshape(DEPTH, 2, S5_BLK, N_CTX_SEQ, 2, S5_GPB, S5_STATE)
    fin = fin.transpose(4, 3, 0, 1, 2, 5, 6).reshape(2, N_CTX_SEQ, DEPTH, 2, S5_GROUPS, S5_STATE)
    new_hgrn = jnp.swapaxes(jnp.stack(fin_hg, axis=1), -1, -2)
    return y_prompt, y_sample, fin[0], fin[1], new_hgrn
```

```python
import functools

import numpy as np
import jax
import jax.numpy as jnp
from jax import lax
from jax.experimental import pallas as pl
from jax.experimental.pallas import tpu as pltpu

F32 = jnp.float32
BF16 = jnp.bfloat16

D_MODEL = 1024
N_CTX_SEQ = 16
CTX_LEN = 256
DEPTH = 2
N_LAT_SEQ = 2
LAT_LEN = 4096
GRID_W = 64
D_BRANCH = 512
N_BRANCH = 4
S5_GROUPS = 32
S5_GROUP = 16
S5_STATE = 64
HG_HEADS = 4
HG_DK = 128
POOL_WINDOWS = (2, 4, 8, 16)
POOL_GROUP = 128
D_FF = 2816
N_SUB = 3
N_IN_PARTS = 10
IN_WIDTH = N_IN_PARTS * D_BRANCH + N_BRANCH * D_MODEL
ADA_WIDTH = N_SUB * 3 * D_MODEL
ALPHA = (2 * DEPTH) ** 0.25
LN_EPS = 1e-5
POS_BASE = 10000.0

N_CTX_TOK = N_CTX_SEQ * CTX_LEN
N_LAT_TOK = N_LAT_SEQ * LAT_LEN
N_TOK = N_CTX_TOK + N_LAT_TOK
N_COND = 1 + N_LAT_SEQ
COND_PAD = 8

LANES = 128
SUBLANES = 8
VMEM_LIMIT = 56 * 1024 * 1024

S5_T = 8
S5_BLK = 4
S5_GPB = S5_GROUPS // S5_BLK
S5_SW = S5_GPB * S5_STATE
S5_CW = S5_T * LANES
CTX_CHUNKS = CTX_LEN // S5_T
LAT_CHUNKS = LAT_LEN // S5_T
S5_ROWS_CTX = N_CTX_TOK // S5_T
S5_ROWS_LAT = N_LAT_TOK // S5_T
S5_ROWS = S5_ROWS_CTX + S5_ROWS_LAT
S5_LAT_GROUPS = LAT_CHUNKS // SUBLANES
S5_POW_ROWS = 24
S5_U_COL = 3

HG_C = 128
HG_LEVELS = 7
HG_TM = 256
HG_TILES = N_TOK // HG_TM
HG_CTX_TILES = N_CTX_TOK // HG_TM
HG_TILES_PER_CTX = CTX_LEN // HG_TM
HG_TILES_PER_LAT = LAT_LEN // HG_TM
HG_E_ROWS = (HG_LEVELS + 1) * HG_C + 8


def _cparams(sem):
    return pltpu.CompilerParams(dimension_semantics=sem, vmem_limit_bytes=VMEM_LIMIT)


def _cond_row(tile, tile_tokens):
    tok = tile * tile_tokens
    return jnp.where(tok < N_CTX_TOK, 0, 1 + (tok - N_CTX_TOK) // LAT_LEN)


def _dot(a, b):
    return jnp.dot(a, b, preferred_element_type=F32)


def _dot_nt(a, b):
    return lax.dot_general(a, b, (((1,), (1,)), ((), ())), preferred_element_type=F32)


def _dot_tn(a, b):
    return lax.dot_general(a, b, (((0,), (0,)), ((), ())), preferred_element_type=F32)


def _split3(x):
    h1 = x.astype(BF16)
    r1 = x - h1.astype(F32)
    h2 = r1.astype(BF16)
    h3 = (r1 - h2.astype(F32)).astype(BF16)
    return h1, h2, h3


def _dot_nt_hi(a3, b3):
    acc = None
    for x in range(2):
        for y in range(2 - x):
            term = _dot_nt(a3[x], b3[y])
            acc = term if acc is None else acc + term
    return acc


def _silu(x):
    return x * jax.nn.sigmoid(x)


def _sigmoid_tanh(x):
    return 0.5 * jnp.tanh(0.5 * x) + 0.5


def _layer_norm(y, g, b):
    mu = jnp.mean(y, axis=-1, keepdims=True)
    yc = y - mu
    var = jnp.mean(yc * yc, axis=-1, keepdims=True)
    return yc * lax.rsqrt(var + LN_EPS) * g + b


def _mod_spec(layer, tile_tokens):
    return pl.BlockSpec((1, 1, N_SUB * 3, D_MODEL),
                        lambda i, *_: (layer, _cond_row(i, tile_tokens), 0, 0))


def _ln_spec(layer, sub):
    return pl.BlockSpec((1, 1, 1, D_MODEL), lambda *_: (layer, sub, 0, 0))


ADA_TN = 1152


def _ada_kernel(c_ref, w_ref, b_ref, o_ref):
    s = _silu(c_ref[...]).astype(BF16)
    o_ref[0] = _dot(s, w_ref[0].astype(BF16)) + b_ref[0]


def _ada(cond, w_ada, b_ada):
    return pl.pallas_call(
        _ada_kernel,
        out_shape=jax.ShapeDtypeStruct((DEPTH, COND_PAD, ADA_WIDTH), F32),
        grid=(DEPTH, ADA_WIDTH // ADA_TN),
        in_specs=[
            pl.BlockSpec((COND_PAD, D_MODEL), lambda l, j: (0, 0)),
            pl.BlockSpec((1, D_MODEL, ADA_TN), lambda l, j: (l, 0, j)),
            pl.BlockSpec((1, 1, ADA_TN), lambda l, j: (l, 0, j)),
        ],
        out_specs=pl.BlockSpec((1, COND_PAD, ADA_TN), lambda l, j: (l, 0, j)),
        compiler_params=_cparams(("parallel", "parallel")),
        name="ada",
    )(cond, w_ada, b_ada.reshape(DEPTH, 1, ADA_WIDTH))


CAST_ROWS = 512


def _cast_kernel(*refs):
    n = len(refs) // 2
    for src, dst in zip(refs[:n], refs[n:]):
        dst[...] = src[...].astype(BF16)


def _cast_bf16(*ws):
    shape = ws[0].shape
    rows, cols = int(np.prod(shape[:-1])), shape[-1]
    spec = pl.BlockSpec((CAST_ROWS, cols), lambda i: (i, 0))
    outs = pl.pallas_call(
        _cast_kernel,
        out_shape=tuple(jax.ShapeDtypeStruct((rows, cols), BF16) for _ in ws),
        grid=(rows // CAST_ROWS,),
        in_specs=[spec] * len(ws),
        out_specs=tuple([spec] * len(ws)),
        compiler_params=_cparams(("parallel",)),
        name="cast_bf16",
    )(*(w.reshape(rows, cols) for w in ws))
    return tuple(o.reshape(shape) for o in outs)


FFN_TM = 1024
FFN_TM_SPLIT = 512
FFN_TC = 256


def _ffn_kernel(*refs, sub, tm, split_input, split_output):
    ctx_tiles = N_CTX_TOK // tm
    is_ctx = pl.program_id(0) < ctx_tiles
    if split_input:
        ctx_ref, lat_ref, pos_ref = refs[:3]
        refs = refs[3:]
        x = jnp.where(is_ctx, ctx_ref[...], lat_ref[...] + pos_ref[...])
    else:
        x = refs[0][...]
        refs = refs[1:]
    mod_ref, w1_ref, w3_ref, w2_ref, g_ref, b_ref = refs[:6]
    out_refs, act_sc = refs[6:-1], refs[-1]
    shift = mod_ref[0, 0, 3 * sub:3 * sub + 1, :]
    scale = mod_ref[0, 0, 3 * sub + 1:3 * sub + 2, :]
    gate = mod_ref[0, 0, 3 * sub + 2:3 * sub + 3, :]
    h = (x * (1.0 + scale) + shift).astype(BF16)
    for c in range(D_FF // FFN_TC):
        cols = slice(c * FFN_TC, (c + 1) * FFN_TC)
        a = _dot(h, w1_ref[0, 0, :, cols])
        b = _dot(h, w3_ref[0, 0, :, cols])
        act_sc[:, cols] = (_silu(a) * b).astype(BF16)
    f = _dot(act_sc[...], w2_ref[0, 0])
    y = _layer_norm(ALPHA * x + gate * (0.5 * f), g_ref[0, 0], b_ref[0, 0])
    if split_output:
        @pl.when(is_ctx)
        def _():
            out_refs[0][...] = y

        @pl.when(jnp.logical_not(is_ctx))
        def _():
            out_refs[1][...] = y
    else:
        out_refs[0][...] = y


def _ffn(x, mod, w1, w3, w2, ln_g, ln_b, layer, sub, split_output=False):
    which = sub // 2
    split_input = isinstance(x, tuple)
    tm = FFN_TM_SPLIT if split_input else FFN_TM
    ctx_tiles = N_CTX_TOK // tm

    def resident(*shape):
        return pl.BlockSpec((1, 1) + shape, lambda i: (layer, which, 0, 0), pipeline_mode=pl.Buffered(1))

    def ctx_tile(i):
        return (jnp.minimum(i, ctx_tiles - 1), 0)

    def lat_tile(i):
        return (jnp.maximum(i - ctx_tiles, 0), 0)

    if split_input:
        pos_tiles = LAT_LEN // tm
        x_specs = [
            pl.BlockSpec((tm, D_MODEL), ctx_tile), pl.BlockSpec((tm, D_MODEL), lat_tile),
            pl.BlockSpec((tm, D_MODEL), lambda i: (jnp.maximum(i - ctx_tiles, 0) % pos_tiles, 0)),
        ]
    else:
        x_specs = [pl.BlockSpec((tm, D_MODEL), lambda i: (i, 0))]
        x = (x,)
    if split_output:
        out_shape = (jax.ShapeDtypeStruct((N_CTX_TOK, D_MODEL), F32), jax.ShapeDtypeStruct((N_LAT_TOK, D_MODEL), F32))
        out_specs = (pl.BlockSpec((tm, D_MODEL), ctx_tile), pl.BlockSpec((tm, D_MODEL), lat_tile))
    else:
        out_shape = jax.ShapeDtypeStruct((N_TOK, D_MODEL), F32)
        out_specs = pl.BlockSpec((tm, D_MODEL), lambda i: (i, 0))

    return pl.pallas_call(
        functools.partial(_ffn_kernel, sub=sub, tm=tm, split_input=split_input, split_output=split_output),
        out_shape=out_shape,
        grid=(N_TOK // tm,),
        in_specs=x_specs + [
            _mod_spec(layer, tm),
            resident(D_MODEL, D_FF), resident(D_MODEL, D_FF), resident(D_FF, D_MODEL),
            _ln_spec(layer, sub), _ln_spec(layer, sub),
        ],
        out_specs=out_specs,
        scratch_shapes=[pltpu.VMEM((tm, D_FF), BF16)],
        compiler_params=_cparams(("arbitrary",) if split_output else ("parallel",)),
        name=f"ffn{sub}",
    )(*x, mod, w1, w3, w2, ln_g, ln_b)


INP_TM = 2048
INP_TN = D_BRANCH
HG_Q_COL, HG_F_COL, HG_V_COL = 4, 5, 7


def _inproj_kernel(x_ref, mod_ref, w_ref, o_ref, fz_ref, u_ref, h_sc, blk_sc):
    j = pl.program_id(1)

    @pl.when(j == 0)
    def _():
        shift = mod_ref[0, 0, 3:4, :]
        scale = mod_ref[0, 0, 4:5, :]
        h_sc[...] = (x_ref[...] * (1.0 + scale) + shift).astype(BF16)

    def project():
        res = _dot(h_sc[...], w_ref[0].astype(BF16))
        o_ref[...] = res.astype(BF16)
        return res

    is_f = (j == HG_F_COL) | (j == HG_F_COL + 1)
    is_u = j == S5_U_COL

    @pl.when(jnp.logical_not(is_f | is_u))
    def _():
        project()

    @pl.when(is_f)
    def _():
        fz_ref[...] = project()

    @pl.when(is_u)
    def _():
        s5_u = project()
        for b in range(S5_BLK):
            blk_sc[b] = s5_u[:, b * LANES:(b + 1) * LANES]
            for t in range(S5_T):
                rows = blk_sc[b, pl.ds(t, INP_TM // S5_T, stride=S5_T), :]
                u_ref[b, :, t * LANES:(t + 1) * LANES] = rows.astype(BF16)


def _inproj(x, mod, w_in, layer):
    return pl.pallas_call(
        _inproj_kernel,
        out_shape=(jax.ShapeDtypeStruct((N_TOK, IN_WIDTH), BF16),
                   jax.ShapeDtypeStruct((N_TOK, 2 * D_BRANCH), F32),
                   jax.ShapeDtypeStruct((S5_BLK, S5_ROWS, S5_CW), BF16)),
        grid=(N_TOK // INP_TM, IN_WIDTH // INP_TN),
        in_specs=[
            pl.BlockSpec((INP_TM, D_MODEL), lambda i, j: (i, 0), pipeline_mode=pl.Buffered(1)),
            _mod_spec(layer, INP_TM),
            pl.BlockSpec((1, D_MODEL, INP_TN), lambda i, j: (layer, 0, j)),
        ],
        out_specs=(pl.BlockSpec((INP_TM, INP_TN), lambda i, j: (i, j)),
                   pl.BlockSpec((INP_TM, D_BRANCH),
                                lambda i, j: (i, jnp.where(j > HG_F_COL, 1, 0))),
                   pl.BlockSpec((S5_BLK, INP_TM // S5_T, S5_CW), lambda i, j: (0, i, 0))),
        scratch_shapes=[pltpu.VMEM((INP_TM, D_MODEL), BF16), pltpu.VMEM((S5_BLK, INP_TM, LANES), F32)],
        compiler_params=_cparams(("parallel", "arbitrary")),
        name="inproj",
    )(x, mod, w_in)


def _s5_tab_kernel(lam_ref, bre_ref, bim_ref, cre_ref, cim_ref, dsk_ref, m_ref, win_ref, wout_ref, pow_ref):
    d = pl.program_id(1)
    fwd = d == 0
    lam_re = lam_ref[0, 0, 0, 0:1, :]
    lam_im = lam_ref[0, 0, 0, 1:2, :]
    dt = jnp.exp(lam_ref[0, 0, 0, 2:3, :])

    def apow(j):
        mag = jnp.exp(lam_re * dt * float(j))
        ang = lam_im * dt * float(j)
        return mag * jnp.cos(ang), mag * jnp.sin(ang)

    a_re, a_im = apow(1)
    den = lam_re * lam_re + lam_im * lam_im
    num_re = a_re - 1.0
    coef_re = (num_re * lam_re + a_im * lam_im) / den
    coef_im = (a_im * lam_re - num_re * lam_im) / den
    ri = lax.broadcasted_iota(jnp.int32, (S5_STATE, S5_SW), 0)
    ci = lax.broadcasted_iota(jnp.int32, (S5_STATE, S5_SW), 1)
    repeat = jnp.where(ri == (ci & (S5_STATE - 1)), 1.0, 0.0).astype(BF16)
    own = (jnp.right_shift(lax.broadcasted_iota(jnp.int32, (LANES, S5_SW), 0), S5_GROUP.bit_length() - 1)
           == jnp.right_shift(lax.broadcasted_iota(jnp.int32, (LANES, S5_SW), 1), S5_STATE.bit_length() - 1))

    def block_diag(ref):
        return jnp.where(own, sum(_dot(piece, repeat) for piece in _split3(ref[0, 0, 0])), 0.0)

    b_re = block_diag(bre_ref)
    b_im = block_diag(bim_ref)
    bb_re = coef_re * b_re - coef_im * b_im
    bb_im = coef_re * b_im + coef_im * b_re
    bbr = _split3(bb_re)
    bbi = _split3(bb_im)
    c_re = block_diag(cre_ref)
    c_im = block_diag(cim_ref)

    kmat = []
    for j in range(S5_T + 1):
        aj_re, aj_im = apow(j)
        pc_re = c_re * aj_re - c_im * aj_im
        pc_im = c_re * aj_im + c_im * aj_re
        if j >= 1:
            r = pl.multiple_of(jnp.where(fwd, j - 1, S5_T - j) * LANES, LANES)
            wout_ref[0, 0, 0, pl.ds(r, LANES), :] = jnp.concatenate([pc_re, -pc_im], axis=1).astype(BF16)
        if j < S5_T:
            pb_re = bb_re * aj_re - bb_im * aj_im
            pb_im = bb_re * aj_im + bb_im * aj_re
            r = pl.multiple_of(jnp.where(fwd, S5_T - 1 - j, j) * LANES, LANES)
            win_ref[0, 0, 0, pl.ds(r, LANES), :] = jnp.concatenate([pb_re, pb_im], axis=1).astype(BF16)
            kmat.append(_dot_nt_hi(bbr, _split3(pc_re)) - _dot_nt_hi(bbi, _split3(pc_im)))

    ri = lax.broadcasted_iota(jnp.int32, (LANES, LANES), 0)
    ci = lax.broadcasted_iota(jnp.int32, (LANES, LANES), 1)
    keep_upper = jnp.where(fwd, 1.0, 0.0)
    kmat[0] = kmat[0] + jnp.where(ri == ci, dsk_ref[0, 0], 0.0) * keep_upper
    for s in range(S5_T):
        for t in range(S5_T):
            tile = kmat[abs(t - s)]
            if t > s:
                tile = tile * keep_upper
            elif t < s:
                tile = tile * (1.0 - keep_upper)
            m_ref[0, 0, 0, s * LANES:(s + 1) * LANES, t * LANES:(t + 1) * LANES] = tile.astype(BF16)

    pow_ref[...] = jnp.zeros_like(pow_ref)
    for i in range(SUBLANES):
        for row, j in ((i, S5_T * (i + 1)), (SUBLANES + i, S5_T * (SUBLANES - i))):
            p_re, p_im = apow(j)
            pow_ref[0, 0, 0, 0, row:row + 1, :] = p_re
            pow_ref[0, 0, 0, 1, row:row + 1, :] = p_im
    p_re, p_im = apow(S5_T * 2 * SUBLANES)
    pow_ref[0, 0, 0, 0, 2 * SUBLANES:2 * SUBLANES + 1, :] = p_re
    pow_ref[0, 0, 0, 1, 2 * SUBLANES:2 * SUBLANES + 1, :] = p_im


def _s5_tables(lam, b_re, b_im, c_re, c_im, d_skip):
    lead = (DEPTH, 2, S5_BLK)

    def spec(*tail):
        return pl.BlockSpec((1, 1, 1) + tail, lambda l, d, b: (l, d, b) + (0,) * len(tail))

    return pl.pallas_call(
        _s5_tab_kernel,
        out_shape=(
            jax.ShapeDtypeStruct(lead + (S5_CW, S5_CW), BF16),
            jax.ShapeDtypeStruct(lead + (S5_CW, 2 * S5_SW), BF16),
            jax.ShapeDtypeStruct(lead + (S5_CW, 2 * S5_SW), BF16),
            jax.ShapeDtypeStruct(lead + (2, S5_POW_ROWS, S5_SW), F32),
        ),
        grid=lead,
        in_specs=[spec(3, S5_SW)] + [spec(LANES, S5_STATE)] * 4
        + [pl.BlockSpec((1, 1, 1, LANES), lambda l, d, b: (l, b, 0, 0))],
        out_specs=(spec(S5_CW, S5_CW), spec(S5_CW, 2 * S5_SW), spec(S5_CW, 2 * S5_SW),
                   spec(2, S5_POW_ROWS, S5_SW)),
        compiler_params=_cparams(("parallel", "parallel", "parallel")),
        name="s5_tables",
    )(lam, b_re, b_im, c_re, c_im, d_skip)


def _s5_param_layout(s5_lam_re, s5_lam_im, s5_log_dt, s5_b_re, s5_b_im, s5_c_re, s5_c_im, s5_d):
    lam = jnp.stack([s5_lam_re, s5_lam_im, jnp.broadcast_to(s5_log_dt[..., None], s5_lam_re.shape)], axis=2)
    lam = lam.reshape(DEPTH, 2, 3, S5_BLK, S5_SW).transpose(0, 1, 3, 2, 4)

    def rows_b(b):
        return jnp.swapaxes(b, -1, -2).reshape(DEPTH, 2, S5_BLK, LANES, S5_STATE)

    def rows_c(c):
        return c.reshape(DEPTH, 2, S5_BLK, LANES, S5_STATE)

    return (lam, rows_b(s5_b_re), rows_b(s5_b_im), rows_c(s5_c_re), rows_c(s5_c_im),
            s5_d.reshape(DEPTH, S5_BLK, 1, LANES))


def _s5_kernel(u_ref, m_ref, win_ref, wout_ref, pow_ref, h0_ref, y_ref, fin_ref, x_sc):
    d = pl.program_id(1)
    u = u_ref[0]
    x_sc[...] = _dot(u, win_ref[0, 0, 0])
    p_re = pow_ref[0, 0, 0, 0]
    p_im = pow_ref[0, 0, 0, 1]
    re = slice(0, S5_SW)
    im = slice(S5_SW, 2 * S5_SW)

    def madd(t_re, t_im, w_re, w_im, s_re, s_im):
        return t_re + w_re * s_re - w_im * s_im, t_im + w_re * s_im + w_im * s_re

    def shift_rows(val, k, n, idx, reverse):
        if reverse:
            return jnp.where(idx < n - k, pltpu.roll(val, val.shape[0] - k, axis=0), 0.0)
        return jnp.where(idx >= k, pltpu.roll(val, k, axis=0), 0.0)

    def pow_row(k):
        row = k - 1 if k <= SUBLANES else 2 * SUBLANES
        return p_re[row:row + 1], p_im[row:row + 1]

    def ctx_scan(reverse):
        s_re = x_sc[0:S5_ROWS_CTX, re]
        s_im = x_sc[0:S5_ROWS_CTX, im]
        c = lax.broadcasted_iota(jnp.int32, (S5_ROWS_CTX, 1), 0) & (CTX_CHUNKS - 1)
        def shift_ctx(val, k):
            if k % SUBLANES:
                return shift_rows(val, k, CTX_CHUNKS, c, reverse)
            v3 = val.reshape(N_CTX_SEQ, CTX_CHUNKS, S5_SW)
            pad = jnp.zeros((N_CTX_SEQ, k, S5_SW), F32)
            moved = (jnp.concatenate([v3[:, k:], pad], axis=1) if reverse
                     else jnp.concatenate([pad, v3[:, :CTX_CHUNKS - k]], axis=1))
            return moved.reshape(S5_ROWS_CTX, S5_SW)

        k = 1
        while k < CTX_CHUNKS:
            w_re, w_im = pow_row(k)
            s_re, s_im = madd(s_re, s_im, w_re, w_im, shift_ctx(s_re, k), shift_ctx(s_im, k))
            k *= 2
        last = 0 if reverse else CTX_CHUNKS - 1
        pick = (lax.broadcasted_iota(jnp.int32, (N_CTX_SEQ, S5_ROWS_CTX), 1)
                == lax.broadcasted_iota(jnp.int32, (N_CTX_SEQ, S5_ROWS_CTX), 0) * CTX_CHUNKS + last)
        pick = jnp.where(pick, 1.0, 0.0).astype(BF16)
        fin_ref[0, 0, :, re] = sum(_dot(pick, piece) for piece in _split3(s_re))
        fin_ref[0, 0, :, im] = sum(_dot(pick, piece) for piece in _split3(s_im))
        x_sc[0:S5_ROWS_CTX, re] = shift_rows(s_re, 1, CTX_CHUNKS, c, reverse)
        x_sc[0:S5_ROWS_CTX, im] = shift_rows(s_im, 1, CTX_CHUNKS, c, reverse)

    def lat_group(k, carry, reverse):
        g = (S5_LAT_GROUPS - 1 - k) if reverse else k
        r8 = lax.broadcasted_iota(jnp.int32, (SUBLANES, 1), 0)
        tab = slice(SUBLANES, 2 * SUBLANES) if reverse else slice(0, SUBLANES)
        out = []
        for q in range(N_LAT_SEQ):
            base = pl.multiple_of(S5_ROWS_CTX + q * LAT_CHUNKS + g * SUBLANES, SUBLANES)
            s_re = x_sc[pl.ds(base, SUBLANES), re]
            s_im = x_sc[pl.ds(base, SUBLANES), im]
            j = 1
            while j < SUBLANES:
                w_re, w_im = pow_row(j)
                s_re, s_im = madd(s_re, s_im, w_re, w_im,
                                  shift_rows(s_re, j, SUBLANES, r8, reverse), shift_rows(s_im, j, SUBLANES, r8, reverse))
                j *= 2
            c_re, c_im = carry[2 * q], carry[2 * q + 1]
            s_re, s_im = madd(s_re, s_im, p_re[tab], p_im[tab], c_re, c_im)
            edge = (r8 == SUBLANES - 1) if reverse else (r8 == 0)
            x_sc[pl.ds(base, SUBLANES), re] = jnp.where(edge, c_re, shift_rows(s_re, 1, SUBLANES, r8, reverse))
            x_sc[pl.ds(base, SUBLANES), im] = jnp.where(edge, c_im, shift_rows(s_im, 1, SUBLANES, r8, reverse))
            end = slice(0, 1) if reverse else slice(SUBLANES - 1, SUBLANES)
            out += [s_re[end], s_im[end]]
        return tuple(out)

    def run(reverse):
        ctx_scan(reverse)
        h0 = h0_ref[0, 0]
        init = []
        for q in range(N_LAT_SEQ):
            init += [h0[q:q + 1, re], h0[q:q + 1, im]]
        lax.fori_loop(0, S5_LAT_GROUPS, functools.partial(lat_group, reverse=reverse), tuple(init))

    @pl.when(d == 0)
    def _():
        run(False)

    @pl.when(d == 1)
    def _():
        run(True)

    y = _dot(u, m_ref[0, 0, 0]) + _dot_nt(x_sc[...].astype(BF16), wout_ref[0, 0, 0])

    @pl.when(d == 0)
    def _():
        y_ref[0] = y

    @pl.when(d == 1)
    def _():
        y_ref[0] += y


def _s5_main(u, m, w_in, w_out, pows, h0, layer):
    def tab(*tail):
        return pl.BlockSpec((1, 1, 1) + tail, lambda b, d: (layer, d, b) + (0,) * len(tail))

    return pl.pallas_call(
        _s5_kernel,
        out_shape=(jax.ShapeDtypeStruct((S5_BLK, S5_ROWS, S5_CW), F32),
                   jax.ShapeDtypeStruct((2, S5_BLK, N_CTX_SEQ, 2 * S5_SW), F32)),
        grid=(S5_BLK, 2),
        in_specs=[
            pl.BlockSpec((1, S5_ROWS, S5_CW), lambda b, d: (b, 0, 0)),
            tab(S5_CW, S5_CW), tab(S5_CW, 2 * S5_SW), tab(S5_CW, 2 * S5_SW), tab(2, S5_POW_ROWS, S5_SW),
            pl.BlockSpec((1, 1, N_LAT_SEQ, 2 * S5_SW), lambda b, d: (d, b, 0, 0)),
        ],
        out_specs=(
            pl.BlockSpec((1, S5_ROWS, S5_CW), lambda b, d: (b, 0, 0)),
            pl.BlockSpec((1, 1, N_CTX_SEQ, 2 * S5_SW), lambda b, d: (d, b, 0, 0)),
        ),
        scratch_shapes=[pltpu.VMEM((S5_ROWS, 2 * S5_SW), F32)],
        compiler_params=_cparams(("parallel", "arbitrary")),
        name="s5",
    )(u, m, w_in, w_out, pows, h0)


def _hg_constants():
    t = np.arange(HG_C)
    ae = np.zeros((2, HG_E_ROWS, HG_C), np.float32)
    mask = np.zeros((2, HG_LEVELS, HG_C, HG_C), np.float32)
    for lvl in range(HG_LEVELS):
        half = 1 << lvl
        pos = t % (2 * half)
        mid = t - pos + half - 1
        upper = pos >= half
        u = t[None, :]
        rows_upper = (u > mid[:, None]) & (u <= t[:, None])
        rows_lower = (u > t[:, None]) & (u <= mid[:, None])
        ae[0, lvl * HG_C:(lvl + 1) * HG_C] = np.where(upper[:, None], rows_upper, rows_lower)
        same = (t[:, None] // (2 * half)) == (t[None, :] // (2 * half))
        mask[0, lvl] = same & upper[:, None] & (~upper)[None, :]
    ae[0, HG_LEVELS * HG_C:(HG_LEVELS + 1) * HG_C] = t[None, :] <= t[:, None]
    ae[0, (HG_LEVELS + 1) * HG_C:] = 1.0
    ae[1] = ae[0][:, ::-1]
    ae[1, :(HG_LEVELS + 1) * HG_C] = ae[1, :(HG_LEVELS + 1) * HG_C].reshape(HG_LEVELS + 1, HG_C, HG_C)[:, ::-1].reshape(-1, HG_C)
    mask[1] = mask[0][:, ::-1, ::-1]
    ae2 = np.concatenate([ae, ae], axis=-1)
    mask2 = np.concatenate([mask, mask], axis=-1)
    pair_diag = np.kron(np.eye(2, dtype=np.float32), np.ones((HG_DK, HG_DK), np.float32))
    return ae2, mask2, pair_diag


def _hg_tile(direction, i):
    return i if direction == 0 else HG_TILES - 1 - i


HG_PAIR = 2 * HG_DK


def _pair_diag(x):
    zero = jnp.zeros((x.shape[0], HG_DK), x.dtype)
    return jnp.concatenate([jnp.concatenate([x[:, :HG_DK], zero], axis=1),
                            jnp.concatenate([zero, x[:, HG_DK:]], axis=1)], axis=0)


def _hg_chunk(direction, rows, q_ref, f_ref, v_ref, lb, ae_ref, mask_ref, pd_ref, o_ref, st_sc):
    f = lb + (1.0 - lb) * jax.nn.sigmoid(f_ref[rows, :])
    logf = jnp.log2(f)
    kk = 1.0 - f
    hi = logf.astype(BF16)
    lo = (logf - hi.astype(F32)).astype(BF16)
    ex = _dot(ae_ref[direction], jnp.concatenate([hi, lo], axis=0))
    dec = jnp.exp2(ex[:(HG_LEVELS + 1) * HG_C])
    cum = ex[HG_LEVELS * HG_C:(HG_LEVELS + 1) * HG_C]
    tot = ex[(HG_LEVELS + 1) * HG_C:(HG_LEVELS + 1) * HG_C + 1]
    q = q_ref[rows, :].astype(F32)
    v_bf = v_ref[rows, :]
    v = v_bf.astype(F32)
    q_in = (q * dec[HG_LEVELS * HG_C:]).astype(BF16)
    k_tail = (kk * jnp.exp2(tot - cum)).astype(BF16)
    dec_tot = jnp.exp2(tot)
    qk = q * kk
    outs = []
    for p in range(HG_HEADS // 2):
        sl = slice(p * HG_PAIR, (p + 1) * HG_PAIR)
        scores = jnp.zeros((HG_C, HG_PAIR), F32)
        for lvl in range(HG_LEVELS):
            g = dec[lvl * HG_C:(lvl + 1) * HG_C, sl]
            scores = scores + mask_ref[direction, lvl] * _dot_nt(
                (q[:, sl] * g).astype(BF16), _pair_diag((kk[:, sl] * g).astype(BF16)))
        diag = jnp.concatenate(
            [jnp.broadcast_to(jnp.sum(qk[:, (2 * p + j) * HG_DK:(2 * p + j + 1) * HG_DK], axis=-1, keepdims=True),
                              (HG_C, HG_DK)) for j in range(2)], axis=-1)
        st = st_sc[direction, p]
        outs.append(_dot(scores.astype(BF16), _pair_diag(v_bf[:, sl])) + diag * v[:, sl]
                    + _dot_nt(q_in[:, sl], st.astype(BF16)))
        st_new = st * dec_tot[:, sl] + pd_ref[...] * _dot_tn(v_bf[:, sl], k_tail[:, sl])
        st_sc[direction, p] = st_new
    o_ref[rows, :] = jnp.concatenate(outs, axis=-1)


def _hg_kernel(qf_ref, ff_ref, vf_ref, qb_ref, fb_ref, vb_ref, lb_ref, ae_ref, mask_ref, pd_ref, h0f_ref, h0b_ref,
               of_ref, ob_ref, finf_ref, finb_ref, st_sc, *, layer):
    i = pl.program_id(0)
    x = lb_ref[:, :, 0, :]
    e = jnp.exp(x - jnp.max(x, axis=0, keepdims=True))
    sm = e / jnp.sum(e, axis=0, keepdims=True)
    lb = jnp.sum(sm[1:layer + 1], axis=0) if layer > 0 else jnp.zeros((2, D_BRANCH), F32)
    for direction, h0_ref in ((0, h0f_ref), (1, h0b_ref)):
        tile = _hg_tile(direction, i)
        first_ctx = 0 if direction == 0 else HG_TILES_PER_CTX - 1
        first_lat = 0 if direction == 0 else HG_TILES_PER_LAT - 1
        is_start = jnp.where(tile < HG_CTX_TILES,
                             tile % HG_TILES_PER_CTX == first_ctx,
                             (tile - HG_CTX_TILES) % HG_TILES_PER_LAT == first_lat)

        @pl.when(is_start)
        def _(direction=direction, h0_ref=h0_ref):
            for p in range(HG_HEADS // 2):
                st_sc[direction, p] = _pair_diag(
                    jnp.concatenate([h0_ref[0, 0, 2 * p], h0_ref[0, 0, 2 * p + 1]], axis=1))

    for c in range(HG_TM // HG_C):
        fwd_rows = slice(c * HG_C, (c + 1) * HG_C)
        bwd_rows = slice(HG_TM - (c + 1) * HG_C, HG_TM - c * HG_C)
        _hg_chunk(0, fwd_rows, qf_ref, ff_ref, vf_ref, lb[0:1], ae_ref, mask_ref, pd_ref, of_ref, st_sc)
        _hg_chunk(1, bwd_rows, qb_ref, fb_ref, vb_ref, lb[1:2], ae_ref, mask_ref, pd_ref, ob_ref, st_sc)

    for direction, fin_ref in ((0, finf_ref), (1, finb_ref)):
        @pl.when(_hg_tile(direction, i) < HG_CTX_TILES)
        def _(direction=direction, fin_ref=fin_ref):
            for h in range(HG_HEADS):
                p, j = divmod(h, 2)
                fin_ref[0, h] = st_sc[direction, p, j * HG_DK:(j + 1) * HG_DK, j * HG_DK:(j + 1) * HG_DK]


def _hgrn(proj, fz, hg_lb, ae, mask, pair_diag, h0t, layer):
    def tok_spec(direction, col):
        return pl.BlockSpec((HG_TM, D_BRANCH), lambda i: (_hg_tile(direction, i), col))

    def fin_spec(direction):
        return pl.BlockSpec((1, HG_HEADS, HG_DK, HG_DK),
                            lambda i: (jnp.minimum(_hg_tile(direction, i) // HG_TILES_PER_CTX, N_CTX_SEQ - 1), 0, 0, 0))

    def h0_spec(direction):
        return pl.BlockSpec((1, 1, HG_HEADS, HG_DK, HG_DK),
                            lambda i: (_cond_row(_hg_tile(direction, i), HG_TM), direction, 0, 0, 0))

    fin_shape = jax.ShapeDtypeStruct((N_CTX_SEQ, HG_HEADS, HG_DK, HG_DK), F32)
    return pl.pallas_call(
        functools.partial(_hg_kernel, layer=layer),
        out_shape=(jax.ShapeDtypeStruct((N_TOK, D_BRANCH), F32), jax.ShapeDtypeStruct((N_TOK, D_BRANCH), F32),
                   fin_shape, fin_shape),
        grid=(HG_TILES,),
        in_specs=[
            tok_spec(0, HG_Q_COL), tok_spec(0, 0), tok_spec(0, HG_V_COL),
            tok_spec(1, HG_Q_COL), tok_spec(1, 1), tok_spec(1, HG_V_COL),
            pl.BlockSpec((DEPTH, 2, 1, D_BRANCH), lambda i: (0, 0, 0, 0)),
            pl.BlockSpec((2, HG_E_ROWS, 2 * HG_C), lambda i: (0, 0, 0)),
            pl.BlockSpec((2, HG_LEVELS, HG_C, HG_PAIR), lambda i: (0, 0, 0, 0)),
            pl.BlockSpec((HG_PAIR, HG_PAIR), lambda i: (0, 0)),
            h0_spec(0), h0_spec(1),
        ],
        out_specs=(
            pl.BlockSpec((HG_TM, D_BRANCH), lambda i: (_hg_tile(0, i), 0)),
            pl.BlockSpec((HG_TM, D_BRANCH), lambda i: (_hg_tile(1, i), 0)),
            fin_spec(0), fin_spec(1),
        ),
        scratch_shapes=[pltpu.VMEM((2, HG_HEADS // 2, HG_PAIR, HG_PAIR), F32)],
        compiler_params=_cparams(("arbitrary",)),
        name="hgrn",
    )(proj, fz, proj, proj, fz, proj, hg_lb.reshape(DEPTH, 2, 1, D_BRANCH), ae, mask, pair_diag, h0t, h0t)


MRG_TM = 256


def _gelu_tanh(x):
    return 0.5 * x * (1.0 + jnp.tanh(0.7978845608028654 * (x + 0.044715 * (x * x * x))))


def _merge_kernel(scb_ref, scc_ref, sch_ref, hgg_ref, pool_ref, g0_ref, g1_ref, g2_ref, g3_ref,
                  yrow_ref, of_ref, ob_ref, x_ref, mod_ref, conv_ref, gluw_ref, glub_ref, ng_ref,
                  poolw_ref, pscale_ref, wbr_ref, wout_ref, lng_ref, lnb_ref, o_ref,
                  ys5_sc, gluw_sc, poolw_sc, wbr_sc, wout_sc):
    tile = pl.program_id(0)

    @pl.when(tile == 0)
    def _():
        gluw_sc[...] = gluw_ref[0].astype(BF16)
        poolw_sc[...] = poolw_ref[0].astype(BF16)
        wbr_sc[...] = wbr_ref[0].astype(BF16)
        wout_sc[...] = wout_ref[0].astype(BF16)

    line = jnp.where(tile * MRG_TM < N_CTX_TOK, CTX_LEN, GRID_W)
    pos = lax.broadcasted_iota(jnp.int32, (MRG_TM, 1), 0) & (line - 1)

    def shifted(val, k):
        rolled = pltpu.roll(val, k % MRG_TM, axis=0)
        ok = (pos >= k) if k > 0 else (pos < line + k)
        return jnp.where(ok, rolled, 0.0)

    m = scc_ref[...].astype(F32) * sch_ref[...].astype(F32)
    conv = conv_ref[0, 0:1, :] * shifted(m, 1) + conv_ref[0, 1:2, :] * m + conv_ref[0, 2:3, :] * shifted(m, -1)
    ya = scb_ref[...].astype(F32) * conv

    for b in range(S5_BLK):
        for t in range(S5_T):
            ys5_sc[b, pl.ds(t, MRG_TM // S5_T, stride=S5_T), :] = yrow_ref[b, :, t * LANES:(t + 1) * LANES]
    z = _gelu_tanh(jnp.concatenate([ys5_sc[b] for b in range(S5_BLK)], axis=-1))
    yb = z * _sigmoid_tanh(_dot(z.astype(BF16), gluw_sc[...]) + glub_ref[0])

    o = of_ref[...] + ob_ref[...]
    normed = []
    for h in range(HG_HEADS):
        oh = o[:, h * HG_DK:(h + 1) * HG_DK]
        ms = jnp.mean(oh * oh, axis=-1, keepdims=True)
        normed.append(oh * lax.rsqrt(ms + LN_EPS) * ng_ref[0])
    gate_c = hgg_ref[...].astype(F32)
    yc = jnp.concatenate(normed, axis=-1) * (gate_c * _sigmoid_tanh(gate_c))

    pu = pool_ref[...].astype(F32)
    posf = pos.astype(F32)
    linef = line.astype(F32)
    pooled = []
    for gi, w in enumerate(POOL_WINDOWS):
        vg = pu[:, gi * POOL_GROUP:(gi + 1) * POOL_GROUP]
        back, fwd, span = vg, vg, 1
        while 2 * span <= w // 2:
            back = back + shifted(back, span)
            fwd = fwd + shifted(fwd, -span)
            span *= 2
        s = shifted(back, 1) + fwd
        cnt = jnp.minimum(posf + w // 2, linef) - jnp.maximum(posf - w // 2, 0.0)
        pg = s / cnt - vg
        pooled.append(_dot(pg.astype(BF16), poolw_sc[gi]))
    yd = jnp.concatenate(pooled, axis=-1) * pscale_ref[0]

    merged = jnp.zeros((MRG_TM, D_MODEL), F32)
    for k, (br, gate_ref) in enumerate(((ya, g0_ref), (yb, g1_ref), (yc, g2_ref), (yd, g3_ref))):
        merged = merged + _sigmoid_tanh(gate_ref[...].astype(F32)) * _dot(br.astype(BF16), wbr_sc[k])
    mix = _dot(merged.astype(BF16), wout_sc[...])
    y = ALPHA * x_ref[...] + mod_ref[0, 0, 5:6, :] * mix
    o_ref[...] = _layer_norm(y, lng_ref[0, 0], lnb_ref[0, 0])


def _merge(proj, y_rows, o_f, o_b, x, mod, conv_w, glu_w, glu_b, norm_g, pool_w, pool_scale, w_branch, w_out,
           ln_g, ln_b, layer):
    def part(col):
        return pl.BlockSpec((MRG_TM, D_BRANCH), lambda i: (i, col))

    def gate(k):
        return pl.BlockSpec((MRG_TM, D_MODEL), lambda i: (i, N_IN_PARTS * D_BRANCH // D_MODEL + k))

    def per_layer(*shape, single=False):
        mode = dict(pipeline_mode=pl.Buffered(1)) if single else {}
        return pl.BlockSpec((1,) + shape, lambda i: (layer,) + (0,) * len(shape), **mode)

    return pl.pallas_call(
        _merge_kernel,
        out_shape=jax.ShapeDtypeStruct((N_TOK, D_MODEL), F32),
        grid=(N_TOK // MRG_TM,),
        in_specs=[
            part(0), part(1), part(2), part(8), part(9), gate(0), gate(1), gate(2), gate(3),
            pl.BlockSpec((S5_BLK, MRG_TM // S5_T, S5_CW), lambda i: (0, i, 0)),
            pl.BlockSpec((MRG_TM, D_BRANCH), lambda i: (i, 0)),
            pl.BlockSpec((MRG_TM, D_BRANCH), lambda i: (i, 0)),
            pl.BlockSpec((MRG_TM, D_MODEL), lambda i: (i, 0)),
            _mod_spec(layer, MRG_TM),
            per_layer(3, D_BRANCH), per_layer(D_BRANCH, D_BRANCH, single=True), per_layer(1, D_BRANCH),
            per_layer(1, HG_DK), per_layer(len(POOL_WINDOWS), POOL_GROUP, POOL_GROUP, single=True),
            per_layer(1, D_BRANCH), per_layer(N_BRANCH, D_BRANCH, D_MODEL, single=True),
            per_layer(D_MODEL, D_MODEL, single=True), _ln_spec(layer, 1), _ln_spec(layer, 1),
        ],
        out_specs=pl.BlockSpec((MRG_TM, D_MODEL), lambda i: (i, 0)),
        scratch_shapes=[
            pltpu.VMEM((S5_BLK, MRG_TM, LANES), F32),
            pltpu.VMEM((D_BRANCH, D_BRANCH), BF16),
            pltpu.VMEM((len(POOL_WINDOWS), POOL_GROUP, POOL_GROUP), BF16),
            pltpu.VMEM((N_BRANCH, D_BRANCH, D_MODEL), BF16),
            pltpu.VMEM((D_MODEL, D_MODEL), BF16),
        ],
        compiler_params=_cparams(("arbitrary",)),
        name="merge",
    )(proj, proj, proj, proj, proj, proj, proj, proj, proj, y_rows, o_f, o_b, x, mod,
      conv_w, glu_w, glu_b.reshape(DEPTH, 1, D_BRANCH), norm_g.reshape(DEPTH, 1, HG_DK), pool_w,
      pool_scale.reshape(DEPTH, 1, D_BRANCH), w_branch, w_out, ln_g, ln_b)


def _grid_pos_embedding():
    rows = LAT_LEN // GRID_W
    quarter = D_MODEL // 4
    omega = POS_BASE ** (-jnp.arange(quarter, dtype=F32) / quarter)
    ar = jnp.arange(rows, dtype=F32)[:, None] * omega
    ac = jnp.arange(GRID_W, dtype=F32)[:, None] * omega
    row_tab = jnp.concatenate([jnp.sin(ar), jnp.cos(ar)], -1)[:, None, :]
    col_tab = jnp.concatenate([jnp.sin(ac), jnp.cos(ac)], -1)[None, :, :]
    shape = (rows, GRID_W, 2 * quarter)
    pos = jnp.concatenate([jnp.broadcast_to(row_tab, shape), jnp.broadcast_to(col_tab, shape)], -1)
    return pos.reshape(LAT_LEN, D_MODEL)


def kernel(x_prompt, x_sample, state_s5_re, state_s5_im, state_hgrn, c, c_ctx, w_ada, b_ada, ln_g, ln_b,
           ffn_w1, ffn_w3, ffn_w2, w_in, sc_conv, s5_lam_re, s5_lam_im, s5_log_dt, s5_b_re, s5_b_im,
           s5_c_re, s5_c_im, s5_d, s5_glu_w, s5_glu_b, hg_lb, hg_norm_g, pool_w, pool_scale, w_branch, w_out):
    x = (x_prompt.reshape(N_CTX_TOK, D_MODEL), x_sample.reshape(N_LAT_TOK, D_MODEL), _grid_pos_embedding())

    cond = jnp.zeros((COND_PAD, D_MODEL), F32).at[0].set(c_ctx).at[1:N_COND].set(c)
    mod = _ada(cond, w_ada, b_ada).reshape(DEPTH, COND_PAD, N_SUB * 3, D_MODEL)
    ln_g4 = ln_g.reshape(DEPTH, N_SUB, 1, D_MODEL)
    ln_b4 = ln_b.reshape(DEPTH, N_SUB, 1, D_MODEL)

    s5_m, s5_win, s5_wout, s5_pow = _s5_tables(*_s5_param_layout(
        s5_lam_re, s5_lam_im, s5_log_dt, s5_b_re, s5_b_im, s5_c_re, s5_c_im, s5_d))

    ffn_w1, ffn_w3 = _cast_bf16(ffn_w1, ffn_w3)
    (ffn_w2,) = _cast_bf16(ffn_w2)

    ae_np, mask_np, pair_np = _hg_constants()
    ae = jnp.asarray(ae_np, BF16)
    mask = jnp.asarray(mask_np, F32)
    pair_diag = jnp.asarray(pair_np, F32)

    fin_s5, fin_hg = [], []
    for l in range(DEPTH):
        x = _ffn(x, mod, ffn_w1, ffn_w3, ffn_w2, ln_g4, ln_b4, l, 0)
        proj, fz, u_rows = _inproj(x, mod, w_in, l)

        h0 = jnp.concatenate([
            state_s5_re[:, l].reshape(N_LAT_SEQ, 2, S5_BLK, S5_SW),
            state_s5_im[:, l].reshape(N_LAT_SEQ, 2, S5_BLK, S5_SW)], axis=-1).transpose(1, 2, 0, 3)
        y_rows, fin = _s5_main(u_rows, s5_m, s5_win, s5_wout, s5_pow, h0, l)
        fin_s5.append(fin)

        h0t = jnp.concatenate([jnp.zeros((1, 2, HG_HEADS, HG_DK, HG_DK), F32),
                               jnp.swapaxes(state_hgrn[:, l], -1, -2)], axis=0)
        o_f, o_b, fin_f, fin_b = _hgrn(proj, fz, hg_lb, ae, mask, pair_diag, h0t, l)
        fin_hg.append(jnp.stack([fin_f, fin_b], axis=1))

        x = _merge(proj, y_rows, o_f, o_b, x, mod, sc_conv, s5_glu_w, s5_glu_b, hg_norm_g,
                   pool_w, pool_scale, w_branch, w_out, ln_g4, ln_b4, l)
        x = _ffn(x, mod, ffn_w1, ffn_w3, ffn_w2, ln_g4, ln_b4, l, 2, split_output=(l == DEPTH - 1))

    y_prompt = x[0].reshape(N_CTX_SEQ, CTX_LEN, D_MODEL)
    y_sample = x[1].reshape(N_LAT_SEQ, LAT_LEN, D_MODEL)
    fin = jnp.stack(fin_s5)
    fin = fin.reshape(DEPTH, 2, S5_BLK, N_CTX_SEQ, 2, S5_GPB, S5_STATE)
    fin = fin.transpose(4, 3, 0, 1, 2, 5, 6).reshape(2, N_CTX_SEQ, DEPTH, 2, S5_GROUPS, S5_STATE)
    new_hgrn = jnp.swapaxes(jnp.stack(fin_hg, axis=1), -1, -2)
    return y_prompt, y_sample, fin[0], fin[1], new_hgrn
```

```python
import functools

import numpy as np
import jax
import jax.numpy as jnp
from jax import lax
from jax.experimental import pallas as pl
from jax.experimental.pallas import tpu as pltpu

F32 = jnp.float32
BF16 = jnp.bfloat16

D_MODEL = 1024
N_CTX_SEQ = 16
CTX_LEN = 256
DEPTH = 2
N_LAT_SEQ = 2
LAT_LEN = 4096
GRID_W = 64
D_BRANCH = 512
N_BRANCH = 4
S5_GROUPS = 32
S5_GROUP = 16
S5_STATE = 64
HG_HEADS = 4
HG_DK = 128
POOL_WINDOWS = (2, 4, 8, 16)
POOL_GROUP = 128
D_FF = 2816
N_SUB = 3
N_IN_PARTS = 10
IN_WIDTH = N_IN_PARTS * D_BRANCH + N_BRANCH * D_MODEL
ADA_WIDTH = N_SUB * 3 * D_MODEL
ALPHA = (2 * DEPTH) ** 0.25
LN_EPS = 1e-5
POS_BASE = 10000.0

N_CTX_TOK = N_CTX_SEQ * CTX_LEN
N_LAT_TOK = N_LAT_SEQ * LAT_LEN
N_TOK = N_CTX_TOK + N_LAT_TOK
N_COND = 1 + N_LAT_SEQ
COND_PAD = 8

LANES = 128
SUBLANES = 8
VMEM_LIMIT = 56 * 1024 * 1024

S5_T = 8
S5_BLK = 4
S5_GPB = S5_GROUPS // S5_BLK
S5_SW = S5_GPB * S5_STATE
S5_CW = S5_T * LANES
CTX_CHUNKS = CTX_LEN // S5_T
LAT_CHUNKS = LAT_LEN // S5_T
S5_ROWS_CTX = N_CTX_TOK // S5_T
S5_ROWS_LAT = N_LAT_TOK // S5_T
S5_ROWS = S5_ROWS_CTX + S5_ROWS_LAT
S5_LAT_GROUPS = LAT_CHUNKS // SUBLANES
S5_POW_ROWS = 24
S5_U_COL = 3

HG_C = 128
HG_LEVELS = 7
HG_TM = 256
HG_TILES = N_TOK // HG_TM
HG_CTX_TILES = N_CTX_TOK // HG_TM
HG_TILES_PER_CTX = CTX_LEN // HG_TM
HG_TILES_PER_LAT = LAT_LEN // HG_TM
HG_E_ROWS = (HG_LEVELS + 1) * HG_C + 8


def _cparams(sem):
    return pltpu.CompilerParams(dimension_semantics=sem, vmem_limit_bytes=VMEM_LIMIT)


def _cond_row(tile, tile_tokens):
    tok = tile * tile_tokens
    return jnp.where(tok < N_CTX_TOK, 0, 1 + (tok - N_CTX_TOK) // LAT_LEN)


def _dot(a, b):
    return jnp.dot(a, b, preferred_element_type=F32)


def _dot_nt(a, b):
    return lax.dot_general(a, b, (((1,), (1,)), ((), ())), preferred_element_type=F32)


def _dot_tn(a, b):
    return lax.dot_general(a, b, (((0,), (0,)), ((), ())), preferred_element_type=F32)


def _split3(x):
    h1 = x.astype(BF16)
    r1 = x - h1.astype(F32)
    h2 = r1.astype(BF16)
    h3 = (r1 - h2.astype(F32)).astype(BF16)
    return h1, h2, h3


def _dot_nt_hi(a3, b3):
    acc = None
    for x in range(2):
        for y in range(2 - x):
            term = _dot_nt(a3[x], b3[y])
            acc = term if acc is None else acc + term
    return acc


def _silu(x):
    return x * jax.nn.sigmoid(x)


def _sigmoid_tanh(x):
    return 0.5 * jnp.tanh(0.5 * x) + 0.5


def _layer_norm(y, g, b):
    mu = jnp.mean(y, axis=-1, keepdims=True)
    yc = y - mu
    var = jnp.mean(yc * yc, axis=-1, keepdims=True)
    return yc * lax.rsqrt(var + LN_EPS) * g + b


def _mod_spec(layer, tile_tokens):
    return pl.BlockSpec((1, 1, N_SUB * 3, D_MODEL),
                        lambda i, *_: (layer, _cond_row(i, tile_tokens), 0, 0))


def _ln_spec(layer, sub):
    return pl.BlockSpec((1, 1, 1, D_MODEL), lambda *_: (layer, sub, 0, 0))


ADA_TN = 1152


def _ada_kernel(c_ref, w_ref, b_ref, o_ref):
    s = _silu(c_ref[...]).astype(BF16)
    o_ref[0] = _dot(s, w_ref[0].astype(BF16)) + b_ref[0]


def _ada(cond, w_ada, b_ada):
    return pl.pallas_call(
        _ada_kernel,
        out_shape=jax.ShapeDtypeStruct((DEPTH, COND_PAD, ADA_WIDTH), F32),
        grid=(DEPTH, ADA_WIDTH // ADA_TN),
        in_specs=[
            pl.BlockSpec((COND_PAD, D_MODEL), lambda l, j: (0, 0)),
            pl.BlockSpec((1, D_MODEL, ADA_TN), lambda l, j: (l, 0, j)),
            pl.BlockSpec((1, 1, ADA_TN), lambda l, j: (l, 0, j)),
        ],
        out_specs=pl.BlockSpec((1, COND_PAD, ADA_TN), lambda l, j: (l, 0, j)),
        compiler_params=_cparams(("parallel", "parallel")),
        name="ada",
    )(cond, w_ada, b_ada.reshape(DEPTH, 1, ADA_WIDTH))


CAST_ROWS = 512


def _cast_kernel(*refs):
    n = len(refs) // 2
    for src, dst in zip(refs[:n], refs[n:]):
        dst[...] = src[...].astype(BF16)


def _cast_bf16(*ws):
    shape = ws[0].shape
    rows, cols = int(np.prod(shape[:-1])), shape[-1]
    spec = pl.BlockSpec((CAST_ROWS, cols), lambda i: (i, 0))
    outs = pl.pallas_call(
        _cast_kernel,
        out_shape=tuple(jax.ShapeDtypeStruct((rows, cols), BF16) for _ in ws),
        grid=(rows // CAST_ROWS,),
        in_specs=[spec] * len(ws),
        out_specs=tuple([spec] * len(ws)),
        compiler_params=_cparams(("parallel",)),
        name="cast_bf16",
    )(*(w.reshape(rows, cols) for w in ws))
    return tuple(o.reshape(shape) for o in outs)


FFN_TM = 1024
FFN_TM_SPLIT = 512
FFN_TC = 256


def _ffn_kernel(*refs, sub, tm, split_input, split_output, emit_next):
    ctx_tiles = N_CTX_TOK // tm
    is_ctx = pl.program_id(0) < ctx_tiles
    if split_input:
        ctx_ref, lat_ref, pos_ref = refs[:3]
        refs = refs[3:]
        x = jnp.where(is_ctx, ctx_ref[...], lat_ref[...] + pos_ref[...])
    else:
        x = refs[0][...]
        refs = refs[1:]
    mod_ref, w1_ref, w3_ref, w2_ref, g_ref, b_ref = refs[:6]
    out_refs, act_sc = refs[6:-1], refs[-1]
    shift = mod_ref[0, 0, 3 * sub:3 * sub + 1, :]
    scale = mod_ref[0, 0, 3 * sub + 1:3 * sub + 2, :]
    gate = mod_ref[0, 0, 3 * sub + 2:3 * sub + 3, :]
    h = (x * (1.0 + scale) + shift).astype(BF16)
    for c in range(D_FF // FFN_TC):
        cols = slice(c * FFN_TC, (c + 1) * FFN_TC)
        a = _dot(h, w1_ref[0, 0, :, cols])
        b = _dot(h, w3_ref[0, 0, :, cols])
        act_sc[:, cols] = (_silu(a) * b).astype(BF16)
    f = _dot(act_sc[...], w2_ref[0, 0])
    y = _layer_norm(ALPHA * x + gate * (0.5 * f), g_ref[0, 0], b_ref[0, 0])
    if emit_next:
        nxt = 3 * (sub + 1)
        out_refs[1][...] = (y * (1.0 + mod_ref[0, 0, nxt + 1:nxt + 2, :]) + mod_ref[0, 0, nxt:nxt + 1, :]).astype(BF16)
    if split_output:
        @pl.when(is_ctx)
        def _():
            out_refs[0][...] = y

        @pl.when(jnp.logical_not(is_ctx))
        def _():
            out_refs[1][...] = y
    else:
        out_refs[0][...] = y


def _ffn(x, mod, w1, w3, w2, ln_g, ln_b, layer, sub, split_output=False):
    which = sub // 2
    split_input = isinstance(x, tuple)
    tm = FFN_TM_SPLIT if split_input else FFN_TM
    ctx_tiles = N_CTX_TOK // tm

    def resident(*shape):
        return pl.BlockSpec((1, 1) + shape, lambda i: (layer, which, 0, 0), pipeline_mode=pl.Buffered(1))

    def ctx_tile(i):
        return (jnp.minimum(i, ctx_tiles - 1), 0)

    def lat_tile(i):
        return (jnp.maximum(i - ctx_tiles, 0), 0)

    if split_input:
        pos_tiles = LAT_LEN // tm
        x_specs = [
            pl.BlockSpec((tm, D_MODEL), ctx_tile), pl.BlockSpec((tm, D_MODEL), lat_tile),
            pl.BlockSpec((tm, D_MODEL), lambda i: (jnp.maximum(i - ctx_tiles, 0) % pos_tiles, 0)),
        ]
    else:
        x_specs = [pl.BlockSpec((tm, D_MODEL), lambda i: (i, 0))]
        x = (x,)
    emit_next = sub == 0
    if split_output:
        out_shape = (jax.ShapeDtypeStruct((N_CTX_TOK, D_MODEL), F32), jax.ShapeDtypeStruct((N_LAT_TOK, D_MODEL), F32))
        out_specs = (pl.BlockSpec((tm, D_MODEL), ctx_tile), pl.BlockSpec((tm, D_MODEL), lat_tile))
    elif emit_next:
        out_shape = (jax.ShapeDtypeStruct((N_TOK, D_MODEL), F32), jax.ShapeDtypeStruct((N_TOK, D_MODEL), BF16))
        out_specs = (pl.BlockSpec((tm, D_MODEL), lambda i: (i, 0)), pl.BlockSpec((tm, D_MODEL), lambda i: (i, 0)))
    else:
        out_shape = jax.ShapeDtypeStruct((N_TOK, D_MODEL), F32)
        out_specs = pl.BlockSpec((tm, D_MODEL), lambda i: (i, 0))

    return pl.pallas_call(
        functools.partial(_ffn_kernel, sub=sub, tm=tm, split_input=split_input, split_output=split_output,
                          emit_next=emit_next),
        out_shape=out_shape,
        grid=(N_TOK // tm,),
        in_specs=x_specs + [
            _mod_spec(layer, tm),
            resident(D_MODEL, D_FF), resident(D_MODEL, D_FF), resident(D_FF, D_MODEL),
            _ln_spec(layer, sub), _ln_spec(layer, sub),
        ],
        out_specs=out_specs,
        scratch_shapes=[pltpu.VMEM((tm, D_FF), BF16)],
        compiler_params=_cparams(("arbitrary",) if split_output else ("parallel",)),
        name=f"ffn{sub}",
    )(*x, mod, w1, w3, w2, ln_g, ln_b)


INP_TM = 2048
INP_PARTS = 2
INP_TN = INP_PARTS * D_BRANCH
INP_STEPS = IN_WIDTH // INP_TN
HG_Q_COL, HG_F_COL, HG_V_COL = 4, 5, 7


def _cast_w_in_kernel(w_ref, o_ref):
    o_ref[0, 0] = w_ref[0].astype(BF16)


def _cast_w_in(w_in):
    return pl.pallas_call(
        _cast_w_in_kernel,
        out_shape=jax.ShapeDtypeStruct((DEPTH, INP_STEPS, D_MODEL, INP_TN), BF16),
        grid=(DEPTH, INP_STEPS),
        in_specs=[pl.BlockSpec((1, D_MODEL, INP_TN), lambda l, j: (l, 0, j))],
        out_specs=pl.BlockSpec((1, 1, D_MODEL, INP_TN), lambda l, j: (l, j, 0, 0)),
        compiler_params=_cparams(("parallel", "parallel")),
        name="cast_w_in",
    )(w_in)


def _inproj_kernel(h_ref, w_ref, o_ref, fz_ref, u_ref, blk_sc):
    j = pl.program_id(1)

    def project():
        res = _dot(h_ref[...], w_ref[0, 0])
        o_ref[...] = res.astype(BF16)
        return res

    def part(res, p):
        k = p % INP_PARTS
        return res[:, k * D_BRANCH:(k + 1) * D_BRANCH]

    special = (S5_U_COL, HG_F_COL, HG_F_COL + 1)
    steps = [p // INP_PARTS for p in special]
    assert len(set(steps)) == len(steps)

    @pl.when(functools.reduce(jnp.logical_and, [j != s for s in steps]))
    def _():
        project()

    for p in (HG_F_COL, HG_F_COL + 1):
        @pl.when(j == p // INP_PARTS)
        def _(p=p):
            fz_ref[...] = part(project(), p)

    @pl.when(j == S5_U_COL // INP_PARTS)
    def _():
        s5_u = part(project(), S5_U_COL)
        for b in range(S5_BLK):
            blk_sc[b] = s5_u[:, b * LANES:(b + 1) * LANES]
            for t in range(S5_T):
                rows = blk_sc[b, pl.ds(t, INP_TM // S5_T, stride=S5_T), :]
                u_ref[b, :, t * LANES:(t + 1) * LANES] = rows.astype(BF16)


def _inproj(h, w_in_bf, layer):
    return pl.pallas_call(
        _inproj_kernel,
        out_shape=(jax.ShapeDtypeStruct((N_TOK, IN_WIDTH), BF16),
                   jax.ShapeDtypeStruct((N_TOK, 2 * D_BRANCH), F32),
                   jax.ShapeDtypeStruct((S5_BLK, S5_ROWS, S5_CW), BF16)),
        grid=(N_TOK // INP_TM, INP_STEPS),
        in_specs=[
            pl.BlockSpec((INP_TM, D_MODEL), lambda i, j: (i, 0)),
            pl.BlockSpec((1, 1, D_MODEL, INP_TN), lambda i, j: (layer, j, 0, 0)),
        ],
        out_specs=(pl.BlockSpec((INP_TM, INP_TN), lambda i, j: (i, j)),
                   pl.BlockSpec((INP_TM, D_BRANCH),
                                lambda i, j: (i, jnp.where(j > HG_F_COL // INP_PARTS, 1, 0))),
                   pl.BlockSpec((S5_BLK, INP_TM // S5_T, S5_CW), lambda i, j: (0, i, 0))),
        scratch_shapes=[pltpu.VMEM((S5_BLK, INP_TM, LANES), F32)],
        compiler_params=_cparams(("parallel", "arbitrary")),
        name="inproj",
    )(h, w_in_bf)


def _s5_tab_kernel(lam_ref, bre_ref, bim_ref, cre_ref, cim_ref, dsk_ref, m_ref, win_ref, wout_ref, pow_ref):
    d = pl.program_id(1)
    fwd = d == 0
    lam_re = lam_ref[0, 0, 0, 0:1, :]
    lam_im = lam_ref[0, 0, 0, 1:2, :]
    dt = jnp.exp(lam_ref[0, 0, 0, 2:3, :])

    def apow(j):
        mag = jnp.exp(lam_re * dt * float(j))
        ang = lam_im * dt * float(j)
        return mag * jnp.cos(ang), mag * jnp.sin(ang)

    a_re, a_im = apow(1)
    den = lam_re * lam_re + lam_im * lam_im
    num_re = a_re - 1.0
    coef_re = (num_re * lam_re + a_im * lam_im) / den
    coef_im = (a_im * lam_re - num_re * lam_im) / den
    ri = lax.broadcasted_iota(jnp.int32, (S5_STATE, S5_SW), 0)
    ci = lax.broadcasted_iota(jnp.int32, (S5_STATE, S5_SW), 1)
    repeat = jnp.where(ri == (ci & (S5_STATE - 1)), 1.0, 0.0).astype(BF16)
    own = (jnp.right_shift(lax.broadcasted_iota(jnp.int32, (LANES, S5_SW), 0), S5_GROUP.bit_length() - 1)
           == jnp.right_shift(lax.broadcasted_iota(jnp.int32, (LANES, S5_SW), 1), S5_STATE.bit_length() - 1))

    def block_diag(ref):
        return jnp.where(own, sum(_dot(piece, repeat) for piece in _split3(ref[0, 0, 0])), 0.0)

    b_re = block_diag(bre_ref)
    b_im = block_diag(bim_ref)
    bb_re = coef_re * b_re - coef_im * b_im
    bb_im = coef_re * b_im + coef_im * b_re
    bbr = _split3(bb_re)
    bbi = _split3(bb_im)
    c_re = block_diag(cre_ref)
    c_im = block_diag(cim_ref)

    kmat = []
    for j in range(S5_T + 1):
        aj_re, aj_im = apow(j)
        pc_re = c_re * aj_re - c_im * aj_im
        pc_im = c_re * aj_im + c_im * aj_re
        if j >= 1:
            r = pl.multiple_of(jnp.where(fwd, j - 1, S5_T - j) * LANES, LANES)
            wout_ref[0, 0, 0, pl.ds(r, LANES), :] = jnp.concatenate([pc_re, -pc_im], axis=1).astype(BF16)
        if j < S5_T:
            pb_re = bb_re * aj_re - bb_im * aj_im
            pb_im = bb_re * aj_im + bb_im * aj_re
            r = pl.multiple_of(jnp.where(fwd, S5_T - 1 - j, j) * LANES, LANES)
            win_ref[0, 0, 0, pl.ds(r, LANES), :] = jnp.concatenate([pb_re, pb_im], axis=1).astype(BF16)
            kmat.append(_dot_nt_hi(bbr, _split3(pc_re)) - _dot_nt_hi(bbi, _split3(pc_im)))

    ri = lax.broadcasted_iota(jnp.int32, (LANES, LANES), 0)
    ci = lax.broadcasted_iota(jnp.int32, (LANES, LANES), 1)
    keep_upper = jnp.where(fwd, 1.0, 0.0)
    kmat[0] = kmat[0] + jnp.where(ri == ci, dsk_ref[0, 0], 0.0) * keep_upper
    for s in range(S5_T):
        for t in range(S5_T):
            tile = kmat[abs(t - s)]
            if t > s:
                tile = tile * keep_upper
            elif t < s:
                tile = tile * (1.0 - keep_upper)
            m_ref[0, 0, 0, s * LANES:(s + 1) * LANES, t * LANES:(t + 1) * LANES] = tile.astype(BF16)

    pow_ref[...] = jnp.zeros_like(pow_ref)
    for i in range(SUBLANES):
        for row, j in ((i, S5_T * (i + 1)), (SUBLANES + i, S5_T * (SUBLANES - i))):
            p_re, p_im = apow(j)
            pow_ref[0, 0, 0, 0, row:row + 1, :] = p_re
            pow_ref[0, 0, 0, 1, row:row + 1, :] = p_im
    p_re, p_im = apow(S5_T * 2 * SUBLANES)
    pow_ref[0, 0, 0, 0, 2 * SUBLANES:2 * SUBLANES + 1, :] = p_re
    pow_ref[0, 0, 0, 1, 2 * SUBLANES:2 * SUBLANES + 1, :] = p_im


def _s5_tables(lam, b_re, b_im, c_re, c_im, d_skip):
    lead = (DEPTH, 2, S5_BLK)

    def spec(*tail):
        return pl.BlockSpec((1, 1, 1) + tail, lambda l, d, b: (l, d, b) + (0,) * len(tail))

    return pl.pallas_call(
        _s5_tab_kernel,
        out_shape=(
            jax.ShapeDtypeStruct(lead + (S5_CW, S5_CW), BF16),
            jax.ShapeDtypeStruct(lead + (S5_CW, 2 * S5_SW), BF16),
            jax.ShapeDtypeStruct(lead + (S5_CW, 2 * S5_SW), BF16),
            jax.ShapeDtypeStruct(lead + (2, S5_POW_ROWS, S5_SW), F32),
        ),
        grid=lead,
        in_specs=[spec(3, S5_SW)] + [spec(LANES, S5_STATE)] * 4
        + [pl.BlockSpec((1, 1, 1, LANES), lambda l, d, b: (l, b, 0, 0))],
        out_specs=(spec(S5_CW, S5_CW), spec(S5_CW, 2 * S5_SW), spec(S5_CW, 2 * S5_SW),
                   spec(2, S5_POW_ROWS, S5_SW)),
        compiler_params=_cparams(("parallel", "parallel", "parallel")),
        name="s5_tables",
    )(lam, b_re, b_im, c_re, c_im, d_skip)


def _s5_param_layout(s5_lam_re, s5_lam_im, s5_log_dt, s5_b_re, s5_b_im, s5_c_re, s5_c_im, s5_d):
    lam = jnp.stack([s5_lam_re, s5_lam_im, jnp.broadcast_to(s5_log_dt[..., None], s5_lam_re.shape)], axis=2)
    lam = lam.reshape(DEPTH, 2, 3, S5_BLK, S5_SW).transpose(0, 1, 3, 2, 4)

    def rows_b(b):
        return jnp.swapaxes(b, -1, -2).reshape(DEPTH, 2, S5_BLK, LANES, S5_STATE)

    def rows_c(c):
        return c.reshape(DEPTH, 2, S5_BLK, LANES, S5_STATE)

    return (lam, rows_b(s5_b_re), rows_b(s5_b_im), rows_c(s5_c_re), rows_c(s5_c_im),
            s5_d.reshape(DEPTH, S5_BLK, 1, LANES))


def _s5_kernel(u_ref, m_ref, win_ref, wout_ref, pow_ref, h0_ref, y_ref, fin_ref, x_sc):
    d = pl.program_id(1)
    u = u_ref[0]
    x_sc[...] = _dot(u, win_ref[0, 0, 0])
    p_re = pow_ref[0, 0, 0, 0]
    p_im = pow_ref[0, 0, 0, 1]
    re = slice(0, S5_SW)
    im = slice(S5_SW, 2 * S5_SW)

    def madd(t_re, t_im, w_re, w_im, s_re, s_im):
        return t_re + w_re * s_re - w_im * s_im, t_im + w_re * s_im + w_im * s_re

    def shift_rows(val, k, n, idx, reverse):
        if reverse:
            return jnp.where(idx < n - k, pltpu.roll(val, val.shape[0] - k, axis=0), 0.0)
        return jnp.where(idx >= k, pltpu.roll(val, k, axis=0), 0.0)

    def pow_row(k):
        row = k - 1 if k <= SUBLANES else 2 * SUBLANES
        return p_re[row:row + 1], p_im[row:row + 1]

    def ctx_scan(reverse):
        s_re = x_sc[0:S5_ROWS_CTX, re]
        s_im = x_sc[0:S5_ROWS_CTX, im]
        c = lax.broadcasted_iota(jnp.int32, (S5_ROWS_CTX, 1), 0) & (CTX_CHUNKS - 1)
        def shift_ctx(val, k):
            if k % SUBLANES:
                return shift_rows(val, k, CTX_CHUNKS, c, reverse)
            v3 = val.reshape(N_CTX_SEQ, CTX_CHUNKS, S5_SW)
            pad = jnp.zeros((N_CTX_SEQ, k, S5_SW), F32)
            moved = (jnp.concatenate([v3[:, k:], pad], axis=1) if reverse
                     else jnp.concatenate([pad, v3[:, :CTX_CHUNKS - k]], axis=1))
            return moved.reshape(S5_ROWS_CTX, S5_SW)

        k = 1
        while k < CTX_CHUNKS:
            w_re, w_im = pow_row(k)
            s_re, s_im = madd(s_re, s_im, w_re, w_im, shift_ctx(s_re, k), shift_ctx(s_im, k))
            k *= 2
        last = 0 if reverse else CTX_CHUNKS - 1
        pick = (lax.broadcasted_iota(jnp.int32, (N_CTX_SEQ, S5_ROWS_CTX), 1)
                == lax.broadcasted_iota(jnp.int32, (N_CTX_SEQ, S5_ROWS_CTX), 0) * CTX_CHUNKS + last)
        pick = jnp.where(pick, 1.0, 0.0).astype(BF16)
        fin_ref[0, 0, :, re] = sum(_dot(pick, piece) for piece in _split3(s_re))
        fin_ref[0, 0, :, im] = sum(_dot(pick, piece) for piece in _split3(s_im))
        x_sc[0:S5_ROWS_CTX, re] = shift_rows(s_re, 1, CTX_CHUNKS, c, reverse)
        x_sc[0:S5_ROWS_CTX, im] = shift_rows(s_im, 1, CTX_CHUNKS, c, reverse)

    def lat_group(k, carry, reverse):
        g = (S5_LAT_GROUPS - 1 - k) if reverse else k
        r8 = lax.broadcasted_iota(jnp.int32, (SUBLANES, 1), 0)
        tab = slice(SUBLANES, 2 * SUBLANES) if reverse else slice(0, SUBLANES)
        out = []
        for q in range(N_LAT_SEQ):
            base = pl.multiple_of(S5_ROWS_CTX + q * LAT_CHUNKS + g * SUBLANES, SUBLANES)
            s_re = x_sc[pl.ds(base, SUBLANES), re]
            s_im = x_sc[pl.ds(base, SUBLANES), im]
            j = 1
            while j < SUBLANES:
                w_re, w_im = pow_row(j)
                s_re, s_im = madd(s_re, s_im, w_re, w_im,
                                  shift_rows(s_re, j, SUBLANES, r8, reverse), shift_rows(s_im, j, SUBLANES, r8, reverse))
                j *= 2
            c_re, c_im = carry[2 * q], carry[2 * q + 1]
            s_re, s_im = madd(s_re, s_im, p_re[tab], p_im[tab], c_re, c_im)
            edge = (r8 == SUBLANES - 1) if reverse else (r8 == 0)
            x_sc[pl.ds(base, SUBLANES), re] = jnp.where(edge, c_re, shift_rows(s_re, 1, SUBLANES, r8, reverse))
            x_sc[pl.ds(base, SUBLANES), im] = jnp.where(edge, c_im, shift_rows(s_im, 1, SUBLANES, r8, reverse))
            end = slice(0, 1) if reverse else slice(SUBLANES - 1, SUBLANES)
            out += [s_re[end], s_im[end]]
        return tuple(out)

    def run(reverse):
        ctx_scan(reverse)
        h0 = h0_ref[0, 0]
        init = []
        for q in range(N_LAT_SEQ):
            init += [h0[q:q + 1, re], h0[q:q + 1, im]]
        lax.fori_loop(0, S5_LAT_GROUPS, functools.partial(lat_group, reverse=reverse), tuple(init))

    @pl.when(d == 0)
    def _():
        run(False)

    @pl.when(d == 1)
    def _():
        run(True)

    y = _dot(u, m_ref[0, 0, 0]) + _dot_nt(x_sc[...].astype(BF16), wout_ref[0, 0, 0])

    @pl.when(d == 0)
    def _():
        y_ref[0] = y

    @pl.when(d == 1)
    def _():
        y_ref[0] += y


def _s5_main(u, m, w_in, w_out, pows, h0, layer):
    def tab(*tail):
        return pl.BlockSpec((1, 1, 1) + tail, lambda b, d: (layer, d, b) + (0,) * len(tail))

    return pl.pallas_call(
        _s5_kernel,
        out_shape=(jax.ShapeDtypeStruct((S5_BLK, S5_ROWS, S5_CW), F32),
                   jax.ShapeDtypeStruct((2, S5_BLK, N_CTX_SEQ, 2 * S5_SW), F32)),
        grid=(S5_BLK, 2),
        in_specs=[
            pl.BlockSpec((1, S5_ROWS, S5_CW), lambda b, d: (b, 0, 0)),
            tab(S5_CW, S5_CW), tab(S5_CW, 2 * S5_SW), tab(S5_CW, 2 * S5_SW), tab(2, S5_POW_ROWS, S5_SW),
            pl.BlockSpec((1, 1, N_LAT_SEQ, 2 * S5_SW), lambda b, d: (d, b, 0, 0)),
        ],
        out_specs=(
            pl.BlockSpec((1, S5_ROWS, S5_CW), lambda b, d: (b, 0, 0)),
            pl.BlockSpec((1, 1, N_CTX_SEQ, 2 * S5_SW), lambda b, d: (d, b, 0, 0)),
        ),
        scratch_shapes=[pltpu.VMEM((S5_ROWS, 2 * S5_SW), F32)],
        compiler_params=_cparams(("parallel", "arbitrary")),
        name="s5",
    )(u, m, w_in, w_out, pows, h0)


def _hg_constants():
    t = np.arange(HG_C)
    ae = np.zeros((2, HG_E_ROWS, HG_C), np.float32)
    mask = np.zeros((2, HG_LEVELS, HG_C, HG_C), np.float32)
    for lvl in range(HG_LEVELS):
        half = 1 << lvl
        pos = t % (2 * half)
        mid = t - pos + half - 1
        upper = pos >= half
        u = t[None, :]
        rows_upper = (u > mid[:, None]) & (u <= t[:, None])
        rows_lower = (u > t[:, None]) & (u <= mid[:, None])
        ae[0, lvl * HG_C:(lvl + 1) * HG_C] = np.where(upper[:, None], rows_upper, rows_lower)
        same = (t[:, None] // (2 * half)) == (t[None, :] // (2 * half))
        mask[0, lvl] = same & upper[:, None] & (~upper)[None, :]
    ae[0, HG_LEVELS * HG_C:(HG_LEVELS + 1) * HG_C] = t[None, :] <= t[:, None]
    ae[0, (HG_LEVELS + 1) * HG_C:] = 1.0
    ae[1] = ae[0][:, ::-1]
    ae[1, :(HG_LEVELS + 1) * HG_C] = ae[1, :(HG_LEVELS + 1) * HG_C].reshape(HG_LEVELS + 1, HG_C, HG_C)[:, ::-1].reshape(-1, HG_C)
    mask[1] = mask[0][:, ::-1, ::-1]
    ae2 = np.concatenate([ae, ae], axis=-1)
    mask2 = np.concatenate([mask, mask], axis=-1)
    pair_diag = np.kron(np.eye(2, dtype=np.float32), np.ones((HG_DK, HG_DK), np.float32))
    return ae2, mask2, pair_diag


def _hg_tile(direction, i):
    return i if direction == 0 else HG_TILES - 1 - i


HG_PAIR = 2 * HG_DK


def _pair_diag(x):
    zero = jnp.zeros((x.shape[0], HG_DK), x.dtype)
    return jnp.concatenate([jnp.concatenate([x[:, :HG_DK], zero], axis=1),
                            jnp.concatenate([zero, x[:, HG_DK:]], axis=1)], axis=0)


def _hg_chunk(direction, rows, q_ref, f_ref, v_ref, lb, ae_ref, mask_ref, pd_ref, o_ref, st_sc):
    f = lb + (1.0 - lb) * jax.nn.sigmoid(f_ref[rows, :])
    logf = jnp.log2(f)
    kk = 1.0 - f
    hi = logf.astype(BF16)
    lo = (logf - hi.astype(F32)).astype(BF16)
    ex = _dot(ae_ref[direction], jnp.concatenate([hi, lo], axis=0))
    dec = jnp.exp2(ex[:(HG_LEVELS + 1) * HG_C])
    cum = ex[HG_LEVELS * HG_C:(HG_LEVELS + 1) * HG_C]
    tot = ex[(HG_LEVELS + 1) * HG_C:(HG_LEVELS + 1) * HG_C + 1]
    q = q_ref[rows, :].astype(F32)
    v_bf = v_ref[rows, :]
    v = v_bf.astype(F32)
    q_in = (q * dec[HG_LEVELS * HG_C:]).astype(BF16)
    k_tail = (kk * jnp.exp2(tot - cum)).astype(BF16)
    dec_tot = jnp.exp2(tot)
    qk = q * kk
    outs = []
    for p in range(HG_HEADS // 2):
        sl = slice(p * HG_PAIR, (p + 1) * HG_PAIR)
        scores = jnp.zeros((HG_C, HG_PAIR), F32)
        for lvl in range(HG_LEVELS):
            g = dec[lvl * HG_C:(lvl + 1) * HG_C, sl]
            scores = scores + mask_ref[direction, lvl] * _dot_nt(
                (q[:, sl] * g).astype(BF16), _pair_diag((kk[:, sl] * g).astype(BF16)))
        diag = jnp.concatenate(
            [jnp.broadcast_to(jnp.sum(qk[:, (2 * p + j) * HG_DK:(2 * p + j + 1) * HG_DK], axis=-1, keepdims=True),
                              (HG_C, HG_DK)) for j in range(2)], axis=-1)
        st = st_sc[direction, p]
        outs.append(_dot(scores.astype(BF16), _pair_diag(v_bf[:, sl])) + diag * v[:, sl]
                    + _dot_nt(q_in[:, sl], st.astype(BF16)))
        st_new = st * dec_tot[:, sl] + pd_ref[...] * _dot_tn(v_bf[:, sl], k_tail[:, sl])
        st_sc[direction, p] = st_new
    o_ref[rows, :] = jnp.concatenate(outs, axis=-1)


def _hg_kernel(qf_ref, ff_ref, vf_ref, qb_ref, fb_ref, vb_ref, lb_ref, ae_ref, mask_ref, pd_ref, h0f_ref, h0b_ref,
               of_ref, ob_ref, finf_ref, finb_ref, st_sc, *, layer):
    i = pl.program_id(0)
    x = lb_ref[:, :, 0, :]
    e = jnp.exp(x - jnp.max(x, axis=0, keepdims=True))
    sm = e / jnp.sum(e, axis=0, keepdims=True)
    lb = jnp.sum(sm[1:layer + 1], axis=0) if layer > 0 else jnp.zeros((2, D_BRANCH), F32)
    for direction, h0_ref in ((0, h0f_ref), (1, h0b_ref)):
        tile = _hg_tile(direction, i)
        first_ctx = 0 if direction == 0 else HG_TILES_PER_CTX - 1
        first_lat = 0 if direction == 0 else HG_TILES_PER_LAT - 1
        is_start = jnp.where(tile < HG_CTX_TILES,
                             tile % HG_TILES_PER_CTX == first_ctx,
                             (tile - HG_CTX_TILES) % HG_TILES_PER_LAT == first_lat)

        @pl.when(is_start)
        def _(direction=direction, h0_ref=h0_ref):
            for p in range(HG_HEADS // 2):
                st_sc[direction, p] = _pair_diag(
                    jnp.concatenate([h0_ref[0, 0, 2 * p], h0_ref[0, 0, 2 * p + 1]], axis=1))

    for c in range(HG_TM // HG_C):
        fwd_rows = slice(c * HG_C, (c + 1) * HG_C)
        bwd_rows = slice(HG_TM - (c + 1) * HG_C, HG_TM - c * HG_C)
        _hg_chunk(0, fwd_rows, qf_ref, ff_ref, vf_ref, lb[0:1], ae_ref, mask_ref, pd_ref, of_ref, st_sc)
        _hg_chunk(1, bwd_rows, qb_ref, fb_ref, vb_ref, lb[1:2], ae_ref, mask_ref, pd_ref, ob_ref, st_sc)

    for direction, fin_ref in ((0, finf_ref), (1, finb_ref)):
        @pl.when(_hg_tile(direction, i) < HG_CTX_TILES)
        def _(direction=direction, fin_ref=fin_ref):
            for h in range(HG_HEADS):
                p, j = divmod(h, 2)
                fin_ref[0, h] = st_sc[direction, p, j * HG_DK:(j + 1) * HG_DK, j * HG_DK:(j + 1) * HG_DK]


def _hgrn(proj, fz, hg_lb, ae, mask, pair_diag, h0t, layer):
    def tok_spec(direction, col):
        return pl.BlockSpec((HG_TM, D_BRANCH), lambda i: (_hg_tile(direction, i), col))

    def fin_spec(direction):
        return pl.BlockSpec((1, HG_HEADS, HG_DK, HG_DK),
                            lambda i: (jnp.minimum(_hg_tile(direction, i) // HG_TILES_PER_CTX, N_CTX_SEQ - 1), 0, 0, 0))

    def h0_spec(direction):
        return pl.BlockSpec((1, 1, HG_HEADS, HG_DK, HG_DK),
                            lambda i: (_cond_row(_hg_tile(direction, i), HG_TM), direction, 0, 0, 0))

    fin_shape = jax.ShapeDtypeStruct((N_CTX_SEQ, HG_HEADS, HG_DK, HG_DK), F32)
    return pl.pallas_call(
        functools.partial(_hg_kernel, layer=layer),
        out_shape=(jax.ShapeDtypeStruct((N_TOK, D_BRANCH), F32), jax.ShapeDtypeStruct((N_TOK, D_BRANCH), F32),
                   fin_shape, fin_shape),
        grid=(HG_TILES,),
        in_specs=[
            tok_spec(0, HG_Q_COL), tok_spec(0, 0), tok_spec(0, HG_V_COL),
            tok_spec(1, HG_Q_COL), tok_spec(1, 1), tok_spec(1, HG_V_COL),
            pl.BlockSpec((DEPTH, 2, 1, D_BRANCH), lambda i: (0, 0, 0, 0)),
            pl.BlockSpec((2, HG_E_ROWS, 2 * HG_C), lambda i: (0, 0, 0)),
            pl.BlockSpec((2, HG_LEVELS, HG_C, HG_PAIR), lambda i: (0, 0, 0, 0)),
            pl.BlockSpec((HG_PAIR, HG_PAIR), lambda i: (0, 0)),
            h0_spec(0), h0_spec(1),
        ],
        out_specs=(
            pl.BlockSpec((HG_TM, D_BRANCH), lambda i: (_hg_tile(0, i), 0)),
            pl.BlockSpec((HG_TM, D_BRANCH), lambda i: (_hg_tile(1, i), 0)),
            fin_spec(0), fin_spec(1),
        ),
        scratch_shapes=[pltpu.VMEM((2, HG_HEADS // 2, HG_PAIR, HG_PAIR), F32)],
        compiler_params=_cparams(("arbitrary",)),
        name="hgrn",
    )(proj, fz, proj, proj, fz, proj, hg_lb.reshape(DEPTH, 2, 1, D_BRANCH), ae, mask, pair_diag, h0t, h0t)


MRG_TM = 256


def _gelu_tanh(x):
    return 0.5 * x * (1.0 + jnp.tanh(0.7978845608028654 * (x + 0.044715 * (x * x * x))))


def _merge_kernel(scb_ref, scc_ref, sch_ref, hgg_ref, pool_ref, g0_ref, g1_ref, g2_ref, g3_ref,
                  yrow_ref, of_ref, ob_ref, x_ref, mod_ref, conv_ref, gluw_ref, glub_ref, ng_ref,
                  poolw_ref, pscale_ref, wbr_ref, wout_ref, lng_ref, lnb_ref, o_ref,
                  ys5_sc, gluw_sc, poolw_sc, wbr_sc, wout_sc):
    tile = pl.program_id(0)

    @pl.when(tile == 0)
    def _():
        gluw_sc[...] = gluw_ref[0].astype(BF16)
        poolw_sc[...] = poolw_ref[0].astype(BF16)
        wbr_sc[...] = wbr_ref[0].astype(BF16)
        wout_sc[...] = wout_ref[0].astype(BF16)

    line = jnp.where(tile * MRG_TM < N_CTX_TOK, CTX_LEN, GRID_W)
    pos = lax.broadcasted_iota(jnp.int32, (MRG_TM, 1), 0) & (line - 1)

    def shifted(val, k):
        rolled = pltpu.roll(val, k % MRG_TM, axis=0)
        ok = (pos >= k) if k > 0 else (pos < line + k)
        return jnp.where(ok, rolled, 0.0)

    m = scc_ref[...].astype(F32) * sch_ref[...].astype(F32)
    conv = conv_ref[0, 0:1, :] * shifted(m, 1) + conv_ref[0, 1:2, :] * m + conv_ref[0, 2:3, :] * shifted(m, -1)
    ya = scb_ref[...].astype(F32) * conv

    for b in range(S5_BLK):
        for t in range(S5_T):
            ys5_sc[b, pl.ds(t, MRG_TM // S5_T, stride=S5_T), :] = yrow_ref[b, :, t * LANES:(t + 1) * LANES]
    z = _gelu_tanh(jnp.concatenate([ys5_sc[b] for b in range(S5_BLK)], axis=-1))
    yb = z * _sigmoid_tanh(_dot(z.astype(BF16), gluw_sc[...]) + glub_ref[0])

    o = of_ref[...] + ob_ref[...]
    normed = []
    for h in range(HG_HEADS):
        oh = o[:, h * HG_DK:(h + 1) * HG_DK]
        ms = jnp.mean(oh * oh, axis=-1, keepdims=True)
        normed.append(oh * lax.rsqrt(ms + LN_EPS) * ng_ref[0])
    gate_c = hgg_ref[...].astype(F32)
    yc = jnp.concatenate(normed, axis=-1) * (gate_c * _sigmoid_tanh(gate_c))

    pu = pool_ref[...].astype(F32)
    posf = pos.astype(F32)
    linef = line.astype(F32)
    pooled = []
    for gi, w in enumerate(POOL_WINDOWS):
        vg = pu[:, gi * POOL_GROUP:(gi + 1) * POOL_GROUP]
        back, fwd, span = vg, vg, 1
        while 2 * span <= w // 2:
            back = back + shifted(back, span)
            fwd = fwd + shifted(fwd, -span)
            span *= 2
        s = shifted(back, 1) + fwd
        cnt = jnp.minimum(posf + w // 2, linef) - jnp.maximum(posf - w // 2, 0.0)
        pg = s / cnt - vg
        pooled.append(_dot(pg.astype(BF16), poolw_sc[gi]))
    yd = jnp.concatenate(pooled, axis=-1) * pscale_ref[0]

    merged = jnp.zeros((MRG_TM, D_MODEL), F32)
    for k, (br, gate_ref) in enumerate(((ya, g0_ref), (yb, g1_ref), (yc, g2_ref), (yd, g3_ref))):
        merged = merged + jax.nn.sigmoid(gate_ref[...].astype(F32)) * _dot(br.astype(BF16), wbr_sc[k])
    mix = _dot(merged.astype(BF16), wout_sc[...])
    y = ALPHA * x_ref[...] + mod_ref[0, 0, 5:6, :] * mix
    o_ref[...] = _layer_norm(y, lng_ref[0, 0], lnb_ref[0, 0])


def _merge(proj, y_rows, o_f, o_b, x, mod, conv_w, glu_w, glu_b, norm_g, pool_w, pool_scale, w_branch, w_out,
           ln_g, ln_b, layer):
    def part(col):
        return pl.BlockSpec((MRG_TM, D_BRANCH), lambda i: (i, col))

    def gate(k):
        return pl.BlockSpec((MRG_TM, D_MODEL), lambda i: (i, N_IN_PARTS * D_BRANCH // D_MODEL + k))

    def per_layer(*shape, single=False):
        mode = dict(pipeline_mode=pl.Buffered(1)) if single else {}
        return pl.BlockSpec((1,) + shape, lambda i: (layer,) + (0,) * len(shape), **mode)

    return pl.pallas_call(
        _merge_kernel,
        out_shape=jax.ShapeDtypeStruct((N_TOK, D_MODEL), F32),
        grid=(N_TOK // MRG_TM,),
        in_specs=[
            part(0), part(1), part(2), part(8), part(9), gate(0), gate(1), gate(2), gate(3),
            pl.BlockSpec((S5_BLK, MRG_TM // S5_T, S5_CW), lambda i: (0, i, 0)),
            pl.BlockSpec((MRG_TM, D_BRANCH), lambda i: (i, 0)),
            pl.BlockSpec((MRG_TM, D_BRANCH), lambda i: (i, 0)),
            pl.BlockSpec((MRG_TM, D_MODEL), lambda i: (i, 0)),
            _mod_spec(layer, MRG_TM),
            per_layer(3, D_BRANCH), per_layer(D_BRANCH, D_BRANCH, single=True), per_layer(1, D_BRANCH),
            per_layer(1, HG_DK), per_layer(len(POOL_WINDOWS), POOL_GROUP, POOL_GROUP, single=True),
            per_layer(1, D_BRANCH), per_layer(N_BRANCH, D_BRANCH, D_MODEL, single=True),
            per_layer(D_MODEL, D_MODEL, single=True), _ln_spec(layer, 1), _ln_spec(layer, 1),
        ],
        out_specs=pl.BlockSpec((MRG_TM, D_MODEL), lambda i: (i, 0)),
        scratch_shapes=[
            pltpu.VMEM((S5_BLK, MRG_TM, LANES), F32),
            pltpu.VMEM((D_BRANCH, D_BRANCH), BF16),
            pltpu.VMEM((len(POOL_WINDOWS), POOL_GROUP, POOL_GROUP), BF16),
            pltpu.VMEM((N_BRANCH, D_BRANCH, D_MODEL), BF16),
            pltpu.VMEM((D_MODEL, D_MODEL), BF16),
        ],
        compiler_params=_cparams(("arbitrary",)),
        name="merge",
    )(proj, proj, proj, proj, proj, proj, proj, proj, proj, y_rows, o_f, o_b, x, mod,
      conv_w, glu_w, glu_b.reshape(DEPTH, 1, D_BRANCH), norm_g.reshape(DEPTH, 1, HG_DK), pool_w,
      pool_scale.reshape(DEPTH, 1, D_BRANCH), w_branch, w_out, ln_g, ln_b)


def _grid_pos_embedding():
    rows = LAT_LEN // GRID_W
    quarter = D_MODEL // 4
    omega = POS_BASE ** (-jnp.arange(quarter, dtype=F32) / quarter)
    ar = jnp.arange(rows, dtype=F32)[:, None] * omega
    ac = jnp.arange(GRID_W, dtype=F32)[:, None] * omega
    row_tab = jnp.concatenate([jnp.sin(ar), jnp.cos(ar)], -1)[:, None, :]
    col_tab = jnp.concatenate([jnp.sin(ac), jnp.cos(ac)], -1)[None, :, :]
    shape = (rows, GRID_W, 2 * quarter)
    pos = jnp.concatenate([jnp.broadcast_to(row_tab, shape), jnp.broadcast_to(col_tab, shape)], -1)
    return pos.reshape(LAT_LEN, D_MODEL)


def kernel(x_prompt, x_sample, state_s5_re, state_s5_im, state_hgrn, c, c_ctx, w_ada, b_ada, ln_g, ln_b,
           ffn_w1, ffn_w3, ffn_w2, w_in, sc_conv, s5_lam_re, s5_lam_im, s5_log_dt, s5_b_re, s5_b_im,
           s5_c_re, s5_c_im, s5_d, s5_glu_w, s5_glu_b, hg_lb, hg_norm_g, pool_w, pool_scale, w_branch, w_out):
    x = (x_prompt.reshape(N_CTX_TOK, D_MODEL), x_sample.reshape(N_LAT_TOK, D_MODEL), _grid_pos_embedding())

    cond = jnp.zeros((COND_PAD, D_MODEL), F32).at[0].set(c_ctx).at[1:N_COND].set(c)
    mod = _ada(cond, w_ada, b_ada).reshape(DEPTH, COND_PAD, N_SUB * 3, D_MODEL)
    ln_g4 = ln_g.reshape(DEPTH, N_SUB, 1, D_MODEL)
    ln_b4 = ln_b.reshape(DEPTH, N_SUB, 1, D_MODEL)

    s5_m, s5_win, s5_wout, s5_pow = _s5_tables(*_s5_param_layout(
        s5_lam_re, s5_lam_im, s5_log_dt, s5_b_re, s5_b_im, s5_c_re, s5_c_im, s5_d))

    ffn_w1, ffn_w3 = _cast_bf16(ffn_w1, ffn_w3)
    (ffn_w2,) = _cast_bf16(ffn_w2)
    w_in_bf = _cast_w_in(w_in)

    ae_np, mask_np, pair_np = _hg_constants()
    ae = jnp.asarray(ae_np, BF16)
    mask = jnp.asarray(mask_np, F32)
    pair_diag = jnp.asarray(pair_np, F32)

    fin_s5, fin_hg = [], []
    for l in range(DEPTH):
        x, h = _ffn(x, mod, ffn_w1, ffn_w3, ffn_w2, ln_g4, ln_b4, l, 0)
        proj, fz, u_rows = _inproj(h, w_in_bf, l)

        h0 = jnp.concatenate([
            state_s5_re[:, l].reshape(N_LAT_SEQ, 2, S5_BLK, S5_SW),
            state_s5_im[:, l].reshape(N_LAT_SEQ, 2, S5_BLK, S5_SW)], axis=-1).transpose(1, 2, 0, 3)
        y_rows, fin = _s5_main(u_rows, s5_m, s5_win, s5_wout, s5_pow, h0, l)
        fin_s5.append(fin)

        h0t = jnp.concatenate([jnp.zeros((1, 2, HG_HEADS, HG_DK, HG_DK), F32),
                               jnp.swapaxes(state_hgrn[:, l], -1, -2)], axis=0)
        o_f, o_b, fin_f, fin_b = _hgrn(proj, fz, hg_lb, ae, mask, pair_diag, h0t, l)
        fin_hg.append(jnp.stack([fin_f, fin_b], axis=1))

        x = _merge(proj, y_rows, o_f, o_b, x, mod, sc_conv, s5_glu_w, s5_glu_b, hg_norm_g,
                   pool_w, pool_scale, w_branch, w_out, ln_g4, ln_b4, l)
        x = _ffn(x, mod, ffn_w1, ffn_w3, ffn_w2, ln_g4, ln_b4, l, 2, split_output=(l == DEPTH - 1))

    y_prompt = x[0].reshape(N_CTX_SEQ, CTX_LEN, D_MODEL)
    y_sample = x[1].reshape(N_LAT_SEQ, LAT_LEN, D_MODEL)
    fin = jnp.stack(fin_s5)
    fin = fin.reshape(DEPTH, 2, S5_BLK, N_CTX_SEQ, 2, S5_GPB, S5_STATE)
    fin = fin.transpose(4, 3, 0, 1, 2, 5, 6).reshape(2, N_CTX_SEQ, DEPTH, 2, S5_GROUPS, S5_STATE)
    new_hgrn = jnp.swapaxes(jnp.stack(fin_hg, axis=1), -1, -2)
    return y_prompt, y_sample, fin[0], fin[1], new_hgrn
```

```python
import functools

import numpy as np
import jax
import jax.numpy as jnp
from jax import lax
from jax.experimental import pallas as pl
from jax.experimental.pallas import tpu as pltpu

F32 = jnp.float32
BF16 = jnp.bfloat16

D_MODEL = 1024
N_CTX_SEQ = 16
CTX_LEN = 256
DEPTH = 2
N_LAT_SEQ = 2
LAT_LEN = 4096
GRID_W = 64
D_BRANCH = 512
N_BRANCH = 4
S5_GROUPS = 32
S5_GROUP = 16
S5_STATE = 64
HG_HEADS = 4
HG_DK = 128
POOL_WINDOWS = (2, 4, 8, 16)
POOL_GROUP = 128
D_FF = 2816
N_SUB = 3
N_IN_PARTS = 10
IN_WIDTH = N_IN_PARTS * D_BRANCH + N_BRANCH * D_MODEL
ADA_WIDTH = N_SUB * 3 * D_MODEL
ALPHA = (2 * DEPTH) ** 0.25
LN_EPS = 1e-5
POS_BASE = 10000.0

N_CTX_TOK = N_CTX_SEQ * CTX_LEN
N_LAT_TOK = N_LAT_SEQ * LAT_LEN
N_TOK = N_CTX_TOK + N_LAT_TOK
N_COND = 1 + N_LAT_SEQ
COND_PAD = 8

LANES = 128
SUBLANES = 8
VMEM_LIMIT = 56 * 1024 * 1024

S5_T = 8
S5_BLK = 4
S5_GPB = S5_GROUPS // S5_BLK
S5_SW = S5_GPB * S5_STATE
S5_CW = S5_T * LANES
CTX_CHUNKS = CTX_LEN // S5_T
LAT_CHUNKS = LAT_LEN // S5_T
S5_ROWS_CTX = N_CTX_TOK // S5_T
S5_ROWS_LAT = N_LAT_TOK // S5_T
S5_ROWS = S5_ROWS_CTX + S5_ROWS_LAT
S5_LAT_GROUPS = LAT_CHUNKS // SUBLANES
S5_POW_ROWS = 24
S5_U_COL = 3

HG_C = 128
HG_LEVELS = 7
HG_TM = 256
HG_TILES = N_TOK // HG_TM
HG_CTX_TILES = N_CTX_TOK // HG_TM
HG_TILES_PER_CTX = CTX_LEN // HG_TM
HG_TILES_PER_LAT = LAT_LEN // HG_TM
HG_E_ROWS = (HG_LEVELS + 1) * HG_C + 8


def _cparams(sem):
    return pltpu.CompilerParams(dimension_semantics=sem, vmem_limit_bytes=VMEM_LIMIT)


def _cond_row(tile, tile_tokens):
    tok = tile * tile_tokens
    return jnp.where(tok < N_CTX_TOK, 0, 1 + (tok - N_CTX_TOK) // LAT_LEN)


def _dot(a, b):
    return jnp.dot(a, b, preferred_element_type=F32)


def _dot_nt(a, b):
    return lax.dot_general(a, b, (((1,), (1,)), ((), ())), preferred_element_type=F32)


def _dot_tn(a, b):
    return lax.dot_general(a, b, (((0,), (0,)), ((), ())), preferred_element_type=F32)


def _split3(x):
    h1 = x.astype(BF16)
    r1 = x - h1.astype(F32)
    h2 = r1.astype(BF16)
    h3 = (r1 - h2.astype(F32)).astype(BF16)
    return h1, h2, h3


def _dot_nt_hi(a3, b3):
    acc = None
    for x in range(2):
        for y in range(2 - x):
            term = _dot_nt(a3[x], b3[y])
            acc = term if acc is None else acc + term
    return acc


def _silu(x):
    return x * jax.nn.sigmoid(x)


def _sigmoid_tanh(x):
    return 0.5 * jnp.tanh(0.5 * x) + 0.5


def _layer_norm(y, g, b):
    mu = jnp.mean(y, axis=-1, keepdims=True)
    yc = y - mu
    var = jnp.mean(yc * yc, axis=-1, keepdims=True)
    return yc * lax.rsqrt(var + LN_EPS) * g + b


def _mod_spec(layer, tile_tokens):
    return pl.BlockSpec((1, 1, N_SUB * 3, D_MODEL),
                        lambda i, *_: (layer, _cond_row(i, tile_tokens), 0, 0))


def _ln_spec(layer, sub):
    return pl.BlockSpec((1, 1, 1, D_MODEL), lambda *_: (layer, sub, 0, 0))


ADA_TN = 1152


def _ada_kernel(c_ref, w_ref, b_ref, o_ref):
    s = _silu(c_ref[...]).astype(BF16)
    o_ref[0] = _dot(s, w_ref[0].astype(BF16)) + b_ref[0]


def _ada(cond, w_ada, b_ada):
    return pl.pallas_call(
        _ada_kernel,
        out_shape=jax.ShapeDtypeStruct((DEPTH, COND_PAD, ADA_WIDTH), F32),
        grid=(DEPTH, ADA_WIDTH // ADA_TN),
        in_specs=[
            pl.BlockSpec((COND_PAD, D_MODEL), lambda l, j: (0, 0)),
            pl.BlockSpec((1, D_MODEL, ADA_TN), lambda l, j: (l, 0, j)),
            pl.BlockSpec((1, 1, ADA_TN), lambda l, j: (l, 0, j)),
        ],
        out_specs=pl.BlockSpec((1, COND_PAD, ADA_TN), lambda l, j: (l, 0, j)),
        compiler_params=_cparams(("parallel", "parallel")),
        name="ada",
    )(cond, w_ada, b_ada.reshape(DEPTH, 1, ADA_WIDTH))


CAST_ROWS = 512


def _cast_kernel(*refs):
    n = len(refs) // 2
    for src, dst in zip(refs[:n], refs[n:]):
        dst[...] = src[...].astype(BF16)


def _cast_bf16(*ws):
    shape = ws[0].shape
    rows, cols = int(np.prod(shape[:-1])), shape[-1]
    spec = pl.BlockSpec((CAST_ROWS, cols), lambda i: (i, 0))
    outs = pl.pallas_call(
        _cast_kernel,
        out_shape=tuple(jax.ShapeDtypeStruct((rows, cols), BF16) for _ in ws),
        grid=(rows // CAST_ROWS,),
        in_specs=[spec] * len(ws),
        out_specs=tuple([spec] * len(ws)),
        compiler_params=_cparams(("parallel",)),
        name="cast_bf16",
    )(*(w.reshape(rows, cols) for w in ws))
    return tuple(o.reshape(shape) for o in outs)


FFN_TM = 1024
FFN_TM_SPLIT = 512
FFN_TC = 256


def _ffn_kernel(*refs, sub, tm, split_input, split_output, emit_next):
    ctx_tiles = N_CTX_TOK // tm
    is_ctx = pl.program_id(0) < ctx_tiles
    if split_input:
        ctx_ref, lat_ref, pos_ref = refs[:3]
        refs = refs[3:]
        x = jnp.where(is_ctx, ctx_ref[...], lat_ref[...] + pos_ref[...])
    else:
        x = refs[0][...]
        refs = refs[1:]
    mod_ref, w1_ref, w3_ref, w2_ref, g_ref, b_ref = refs[:6]
    out_refs, act_sc = refs[6:-1], refs[-1]
    shift = mod_ref[0, 0, 3 * sub:3 * sub + 1, :]
    scale = mod_ref[0, 0, 3 * sub + 1:3 * sub + 2, :]
    gate = mod_ref[0, 0, 3 * sub + 2:3 * sub + 3, :]
    h = (x * (1.0 + scale) + shift).astype(BF16)
    for c in range(D_FF // FFN_TC):
        cols = slice(c * FFN_TC, (c + 1) * FFN_TC)
        a = _dot(h, w1_ref[0, 0, :, cols])
        b = _dot(h, w3_ref[0, 0, :, cols])
        act_sc[:, cols] = (_silu(a) * b).astype(BF16)
    f = _dot(act_sc[...], w2_ref[0, 0])
    y = _layer_norm(ALPHA * x + gate * (0.5 * f), g_ref[0, 0], b_ref[0, 0])
    if emit_next:
        nxt = 3 * (sub + 1)
        out_refs[1][...] = (y * (1.0 + mod_ref[0, 0, nxt + 1:nxt + 2, :]) + mod_ref[0, 0, nxt:nxt + 1, :]).astype(BF16)
    if split_output:
        @pl.when(is_ctx)
        def _():
            out_refs[0][...] = y

        @pl.when(jnp.logical_not(is_ctx))
        def _():
            out_refs[1][...] = y
    else:
        out_refs[0][...] = y


def _ffn(x, mod, w1, w3, w2, ln_g, ln_b, layer, sub, split_output=False):
    which = sub // 2
    split_input = isinstance(x, tuple)
    tm = FFN_TM_SPLIT if split_input else FFN_TM
    ctx_tiles = N_CTX_TOK // tm

    def resident(*shape):
        return pl.BlockSpec((1, 1) + shape, lambda i: (layer, which, 0, 0), pipeline_mode=pl.Buffered(1))

    def ctx_tile(i):
        return (jnp.minimum(i, ctx_tiles - 1), 0)

    def lat_tile(i):
        return (jnp.maximum(i - ctx_tiles, 0), 0)

    if split_input:
        pos_tiles = LAT_LEN // tm
        x_specs = [
            pl.BlockSpec((tm, D_MODEL), ctx_tile), pl.BlockSpec((tm, D_MODEL), lat_tile),
            pl.BlockSpec((tm, D_MODEL), lambda i: (jnp.maximum(i - ctx_tiles, 0) % pos_tiles, 0)),
        ]
    else:
        x_specs = [pl.BlockSpec((tm, D_MODEL), lambda i: (i, 0))]
        x = (x,)
    emit_next = sub == 0
    if split_output:
        out_shape = (jax.ShapeDtypeStruct((N_CTX_TOK, D_MODEL), F32), jax.ShapeDtypeStruct((N_LAT_TOK, D_MODEL), F32))
        out_specs = (pl.BlockSpec((tm, D_MODEL), ctx_tile), pl.BlockSpec((tm, D_MODEL), lat_tile))
    elif emit_next:
        out_shape = (jax.ShapeDtypeStruct((N_TOK, D_MODEL), F32), jax.ShapeDtypeStruct((N_TOK, D_MODEL), BF16))
        out_specs = (pl.BlockSpec((tm, D_MODEL), lambda i: (i, 0)), pl.BlockSpec((tm, D_MODEL), lambda i: (i, 0)))
    else:
        out_shape = jax.ShapeDtypeStruct((N_TOK, D_MODEL), F32)
        out_specs = pl.BlockSpec((tm, D_MODEL), lambda i: (i, 0))

    return pl.pallas_call(
        functools.partial(_ffn_kernel, sub=sub, tm=tm, split_input=split_input, split_output=split_output,
                          emit_next=emit_next),
        out_shape=out_shape,
        grid=(N_TOK // tm,),
        in_specs=x_specs + [
            _mod_spec(layer, tm),
            resident(D_MODEL, D_FF), resident(D_MODEL, D_FF), resident(D_FF, D_MODEL),
            _ln_spec(layer, sub), _ln_spec(layer, sub),
        ],
        out_specs=out_specs,
        scratch_shapes=[pltpu.VMEM((tm, D_FF), BF16)],
        compiler_params=_cparams(("arbitrary",) if split_output else ("parallel",)),
        name=f"ffn{sub}",
    )(*x, mod, w1, w3, w2, ln_g, ln_b)


INP_TM = 2048
INP_PARTS = 2
INP_TN = INP_PARTS * D_BRANCH
INP_STEPS = IN_WIDTH // INP_TN
HG_Q_COL, HG_F_COL, HG_V_COL = 4, 5, 7


def _cast_w_in_kernel(w_ref, o_ref):
    o_ref[0, 0] = w_ref[0].astype(BF16)


def _cast_w_in(w_in):
    return pl.pallas_call(
        _cast_w_in_kernel,
        out_shape=jax.ShapeDtypeStruct((DEPTH, INP_STEPS, D_MODEL, INP_TN), BF16),
        grid=(DEPTH, INP_STEPS),
        in_specs=[pl.BlockSpec((1, D_MODEL, INP_TN), lambda l, j: (l, 0, j))],
        out_specs=pl.BlockSpec((1, 1, D_MODEL, INP_TN), lambda l, j: (l, j, 0, 0)),
        compiler_params=_cparams(("parallel", "parallel")),
        name="cast_w_in",
    )(w_in)


def _inproj_kernel(h_ref, w_ref, o_ref, fz_ref, u_ref, blk_sc):
    j = pl.program_id(1)

    def project(gate=False):
        res = _dot(h_ref[...], w_ref[0, 0])
        o_ref[...] = (_sigmoid_tanh(res) if gate else res).astype(BF16)
        return res

    def part(res, p):
        k = p % INP_PARTS
        return res[:, k * D_BRANCH:(k + 1) * D_BRANCH]

    special = (S5_U_COL, HG_F_COL, HG_F_COL + 1)
    steps = [p // INP_PARTS for p in special]
    assert len(set(steps)) == len(steps)

    first_gate = N_IN_PARTS // INP_PARTS
    assert N_IN_PARTS % INP_PARTS == 0 and max(steps) < first_gate

    @pl.when(functools.reduce(jnp.logical_and, [j != s for s in steps] + [j < first_gate]))
    def _():
        project()

    @pl.when(j >= first_gate)
    def _():
        project(gate=True)

    for p in (HG_F_COL, HG_F_COL + 1):
        @pl.when(j == p // INP_PARTS)
        def _(p=p):
            fz_ref[...] = part(project(), p)

    @pl.when(j == S5_U_COL // INP_PARTS)
    def _():
        s5_u = part(project(), S5_U_COL)
        for b in range(S5_BLK):
            blk_sc[b] = s5_u[:, b * LANES:(b + 1) * LANES]
            for t in range(S5_T):
                rows = blk_sc[b, pl.ds(t, INP_TM // S5_T, stride=S5_T), :]
                u_ref[b, :, t * LANES:(t + 1) * LANES] = rows.astype(BF16)


def _inproj(h, w_in_bf, layer):
    return pl.pallas_call(
        _inproj_kernel,
        out_shape=(jax.ShapeDtypeStruct((N_TOK, IN_WIDTH), BF16),
                   jax.ShapeDtypeStruct((N_TOK, 2 * D_BRANCH), F32),
                   jax.ShapeDtypeStruct((S5_BLK, S5_ROWS, S5_CW), BF16)),
        grid=(N_TOK // INP_TM, INP_STEPS),
        in_specs=[
            pl.BlockSpec((INP_TM, D_MODEL), lambda i, j: (i, 0)),
            pl.BlockSpec((1, 1, D_MODEL, INP_TN), lambda i, j: (layer, j, 0, 0)),
        ],
        out_specs=(pl.BlockSpec((INP_TM, INP_TN), lambda i, j: (i, j)),
                   pl.BlockSpec((INP_TM, D_BRANCH),
                                lambda i, j: (i, jnp.where(j > HG_F_COL // INP_PARTS, 1, 0))),
                   pl.BlockSpec((S5_BLK, INP_TM // S5_T, S5_CW), lambda i, j: (0, i, 0))),
        scratch_shapes=[pltpu.VMEM((S5_BLK, INP_TM, LANES), F32)],
        compiler_params=_cparams(("parallel", "arbitrary")),
        name="inproj",
    )(h, w_in_bf)


def _s5_tab_kernel(lam_ref, bre_ref, bim_ref, cre_ref, cim_ref, dsk_ref, m_ref, win_ref, wout_ref, pow_ref):
    d = pl.program_id(1)
    fwd = d == 0
    lam_re = lam_ref[0, 0, 0, 0:1, :]
    lam_im = lam_ref[0, 0, 0, 1:2, :]
    dt = jnp.exp(lam_ref[0, 0, 0, 2:3, :])

    def apow(j):
        mag = jnp.exp(lam_re * dt * float(j))
        ang = lam_im * dt * float(j)
        return mag * jnp.cos(ang), mag * jnp.sin(ang)

    a_re, a_im = apow(1)
    den = lam_re * lam_re + lam_im * lam_im
    num_re = a_re - 1.0
    coef_re = (num_re * lam_re + a_im * lam_im) / den
    coef_im = (a_im * lam_re - num_re * lam_im) / den
    ri = lax.broadcasted_iota(jnp.int32, (S5_STATE, S5_SW), 0)
    ci = lax.broadcasted_iota(jnp.int32, (S5_STATE, S5_SW), 1)
    repeat = jnp.where(ri == (ci & (S5_STATE - 1)), 1.0, 0.0).astype(BF16)
    own = (jnp.right_shift(lax.broadcasted_iota(jnp.int32, (LANES, S5_SW), 0), S5_GROUP.bit_length() - 1)
           == jnp.right_shift(lax.broadcasted_iota(jnp.int32, (LANES, S5_SW), 1), S5_STATE.bit_length() - 1))

    def block_diag(ref):
        return jnp.where(own, sum(_dot(piece, repeat) for piece in _split3(ref[0, 0, 0])), 0.0)

    b_re = block_diag(bre_ref)
    b_im = block_diag(bim_ref)
    bb_re = coef_re * b_re - coef_im * b_im
    bb_im = coef_re * b_im + coef_im * b_re
    bbr = _split3(bb_re)
    bbi = _split3(bb_im)
    c_re = block_diag(cre_ref)
    c_im = block_diag(cim_ref)

    kmat = []
    for j in range(S5_T + 1):
        aj_re, aj_im = apow(j)
        pc_re = c_re * aj_re - c_im * aj_im
        pc_im = c_re * aj_im + c_im * aj_re
        if j >= 1:
            r = pl.multiple_of(jnp.where(fwd, j - 1, S5_T - j) * LANES, LANES)
            wout_ref[0, 0, 0, pl.ds(r, LANES), :] = jnp.concatenate([pc_re, -pc_im], axis=1).astype(BF16)
        if j < S5_T:
            pb_re = bb_re * aj_re - bb_im * aj_im
            pb_im = bb_re * aj_im + bb_im * aj_re
            r = pl.multiple_of(jnp.where(fwd, S5_T - 1 - j, j) * LANES, LANES)
            win_ref[0, 0, 0, pl.ds(r, LANES), :] = jnp.concatenate([pb_re, pb_im], axis=1).astype(BF16)
            kmat.append(_dot_nt_hi(bbr, _split3(pc_re)) - _dot_nt_hi(bbi, _split3(pc_im)))

    ri = lax.broadcasted_iota(jnp.int32, (LANES, LANES), 0)
    ci = lax.broadcasted_iota(jnp.int32, (LANES, LANES), 1)
    keep_upper = jnp.where(fwd, 1.0, 0.0)
    kmat[0] = kmat[0] + jnp.where(ri == ci, dsk_ref[0, 0], 0.0) * keep_upper
    for s in range(S5_T):
        for t in range(S5_T):
            tile = kmat[abs(t - s)]
            if t > s:
                tile = tile * keep_upper
            elif t < s:
                tile = tile * (1.0 - keep_upper)
            m_ref[0, 0, 0, s * LANES:(s + 1) * LANES, t * LANES:(t + 1) * LANES] = tile.astype(BF16)

    pow_ref[...] = jnp.zeros_like(pow_ref)
    for i in range(SUBLANES):
        for row, j in ((i, S5_T * (i + 1)), (SUBLANES + i, S5_T * (SUBLANES - i))):
            p_re, p_im = apow(j)
            pow_ref[0, 0, 0, 0, row:row + 1, :] = p_re
            pow_ref[0, 0, 0, 1, row:row + 1, :] = p_im
    p_re, p_im = apow(S5_T * 2 * SUBLANES)
    pow_ref[0, 0, 0, 0, 2 * SUBLANES:2 * SUBLANES + 1, :] = p_re
    pow_ref[0, 0, 0, 1, 2 * SUBLANES:2 * SUBLANES + 1, :] = p_im


def _s5_tables(lam, b_re, b_im, c_re, c_im, d_skip):
    lead = (DEPTH, 2, S5_BLK)

    def spec(*tail):
        return pl.BlockSpec((1, 1, 1) + tail, lambda l, d, b: (l, d, b) + (0,) * len(tail))

    return pl.pallas_call(
        _s5_tab_kernel,
        out_shape=(
            jax.ShapeDtypeStruct(lead + (S5_CW, S5_CW), BF16),
            jax.ShapeDtypeStruct(lead + (S5_CW, 2 * S5_SW), BF16),
            jax.ShapeDtypeStruct(lead + (S5_CW, 2 * S5_SW), BF16),
            jax.ShapeDtypeStruct(lead + (2, S5_POW_ROWS, S5_SW), F32),
        ),
        grid=lead,
        in_specs=[spec(3, S5_SW)] + [spec(LANES, S5_STATE)] * 4
        + [pl.BlockSpec((1, 1, 1, LANES), lambda l, d, b: (l, b, 0, 0))],
        out_specs=(spec(S5_CW, S5_CW), spec(S5_CW, 2 * S5_SW), spec(S5_CW, 2 * S5_SW),
                   spec(2, S5_POW_ROWS, S5_SW)),
        compiler_params=_cparams(("parallel", "parallel", "parallel")),
        name="s5_tables",
    )(lam, b_re, b_im, c_re, c_im, d_skip)


def _s5_param_layout(s5_lam_re, s5_lam_im, s5_log_dt, s5_b_re, s5_b_im, s5_c_re, s5_c_im, s5_d):
    lam = jnp.stack([s5_lam_re, s5_lam_im, jnp.broadcast_to(s5_log_dt[..., None], s5_lam_re.shape)], axis=2)
    lam = lam.reshape(DEPTH, 2, 3, S5_BLK, S5_SW).transpose(0, 1, 3, 2, 4)

    def rows_b(b):
        return jnp.swapaxes(b, -1, -2).reshape(DEPTH, 2, S5_BLK, LANES, S5_STATE)

    def rows_c(c):
        return c.reshape(DEPTH, 2, S5_BLK, LANES, S5_STATE)

    return (lam, rows_b(s5_b_re), rows_b(s5_b_im), rows_c(s5_c_re), rows_c(s5_c_im),
            s5_d.reshape(DEPTH, S5_BLK, 1, LANES))


def _s5_kernel(u_ref, m_ref, win_ref, wout_ref, pow_ref, h0_ref, y_ref, fin_ref, x_sc):
    d = pl.program_id(1)
    u = u_ref[0]
    x_sc[...] = _dot(u, win_ref[0, 0, 0])
    p_re = pow_ref[0, 0, 0, 0]
    p_im = pow_ref[0, 0, 0, 1]
    re = slice(0, S5_SW)
    im = slice(S5_SW, 2 * S5_SW)

    def madd(t_re, t_im, w_re, w_im, s_re, s_im):
        return t_re + w_re * s_re - w_im * s_im, t_im + w_re * s_im + w_im * s_re

    def shift_rows(val, k, n, idx, reverse):
        if reverse:
            return jnp.where(idx < n - k, pltpu.roll(val, val.shape[0] - k, axis=0), 0.0)
        return jnp.where(idx >= k, pltpu.roll(val, k, axis=0), 0.0)

    def pow_row(k):
        row = k - 1 if k <= SUBLANES else 2 * SUBLANES
        return p_re[row:row + 1], p_im[row:row + 1]

    def ctx_scan(reverse):
        s_re = x_sc[0:S5_ROWS_CTX, re]
        s_im = x_sc[0:S5_ROWS_CTX, im]
        c = lax.broadcasted_iota(jnp.int32, (S5_ROWS_CTX, 1), 0) & (CTX_CHUNKS - 1)
        def shift_ctx(val, k):
            if k % SUBLANES:
                return shift_rows(val, k, CTX_CHUNKS, c, reverse)
            v3 = val.reshape(N_CTX_SEQ, CTX_CHUNKS, S5_SW)
            pad = jnp.zeros((N_CTX_SEQ, k, S5_SW), F32)
            moved = (jnp.concatenate([v3[:, k:], pad], axis=1) if reverse
                     else jnp.concatenate([pad, v3[:, :CTX_CHUNKS - k]], axis=1))
            return moved.reshape(S5_ROWS_CTX, S5_SW)

        k = 1
        while k < CTX_CHUNKS:
            w_re, w_im = pow_row(k)
            s_re, s_im = madd(s_re, s_im, w_re, w_im, shift_ctx(s_re, k), shift_ctx(s_im, k))
            k *= 2
        last = 0 if reverse else CTX_CHUNKS - 1
        pick = (lax.broadcasted_iota(jnp.int32, (N_CTX_SEQ, S5_ROWS_CTX), 1)
                == lax.broadcasted_iota(jnp.int32, (N_CTX_SEQ, S5_ROWS_CTX), 0) * CTX_CHUNKS + last)
        pick = jnp.where(pick, 1.0, 0.0).astype(BF16)
        fin_ref[0, 0, :, re] = sum(_dot(pick, piece) for piece in _split3(s_re))
        fin_ref[0, 0, :, im] = sum(_dot(pick, piece) for piece in _split3(s_im))
        x_sc[0:S5_ROWS_CTX, re] = shift_rows(s_re, 1, CTX_CHUNKS, c, reverse)
        x_sc[0:S5_ROWS_CTX, im] = shift_rows(s_im, 1, CTX_CHUNKS, c, reverse)

    def lat_group(k, carry, reverse):
        g = (S5_LAT_GROUPS - 1 - k) if reverse else k
        r8 = lax.broadcasted_iota(jnp.int32, (SUBLANES, 1), 0)
        tab = slice(SUBLANES, 2 * SUBLANES) if reverse else slice(0, SUBLANES)
        out = []
        for q in range(N_LAT_SEQ):
            base = pl.multiple_of(S5_ROWS_CTX + q * LAT_CHUNKS + g * SUBLANES, SUBLANES)
            s_re = x_sc[pl.ds(base, SUBLANES), re]
            s_im = x_sc[pl.ds(base, SUBLANES), im]
            j = 1
            while j < SUBLANES:
                w_re, w_im = pow_row(j)
                s_re, s_im = madd(s_re, s_im, w_re, w_im,
                                  shift_rows(s_re, j, SUBLANES, r8, reverse), shift_rows(s_im, j, SUBLANES, r8, reverse))
                j *= 2
            c_re, c_im = carry[2 * q], carry[2 * q + 1]
            s_re, s_im = madd(s_re, s_im, p_re[tab], p_im[tab], c_re, c_im)
            edge = (r8 == SUBLANES - 1) if reverse else (r8 == 0)
            x_sc[pl.ds(base, SUBLANES), re] = jnp.where(edge, c_re, shift_rows(s_re, 1, SUBLANES, r8, reverse))
            x_sc[pl.ds(base, SUBLANES), im] = jnp.where(edge, c_im, shift_rows(s_im, 1, SUBLANES, r8, reverse))
            end = slice(0, 1) if reverse else slice(SUBLANES - 1, SUBLANES)
            out += [s_re[end], s_im[end]]
        return tuple(out)

    def run(reverse):
        y_local = _dot(u, m_ref[0, 0, 0])
        ctx_scan(reverse)
        if reverse:
            y_ref[0] += y_local
        else:
            y_ref[0] = y_local
        h0 = h0_ref[0, 0]
        init = []
        for q in range(N_LAT_SEQ):
            init += [h0[q:q + 1, re], h0[q:q + 1, im]]
        lax.fori_loop(0, S5_LAT_GROUPS, functools.partial(lat_group, reverse=reverse), tuple(init))

    @pl.when(d == 0)
    def _():
        run(False)

    @pl.when(d == 1)
    def _():
        run(True)

    y_ref[0] += _dot_nt(x_sc[...].astype(BF16), wout_ref[0, 0, 0])


def _s5_main(u, m, w_in, w_out, pows, h0, layer):
    def tab(*tail):
        return pl.BlockSpec((1, 1, 1) + tail, lambda b, d: (layer, d, b) + (0,) * len(tail))

    return pl.pallas_call(
        _s5_kernel,
        out_shape=(jax.ShapeDtypeStruct((S5_BLK, S5_ROWS, S5_CW), F32),
                   jax.ShapeDtypeStruct((2, S5_BLK, N_CTX_SEQ, 2 * S5_SW), F32)),
        grid=(S5_BLK, 2),
        in_specs=[
            pl.BlockSpec((1, S5_ROWS, S5_CW), lambda b, d: (b, 0, 0)),
            tab(S5_CW, S5_CW), tab(S5_CW, 2 * S5_SW), tab(S5_CW, 2 * S5_SW), tab(2, S5_POW_ROWS, S5_SW),
            pl.BlockSpec((1, 1, N_LAT_SEQ, 2 * S5_SW), lambda b, d: (d, b, 0, 0)),
        ],
        out_specs=(
            pl.BlockSpec((1, S5_ROWS, S5_CW), lambda b, d: (b, 0, 0)),
            pl.BlockSpec((1, 1, N_CTX_SEQ, 2 * S5_SW), lambda b, d: (d, b, 0, 0)),
        ),
        scratch_shapes=[pltpu.VMEM((S5_ROWS, 2 * S5_SW), F32)],
        compiler_params=_cparams(("parallel", "arbitrary")),
        name="s5",
    )(u, m, w_in, w_out, pows, h0)


def _hg_constants():
    t = np.arange(HG_C)
    ae = np.zeros((2, HG_E_ROWS, HG_C), np.float32)
    mask = np.zeros((2, HG_LEVELS, HG_C, HG_C), np.float32)
    for lvl in range(HG_LEVELS):
        half = 1 << lvl
        pos = t % (2 * half)
        mid = t - pos + half - 1
        upper = pos >= half
        u = t[None, :]
        rows_upper = (u > mid[:, None]) & (u <= t[:, None])
        rows_lower = (u > t[:, None]) & (u <= mid[:, None])
        ae[0, lvl * HG_C:(lvl + 1) * HG_C] = np.where(upper[:, None], rows_upper, rows_lower)
        same = (t[:, None] // (2 * half)) == (t[None, :] // (2 * half))
        mask[0, lvl] = same & upper[:, None] & (~upper)[None, :]
    ae[0, HG_LEVELS * HG_C:(HG_LEVELS + 1) * HG_C] = t[None, :] <= t[:, None]
    ae[0, (HG_LEVELS + 1) * HG_C:] = 1.0
    ae[1] = ae[0][:, ::-1]
    ae[1, :(HG_LEVELS + 1) * HG_C] = ae[1, :(HG_LEVELS + 1) * HG_C].reshape(HG_LEVELS + 1, HG_C, HG_C)[:, ::-1].reshape(-1, HG_C)
    mask[1] = mask[0][:, ::-1, ::-1]
    ae2 = np.concatenate([ae, ae], axis=-1)
    mask2 = np.concatenate([mask, mask], axis=-1)
    pair_diag = np.kron(np.eye(2, dtype=np.float32), np.ones((HG_DK, HG_DK), np.float32))
    return ae2, mask2, pair_diag


def _hg_tile(direction, i):
    return i if direction == 0 else HG_TILES - 1 - i


HG_PAIR = 2 * HG_DK


def _pair_diag(x):
    zero = jnp.zeros((x.shape[0], HG_DK), x.dtype)
    return jnp.concatenate([jnp.concatenate([x[:, :HG_DK], zero], axis=1),
                            jnp.concatenate([zero, x[:, HG_DK:]], axis=1)], axis=0)


def _hg_chunk(direction, rows, q_ref, f_ref, v_ref, lb, ae_ref, mask_ref, pd_ref, o_ref, st_sc):
    f = lb + (1.0 - lb) * jax.nn.sigmoid(f_ref[rows, :])
    logf = jnp.log2(f)
    kk = 1.0 - f
    hi = logf.astype(BF16)
    lo = (logf - hi.astype(F32)).astype(BF16)
    ex = _dot(ae_ref[direction], jnp.concatenate([hi, lo], axis=0))
    dec = jnp.exp2(ex[:(HG_LEVELS + 1) * HG_C])
    cum = ex[HG_LEVELS * HG_C:(HG_LEVELS + 1) * HG_C]
    tot = ex[(HG_LEVELS + 1) * HG_C:(HG_LEVELS + 1) * HG_C + 1]
    q = q_ref[rows, :].astype(F32)
    v_bf = v_ref[rows, :]
    v = v_bf.astype(F32)
    q_in = (q * dec[HG_LEVELS * HG_C:]).astype(BF16)
    k_tail = (kk * jnp.exp2(tot - cum)).astype(BF16)
    dec_tot = jnp.exp2(tot)
    qk = q * kk
    outs = []
    for p in range(HG_HEADS // 2):
        sl = slice(p * HG_PAIR, (p + 1) * HG_PAIR)
        scores = jnp.zeros((HG_C, HG_PAIR), F32)
        for lvl in range(HG_LEVELS):
            g = dec[lvl * HG_C:(lvl + 1) * HG_C, sl]
            scores = scores + mask_ref[direction, lvl] * _dot_nt(
                (q[:, sl] * g).astype(BF16), _pair_diag((kk[:, sl] * g).astype(BF16)))
        diag = jnp.concatenate(
            [jnp.broadcast_to(jnp.sum(qk[:, (2 * p + j) * HG_DK:(2 * p + j + 1) * HG_DK], axis=-1, keepdims=True),
                              (HG_C, HG_DK)) for j in range(2)], axis=-1)
        st = st_sc[direction, p]
        outs.append(_dot(scores.astype(BF16), _pair_diag(v_bf[:, sl])) + diag * v[:, sl]
                    + _dot_nt(q_in[:, sl], st.astype(BF16)))
        st_new = st * dec_tot[:, sl] + pd_ref[...] * _dot_tn(v_bf[:, sl], k_tail[:, sl])
        st_sc[direction, p] = st_new
    o_ref[rows, :] = jnp.concatenate(outs, axis=-1)


def _hg_kernel(qf_ref, ff_ref, vf_ref, qb_ref, fb_ref, vb_ref, lb_ref, ae_ref, mask_ref, pd_ref, h0f_ref, h0b_ref,
               of_ref, ob_ref, finf_ref, finb_ref, st_sc, *, layer):
    i = pl.program_id(0)
    x = lb_ref[:, :, 0, :]
    e = jnp.exp(x - jnp.max(x, axis=0, keepdims=True))
    sm = e / jnp.sum(e, axis=0, keepdims=True)
    lb = jnp.sum(sm[1:layer + 1], axis=0) if layer > 0 else jnp.zeros((2, D_BRANCH), F32)
    for direction, h0_ref in ((0, h0f_ref), (1, h0b_ref)):
        tile = _hg_tile(direction, i)
        first_ctx = 0 if direction == 0 else HG_TILES_PER_CTX - 1
        first_lat = 0 if direction == 0 else HG_TILES_PER_LAT - 1
        is_start = jnp.where(tile < HG_CTX_TILES,
                             tile % HG_TILES_PER_CTX == first_ctx,
                             (tile - HG_CTX_TILES) % HG_TILES_PER_LAT == first_lat)

        @pl.when(is_start)
        def _(direction=direction, h0_ref=h0_ref):
            for p in range(HG_HEADS // 2):
                st_sc[direction, p] = _pair_diag(
                    jnp.concatenate([h0_ref[0, 0, 2 * p], h0_ref[0, 0, 2 * p + 1]], axis=1))

    for c in range(HG_TM // HG_C):
        fwd_rows = slice(c * HG_C, (c + 1) * HG_C)
        bwd_rows = slice(HG_TM - (c + 1) * HG_C, HG_TM - c * HG_C)
        _hg_chunk(0, fwd_rows, qf_ref, ff_ref, vf_ref, lb[0:1], ae_ref, mask_ref, pd_ref, of_ref, st_sc)
        _hg_chunk(1, bwd_rows, qb_ref, fb_ref, vb_ref, lb[1:2], ae_ref, mask_ref, pd_ref, ob_ref, st_sc)

    for direction, fin_ref in ((0, finf_ref), (1, finb_ref)):
        @pl.when(_hg_tile(direction, i) < HG_CTX_TILES)
        def _(direction=direction, fin_ref=fin_ref):
            for h in range(HG_HEADS):
                p, j = divmod(h, 2)
                fin_ref[0, h] = st_sc[direction, p, j * HG_DK:(j + 1) * HG_DK, j * HG_DK:(j + 1) * HG_DK]


def _hgrn(proj, fz, hg_lb, ae, mask, pair_diag, h0t, layer):
    def tok_spec(direction, col):
        return pl.BlockSpec((HG_TM, D_BRANCH), lambda i: (_hg_tile(direction, i), col))

    def fin_spec(direction):
        return pl.BlockSpec((1, HG_HEADS, HG_DK, HG_DK),
                            lambda i: (jnp.minimum(_hg_tile(direction, i) // HG_TILES_PER_CTX, N_CTX_SEQ - 1), 0, 0, 0))

    def h0_spec(direction):
        return pl.BlockSpec((1, 1, HG_HEADS, HG_DK, HG_DK),
                            lambda i: (_cond_row(_hg_tile(direction, i), HG_TM), direction, 0, 0, 0))

    fin_shape = jax.ShapeDtypeStruct((N_CTX_SEQ, HG_HEADS, HG_DK, HG_DK), F32)
    return pl.pallas_call(
        functools.partial(_hg_kernel, layer=layer),
        out_shape=(jax.ShapeDtypeStruct((N_TOK, D_BRANCH), F32), jax.ShapeDtypeStruct((N_TOK, D_BRANCH), F32),
                   fin_shape, fin_shape),
        grid=(HG_TILES,),
        in_specs=[
            tok_spec(0, HG_Q_COL), tok_spec(0, 0), tok_spec(0, HG_V_COL),
            tok_spec(1, HG_Q_COL), tok_spec(1, 1), tok_spec(1, HG_V_COL),
            pl.BlockSpec((DEPTH, 2, 1, D_BRANCH), lambda i: (0, 0, 0, 0)),
            pl.BlockSpec((2, HG_E_ROWS, 2 * HG_C), lambda i: (0, 0, 0)),
            pl.BlockSpec((2, HG_LEVELS, HG_C, HG_PAIR), lambda i: (0, 0, 0, 0)),
            pl.BlockSpec((HG_PAIR, HG_PAIR), lambda i: (0, 0)),
            h0_spec(0), h0_spec(1),
        ],
        out_specs=(
            pl.BlockSpec((HG_TM, D_BRANCH), lambda i: (_hg_tile(0, i), 0)),
            pl.BlockSpec((HG_TM, D_BRANCH), lambda i: (_hg_tile(1, i), 0)),
            fin_spec(0), fin_spec(1),
        ),
        scratch_shapes=[pltpu.VMEM((2, HG_HEADS // 2, HG_PAIR, HG_PAIR), F32)],
        compiler_params=_cparams(("arbitrary",)),
        name="hgrn",
    )(proj, fz, proj, proj, fz, proj, hg_lb.reshape(DEPTH, 2, 1, D_BRANCH), ae, mask, pair_diag, h0t, h0t)


MRG_TM = 256


def _gelu_tanh(x):
    return 0.5 * x * (1.0 + jnp.tanh(0.7978845608028654 * (x + 0.044715 * (x * x * x))))


def _merge_kernel(scb_ref, scc_ref, sch_ref, hgg_ref, pool_ref, g0_ref, g1_ref, g2_ref, g3_ref,
                  yrow_ref, of_ref, ob_ref, x_ref, mod_ref, conv_ref, gluw_ref, glub_ref, ng_ref,
                  poolw_ref, pscale_ref, wbr_ref, wout_ref, lng_ref, lnb_ref, o_ref,
                  ys5_sc, gluw_sc, poolw_sc, wbr_sc, wout_sc):
    tile = pl.program_id(0)

    @pl.when(tile == 0)
    def _():
        gluw_sc[...] = gluw_ref[0].astype(BF16)
        poolw_sc[...] = poolw_ref[0].astype(BF16)
        wbr_sc[...] = wbr_ref[0].astype(BF16)
        wout_sc[...] = wout_ref[0].astype(BF16)

    line = jnp.where(tile * MRG_TM < N_CTX_TOK, CTX_LEN, GRID_W)
    pos = lax.broadcasted_iota(jnp.int32, (MRG_TM, 1), 0) & (line - 1)

    def shifted(val, k):
        rolled = pltpu.roll(val, k % MRG_TM, axis=0)
        ok = (pos >= k) if k > 0 else (pos < line + k)
        return jnp.where(ok, rolled, 0.0)

    m = scc_ref[...].astype(F32) * sch_ref[...].astype(F32)
    conv = conv_ref[0, 0:1, :] * shifted(m, 1) + conv_ref[0, 1:2, :] * m + conv_ref[0, 2:3, :] * shifted(m, -1)
    ya = scb_ref[...].astype(F32) * conv

    for b in range(S5_BLK):
        for t in range(S5_T):
            ys5_sc[b, pl.ds(t, MRG_TM // S5_T, stride=S5_T), :] = yrow_ref[b, :, t * LANES:(t + 1) * LANES]
    z = _gelu_tanh(jnp.concatenate([ys5_sc[b] for b in range(S5_BLK)], axis=-1))
    yb = z * _sigmoid_tanh(_dot(z.astype(BF16), gluw_sc[...]) + glub_ref[0])

    o = of_ref[...] + ob_ref[...]
    normed = []
    for h in range(HG_HEADS):
        oh = o[:, h * HG_DK:(h + 1) * HG_DK]
        ms = jnp.mean(oh * oh, axis=-1, keepdims=True)
        normed.append(oh * lax.rsqrt(ms + LN_EPS) * ng_ref[0])
    gate_c = hgg_ref[...].astype(F32)
    yc = jnp.concatenate(normed, axis=-1) * (gate_c * _sigmoid_tanh(gate_c))

    pu = pool_ref[...].astype(F32)
    posf = pos.astype(F32)
    linef = line.astype(F32)
    pooled = []
    for gi, w in enumerate(POOL_WINDOWS):
        vg = pu[:, gi * POOL_GROUP:(gi + 1) * POOL_GROUP]
        back, fwd, span = vg, vg, 1
        while 2 * span <= w // 2:
            back = back + shifted(back, span)
            fwd = fwd + shifted(fwd, -span)
            span *= 2
        s = shifted(back, 1) + fwd
        cnt = jnp.minimum(posf + w // 2, linef) - jnp.maximum(posf - w // 2, 0.0)
        pg = s / cnt - vg
        pooled.append(_dot(pg.astype(BF16), poolw_sc[gi]))
    yd = jnp.concatenate(pooled, axis=-1) * pscale_ref[0]

    merged = jnp.zeros((MRG_TM, D_MODEL), F32)
    for k, (br, gate_ref) in enumerate(((ya, g0_ref), (yb, g1_ref), (yc, g2_ref), (yd, g3_ref))):
        merged = merged + gate_ref[...].astype(F32) * _dot(br.astype(BF16), wbr_sc[k])
    mix = _dot(merged.astype(BF16), wout_sc[...])
    y = ALPHA * x_ref[...] + mod_ref[0, 0, 5:6, :] * mix
    o_ref[...] = _layer_norm(y, lng_ref[0, 0], lnb_ref[0, 0])


def _merge(proj, y_rows, o_f, o_b, x, mod, conv_w, glu_w, glu_b, norm_g, pool_w, pool_scale, w_branch, w_out,
           ln_g, ln_b, layer):
    def part(col):
        return pl.BlockSpec((MRG_TM, D_BRANCH), lambda i: (i, col))

    def gate(k):
        return pl.BlockSpec((MRG_TM, D_MODEL), lambda i: (i, N_IN_PARTS * D_BRANCH // D_MODEL + k))

    def per_layer(*shape, single=False):
        mode = dict(pipeline_mode=pl.Buffered(1)) if single else {}
        return pl.BlockSpec((1,) + shape, lambda i: (layer,) + (0,) * len(shape), **mode)

    return pl.pallas_call(
        _merge_kernel,
        out_shape=jax.ShapeDtypeStruct((N_TOK, D_MODEL), F32),
        grid=(N_TOK // MRG_TM,),
        in_specs=[
            part(0), part(1), part(2), part(8), part(9), gate(0), gate(1), gate(2), gate(3),
            pl.BlockSpec((S5_BLK, MRG_TM // S5_T, S5_CW), lambda i: (0, i, 0)),
            pl.BlockSpec((MRG_TM, D_BRANCH), lambda i: (i, 0)),
            pl.BlockSpec((MRG_TM, D_BRANCH), lambda i: (i, 0)),
            pl.BlockSpec((MRG_TM, D_MODEL), lambda i: (i, 0)),
            _mod_spec(layer, MRG_TM),
            per_layer(3, D_BRANCH), per_layer(D_BRANCH, D_BRANCH, single=True), per_layer(1, D_BRANCH),
            per_layer(1, HG_DK), per_layer(len(POOL_WINDOWS), POOL_GROUP, POOL_GROUP, single=True),
            per_layer(1, D_BRANCH), per_layer(N_BRANCH, D_BRANCH, D_MODEL, single=True),
            per_layer(D_MODEL, D_MODEL, single=True), _ln_spec(layer, 1), _ln_spec(layer, 1),
        ],
        out_specs=pl.BlockSpec((MRG_TM, D_MODEL), lambda i: (i, 0)),
        scratch_shapes=[
            pltpu.VMEM((S5_BLK, MRG_TM, LANES), F32),
            pltpu.VMEM((D_BRANCH, D_BRANCH), BF16),
            pltpu.VMEM((len(POOL_WINDOWS), POOL_GROUP, POOL_GROUP), BF16),
            pltpu.VMEM((N_BRANCH, D_BRANCH, D_MODEL), BF16),
            pltpu.VMEM((D_MODEL, D_MODEL), BF16),
        ],
        compiler_params=_cparams(("arbitrary",)),
        name="merge",
    )(proj, proj, proj, proj, proj, proj, proj, proj, proj, y_rows, o_f, o_b, x, mod,
      conv_w, glu_w, glu_b.reshape(DEPTH, 1, D_BRANCH), norm_g.reshape(DEPTH, 1, HG_DK), pool_w,
      pool_scale.reshape(DEPTH, 1, D_BRANCH), w_branch, w_out, ln_g, ln_b)


def _grid_pos_embedding():
    rows = LAT_LEN // GRID_W
    quarter = D_MODEL // 4
    omega = POS_BASE ** (-jnp.arange(quarter, dtype=F32) / quarter)
    ar = jnp.arange(rows, dtype=F32)[:, None] * omega
    ac = jnp.arange(GRID_W, dtype=F32)[:, None] * omega
    row_tab = jnp.concatenate([jnp.sin(ar), jnp.cos(ar)], -1)[:, None, :]
    col_tab = jnp.concatenate([jnp.sin(ac), jnp.cos(ac)], -1)[None, :, :]
    shape = (rows, GRID_W, 2 * quarter)
    pos = jnp.concatenate([jnp.broadcast_to(row_tab, shape), jnp.broadcast_to(col_tab, shape)], -1)
    return pos.reshape(LAT_LEN, D_MODEL)


def kernel(x_prompt, x_sample, state_s5_re, state_s5_im, state_hgrn, c, c_ctx, w_ada, b_ada, ln_g, ln_b,
           ffn_w1, ffn_w3, ffn_w2, w_in, sc_conv, s5_lam_re, s5_lam_im, s5_log_dt, s5_b_re, s5_b_im,
           s5_c_re, s5_c_im, s5_d, s5_glu_w, s5_glu_b, hg_lb, hg_norm_g, pool_w, pool_scale, w_branch, w_out):
    x = (x_prompt.reshape(N_CTX_TOK, D_MODEL), x_sample.reshape(N_LAT_TOK, D_MODEL), _grid_pos_embedding())

    cond = jnp.zeros((COND_PAD, D_MODEL), F32).at[0].set(c_ctx).at[1:N_COND].set(c)
    mod = _ada(cond, w_ada, b_ada).reshape(DEPTH, COND_PAD, N_SUB * 3, D_MODEL)
    ln_g4 = ln_g.reshape(DEPTH, N_SUB, 1, D_MODEL)
    ln_b4 = ln_b.reshape(DEPTH, N_SUB, 1, D_MODEL)

    s5_m, s5_win, s5_wout, s5_pow = _s5_tables(*_s5_param_layout(
        s5_lam_re, s5_lam_im, s5_log_dt, s5_b_re, s5_b_im, s5_c_re, s5_c_im, s5_d))

    ffn_w1, ffn_w3 = _cast_bf16(ffn_w1, ffn_w3)
    (ffn_w2,) = _cast_bf16(ffn_w2)
    w_in_bf = _cast_w_in(w_in)

    ae_np, mask_np, pair_np = _hg_constants()
    ae = jnp.asarray(ae_np, BF16)
    mask = jnp.asarray(mask_np, F32)
    pair_diag = jnp.asarray(pair_np, F32)

    fin_s5, fin_hg = [], []
    for l in range(DEPTH):
        x, h = _ffn(x, mod, ffn_w1, ffn_w3, ffn_w2, ln_g4, ln_b4, l, 0)
        proj, fz, u_rows = _inproj(h, w_in_bf, l)

        h0 = jnp.concatenate([
            state_s5_re[:, l].reshape(N_LAT_SEQ, 2, S5_BLK, S5_SW),
            state_s5_im[:, l].reshape(N_LAT_SEQ, 2, S5_BLK, S5_SW)], axis=-1).transpose(1, 2, 0, 3)
        y_rows, fin = _s5_main(u_rows, s5_m, s5_win, s5_wout, s5_pow, h0, l)
        fin_s5.append(fin)

        h0t = jnp.concatenate([jnp.zeros((1, 2, HG_HEADS, HG_DK, HG_DK), F32),
                               jnp.swapaxes(state_hgrn[:, l], -1, -2)], axis=0)
        o_f, o_b, fin_f, fin_b = _hgrn(proj, fz, hg_lb, ae, mask, pair_diag, h0t, l)
        fin_hg.append(jnp.stack([fin_f, fin_b], axis=1))

        x = _merge(proj, y_rows, o_f, o_b, x, mod, sc_conv, s5_glu_w, s5_glu_b, hg_norm_g,
                   pool_w, pool_scale, w_branch, w_out, ln_g4, ln_b4, l)
        x = _ffn(x, mod, ffn_w1, ffn_w3, ffn_w2, ln_g4, ln_b4, l, 2, split_output=(l == DEPTH - 1))

    y_prompt = x[0].reshape(N_CTX_SEQ, CTX_LEN, D_MODEL)
    y_sample = x[1].reshape(N_LAT_SEQ, LAT_LEN, D_MODEL)
    fin = jnp.stack(fin_s5)
    fin = fin.reshape(DEPTH, 2, S5_BLK, N_CTX_SEQ, 2, S5_GPB, S5_STATE)
    fin = fin.transpose(4, 3, 0, 1, 2, 5, 6).reshape(2, N_CTX_SEQ, DEPTH, 2, S5_GROUPS, S5_STATE)
    new_hgrn = jnp.swapaxes(jnp.stack(fin_hg, axis=1), -1, -2)
    return y_prompt, y_sample, fin[0], fin[1], new_hgrn
```

```python
import functools

import numpy as np
import jax
import jax.numpy as jnp
from jax import lax
from jax.experimental import pallas as pl
from jax.experimental.pallas import tpu as pltpu

F32 = jnp.float32
BF16 = jnp.bfloat16

D_MODEL = 1024
N_CTX_SEQ = 16
CTX_LEN = 256
DEPTH = 2
N_LAT_SEQ = 2
LAT_LEN = 4096
GRID_W = 64
D_BRANCH = 512
N_BRANCH = 4
S5_GROUPS = 32
S5_GROUP = 16
S5_STATE = 64
HG_HEADS = 4
HG_DK = 128
POOL_WINDOWS = (2, 4, 8, 16)
POOL_GROUP = 128
D_FF = 2816
N_SUB = 3
N_IN_PARTS = 10
IN_WIDTH = N_IN_PARTS * D_BRANCH + N_BRANCH * D_MODEL
ADA_WIDTH = N_SUB * 3 * D_MODEL
ALPHA = (2 * DEPTH) ** 0.25
LN_EPS = 1e-5
POS_BASE = 10000.0

N_CTX_TOK = N_CTX_SEQ * CTX_LEN
N_LAT_TOK = N_LAT_SEQ * LAT_LEN
N_TOK = N_CTX_TOK + N_LAT_TOK
N_COND = 1 + N_LAT_SEQ
COND_PAD = 8

LANES = 128
SUBLANES = 8
VMEM_LIMIT = 56 * 1024 * 1024

S5_T = 8
S5_BLK = 4
S5_GPB = S5_GROUPS // S5_BLK
S5_SW = S5_GPB * S5_STATE
S5_CW = S5_T * LANES
CTX_CHUNKS = CTX_LEN // S5_T
LAT_CHUNKS = LAT_LEN // S5_T
S5_ROWS_CTX = N_CTX_TOK // S5_T
S5_ROWS_LAT = N_LAT_TOK // S5_T
S5_ROWS = S5_ROWS_CTX + S5_ROWS_LAT
S5_LAT_GROUPS = LAT_CHUNKS // SUBLANES
S5_POW_ROWS = 24
S5_U_COL = 3

HG_C = 128
HG_LEVELS = 7
HG_TM = 256
HG_TILES = N_TOK // HG_TM
HG_CTX_TILES = N_CTX_TOK // HG_TM
HG_TILES_PER_CTX = CTX_LEN // HG_TM
HG_TILES_PER_LAT = LAT_LEN // HG_TM
HG_E_ROWS = (HG_LEVELS + 1) * HG_C + 8


def _cparams(sem):
    return pltpu.CompilerParams(dimension_semantics=sem, vmem_limit_bytes=VMEM_LIMIT)


def _cond_row(tile, tile_tokens):
    tok = tile * tile_tokens
    return jnp.where(tok < N_CTX_TOK, 0, 1 + (tok - N_CTX_TOK) // LAT_LEN)


def _dot(a, b):
    return jnp.dot(a, b, preferred_element_type=F32)


def _dot_nt(a, b):
    return lax.dot_general(a, b, (((1,), (1,)), ((), ())), preferred_element_type=F32)


def _dot_tn(a, b):
    return lax.dot_general(a, b, (((0,), (0,)), ((), ())), preferred_element_type=F32)


def _split3(x):
    h1 = x.astype(BF16)
    r1 = x - h1.astype(F32)
    h2 = r1.astype(BF16)
    h3 = (r1 - h2.astype(F32)).astype(BF16)
    return h1, h2, h3


def _dot_nt_hi(a3, b3):
    acc = None
    for x in range(2):
        for y in range(2 - x):
            term = _dot_nt(a3[x], b3[y])
            acc = term if acc is None else acc + term
    return acc


def _silu(x):
    return x * jax.nn.sigmoid(x)


def _sigmoid_tanh(x):
    return 0.5 * jnp.tanh(0.5 * x) + 0.5


def _layer_norm(y, g, b):
    mu = jnp.mean(y, axis=-1, keepdims=True)
    yc = y - mu
    var = jnp.mean(yc * yc, axis=-1, keepdims=True)
    return yc * lax.rsqrt(var + LN_EPS) * g + b


def _mod_spec(layer, tile_tokens):
    return pl.BlockSpec((1, 1, N_SUB * 3, D_MODEL),
                        lambda i, *_: (layer, _cond_row(i, tile_tokens), 0, 0))


def _ln_spec(layer, sub):
    return pl.BlockSpec((1, 1, 1, D_MODEL), lambda *_: (layer, sub, 0, 0))


ADA_TN = 1152


def _ada_kernel(c_ref, w_ref, b_ref, o_ref):
    s = _silu(c_ref[...]).astype(BF16)
    o_ref[0] = _dot(s, w_ref[0].astype(BF16)) + b_ref[0]


def _ada(cond, w_ada, b_ada):
    return pl.pallas_call(
        _ada_kernel,
        out_shape=jax.ShapeDtypeStruct((DEPTH, COND_PAD, ADA_WIDTH), F32),
        grid=(DEPTH, ADA_WIDTH // ADA_TN),
        in_specs=[
            pl.BlockSpec((COND_PAD, D_MODEL), lambda l, j: (0, 0)),
            pl.BlockSpec((1, D_MODEL, ADA_TN), lambda l, j: (l, 0, j)),
            pl.BlockSpec((1, 1, ADA_TN), lambda l, j: (l, 0, j)),
        ],
        out_specs=pl.BlockSpec((1, COND_PAD, ADA_TN), lambda l, j: (l, 0, j)),
        compiler_params=_cparams(("parallel", "parallel")),
        name="ada",
    )(cond, w_ada, b_ada.reshape(DEPTH, 1, ADA_WIDTH))


FFN_TM = 1024
FFN_TM_SPLIT = 512
FFN_TC = 256
FFN_CHUNKS = D_FF // FFN_TC


def _ffn_kernel(*refs, sub, tm, split_input, split_output, emit_next):
    step = pl.program_id(0)
    tile = step - FFN_CHUNKS
    is_ctx = tile < N_CTX_TOK // tm
    n_x = 3 if split_input else 1
    x_refs, refs = refs[:n_x], refs[n_x:]
    mod_ref, w1_ref, w3_ref, w2_ref, g_ref, b_ref = refs[:6]
    out_refs = refs[6:-4]
    w1_sc, w3_sc, w2_sc, act_sc = refs[-4:]

    @pl.when(step < FFN_CHUNKS)
    def _():
        w1_sc[step] = w1_ref[0, 0].astype(BF16)
        w3_sc[step] = w3_ref[0, 0].astype(BF16)
        w2_sc[pl.ds(pl.multiple_of(step * FFN_TC, FFN_TC), FFN_TC), :] = w2_ref[0, 0].astype(BF16)

    @pl.when(step >= FFN_CHUNKS)
    def _():
        if split_input:
            ctx_ref, lat_ref, pos_ref = x_refs
            x = jnp.where(is_ctx, ctx_ref[...], lat_ref[...] + pos_ref[...])
        else:
            x = x_refs[0][...]
        shift = mod_ref[0, 0, 3 * sub:3 * sub + 1, :]
        scale = mod_ref[0, 0, 3 * sub + 1:3 * sub + 2, :]
        gate = mod_ref[0, 0, 3 * sub + 2:3 * sub + 3, :]
        h = (x * (1.0 + scale) + shift).astype(BF16)
        for c in range(FFN_CHUNKS):
            a = _dot(h, w1_sc[c])
            b = _dot(h, w3_sc[c])
            act_sc[:, c * FFN_TC:(c + 1) * FFN_TC] = (_silu(a) * b).astype(BF16)
        f = _dot(act_sc[...], w2_sc[...])
        y = _layer_norm(ALPHA * x + gate * (0.5 * f), g_ref[0, 0], b_ref[0, 0])
        if emit_next:
            nxt = 3 * (sub + 1)
            out_refs[1][...] = (y * (1.0 + mod_ref[0, 0, nxt + 1:nxt + 2, :])
                                + mod_ref[0, 0, nxt:nxt + 1, :]).astype(BF16)
        if split_output:
            @pl.when(is_ctx)
            def _():
                out_refs[0][...] = y

            @pl.when(jnp.logical_not(is_ctx))
            def _():
                out_refs[1][...] = y
        else:
            out_refs[0][...] = y


def _ffn(x, mod, w1, w3, w2, ln_g, ln_b, layer, sub, split_output=False):
    which = sub // 2
    split_input = isinstance(x, tuple)
    tm = FFN_TM_SPLIT if split_input else FFN_TM
    ctx_tiles = N_CTX_TOK // tm

    def chunk(s):
        return jnp.minimum(s, FFN_CHUNKS - 1)

    def tile(s):
        return jnp.maximum(s - FFN_CHUNKS, 0)

    def tok_spec(index):
        return pl.BlockSpec((tm, D_MODEL), lambda s: (index(tile(s)), 0))

    def ctx_tile(t):
        return jnp.minimum(t, ctx_tiles - 1)

    def lat_tile(t):
        return jnp.maximum(t - ctx_tiles, 0)

    if split_input:
        pos_tiles = LAT_LEN // tm
        x_specs = [tok_spec(ctx_tile), tok_spec(lat_tile), tok_spec(lambda t: lat_tile(t) % pos_tiles)]
    else:
        x_specs = [tok_spec(lambda t: t)]
        x = (x,)
    emit_next = sub == 0
    if split_output:
        out_shape = (jax.ShapeDtypeStruct((N_CTX_TOK, D_MODEL), F32), jax.ShapeDtypeStruct((N_LAT_TOK, D_MODEL), F32))
        out_specs = (tok_spec(ctx_tile), tok_spec(lat_tile))
    elif emit_next:
        out_shape = (jax.ShapeDtypeStruct((N_TOK, D_MODEL), F32), jax.ShapeDtypeStruct((N_TOK, D_MODEL), BF16))
        out_specs = (tok_spec(lambda t: t), tok_spec(lambda t: t))
    else:
        out_shape = jax.ShapeDtypeStruct((N_TOK, D_MODEL), F32)
        out_specs = tok_spec(lambda t: t)

    return pl.pallas_call(
        functools.partial(_ffn_kernel, sub=sub, tm=tm, split_input=split_input, split_output=split_output,
                          emit_next=emit_next),
        out_shape=out_shape,
        grid=(FFN_CHUNKS + N_TOK // tm,),
        in_specs=x_specs + [
            pl.BlockSpec((1, 1, N_SUB * 3, D_MODEL), lambda s: (layer, _cond_row(tile(s), tm), 0, 0)),
            pl.BlockSpec((1, 1, D_MODEL, FFN_TC), lambda s: (layer, which, 0, chunk(s))),
            pl.BlockSpec((1, 1, D_MODEL, FFN_TC), lambda s: (layer, which, 0, chunk(s))),
            pl.BlockSpec((1, 1, FFN_TC, D_MODEL), lambda s: (layer, which, chunk(s), 0)),
            _ln_spec(layer, sub), _ln_spec(layer, sub),
        ],
        out_specs=out_specs,
        scratch_shapes=[pltpu.VMEM((FFN_CHUNKS, D_MODEL, FFN_TC), BF16), pltpu.VMEM((FFN_CHUNKS, D_MODEL, FFN_TC), BF16),
                        pltpu.VMEM((D_FF, D_MODEL), BF16), pltpu.VMEM((tm, D_FF), BF16)],
        compiler_params=_cparams(("arbitrary",)),
        name=f"ffn{sub}",
    )(*x, mod, w1, w3, w2, ln_g, ln_b)


INP_TM = 2048
INP_PARTS = 2
INP_TN = INP_PARTS * D_BRANCH
INP_STEPS = IN_WIDTH // INP_TN
HG_Q_COL, HG_F_COL, HG_V_COL = 4, 5, 7


def _cast_w_in_kernel(w_ref, o_ref):
    o_ref[0, 0] = w_ref[0].astype(BF16)


def _cast_w_in(w_in):
    return pl.pallas_call(
        _cast_w_in_kernel,
        out_shape=jax.ShapeDtypeStruct((DEPTH, INP_STEPS, D_MODEL, INP_TN), BF16),
        grid=(DEPTH, INP_STEPS),
        in_specs=[pl.BlockSpec((1, D_MODEL, INP_TN), lambda l, j: (l, 0, j))],
        out_specs=pl.BlockSpec((1, 1, D_MODEL, INP_TN), lambda l, j: (l, j, 0, 0)),
        compiler_params=_cparams(("parallel", "parallel")),
        name="cast_w_in",
    )(w_in)


def _inproj_kernel(h_ref, w_ref, o_ref, fz_ref, u_ref, blk_sc):
    j = pl.program_id(1)

    def project(gate=False):
        res = _dot(h_ref[...], w_ref[0, 0])
        o_ref[...] = (_sigmoid_tanh(res) if gate else res).astype(BF16)
        return res

    def part(res, p):
        k = p % INP_PARTS
        return res[:, k * D_BRANCH:(k + 1) * D_BRANCH]

    special = (S5_U_COL, HG_F_COL, HG_F_COL + 1)
    steps = [p // INP_PARTS for p in special]
    assert len(set(steps)) == len(steps)

    first_gate = N_IN_PARTS // INP_PARTS
    assert N_IN_PARTS % INP_PARTS == 0 and max(steps) < first_gate

    @pl.when(functools.reduce(jnp.logical_and, [j != s for s in steps] + [j < first_gate]))
    def _():
        project()

    @pl.when(j >= first_gate)
    def _():
        project(gate=True)

    for p in (HG_F_COL, HG_F_COL + 1):
        @pl.when(j == p // INP_PARTS)
        def _(p=p):
            fz_ref[...] = part(project(), p)

    @pl.when(j == S5_U_COL // INP_PARTS)
    def _():
        s5_u = part(project(), S5_U_COL)
        for b in range(S5_BLK):
            blk_sc[b] = s5_u[:, b * LANES:(b + 1) * LANES]
            for t in range(S5_T):
                rows = blk_sc[b, pl.ds(t, INP_TM // S5_T, stride=S5_T), :]
                u_ref[b, :, t * LANES:(t + 1) * LANES] = rows.astype(BF16)


def _inproj(h, w_in_bf, layer):
    return pl.pallas_call(
        _inproj_kernel,
        out_shape=(jax.ShapeDtypeStruct((N_TOK, IN_WIDTH), BF16),
                   jax.ShapeDtypeStruct((N_TOK, 2 * D_BRANCH), F32),
                   jax.ShapeDtypeStruct((S5_BLK, S5_ROWS, S5_CW), BF16)),
        grid=(N_TOK // INP_TM, INP_STEPS),
        in_specs=[
            pl.BlockSpec((INP_TM, D_MODEL), lambda i, j: (i, 0)),
            pl.BlockSpec((1, 1, D_MODEL, INP_TN), lambda i, j: (layer, j, 0, 0)),
        ],
        out_specs=(pl.BlockSpec((INP_TM, INP_TN), lambda i, j: (i, j)),
                   pl.BlockSpec((INP_TM, D_BRANCH),
                                lambda i, j: (i, jnp.where(j > HG_F_COL // INP_PARTS, 1, 0))),
                   pl.BlockSpec((S5_BLK, INP_TM // S5_T, S5_CW), lambda i, j: (0, i, 0))),
        scratch_shapes=[pltpu.VMEM((S5_BLK, INP_TM, LANES), F32)],
        compiler_params=_cparams(("parallel", "arbitrary")),
        name="inproj",
    )(h, w_in_bf)


def _s5_tab_kernel(lam_ref, bre_ref, bim_ref, cre_ref, cim_ref, dsk_ref, m_ref, win_ref, wout_ref, pow_ref):
    d = pl.program_id(1)
    fwd = d == 0
    lam_re = lam_ref[0, 0, 0, 0:1, :]
    lam_im = lam_ref[0, 0, 0, 1:2, :]
    dt = jnp.exp(lam_ref[0, 0, 0, 2:3, :])

    def apow(j):
        mag = jnp.exp(lam_re * dt * float(j))
        ang = lam_im * dt * float(j)
        return mag * jnp.cos(ang), mag * jnp.sin(ang)

    a_re, a_im = apow(1)
    den = lam_re * lam_re + lam_im * lam_im
    num_re = a_re - 1.0
    coef_re = (num_re * lam_re + a_im * lam_im) / den
    coef_im = (a_im * lam_re - num_re * lam_im) / den
    ri = lax.broadcasted_iota(jnp.int32, (S5_STATE, S5_SW), 0)
    ci = lax.broadcasted_iota(jnp.int32, (S5_STATE, S5_SW), 1)
    repeat = jnp.where(ri == (ci & (S5_STATE - 1)), 1.0, 0.0).astype(BF16)
    own = (jnp.right_shift(lax.broadcasted_iota(jnp.int32, (LANES, S5_SW), 0), S5_GROUP.bit_length() - 1)
           == jnp.right_shift(lax.broadcasted_iota(jnp.int32, (LANES, S5_SW), 1), S5_STATE.bit_length() - 1))

    def block_diag(ref):
        return jnp.where(own, sum(_dot(piece, repeat) for piece in _split3(ref[0, 0, 0])), 0.0)

    b_re = block_diag(bre_ref)
    b_im = block_diag(bim_ref)
    bb_re = coef_re * b_re - coef_im * b_im
    bb_im = coef_re * b_im + coef_im * b_re
    bbr = _split3(bb_re)
    bbi = _split3(bb_im)
    c_re = block_diag(cre_ref)
    c_im = block_diag(cim_ref)

    kmat = []
    for j in range(S5_T + 1):
        aj_re, aj_im = apow(j)
        pc_re = c_re * aj_re - c_im * aj_im
        pc_im = c_re * aj_im + c_im * aj_re
        if j >= 1:
            r = pl.multiple_of(jnp.where(fwd, j - 1, S5_T - j) * LANES, LANES)
            wout_ref[0, 0, 0, pl.ds(r, LANES), :] = jnp.concatenate([pc_re, -pc_im], axis=1).astype(BF16)
        if j < S5_T:
            pb_re = bb_re * aj_re - bb_im * aj_im
            pb_im = bb_re * aj_im + bb_im * aj_re
            r = pl.multiple_of(jnp.where(fwd, S5_T - 1 - j, j) * LANES, LANES)
            win_ref[0, 0, 0, pl.ds(r, LANES), :] = jnp.concatenate([pb_re, pb_im], axis=1).astype(BF16)
            kmat.append(_dot_nt_hi(bbr, _split3(pc_re)) - _dot_nt_hi(bbi, _split3(pc_im)))

    ri = lax.broadcasted_iota(jnp.int32, (LANES, LANES), 0)
    ci = lax.broadcasted_iota(jnp.int32, (LANES, LANES), 1)
    keep_upper = jnp.where(fwd, 1.0, 0.0)
    kmat[0] = kmat[0] + jnp.where(ri == ci, dsk_ref[0, 0], 0.0) * keep_upper
    for s in range(S5_T):
        for t in range(S5_T):
            tile = kmat[abs(t - s)]
            if t > s:
                tile = tile * keep_upper
            elif t < s:
                tile = tile * (1.0 - keep_upper)
            m_ref[0, 0, 0, s * LANES:(s + 1) * LANES, t * LANES:(t + 1) * LANES] = tile.astype(BF16)

    pow_ref[...] = jnp.zeros_like(pow_ref)
    for i in range(SUBLANES):
        for row, j in ((i, S5_T * (i + 1)), (SUBLANES + i, S5_T * (SUBLANES - i))):
            p_re, p_im = apow(j)
            pow_ref[0, 0, 0, 0, row:row + 1, :] = p_re
            pow_ref[0, 0, 0, 1, row:row + 1, :] = p_im
    p_re, p_im = apow(S5_T * 2 * SUBLANES)
    pow_ref[0, 0, 0, 0, 2 * SUBLANES:2 * SUBLANES + 1, :] = p_re
    pow_ref[0, 0, 0, 1, 2 * SUBLANES:2 * SUBLANES + 1, :] = p_im


def _s5_tables(lam, b_re, b_im, c_re, c_im, d_skip):
    lead = (DEPTH, 2, S5_BLK)

    def spec(*tail):
        return pl.BlockSpec((1, 1, 1) + tail, lambda l, d, b: (l, d, b) + (0,) * len(tail))

    return pl.pallas_call(
        _s5_tab_kernel,
        out_shape=(
            jax.ShapeDtypeStruct(lead + (S5_CW, S5_CW), BF16),
            jax.ShapeDtypeStruct(lead + (S5_CW, 2 * S5_SW), BF16),
            jax.ShapeDtypeStruct(lead + (S5_CW, 2 * S5_SW), BF16),
            jax.ShapeDtypeStruct(lead + (2, S5_POW_ROWS, S5_SW), F32),
        ),
        grid=lead,
        in_specs=[spec(3, S5_SW)] + [spec(LANES, S5_STATE)] * 4
        + [pl.BlockSpec((1, 1, 1, LANES), lambda l, d, b: (l, b, 0, 0))],
        out_specs=(spec(S5_CW, S5_CW), spec(S5_CW, 2 * S5_SW), spec(S5_CW, 2 * S5_SW),
                   spec(2, S5_POW_ROWS, S5_SW)),
        compiler_params=_cparams(("parallel", "parallel", "parallel")),
        name="s5_tables",
    )(lam, b_re, b_im, c_re, c_im, d_skip)


def _s5_param_layout(s5_lam_re, s5_lam_im, s5_log_dt, s5_b_re, s5_b_im, s5_c_re, s5_c_im, s5_d):
    lam = jnp.stack([s5_lam_re, s5_lam_im, jnp.broadcast_to(s5_log_dt[..., None], s5_lam_re.shape)], axis=2)
    lam = lam.reshape(DEPTH, 2, 3, S5_BLK, S5_SW).transpose(0, 1, 3, 2, 4)

    def rows_b(b):
        return jnp.swapaxes(b, -1, -2).reshape(DEPTH, 2, S5_BLK, LANES, S5_STATE)

    def rows_c(c):
        return c.reshape(DEPTH, 2, S5_BLK, LANES, S5_STATE)

    return (lam, rows_b(s5_b_re), rows_b(s5_b_im), rows_c(s5_c_re), rows_c(s5_c_im),
            s5_d.reshape(DEPTH, S5_BLK, 1, LANES))


def _s5_kernel(u_ref, m_ref, win_ref, wout_ref, pow_ref, h0_ref, y_ref, fin_ref, x_sc):
    d = pl.program_id(1)
    u = u_ref[0]
    x_sc[...] = _dot(u, win_ref[0, 0, 0])
    p_re = pow_ref[0, 0, 0, 0]
    p_im = pow_ref[0, 0, 0, 1]
    re = slice(0, S5_SW)
    im = slice(S5_SW, 2 * S5_SW)

    def madd(t_re, t_im, w_re, w_im, s_re, s_im):
        return t_re + w_re * s_re - w_im * s_im, t_im + w_re * s_im + w_im * s_re

    def shift_rows(val, k, n, idx, reverse):
        if reverse:
            return jnp.where(idx < n - k, pltpu.roll(val, val.shape[0] - k, axis=0), 0.0)
        return jnp.where(idx >= k, pltpu.roll(val, k, axis=0), 0.0)

    def pow_row(k):
        row = k - 1 if k <= SUBLANES else 2 * SUBLANES
        return p_re[row:row + 1], p_im[row:row + 1]

    def ctx_scan(reverse):
        s_re = x_sc[0:S5_ROWS_CTX, re]
        s_im = x_sc[0:S5_ROWS_CTX, im]
        c = lax.broadcasted_iota(jnp.int32, (S5_ROWS_CTX, 1), 0) & (CTX_CHUNKS - 1)
        def shift_ctx(val, k):
            if k % SUBLANES:
                return shift_rows(val, k, CTX_CHUNKS, c, reverse)
            v3 = val.reshape(N_CTX_SEQ, CTX_CHUNKS, S5_SW)
            pad = jnp.zeros((N_CTX_SEQ, k, S5_SW), F32)
            moved = (jnp.concatenate([v3[:, k:], pad], axis=1) if reverse
                     else jnp.concatenate([pad, v3[:, :CTX_CHUNKS - k]], axis=1))
            return moved.reshape(S5_ROWS_CTX, S5_SW)

        k = 1
        while k < CTX_CHUNKS:
            w_re, w_im = pow_row(k)
            s_re, s_im = madd(s_re, s_im, w_re, w_im, shift_ctx(s_re, k), shift_ctx(s_im, k))
            k *= 2
        last = 0 if reverse else CTX_CHUNKS - 1
        pick = (lax.broadcasted_iota(jnp.int32, (N_CTX_SEQ, S5_ROWS_CTX), 1)
                == lax.broadcasted_iota(jnp.int32, (N_CTX_SEQ, S5_ROWS_CTX), 0) * CTX_CHUNKS + last)
        pick = jnp.where(pick, 1.0, 0.0).astype(BF16)
        fin_ref[0, 0, :, re] = sum(_dot(pick, piece) for piece in _split3(s_re))
        fin_ref[0, 0, :, im] = sum(_dot(pick, piece) for piece in _split3(s_im))
        x_sc[0:S5_ROWS_CTX, re] = shift_rows(s_re, 1, CTX_CHUNKS, c, reverse)
        x_sc[0:S5_ROWS_CTX, im] = shift_rows(s_im, 1, CTX_CHUNKS, c, reverse)

    def lat_group(k, carry, reverse):
        g = (S5_LAT_GROUPS - 1 - k) if reverse else k
        r8 = lax.broadcasted_iota(jnp.int32, (SUBLANES, 1), 0)
        tab = slice(SUBLANES, 2 * SUBLANES) if reverse else slice(0, SUBLANES)
        out = []
        for q in range(N_LAT_SEQ):
            base = pl.multiple_of(S5_ROWS_CTX + q * LAT_CHUNKS + g * SUBLANES, SUBLANES)
            s_re = x_sc[pl.ds(base, SUBLANES), re]
            s_im = x_sc[pl.ds(base, SUBLANES), im]
            j = 1
            while j < SUBLANES:
                w_re, w_im = pow_row(j)
                s_re, s_im = madd(s_re, s_im, w_re, w_im,
                                  shift_rows(s_re, j, SUBLANES, r8, reverse), shift_rows(s_im, j, SUBLANES, r8, reverse))
                j *= 2
            c_re, c_im = carry[2 * q], carry[2 * q + 1]
            s_re, s_im = madd(s_re, s_im, p_re[tab], p_im[tab], c_re, c_im)
            edge = (r8 == SUBLANES - 1) if reverse else (r8 == 0)
            x_sc[pl.ds(base, SUBLANES), re] = jnp.where(edge, c_re, shift_rows(s_re, 1, SUBLANES, r8, reverse))
            x_sc[pl.ds(base, SUBLANES), im] = jnp.where(edge, c_im, shift_rows(s_im, 1, SUBLANES, r8, reverse))
            end = slice(0, 1) if reverse else slice(SUBLANES - 1, SUBLANES)
            out += [s_re[end], s_im[end]]
        return tuple(out)

    def run(reverse):
        y_local = _dot(u, m_ref[0, 0, 0])
        ctx_scan(reverse)
        if reverse:
            y_ref[0] += y_local
        else:
            y_ref[0] = y_local
        h0 = h0_ref[0, 0]
        init = []
        for q in range(N_LAT_SEQ):
            init += [h0[q:q + 1, re], h0[q:q + 1, im]]
        lax.fori_loop(0, S5_LAT_GROUPS, functools.partial(lat_group, reverse=reverse), tuple(init))

    @pl.when(d == 0)
    def _():
        run(False)

    @pl.when(d == 1)
    def _():
        run(True)

    y_ref[0] += _dot_nt(x_sc[...].astype(BF16), wout_ref[0, 0, 0])


def _s5_main(u, m, w_in, w_out, pows, h0, layer):
    def tab(*tail):
        return pl.BlockSpec((1, 1, 1) + tail, lambda b, d: (layer, d, b) + (0,) * len(tail))

    return pl.pallas_call(
        _s5_kernel,
        out_shape=(jax.ShapeDtypeStruct((S5_BLK, S5_ROWS, S5_CW), F32),
                   jax.ShapeDtypeStruct((2, S5_BLK, N_CTX_SEQ, 2 * S5_SW), F32)),
        grid=(S5_BLK, 2),
        in_specs=[
            pl.BlockSpec((1, S5_ROWS, S5_CW), lambda b, d: (b, 0, 0)),
            tab(S5_CW, S5_CW), tab(S5_CW, 2 * S5_SW), tab(S5_CW, 2 * S5_SW), tab(2, S5_POW_ROWS, S5_SW),
            pl.BlockSpec((1, 1, N_LAT_SEQ, 2 * S5_SW), lambda b, d: (d, b, 0, 0)),
        ],
        out_specs=(
            pl.BlockSpec((1, S5_ROWS, S5_CW), lambda b, d: (b, 0, 0)),
            pl.BlockSpec((1, 1, N_CTX_SEQ, 2 * S5_SW), lambda b, d: (d, b, 0, 0)),
        ),
        scratch_shapes=[pltpu.VMEM((S5_ROWS, 2 * S5_SW), F32)],
        compiler_params=_cparams(("parallel", "arbitrary")),
        name="s5",
    )(u, m, w_in, w_out, pows, h0)


def _hg_constants():
    t = np.arange(HG_C)
    ae = np.zeros((2, HG_E_ROWS, HG_C), np.float32)
    mask = np.zeros((2, HG_LEVELS, HG_C, HG_C), np.float32)
    for lvl in range(HG_LEVELS):
        half = 1 << lvl
        pos = t % (2 * half)
        mid = t - pos + half - 1
        upper = pos >= half
        u = t[None, :]
        rows_upper = (u > mid[:, None]) & (u <= t[:, None])
        rows_lower = (u > t[:, None]) & (u <= mid[:, None])
        ae[0, lvl * HG_C:(lvl + 1) * HG_C] = np.where(upper[:, None], rows_upper, rows_lower)
        same = (t[:, None] // (2 * half)) == (t[None, :] // (2 * half))
        mask[0, lvl] = same & upper[:, None] & (~upper)[None, :]
    ae[0, HG_LEVELS * HG_C:(HG_LEVELS + 1) * HG_C] = t[None, :] <= t[:, None]
    ae[0, (HG_LEVELS + 1) * HG_C:] = 1.0
    ae[1] = ae[0][:, ::-1]
    ae[1, :(HG_LEVELS + 1) * HG_C] = ae[1, :(HG_LEVELS + 1) * HG_C].reshape(HG_LEVELS + 1, HG_C, HG_C)[:, ::-1].reshape(-1, HG_C)
    mask[1] = mask[0][:, ::-1, ::-1]
    ae2 = np.concatenate([ae, ae], axis=-1)
    mask2 = np.concatenate([mask, mask], axis=-1)
    pair_diag = np.kron(np.eye(2, dtype=np.float32), np.ones((HG_DK, HG_DK), np.float32))
    return ae2, mask2, pair_diag


def _hg_tile(direction, i):
    return i if direction == 0 else HG_TILES - 1 - i


HG_PAIR = 2 * HG_DK


def _pair_diag(x):
    zero = jnp.zeros((x.shape[0], HG_DK), x.dtype)
    return jnp.concatenate([jnp.concatenate([x[:, :HG_DK], zero], axis=1),
                            jnp.concatenate([zero, x[:, HG_DK:]], axis=1)], axis=0)


def _hg_chunk(direction, rows, q_ref, f_ref, v_ref, lb, ae_ref, mask_ref, pd_ref, o_ref, st_sc):
    f = lb + (1.0 - lb) * jax.nn.sigmoid(f_ref[rows, :])
    logf = jnp.log2(f)
    kk = 1.0 - f
    hi = logf.astype(BF16)
    lo = (logf - hi.astype(F32)).astype(BF16)
    ex = _dot(ae_ref[direction], jnp.concatenate([hi, lo], axis=0))
    dec = jnp.exp2(ex[:(HG_LEVELS + 1) * HG_C])
    cum = ex[HG_LEVELS * HG_C:(HG_LEVELS + 1) * HG_C]
    tot = ex[(HG_LEVELS + 1) * HG_C:(HG_LEVELS + 1) * HG_C + 1]
    q = q_ref[rows, :].astype(F32)
    v_bf = v_ref[rows, :]
    v = v_bf.astype(F32)
    q_in = (q * dec[HG_LEVELS * HG_C:]).astype(BF16)
    k_tail = (kk * jnp.exp2(tot - cum)).astype(BF16)
    dec_tot = jnp.exp2(tot)
    qk = q * kk
    outs = []
    for p in range(HG_HEADS // 2):
        sl = slice(p * HG_PAIR, (p + 1) * HG_PAIR)
        scores = jnp.zeros((HG_C, HG_PAIR), F32)
        for lvl in range(HG_LEVELS):
            g = dec[lvl * HG_C:(lvl + 1) * HG_C, sl]
            scores = scores + mask_ref[direction, lvl] * _dot_nt(
                (q[:, sl] * g).astype(BF16), _pair_diag((kk[:, sl] * g).astype(BF16)))
        diag = jnp.concatenate(
            [jnp.broadcast_to(jnp.sum(qk[:, (2 * p + j) * HG_DK:(2 * p + j + 1) * HG_DK], axis=-1, keepdims=True),
                              (HG_C, HG_DK)) for j in range(2)], axis=-1)
        st = st_sc[direction, p]
        outs.append(_dot(scores.astype(BF16), _pair_diag(v_bf[:, sl])) + diag * v[:, sl]
                    + _dot_nt(q_in[:, sl], st.astype(BF16)))
        st_new = st * dec_tot[:, sl] + pd_ref[...] * _dot_tn(v_bf[:, sl], k_tail[:, sl])
        st_sc[direction, p] = st_new
    o_ref[rows, :] = jnp.concatenate(outs, axis=-1)


def _hg_kernel(qf_ref, ff_ref, vf_ref, qb_ref, fb_ref, vb_ref, lb_ref, ae_ref, mask_ref, pd_ref, h0f_ref, h0b_ref,
               of_ref, ob_ref, finf_ref, finb_ref, st_sc, *, layer):
    i = pl.program_id(0)
    x = lb_ref[:, :, 0, :]
    e = jnp.exp(x - jnp.max(x, axis=0, keepdims=True))
    sm = e / jnp.sum(e, axis=0, keepdims=True)
    lb = jnp.sum(sm[1:layer + 1], axis=0) if layer > 0 else jnp.zeros((2, D_BRANCH), F32)
    for direction, h0_ref in ((0, h0f_ref), (1, h0b_ref)):
        tile = _hg_tile(direction, i)
        first_ctx = 0 if direction == 0 else HG_TILES_PER_CTX - 1
        first_lat = 0 if direction == 0 else HG_TILES_PER_LAT - 1
        is_start = jnp.where(tile < HG_CTX_TILES,
                             tile % HG_TILES_PER_CTX == first_ctx,
                             (tile - HG_CTX_TILES) % HG_TILES_PER_LAT == first_lat)

        @pl.when(is_start)
        def _(direction=direction, h0_ref=h0_ref):
            for p in range(HG_HEADS // 2):
                st_sc[direction, p] = _pair_diag(
                    jnp.concatenate([h0_ref[0, 0, 2 * p], h0_ref[0, 0, 2 * p + 1]], axis=1))

    for c in range(HG_TM // HG_C):
        fwd_rows = slice(c * HG_C, (c + 1) * HG_C)
        bwd_rows = slice(HG_TM - (c + 1) * HG_C, HG_TM - c * HG_C)
        _hg_chunk(0, fwd_rows, qf_ref, ff_ref, vf_ref, lb[0:1], ae_ref, mask_ref, pd_ref, of_ref, st_sc)
        _hg_chunk(1, bwd_rows, qb_ref, fb_ref, vb_ref, lb[1:2], ae_ref, mask_ref, pd_ref, ob_ref, st_sc)

    for direction, fin_ref in ((0, finf_ref), (1, finb_ref)):
        @pl.when(_hg_tile(direction, i) < HG_CTX_TILES)
        def _(direction=direction, fin_ref=fin_ref):
            for h in range(HG_HEADS):
                p, j = divmod(h, 2)
                fin_ref[0, h] = st_sc[direction, p, j * HG_DK:(j + 1) * HG_DK, j * HG_DK:(j + 1) * HG_DK]


def _hgrn(proj, fz, hg_lb, ae, mask, pair_diag, h0t, layer):
    def tok_spec(direction, col):
        return pl.BlockSpec((HG_TM, D_BRANCH), lambda i: (_hg_tile(direction, i), col))

    def fin_spec(direction):
        return pl.BlockSpec((1, HG_HEADS, HG_DK, HG_DK),
                            lambda i: (jnp.minimum(_hg_tile(direction, i) // HG_TILES_PER_CTX, N_CTX_SEQ - 1), 0, 0, 0))

    def h0_spec(direction):
        return pl.BlockSpec((1, 1, HG_HEADS, HG_DK, HG_DK),
                            lambda i: (_cond_row(_hg_tile(direction, i), HG_TM), direction, 0, 0, 0))

    fin_shape = jax.ShapeDtypeStruct((N_CTX_SEQ, HG_HEADS, HG_DK, HG_DK), F32)
    return pl.pallas_call(
        functools.partial(_hg_kernel, layer=layer),
        out_shape=(jax.ShapeDtypeStruct((N_TOK, D_BRANCH), F32), jax.ShapeDtypeStruct((N_TOK, D_BRANCH), F32),
                   fin_shape, fin_shape),
        grid=(HG_TILES,),
        in_specs=[
            tok_spec(0, HG_Q_COL), tok_spec(0, 0), tok_spec(0, HG_V_COL),
            tok_spec(1, HG_Q_COL), tok_spec(1, 1), tok_spec(1, HG_V_COL),
            pl.BlockSpec((DEPTH, 2, 1, D_BRANCH), lambda i: (0, 0, 0, 0)),
            pl.BlockSpec((2, HG_E_ROWS, 2 * HG_C), lambda i: (0, 0, 0)),
            pl.BlockSpec((2, HG_LEVELS, HG_C, HG_PAIR), lambda i: (0, 0, 0, 0)),
            pl.BlockSpec((HG_PAIR, HG_PAIR), lambda i: (0, 0)),
            h0_spec(0), h0_spec(1),
        ],
        out_specs=(
            pl.BlockSpec((HG_TM, D_BRANCH), lambda i: (_hg_tile(0, i), 0)),
            pl.BlockSpec((HG_TM, D_BRANCH), lambda i: (_hg_tile(1, i), 0)),
            fin_spec(0), fin_spec(1),
        ),
        scratch_shapes=[pltpu.VMEM((2, HG_HEADS // 2, HG_PAIR, HG_PAIR), F32)],
        compiler_params=_cparams(("arbitrary",)),
        name="hgrn",
    )(proj, fz, proj, proj, fz, proj, hg_lb.reshape(DEPTH, 2, 1, D_BRANCH), ae, mask, pair_diag, h0t, h0t)


MRG_TM = 256


def _gelu_tanh(x):
    return 0.5 * x * (1.0 + jnp.tanh(0.7978845608028654 * (x + 0.044715 * (x * x * x))))


def _merge_kernel(scb_ref, scc_ref, sch_ref, hgg_ref, pool_ref, g0_ref, g1_ref, g2_ref, g3_ref,
                  yrow_ref, of_ref, ob_ref, x_ref, mod_ref, conv_ref, gluw_ref, glub_ref, ng_ref,
                  poolw_ref, pscale_ref, wbr_ref, wout_ref, lng_ref, lnb_ref, o_ref,
                  ys5_sc, gluw_sc, poolw_sc, wbr_sc, wout_sc):
    tile = pl.program_id(0)

    @pl.when(tile == 0)
    def _():
        gluw_sc[...] = gluw_ref[0].astype(BF16)
        poolw_sc[...] = poolw_ref[0].astype(BF16)
        wbr_sc[...] = wbr_ref[0].astype(BF16)
        wout_sc[...] = wout_ref[0].astype(BF16)

    line = jnp.where(tile * MRG_TM < N_CTX_TOK, CTX_LEN, GRID_W)
    pos = lax.broadcasted_iota(jnp.int32, (MRG_TM, 1), 0) & (line - 1)

    def shifted(val, k):
        rolled = pltpu.roll(val, k % MRG_TM, axis=0)
        ok = (pos >= k) if k > 0 else (pos < line + k)
        return jnp.where(ok, rolled, 0.0)

    m = scc_ref[...].astype(F32) * sch_ref[...].astype(F32)
    conv = conv_ref[0, 0:1, :] * shifted(m, 1) + conv_ref[0, 1:2, :] * m + conv_ref[0, 2:3, :] * shifted(m, -1)
    ya = scb_ref[...].astype(F32) * conv

    for b in range(S5_BLK):
        for t in range(S5_T):
            ys5_sc[b, pl.ds(t, MRG_TM // S5_T, stride=S5_T), :] = yrow_ref[b, :, t * LANES:(t + 1) * LANES]
    z = _gelu_tanh(jnp.concatenate([ys5_sc[b] for b in range(S5_BLK)], axis=-1))
    yb = z * _sigmoid_tanh(_dot(z.astype(BF16), gluw_sc[...]) + glub_ref[0])

    o = of_ref[...] + ob_ref[...]
    normed = []
    for h in range(HG_HEADS):
        oh = o[:, h * HG_DK:(h + 1) * HG_DK]
        ms = jnp.mean(oh * oh, axis=-1, keepdims=True)
        normed.append(oh * lax.rsqrt(ms + LN_EPS) * ng_ref[0])
    gate_c = hgg_ref[...].astype(F32)
    yc = jnp.concatenate(normed, axis=-1) * (gate_c * _sigmoid_tanh(gate_c))

    pu = pool_ref[...].astype(F32)
    posf = pos.astype(F32)
    linef = line.astype(F32)
    pooled = []
    for gi, w in enumerate(POOL_WINDOWS):
        vg = pu[:, gi * POOL_GROUP:(gi + 1) * POOL_GROUP]
        back, fwd, span = vg, vg, 1
        while 2 * span <= w // 2:
            back = back + shifted(back, span)
            fwd = fwd + shifted(fwd, -span)
            span *= 2
        s = shifted(back, 1) + fwd
        cnt = jnp.minimum(posf + w // 2, linef) - jnp.maximum(posf - w // 2, 0.0)
        pg = s / cnt - vg
        pooled.append(_dot(pg.astype(BF16), poolw_sc[gi]))
    yd = jnp.concatenate(pooled, axis=-1) * pscale_ref[0]

    merged = jnp.zeros((MRG_TM, D_MODEL), F32)
    for k, (br, gate_ref) in enumerate(((ya, g0_ref), (yb, g1_ref), (yc, g2_ref), (yd, g3_ref))):
        merged = merged + gate_ref[...].astype(F32) * _dot(br.astype(BF16), wbr_sc[k])
    mix = _dot(merged.astype(BF16), wout_sc[...])
    y = ALPHA * x_ref[...] + mod_ref[0, 0, 5:6, :] * mix
    o_ref[...] = _layer_norm(y, lng_ref[0, 0], lnb_ref[0, 0])


def _merge(proj, y_rows, o_f, o_b, x, mod, conv_w, glu_w, glu_b, norm_g, pool_w, pool_scale, w_branch, w_out,
           ln_g, ln_b, layer):
    def part(col):
        return pl.BlockSpec((MRG_TM, D_BRANCH), lambda i: (i, col))

    def gate(k):
        return pl.BlockSpec((MRG_TM, D_MODEL), lambda i: (i, N_IN_PARTS * D_BRANCH // D_MODEL + k))

    def per_layer(*shape, single=False):
        mode = dict(pipeline_mode=pl.Buffered(1)) if single else {}
        return pl.BlockSpec((1,) + shape, lambda i: (layer,) + (0,) * len(shape), **mode)

    return pl.pallas_call(
        _merge_kernel,
        out_shape=jax.ShapeDtypeStruct((N_TOK, D_MODEL), F32),
        grid=(N_TOK // MRG_TM,),
        in_specs=[
            part(0), part(1), part(2), part(8), part(9), gate(0), gate(1), gate(2), gate(3),
            pl.BlockSpec((S5_BLK, MRG_TM // S5_T, S5_CW), lambda i: (0, i, 0)),
            pl.BlockSpec((MRG_TM, D_BRANCH), lambda i: (i, 0)),
            pl.BlockSpec((MRG_TM, D_BRANCH), lambda i: (i, 0)),
            pl.BlockSpec((MRG_TM, D_MODEL), lambda i: (i, 0)),
            _mod_spec(layer, MRG_TM),
            per_layer(3, D_BRANCH), per_layer(D_BRANCH, D_BRANCH, single=True), per_layer(1, D_BRANCH),
            per_layer(1, HG_DK), per_layer(len(POOL_WINDOWS), POOL_GROUP, POOL_GROUP, single=True),
            per_layer(1, D_BRANCH), per_layer(N_BRANCH, D_BRANCH, D_MODEL, single=True),
            per_layer(D_MODEL, D_MODEL, single=True), _ln_spec(layer, 1), _ln_spec(layer, 1),
        ],
        out_specs=pl.BlockSpec((MRG_TM, D_MODEL), lambda i: (i, 0)),
        scratch_shapes=[
            pltpu.VMEM((S5_BLK, MRG_TM, LANES), F32),
            pltpu.VMEM((D_BRANCH, D_BRANCH), BF16),
            pltpu.VMEM((len(POOL_WINDOWS), POOL_GROUP, POOL_GROUP), BF16),
            pltpu.VMEM((N_BRANCH, D_BRANCH, D_MODEL), BF16),
            pltpu.VMEM((D_MODEL, D_MODEL), BF16),
        ],
        compiler_params=_cparams(("arbitrary",)),
        name="merge",
    )(proj, proj, proj, proj, proj, proj, proj, proj, proj, y_rows, o_f, o_b, x, mod,
      conv_w, glu_w, glu_b.reshape(DEPTH, 1, D_BRANCH), norm_g.reshape(DEPTH, 1, HG_DK), pool_w,
      pool_scale.reshape(DEPTH, 1, D_BRANCH), w_branch, w_out, ln_g, ln_b)


def _grid_pos_embedding():
    rows = LAT_LEN // GRID_W
    quarter = D_MODEL // 4
    omega = POS_BASE ** (-jnp.arange(quarter, dtype=F32) / quarter)
    ar = jnp.arange(rows, dtype=F32)[:, None] * omega
    ac = jnp.arange(GRID_W, dtype=F32)[:, None] * omega
    row_tab = jnp.concatenate([jnp.sin(ar), jnp.cos(ar)], -1)[:, None, :]
    col_tab = jnp.concatenate([jnp.sin(ac), jnp.cos(ac)], -1)[None, :, :]
    shape = (rows, GRID_W, 2 * quarter)
    pos = jnp.concatenate([jnp.broadcast_to(row_tab, shape), jnp.broadcast_to(col_tab, shape)], -1)
    return pos.reshape(LAT_LEN, D_MODEL)


def kernel(x_prompt, x_sample, state_s5_re, state_s5_im, state_hgrn, c, c_ctx, w_ada, b_ada, ln_g, ln_b,
           ffn_w1, ffn_w3, ffn_w2, w_in, sc_conv, s5_lam_re, s5_lam_im, s5_log_dt, s5_b_re, s5_b_im,
           s5_c_re, s5_c_im, s5_d, s5_glu_w, s5_glu_b, hg_lb, hg_norm_g, pool_w, pool_scale, w_branch, w_out):
    x = (x_prompt.reshape(N_CTX_TOK, D_MODEL), x_sample.reshape(N_LAT_TOK, D_MODEL), _grid_pos_embedding())

    cond = jnp.zeros((COND_PAD, D_MODEL), F32).at[0].set(c_ctx).at[1:N_COND].set(c)
    mod = _ada(cond, w_ada, b_ada).reshape(DEPTH, COND_PAD, N_SUB * 3, D_MODEL)
    ln_g4 = ln_g.reshape(DEPTH, N_SUB, 1, D_MODEL)
    ln_b4 = ln_b.reshape(DEPTH, N_SUB, 1, D_MODEL)

    s5_m, s5_win, s5_wout, s5_pow = _s5_tables(*_s5_param_layout(
        s5_lam_re, s5_lam_im, s5_log_dt, s5_b_re, s5_b_im, s5_c_re, s5_c_im, s5_d))

    w_in_bf = _cast_w_in(w_in)

    ae_np, mask_np, pair_np = _hg_constants()
    ae = jnp.asarray(ae_np, BF16)
    mask = jnp.asarray(mask_np, F32)
    pair_diag = jnp.asarray(pair_np, F32)

    fin_s5, fin_hg = [], []
    for l in range(DEPTH):
        x, h = _ffn(x, mod, ffn_w1, ffn_w3, ffn_w2, ln_g4, ln_b4, l, 0)
        proj, fz, u_rows = _inproj(h, w_in_bf, l)

        h0 = jnp.concatenate([
            state_s5_re[:, l].reshape(N_LAT_SEQ, 2, S5_BLK, S5_SW),
            state_s5_im[:, l].reshape(N_LAT_SEQ, 2, S5_BLK, S5_SW)], axis=-1).transpose(1, 2, 0, 3)
        y_rows, fin = _s5_main(u_rows, s5_m, s5_win, s5_wout, s5_pow, h0, l)
        fin_s5.append(fin)

        h0t = jnp.concatenate([jnp.zeros((1, 2, HG_HEADS, HG_DK, HG_DK), F32),
                               jnp.swapaxes(state_hgrn[:, l], -1, -2)], axis=0)
        o_f, o_b, fin_f, fin_b = _hgrn(proj, fz, hg_lb, ae, mask, pair_diag, h0t, l)
        fin_hg.append(jnp.stack([fin_f, fin_b], axis=1))

        x = _merge(proj, y_rows, o_f, o_b, x, mod, sc_conv, s5_glu_w, s5_glu_b, hg_norm_g,
                   pool_w, pool_scale, w_branch, w_out, ln_g4, ln_b4, l)
        x = _ffn(x, mod, ffn_w1, ffn_w3, ffn_w2, ln_g4, ln_b4, l, 2, split_output=(l == DEPTH - 1))

    y_prompt = x[0].reshape(N_CTX_SEQ, CTX_LEN, D_MODEL)
    y_sample = x[1].reshape(N_LAT_SEQ, LAT_LEN, D_MODEL)
    fin = jnp.stack(fin_s5)
    fin = fin.reshape(DEPTH, 2, S5_BLK, N_CTX_SEQ, 2, S5_GPB, S5_STATE)
    fin = fin.transpose(4, 3, 0, 1, 2, 5, 6).reshape(2, N_CTX_SEQ, DEPTH, 2, S5_GROUPS, S5_STATE)
    new_hgrn = jnp.swapaxes(jnp.stack(fin_hg, axis=1), -1, -2)
    return y_prompt, y_sample, fin[0], fin[1], new_hgrn
```

```python
import functools

import numpy as np
import jax
import jax.numpy as jnp
from jax import lax
from jax.experimental import pallas as pl
from jax.experimental.pallas import tpu as pltpu

F32 = jnp.float32
BF16 = jnp.bfloat16

D_MODEL = 1024
N_CTX_SEQ = 16
CTX_LEN = 256
DEPTH = 2
N_LAT_SEQ = 2
LAT_LEN = 4096
GRID_W = 64
D_BRANCH = 512
N_BRANCH = 4
S5_GROUPS = 32
S5_GROUP = 16
S5_STATE = 64
HG_HEADS = 4
HG_DK = 128
POOL_WINDOWS = (2, 4, 8, 16)
POOL_GROUP = 128
D_FF = 2816
N_SUB = 3
N_IN_PARTS = 10
IN_WIDTH = N_IN_PARTS * D_BRANCH + N_BRANCH * D_MODEL
ADA_WIDTH = N_SUB * 3 * D_MODEL
ALPHA = (2 * DEPTH) ** 0.25
LN_EPS = 1e-5
POS_BASE = 10000.0

N_CTX_TOK = N_CTX_SEQ * CTX_LEN
N_LAT_TOK = N_LAT_SEQ * LAT_LEN
N_TOK = N_CTX_TOK + N_LAT_TOK
N_COND = 1 + N_LAT_SEQ
COND_PAD = 8

LANES = 128
SUBLANES = 8
VMEM_LIMIT = 56 * 1024 * 1024

S5_T = 8
S5_BLK = 4
S5_GPB = S5_GROUPS // S5_BLK
S5_SW = S5_GPB * S5_STATE
S5_CW = S5_T * LANES
CTX_CHUNKS = CTX_LEN // S5_T
LAT_CHUNKS = LAT_LEN // S5_T
S5_ROWS_CTX = N_CTX_TOK // S5_T
S5_ROWS_LAT = N_LAT_TOK // S5_T
S5_ROWS = S5_ROWS_CTX + S5_ROWS_LAT
S5_LAT_GROUPS = LAT_CHUNKS // SUBLANES
S5_POW_ROWS = 3 * SUBLANES
S5_U_COL = 3

HG_C = 128
HG_LEVELS = 7
HG_TM = 256
HG_TILES = N_TOK // HG_TM
HG_CTX_TILES = N_CTX_TOK // HG_TM
HG_TILES_PER_CTX = CTX_LEN // HG_TM
HG_TILES_PER_LAT = LAT_LEN // HG_TM
HG_E_ROWS = (HG_LEVELS + 1) * HG_C + SUBLANES


def _cparams(sem):
    return pltpu.CompilerParams(dimension_semantics=sem, vmem_limit_bytes=VMEM_LIMIT)


def _cond_row(tile, tile_tokens):
    tok = tile * tile_tokens
    return jnp.where(tok < N_CTX_TOK, 0, 1 + (tok - N_CTX_TOK) // LAT_LEN)


def _dot(a, b):
    return jnp.dot(a, b, preferred_element_type=F32)


def _dot_nt(a, b):
    return lax.dot_general(a, b, (((1,), (1,)), ((), ())), preferred_element_type=F32)


def _dot_tn(a, b):
    return lax.dot_general(a, b, (((0,), (0,)), ((), ())), preferred_element_type=F32)


def _split3(x):
    h1 = x.astype(BF16)
    r1 = x - h1.astype(F32)
    h2 = r1.astype(BF16)
    h3 = (r1 - h2.astype(F32)).astype(BF16)
    return h1, h2, h3


def _dot_nt_hi(a3, b3):
    acc = None
    for x in range(2):
        for y in range(2 - x):
            term = _dot_nt(a3[x], b3[y])
            acc = term if acc is None else acc + term
    return acc


def _silu(x):
    return x * jax.nn.sigmoid(x)


def _sigmoid_tanh(x):
    return 0.5 * jnp.tanh(0.5 * x) + 0.5


def _layer_norm(y, g, b):
    mu = jnp.mean(y, axis=-1, keepdims=True)
    yc = y - mu
    var = jnp.mean(yc * yc, axis=-1, keepdims=True)
    return yc * lax.rsqrt(var + LN_EPS) * g + b


def _mod_spec(layer, tile_tokens):
    return pl.BlockSpec((1, 1, N_SUB * 3, D_MODEL),
                        lambda i, *_: (layer, _cond_row(i, tile_tokens), 0, 0))


def _ln_spec(layer, sub):
    return pl.BlockSpec((1, 1, 1, D_MODEL), lambda *_: (layer, sub, 0, 0))


ADA_TN = 1152


def _ada_kernel(c_ref, w_ref, b_ref, o_ref):
    s = _silu(c_ref[...]).astype(BF16)
    o_ref[0] = _dot(s, w_ref[0].astype(BF16)) + b_ref[0]


def _ada(cond, w_ada, b_ada):
    return pl.pallas_call(
        _ada_kernel,
        out_shape=jax.ShapeDtypeStruct((DEPTH, COND_PAD, ADA_WIDTH), F32),
        grid=(DEPTH, ADA_WIDTH // ADA_TN),
        in_specs=[
            pl.BlockSpec((COND_PAD, D_MODEL), lambda l, j: (0, 0)),
            pl.BlockSpec((1, D_MODEL, ADA_TN), lambda l, j: (l, 0, j)),
            pl.BlockSpec((1, 1, ADA_TN), lambda l, j: (l, 0, j)),
        ],
        out_specs=pl.BlockSpec((1, COND_PAD, ADA_TN), lambda l, j: (l, 0, j)),
        compiler_params=_cparams(("parallel", "parallel")),
        name="ada",
    )(cond, w_ada, b_ada.reshape(DEPTH, 1, ADA_WIDTH))


FFN_TM = 1024
FFN_TM_SPLIT = 512
FFN_TC = 256
FFN_CHUNKS = D_FF // FFN_TC


def _ffn_kernel(*refs, sub, tm, split_input, split_output, emit_next):
    step = pl.program_id(0)
    tile = step - FFN_CHUNKS
    is_ctx = tile < N_CTX_TOK // tm
    n_x = 3 if split_input else 1
    x_refs, refs = refs[:n_x], refs[n_x:]
    mod_ref, w1_ref, w3_ref, w2_ref, g_ref, b_ref = refs[:6]
    out_refs = refs[6:-4]
    w1_sc, w3_sc, w2_sc, act_sc = refs[-4:]

    @pl.when(step < FFN_CHUNKS)
    def _():
        w1_sc[step] = w1_ref[0, 0].astype(BF16)
        w3_sc[step] = w3_ref[0, 0].astype(BF16)
        w2_sc[pl.ds(pl.multiple_of(step * FFN_TC, FFN_TC), FFN_TC), :] = w2_ref[0, 0].astype(BF16)

    @pl.when(step >= FFN_CHUNKS)
    def _():
        if split_input:
            ctx_ref, lat_ref, pos_ref = x_refs
            x = jnp.where(is_ctx, ctx_ref[...], lat_ref[...] + pos_ref[...])
        else:
            x = x_refs[0][...]
        shift = mod_ref[0, 0, 3 * sub:3 * sub + 1, :]
        scale = mod_ref[0, 0, 3 * sub + 1:3 * sub + 2, :]
        gate = mod_ref[0, 0, 3 * sub + 2:3 * sub + 3, :]
        h = (x * (1.0 + scale) + shift).astype(BF16)
        for c in range(FFN_CHUNKS):
            a = _dot(h, w1_sc[c])
            b = _dot(h, w3_sc[c])
            act_sc[:, c * FFN_TC:(c + 1) * FFN_TC] = (_silu(a) * b).astype(BF16)
        f = _dot(act_sc[...], w2_sc[...])
        y = _layer_norm(ALPHA * x + gate * (0.5 * f), g_ref[0, 0], b_ref[0, 0])
        if emit_next:
            nxt = 3 * (sub + 1)
            out_refs[1][...] = (y * (1.0 + mod_ref[0, 0, nxt + 1:nxt + 2, :])
                                + mod_ref[0, 0, nxt:nxt + 1, :]).astype(BF16)
        if split_output:
            @pl.when(is_ctx)
            def _():
                out_refs[0][...] = y

            @pl.when(jnp.logical_not(is_ctx))
            def _():
                out_refs[1][...] = y
        else:
            out_refs[0][...] = y


def _ffn(x, mod, w1, w3, w2, ln_g, ln_b, layer, sub, split_output=False):
    which = sub // 2
    split_input = isinstance(x, tuple)
    tm = FFN_TM_SPLIT if split_input else FFN_TM
    ctx_tiles = N_CTX_TOK // tm

    def chunk(s):
        return jnp.minimum(s, FFN_CHUNKS - 1)

    def tile(s):
        return jnp.maximum(s - FFN_CHUNKS, 0)

    def tok_spec(index):
        return pl.BlockSpec((tm, D_MODEL), lambda s: (index(tile(s)), 0))

    def ctx_tile(t):
        return jnp.minimum(t, ctx_tiles - 1)

    def lat_tile(t):
        return jnp.maximum(t - ctx_tiles, 0)

    if split_input:
        pos_tiles = LAT_LEN // tm
        x_specs = [tok_spec(ctx_tile), tok_spec(lat_tile), tok_spec(lambda t: lat_tile(t) % pos_tiles)]
    else:
        x_specs = [tok_spec(lambda t: t)]
        x = (x,)
    emit_next = sub == 0
    if split_output:
        out_shape = (jax.ShapeDtypeStruct((N_CTX_TOK, D_MODEL), F32), jax.ShapeDtypeStruct((N_LAT_TOK, D_MODEL), F32))
        out_specs = (tok_spec(ctx_tile), tok_spec(lat_tile))
    elif emit_next:
        out_shape = (jax.ShapeDtypeStruct((N_TOK, D_MODEL), F32), jax.ShapeDtypeStruct((N_TOK, D_MODEL), BF16))
        out_specs = (tok_spec(lambda t: t), tok_spec(lambda t: t))
    else:
        out_shape = jax.ShapeDtypeStruct((N_TOK, D_MODEL), F32)
        out_specs = tok_spec(lambda t: t)

    return pl.pallas_call(
        functools.partial(_ffn_kernel, sub=sub, tm=tm, split_input=split_input, split_output=split_output,
                          emit_next=emit_next),
        out_shape=out_shape,
        grid=(FFN_CHUNKS + N_TOK // tm,),
        in_specs=x_specs + [
            pl.BlockSpec((1, 1, N_SUB * 3, D_MODEL), lambda s: (layer, _cond_row(tile(s), tm), 0, 0)),
            pl.BlockSpec((1, 1, D_MODEL, FFN_TC), lambda s: (layer, which, 0, chunk(s))),
            pl.BlockSpec((1, 1, D_MODEL, FFN_TC), lambda s: (layer, which, 0, chunk(s))),
            pl.BlockSpec((1, 1, FFN_TC, D_MODEL), lambda s: (layer, which, chunk(s), 0)),
            _ln_spec(layer, sub), _ln_spec(layer, sub),
        ],
        out_specs=out_specs,
        scratch_shapes=[pltpu.VMEM((FFN_CHUNKS, D_MODEL, FFN_TC), BF16), pltpu.VMEM((FFN_CHUNKS, D_MODEL, FFN_TC), BF16),
                        pltpu.VMEM((D_FF, D_MODEL), BF16), pltpu.VMEM((tm, D_FF), BF16)],
        compiler_params=_cparams(("arbitrary",)),
        name=f"ffn{sub}",
    )(*x, mod, w1, w3, w2, ln_g, ln_b)


INP_TM = 2048
INP_PARTS = 2
INP_TN = INP_PARTS * D_BRANCH
INP_STEPS = IN_WIDTH // INP_TN
HG_Q_COL, HG_F_COL, HG_V_COL = 4, 5, 7


def _cast_w_in_kernel(w_ref, o_ref):
    o_ref[0, 0] = w_ref[0].astype(BF16)


def _cast_w_in(w_in):
    return pl.pallas_call(
        _cast_w_in_kernel,
        out_shape=jax.ShapeDtypeStruct((DEPTH, INP_STEPS, D_MODEL, INP_TN), BF16),
        grid=(DEPTH, INP_STEPS),
        in_specs=[pl.BlockSpec((1, D_MODEL, INP_TN), lambda l, j: (l, 0, j))],
        out_specs=pl.BlockSpec((1, 1, D_MODEL, INP_TN), lambda l, j: (l, j, 0, 0)),
        compiler_params=_cparams(("parallel", "parallel")),
        name="cast_w_in",
    )(w_in)


def _inproj_kernel(h_ref, w_ref, o_ref, fz_ref, u_ref, blk_sc):
    j = pl.program_id(1)

    def project(gate=False):
        res = _dot(h_ref[...], w_ref[0, 0])
        o_ref[...] = (_sigmoid_tanh(res) if gate else res).astype(BF16)
        return res

    def part(res, p):
        k = p % INP_PARTS
        return res[:, k * D_BRANCH:(k + 1) * D_BRANCH]

    special = (S5_U_COL, HG_F_COL, HG_F_COL + 1)
    steps = [p // INP_PARTS for p in special]
    assert len(set(steps)) == len(steps)

    first_gate = N_IN_PARTS // INP_PARTS
    assert N_IN_PARTS % INP_PARTS == 0 and max(steps) < first_gate

    @pl.when(functools.reduce(jnp.logical_and, [j != s for s in steps] + [j < first_gate]))
    def _():
        project()

    @pl.when(j >= first_gate)
    def _():
        project(gate=True)

    for p in (HG_F_COL, HG_F_COL + 1):
        @pl.when(j == p // INP_PARTS)
        def _(p=p):
            fz_ref[...] = part(project(), p)

    @pl.when(j == S5_U_COL // INP_PARTS)
    def _():
        s5_u = part(project(), S5_U_COL)
        for b in range(S5_BLK):
            blk_sc[b] = s5_u[:, b * LANES:(b + 1) * LANES]
            for t in range(S5_T):
                rows = blk_sc[b, pl.ds(t, INP_TM // S5_T, stride=S5_T), :]
                u_ref[b, :, t * LANES:(t + 1) * LANES] = rows.astype(BF16)


def _inproj(h, w_in_bf, layer):
    return pl.pallas_call(
        _inproj_kernel,
        out_shape=(jax.ShapeDtypeStruct((N_TOK, IN_WIDTH), BF16),
                   jax.ShapeDtypeStruct((N_TOK, 2 * D_BRANCH), F32),
                   jax.ShapeDtypeStruct((S5_BLK, S5_ROWS, S5_CW), BF16)),
        grid=(N_TOK // INP_TM, INP_STEPS),
        in_specs=[
            pl.BlockSpec((INP_TM, D_MODEL), lambda i, j: (i, 0)),
            pl.BlockSpec((1, 1, D_MODEL, INP_TN), lambda i, j: (layer, j, 0, 0)),
        ],
        out_specs=(pl.BlockSpec((INP_TM, INP_TN), lambda i, j: (i, j)),
                   pl.BlockSpec((INP_TM, D_BRANCH),
                                lambda i, j: (i, jnp.where(j > HG_F_COL // INP_PARTS, 1, 0))),
                   pl.BlockSpec((S5_BLK, INP_TM // S5_T, S5_CW), lambda i, j: (0, i, 0))),
        scratch_shapes=[pltpu.VMEM((S5_BLK, INP_TM, LANES), F32)],
        compiler_params=_cparams(("parallel", "arbitrary")),
        name="inproj",
    )(h, w_in_bf)


def _s5_tab_kernel(lam_ref, bre_ref, bim_ref, cre_ref, cim_ref, dsk_ref, m_ref, win_ref, wout_ref, pow_ref):
    d = pl.program_id(1)
    fwd = d == 0
    lam_re = lam_ref[0, 0, 0, 0:1, :]
    lam_im = lam_ref[0, 0, 0, 1:2, :]
    dt = jnp.exp(lam_ref[0, 0, 0, 2:3, :])

    def apow(j):
        mag = jnp.exp(lam_re * dt * float(j))
        ang = lam_im * dt * float(j)
        return mag * jnp.cos(ang), mag * jnp.sin(ang)

    a_re, a_im = apow(1)
    den = lam_re * lam_re + lam_im * lam_im
    num_re = a_re - 1.0
    coef_re = (num_re * lam_re + a_im * lam_im) / den
    coef_im = (a_im * lam_re - num_re * lam_im) / den
    ri = lax.broadcasted_iota(jnp.int32, (S5_STATE, S5_SW), 0)
    ci = lax.broadcasted_iota(jnp.int32, (S5_STATE, S5_SW), 1)
    repeat = jnp.where(ri == (ci & (S5_STATE - 1)), 1.0, 0.0).astype(BF16)
    own = (jnp.right_shift(lax.broadcasted_iota(jnp.int32, (LANES, S5_SW), 0), S5_GROUP.bit_length() - 1)
           == jnp.right_shift(lax.broadcasted_iota(jnp.int32, (LANES, S5_SW), 1), S5_STATE.bit_length() - 1))

    def block_diag(ref):
        return jnp.where(own, sum(_dot(piece, repeat) for piece in _split3(ref[0, 0, 0])), 0.0)

    b_re = block_diag(bre_ref)
    b_im = block_diag(bim_ref)
    bb_re = coef_re * b_re - coef_im * b_im
    bb_im = coef_re * b_im + coef_im * b_re
    bbr = _split3(bb_re)
    bbi = _split3(bb_im)
    c_re = block_diag(cre_ref)
    c_im = block_diag(cim_ref)

    kmat = []
    for j in range(S5_T + 1):
        aj_re, aj_im = apow(j)
        pc_re = c_re * aj_re - c_im * aj_im
        pc_im = c_re * aj_im + c_im * aj_re
        if j >= 1:
            r = pl.multiple_of(jnp.where(fwd, j - 1, S5_T - j) * LANES, LANES)
            wout_ref[0, 0, 0, pl.ds(r, LANES), :] = jnp.concatenate([pc_re, -pc_im], axis=1).astype(BF16)
        if j < S5_T:
            pb_re = bb_re * aj_re - bb_im * aj_im
            pb_im = bb_re * aj_im + bb_im * aj_re
            r = pl.multiple_of(jnp.where(fwd, S5_T - 1 - j, j) * LANES, LANES)
            win_ref[0, 0, 0, pl.ds(r, LANES), :] = jnp.concatenate([pb_re, pb_im], axis=1).astype(BF16)
            kmat.append(_dot_nt_hi(bbr, _split3(pc_re)) - _dot_nt_hi(bbi, _split3(pc_im)))

    ri = lax.broadcasted_iota(jnp.int32, (LANES, LANES), 0)
    ci = lax.broadcasted_iota(jnp.int32, (LANES, LANES), 1)
    keep_upper = jnp.where(fwd, 1.0, 0.0)
    kmat[0] = kmat[0] + jnp.where(ri == ci, dsk_ref[0, 0], 0.0) * keep_upper
    for s in range(S5_T):
        for t in range(S5_T):
            tile = kmat[abs(t - s)]
            if t > s:
                tile = tile * keep_upper
            elif t < s:
                tile = tile * (1.0 - keep_upper)
            m_ref[0, 0, 0, s * LANES:(s + 1) * LANES, t * LANES:(t + 1) * LANES] = tile.astype(BF16)

    pow_ref[...] = jnp.zeros_like(pow_ref)
    for i in range(SUBLANES):
        for row, j in ((i, S5_T * (i + 1)), (SUBLANES + i, S5_T * (SUBLANES - i))):
            p_re, p_im = apow(j)
            pow_ref[0, 0, 0, 0, row:row + 1, :] = p_re
            pow_ref[0, 0, 0, 1, row:row + 1, :] = p_im
    p_re, p_im = apow(S5_T * 2 * SUBLANES)
    pow_ref[0, 0, 0, 0, 2 * SUBLANES:2 * SUBLANES + 1, :] = p_re
    pow_ref[0, 0, 0, 1, 2 * SUBLANES:2 * SUBLANES + 1, :] = p_im


def _s5_tables(lam, b_re, b_im, c_re, c_im, d_skip):
    lead = (DEPTH, 2, S5_BLK)

    def spec(*tail):
        return pl.BlockSpec((1, 1, 1) + tail, lambda l, d, b: (l, d, b) + (0,) * len(tail))

    return pl.pallas_call(
        _s5_tab_kernel,
        out_shape=(
            jax.ShapeDtypeStruct(lead + (S5_CW, S5_CW), BF16),
            jax.ShapeDtypeStruct(lead + (S5_CW, 2 * S5_SW), BF16),
            jax.ShapeDtypeStruct(lead + (S5_CW, 2 * S5_SW), BF16),
            jax.ShapeDtypeStruct(lead + (2, S5_POW_ROWS, S5_SW), F32),
        ),
        grid=lead,
        in_specs=[spec(3, S5_SW)] + [spec(LANES, S5_STATE)] * 4
        + [pl.BlockSpec((1, 1, 1, LANES), lambda l, d, b: (l, b, 0, 0))],
        out_specs=(spec(S5_CW, S5_CW), spec(S5_CW, 2 * S5_SW), spec(S5_CW, 2 * S5_SW),
                   spec(2, S5_POW_ROWS, S5_SW)),
        compiler_params=_cparams(("parallel", "parallel", "parallel")),
        name="s5_tables",
    )(lam, b_re, b_im, c_re, c_im, d_skip)


def _s5_param_layout(s5_lam_re, s5_lam_im, s5_log_dt, s5_b_re, s5_b_im, s5_c_re, s5_c_im, s5_d):
    lam = jnp.stack([s5_lam_re, s5_lam_im, jnp.broadcast_to(s5_log_dt[..., None], s5_lam_re.shape)], axis=2)
    lam = lam.reshape(DEPTH, 2, 3, S5_BLK, S5_SW).transpose(0, 1, 3, 2, 4)

    def rows_b(b):
        return jnp.swapaxes(b, -1, -2).reshape(DEPTH, 2, S5_BLK, LANES, S5_STATE)

    def rows_c(c):
        return c.reshape(DEPTH, 2, S5_BLK, LANES, S5_STATE)

    return (lam, rows_b(s5_b_re), rows_b(s5_b_im), rows_c(s5_c_re), rows_c(s5_c_im),
            s5_d.reshape(DEPTH, S5_BLK, 1, LANES))


def _s5_kernel(u_ref, m_ref, win_ref, wout_ref, pow_ref, h0_ref, y_ref, fin_ref, x_sc):
    d = pl.program_id(1)
    u = u_ref[0]
    x_sc[...] = _dot(u, win_ref[0, 0, 0])
    p_re = pow_ref[0, 0, 0, 0]
    p_im = pow_ref[0, 0, 0, 1]
    re = slice(0, S5_SW)
    im = slice(S5_SW, 2 * S5_SW)

    def madd(t_re, t_im, w_re, w_im, s_re, s_im):
        return t_re + w_re * s_re - w_im * s_im, t_im + w_re * s_im + w_im * s_re

    def shift_rows(val, k, n, idx, reverse):
        if reverse:
            return jnp.where(idx < n - k, pltpu.roll(val, val.shape[0] - k, axis=0), 0.0)
        return jnp.where(idx >= k, pltpu.roll(val, k, axis=0), 0.0)

    def pow_row(k):
        row = k - 1 if k <= SUBLANES else 2 * SUBLANES
        return p_re[row:row + 1], p_im[row:row + 1]

    def ctx_scan(reverse):
        s_re = x_sc[0:S5_ROWS_CTX, re]
        s_im = x_sc[0:S5_ROWS_CTX, im]
        c = lax.broadcasted_iota(jnp.int32, (S5_ROWS_CTX, 1), 0) & (CTX_CHUNKS - 1)
        def shift_ctx(val, k):
            if k % SUBLANES:
                return shift_rows(val, k, CTX_CHUNKS, c, reverse)
            v3 = val.reshape(N_CTX_SEQ, CTX_CHUNKS, S5_SW)
            pad = jnp.zeros((N_CTX_SEQ, k, S5_SW), F32)
            moved = (jnp.concatenate([v3[:, k:], pad], axis=1) if reverse
                     else jnp.concatenate([pad, v3[:, :CTX_CHUNKS - k]], axis=1))
            return moved.reshape(S5_ROWS_CTX, S5_SW)

        k = 1
        while k < CTX_CHUNKS:
            w_re, w_im = pow_row(k)
            s_re, s_im = madd(s_re, s_im, w_re, w_im, shift_ctx(s_re, k), shift_ctx(s_im, k))
            k *= 2
        last = 0 if reverse else CTX_CHUNKS - 1
        pick = (lax.broadcasted_iota(jnp.int32, (N_CTX_SEQ, S5_ROWS_CTX), 1)
                == lax.broadcasted_iota(jnp.int32, (N_CTX_SEQ, S5_ROWS_CTX), 0) * CTX_CHUNKS + last)
        pick = jnp.where(pick, 1.0, 0.0).astype(BF16)
        fin_ref[0, 0, :, re] = sum(_dot(pick, piece) for piece in _split3(s_re))
        fin_ref[0, 0, :, im] = sum(_dot(pick, piece) for piece in _split3(s_im))
        x_sc[0:S5_ROWS_CTX, re] = shift_rows(s_re, 1, CTX_CHUNKS, c, reverse)
        x_sc[0:S5_ROWS_CTX, im] = shift_rows(s_im, 1, CTX_CHUNKS, c, reverse)

    def lat_group(k, carry, reverse):
        g = (S5_LAT_GROUPS - 1 - k) if reverse else k
        r8 = lax.broadcasted_iota(jnp.int32, (SUBLANES, 1), 0)
        tab = slice(SUBLANES, 2 * SUBLANES) if reverse else slice(0, SUBLANES)
        out = []
        for q in range(N_LAT_SEQ):
            base = pl.multiple_of(S5_ROWS_CTX + q * LAT_CHUNKS + g * SUBLANES, SUBLANES)
            s_re = x_sc[pl.ds(base, SUBLANES), re]
            s_im = x_sc[pl.ds(base, SUBLANES), im]
            j = 1
            while j < SUBLANES:
                w_re, w_im = pow_row(j)
                s_re, s_im = madd(s_re, s_im, w_re, w_im,
                                  shift_rows(s_re, j, SUBLANES, r8, reverse), shift_rows(s_im, j, SUBLANES, r8, reverse))
                j *= 2
            c_re, c_im = carry[2 * q], carry[2 * q + 1]
            s_re, s_im = madd(s_re, s_im, p_re[tab], p_im[tab], c_re, c_im)
            edge = (r8 == SUBLANES - 1) if reverse else (r8 == 0)
            x_sc[pl.ds(base, SUBLANES), re] = jnp.where(edge, c_re, shift_rows(s_re, 1, SUBLANES, r8, reverse))
            x_sc[pl.ds(base, SUBLANES), im] = jnp.where(edge, c_im, shift_rows(s_im, 1, SUBLANES, r8, reverse))
            end = slice(0, 1) if reverse else slice(SUBLANES - 1, SUBLANES)
            out += [s_re[end], s_im[end]]
        return tuple(out)

    def run(reverse):
        for t0 in range(0, S5_T, 2):
            src = slice(t0 * LANES, S5_CW) if reverse else slice(0, (t0 + 2) * LANES)
            dst = slice(t0 * LANES, (t0 + 2) * LANES)
            part = _dot(u[:, src], m_ref[0, 0, 0, src, dst])
            if reverse:
                y_ref[0, :, dst] += part
            else:
                y_ref[0, :, dst] = part
        ctx_scan(reverse)
        h0 = h0_ref[0, 0]
        init = []
        for q in range(N_LAT_SEQ):
            init += [h0[q:q + 1, re], h0[q:q + 1, im]]
        lax.fori_loop(0, S5_LAT_GROUPS, functools.partial(lat_group, reverse=reverse), tuple(init))

    @pl.when(d == 0)
    def _():
        run(False)

    @pl.when(d == 1)
    def _():
        run(True)

    y_ref[0] += _dot_nt(x_sc[...].astype(BF16), wout_ref[0, 0, 0])


def _s5_main(u, m, w_in, w_out, pows, h0, layer):
    def tab(*tail):
        return pl.BlockSpec((1, 1, 1) + tail, lambda b, d: (layer, d, b) + (0,) * len(tail))

    return pl.pallas_call(
        _s5_kernel,
        out_shape=(jax.ShapeDtypeStruct((S5_BLK, S5_ROWS, S5_CW), F32),
                   jax.ShapeDtypeStruct((2, S5_BLK, N_CTX_SEQ, 2 * S5_SW), F32)),
        grid=(S5_BLK, 2),
        in_specs=[
            pl.BlockSpec((1, S5_ROWS, S5_CW), lambda b, d: (b, 0, 0)),
            tab(S5_CW, S5_CW), tab(S5_CW, 2 * S5_SW), tab(S5_CW, 2 * S5_SW), tab(2, S5_POW_ROWS, S5_SW),
            pl.BlockSpec((1, 1, N_LAT_SEQ, 2 * S5_SW), lambda b, d: (d, b, 0, 0)),
        ],
        out_specs=(
            pl.BlockSpec((1, S5_ROWS, S5_CW), lambda b, d: (b, 0, 0)),
            pl.BlockSpec((1, 1, N_CTX_SEQ, 2 * S5_SW), lambda b, d: (d, b, 0, 0)),
        ),
        scratch_shapes=[pltpu.VMEM((S5_ROWS, 2 * S5_SW), F32)],
        compiler_params=_cparams(("parallel", "arbitrary")),
        name="s5",
    )(u, m, w_in, w_out, pows, h0)


def _hg_constants():
    t = np.arange(HG_C)
    ae = np.zeros((2, HG_E_ROWS, HG_C), np.float32)
    mask = np.zeros((2, HG_LEVELS, HG_C, HG_C), np.float32)
    for lvl in range(HG_LEVELS):
        half = 1 << lvl
        pos = t % (2 * half)
        mid = t - pos + half - 1
        upper = pos >= half
        u = t[None, :]
        rows_upper = (u > mid[:, None]) & (u <= t[:, None])
        rows_lower = (u > t[:, None]) & (u <= mid[:, None])
        ae[0, lvl * HG_C:(lvl + 1) * HG_C] = np.where(upper[:, None], rows_upper, rows_lower)
        same = (t[:, None] // (2 * half)) == (t[None, :] // (2 * half))
        mask[0, lvl] = same & upper[:, None] & (~upper)[None, :]
    ae[0, HG_LEVELS * HG_C:(HG_LEVELS + 1) * HG_C] = t[None, :] <= t[:, None]
    ae[0, (HG_LEVELS + 1) * HG_C:] = 1.0
    ae[1] = ae[0][:, ::-1]
    ae[1, :(HG_LEVELS + 1) * HG_C] = ae[1, :(HG_LEVELS + 1) * HG_C].reshape(HG_LEVELS + 1, HG_C, HG_C)[:, ::-1].reshape(-1, HG_C)
    mask[1] = mask[0][:, ::-1, ::-1]
    ae2 = np.concatenate([ae, ae], axis=-1)
    mask2 = np.concatenate([mask, mask], axis=-1)
    pair_diag = np.kron(np.eye(2, dtype=np.float32), np.ones((HG_DK, HG_DK), np.float32))
    return ae2, mask2, pair_diag


def _hg_tile(direction, i):
    return i if direction == 0 else HG_TILES - 1 - i


HG_PAIR = 2 * HG_DK


def _pair_diag(x):
    zero = jnp.zeros((x.shape[0], HG_DK), x.dtype)
    return jnp.concatenate([jnp.concatenate([x[:, :HG_DK], zero], axis=1),
                            jnp.concatenate([zero, x[:, HG_DK:]], axis=1)], axis=0)


def _hg_chunk(direction, rows, q_ref, f_ref, v_ref, lb, ae_ref, mask_ref, pd_ref, o_ref, st_sc):
    f = lb + (1.0 - lb) * jax.nn.sigmoid(f_ref[rows, :])
    logf = jnp.log2(f)
    kk = 1.0 - f
    hi = logf.astype(BF16)
    lo = (logf - hi.astype(F32)).astype(BF16)
    ex = _dot(ae_ref[direction], jnp.concatenate([hi, lo], axis=0))
    dec = jnp.exp2(ex[:(HG_LEVELS + 1) * HG_C])
    cum = ex[HG_LEVELS * HG_C:(HG_LEVELS + 1) * HG_C]
    tot = ex[(HG_LEVELS + 1) * HG_C:(HG_LEVELS + 1) * HG_C + 1]
    q = q_ref[rows, :].astype(F32)
    v_bf = v_ref[rows, :]
    v = v_bf.astype(F32)
    q_in = (q * dec[HG_LEVELS * HG_C:]).astype(BF16)
    k_tail = (kk * jnp.exp2(tot - cum)).astype(BF16)
    dec_tot = jnp.exp2(tot)
    qk = q * kk
    outs = []
    for p in range(HG_HEADS // 2):
        sl = slice(p * HG_PAIR, (p + 1) * HG_PAIR)
        scores = jnp.zeros((HG_C, HG_PAIR), F32)
        for lvl in range(HG_LEVELS):
            g = dec[lvl * HG_C:(lvl + 1) * HG_C, sl]
            scores = scores + mask_ref[direction, lvl] * _dot_nt(
                (q[:, sl] * g).astype(BF16), _pair_diag((kk[:, sl] * g).astype(BF16)))
        diag = jnp.concatenate(
            [jnp.broadcast_to(jnp.sum(qk[:, (2 * p + j) * HG_DK:(2 * p + j + 1) * HG_DK], axis=-1, keepdims=True),
                              (HG_C, HG_DK)) for j in range(2)], axis=-1)
        st = st_sc[direction, p]
        outs.append(_dot(scores.astype(BF16), _pair_diag(v_bf[:, sl])) + diag * v[:, sl]
                    + _dot_nt(q_in[:, sl], st.astype(BF16)))
        st_new = st * dec_tot[:, sl] + pd_ref[...] * _dot_tn(v_bf[:, sl], k_tail[:, sl])
        st_sc[direction, p] = st_new
    o_ref[rows, :] = jnp.concatenate(outs, axis=-1)


def _hg_kernel(qf_ref, ff_ref, vf_ref, qb_ref, fb_ref, vb_ref, lb_ref, ae_ref, mask_ref, pd_ref, h0f_ref, h0b_ref,
               of_ref, ob_ref, finf_ref, finb_ref, st_sc, *, layer):
    i = pl.program_id(0)
    x = lb_ref[:, :, 0, :]
    e = jnp.exp(x - jnp.max(x, axis=0, keepdims=True))
    sm = e / jnp.sum(e, axis=0, keepdims=True)
    lb = jnp.sum(sm[1:layer + 1], axis=0) if layer > 0 else jnp.zeros((2, D_BRANCH), F32)
    for direction, h0_ref in ((0, h0f_ref), (1, h0b_ref)):
        tile = _hg_tile(direction, i)
        first_ctx = 0 if direction == 0 else HG_TILES_PER_CTX - 1
        first_lat = 0 if direction == 0 else HG_TILES_PER_LAT - 1
        is_start = jnp.where(tile < HG_CTX_TILES,
                             tile % HG_TILES_PER_CTX == first_ctx,
                             (tile - HG_CTX_TILES) % HG_TILES_PER_LAT == first_lat)

        @pl.when(is_start)
        def _(direction=direction, h0_ref=h0_ref):
            for p in range(HG_HEADS // 2):
                st_sc[direction, p] = _pair_diag(
                    jnp.concatenate([h0_ref[0, 0, 2 * p], h0_ref[0, 0, 2 * p + 1]], axis=1))

    for c in range(HG_TM // HG_C):
        fwd_rows = slice(c * HG_C, (c + 1) * HG_C)
        bwd_rows = slice(HG_TM - (c + 1) * HG_C, HG_TM - c * HG_C)
        _hg_chunk(0, fwd_rows, qf_ref, ff_ref, vf_ref, lb[0:1], ae_ref, mask_ref, pd_ref, of_ref, st_sc)
        _hg_chunk(1, bwd_rows, qb_ref, fb_ref, vb_ref, lb[1:2], ae_ref, mask_ref, pd_ref, ob_ref, st_sc)

    for direction, fin_ref in ((0, finf_ref), (1, finb_ref)):
        @pl.when(_hg_tile(direction, i) < HG_CTX_TILES)
        def _(direction=direction, fin_ref=fin_ref):
            for h in range(HG_HEADS):
                p, j = divmod(h, 2)
                fin_ref[0, h] = st_sc[direction, p, j * HG_DK:(j + 1) * HG_DK, j * HG_DK:(j + 1) * HG_DK]


def _hgrn(proj, fz, hg_lb, ae, mask, pair_diag, h0t, layer):
    def tok_spec(direction, col):
        return pl.BlockSpec((HG_TM, D_BRANCH), lambda i: (_hg_tile(direction, i), col))

    def fin_spec(direction):
        return pl.BlockSpec((1, HG_HEADS, HG_DK, HG_DK),
                            lambda i: (jnp.minimum(_hg_tile(direction, i) // HG_TILES_PER_CTX, N_CTX_SEQ - 1), 0, 0, 0))

    def h0_spec(direction):
        return pl.BlockSpec((1, 1, HG_HEADS, HG_DK, HG_DK),
                            lambda i: (_cond_row(_hg_tile(direction, i), HG_TM), direction, 0, 0, 0))

    fin_shape = jax.ShapeDtypeStruct((N_CTX_SEQ, HG_HEADS, HG_DK, HG_DK), F32)
    return pl.pallas_call(
        functools.partial(_hg_kernel, layer=layer),
        out_shape=(jax.ShapeDtypeStruct((N_TOK, D_BRANCH), F32), jax.ShapeDtypeStruct((N_TOK, D_BRANCH), F32),
                   fin_shape, fin_shape),
        grid=(HG_TILES,),
        in_specs=[
            tok_spec(0, HG_Q_COL), tok_spec(0, 0), tok_spec(0, HG_V_COL),
            tok_spec(1, HG_Q_COL), tok_spec(1, 1), tok_spec(1, HG_V_COL),
            pl.BlockSpec((DEPTH, 2, 1, D_BRANCH), lambda i: (0, 0, 0, 0)),
            pl.BlockSpec((2, HG_E_ROWS, 2 * HG_C), lambda i: (0, 0, 0)),
            pl.BlockSpec((2, HG_LEVELS, HG_C, HG_PAIR), lambda i: (0, 0, 0, 0)),
            pl.BlockSpec((HG_PAIR, HG_PAIR), lambda i: (0, 0)),
            h0_spec(0), h0_spec(1),
        ],
        out_specs=(
            pl.BlockSpec((HG_TM, D_BRANCH), lambda i: (_hg_tile(0, i), 0)),
            pl.BlockSpec((HG_TM, D_BRANCH), lambda i: (_hg_tile(1, i), 0)),
            fin_spec(0), fin_spec(1),
        ),
        scratch_shapes=[pltpu.VMEM((2, HG_HEADS // 2, HG_PAIR, HG_PAIR), F32)],
        compiler_params=_cparams(("arbitrary",)),
        name="hgrn",
    )(proj, fz, proj, proj, fz, proj, hg_lb.reshape(DEPTH, 2, 1, D_BRANCH), ae, mask, pair_diag, h0t, h0t)


MRG_TM = 256


def _gelu_tanh(x):
    return 0.5 * x * (1.0 + jnp.tanh(0.7978845608028654 * (x + 0.044715 * (x * x * x))))


def _merge_kernel(scb_ref, scc_ref, sch_ref, hgg_ref, pool_ref, g0_ref, g1_ref, g2_ref, g3_ref,
                  yrow_ref, of_ref, ob_ref, x_ref, mod_ref, conv_ref, gluw_ref, glub_ref, ng_ref,
                  poolw_ref, pscale_ref, wbr_ref, wout_ref, lng_ref, lnb_ref, o_ref,
                  ys5_sc, gluw_sc, poolw_sc, wbr_sc, wout_sc):
    tile = pl.program_id(0)

    @pl.when(tile == 0)
    def _():
        gluw_sc[...] = gluw_ref[0].astype(BF16)
        poolw_sc[...] = poolw_ref[0].astype(BF16)
        wbr_sc[...] = wbr_ref[0].astype(BF16)
        wout_sc[...] = wout_ref[0].astype(BF16)

    line = jnp.where(tile * MRG_TM < N_CTX_TOK, CTX_LEN, GRID_W)
    pos = lax.broadcasted_iota(jnp.int32, (MRG_TM, 1), 0) & (line - 1)

    def shifted(val, k):
        rolled = pltpu.roll(val, k % MRG_TM, axis=0)
        ok = (pos >= k) if k > 0 else (pos < line + k)
        return jnp.where(ok, rolled, 0.0)

    m = scc_ref[...].astype(F32) * sch_ref[...].astype(F32)
    conv = conv_ref[0, 0:1, :] * shifted(m, 1) + conv_ref[0, 1:2, :] * m + conv_ref[0, 2:3, :] * shifted(m, -1)
    ya = scb_ref[...].astype(F32) * conv

    for b in range(S5_BLK):
        for t in range(S5_T):
            ys5_sc[b, pl.ds(t, MRG_TM // S5_T, stride=S5_T), :] = yrow_ref[b, :, t * LANES:(t + 1) * LANES]
    z = _gelu_tanh(jnp.concatenate([ys5_sc[b] for b in range(S5_BLK)], axis=-1))
    yb = z * _sigmoid_tanh(_dot(z.astype(BF16), gluw_sc[...]) + glub_ref[0])

    o = of_ref[...] + ob_ref[...]
    normed = []
    for h in range(HG_HEADS):
        oh = o[:, h * HG_DK:(h + 1) * HG_DK]
        ms = jnp.mean(oh * oh, axis=-1, keepdims=True)
        normed.append(oh * lax.rsqrt(ms + LN_EPS) * ng_ref[0])
    gate_c = hgg_ref[...].astype(F32)
    yc = jnp.concatenate(normed, axis=-1) * (gate_c * _sigmoid_tanh(gate_c))

    pu = pool_ref[...].astype(F32)
    posf = pos.astype(F32)
    linef = line.astype(F32)
    pooled = []
    for gi, w in enumerate(POOL_WINDOWS):
        vg = pu[:, gi * POOL_GROUP:(gi + 1) * POOL_GROUP]
        back, fwd, span = vg, vg, 1
        while 2 * span <= w // 2:
            back = back + shifted(back, span)
            fwd = fwd + shifted(fwd, -span)
            span *= 2
        s = shifted(back, 1) + fwd
        cnt = jnp.minimum(posf + w // 2, linef) - jnp.maximum(posf - w // 2, 0.0)
        pg = s / cnt - vg
        pooled.append(_dot(pg.astype(BF16), poolw_sc[gi]))
    yd = jnp.concatenate(pooled, axis=-1) * pscale_ref[0]

    merged = jnp.zeros((MRG_TM, D_MODEL), F32)
    for k, (br, gate_ref) in enumerate(((ya, g0_ref), (yb, g1_ref), (yc, g2_ref), (yd, g3_ref))):
        merged = merged + gate_ref[...].astype(F32) * _dot(br.astype(BF16), wbr_sc[k])
    mix = _dot(merged.astype(BF16), wout_sc[...])
    y = ALPHA * x_ref[...] + mod_ref[0, 0, 5:6, :] * mix
    o_ref[...] = _layer_norm(y, lng_ref[0, 0], lnb_ref[0, 0])


def _merge(proj, y_rows, o_f, o_b, x, mod, conv_w, glu_w, glu_b, norm_g, pool_w, pool_scale, w_branch, w_out,
           ln_g, ln_b, layer):
    def part(col):
        return pl.BlockSpec((MRG_TM, D_BRANCH), lambda i: (i, col))

    def gate(k):
        return pl.BlockSpec((MRG_TM, D_MODEL), lambda i: (i, N_IN_PARTS * D_BRANCH // D_MODEL + k))

    def per_layer(*shape, single=False):
        mode = dict(pipeline_mode=pl.Buffered(1)) if single else {}
        return pl.BlockSpec((1,) + shape, lambda i: (layer,) + (0,) * len(shape), **mode)

    return pl.pallas_call(
        _merge_kernel,
        out_shape=jax.ShapeDtypeStruct((N_TOK, D_MODEL), F32),
        grid=(N_TOK // MRG_TM,),
        in_specs=[
            part(0), part(1), part(2), part(8), part(9), gate(0), gate(1), gate(2), gate(3),
            pl.BlockSpec((S5_BLK, MRG_TM // S5_T, S5_CW), lambda i: (0, i, 0)),
            pl.BlockSpec((MRG_TM, D_BRANCH), lambda i: (i, 0)),
            pl.BlockSpec((MRG_TM, D_BRANCH), lambda i: (i, 0)),
            pl.BlockSpec((MRG_TM, D_MODEL), lambda i: (i, 0)),
            _mod_spec(layer, MRG_TM),
            per_layer(3, D_BRANCH), per_layer(D_BRANCH, D_BRANCH, single=True), per_layer(1, D_BRANCH),
            per_layer(1, HG_DK), per_layer(len(POOL_WINDOWS), POOL_GROUP, POOL_GROUP, single=True),
            per_layer(1, D_BRANCH), per_layer(N_BRANCH, D_BRANCH, D_MODEL, single=True),
            per_layer(D_MODEL, D_MODEL, single=True), _ln_spec(layer, 1), _ln_spec(layer, 1),
        ],
        out_specs=pl.BlockSpec((MRG_TM, D_MODEL), lambda i: (i, 0)),
        scratch_shapes=[
            pltpu.VMEM((S5_BLK, MRG_TM, LANES), F32),
            pltpu.VMEM((D_BRANCH, D_BRANCH), BF16),
            pltpu.VMEM((len(POOL_WINDOWS), POOL_GROUP, POOL_GROUP), BF16),
            pltpu.VMEM((N_BRANCH, D_BRANCH, D_MODEL), BF16),
            pltpu.VMEM((D_MODEL, D_MODEL), BF16),
        ],
        compiler_params=_cparams(("arbitrary",)),
        name="merge",
    )(proj, proj, proj, proj, proj, proj, proj, proj, proj, y_rows, o_f, o_b, x, mod,
      conv_w, glu_w, glu_b.reshape(DEPTH, 1, D_BRANCH), norm_g.reshape(DEPTH, 1, HG_DK), pool_w,
      pool_scale.reshape(DEPTH, 1, D_BRANCH), w_branch, w_out, ln_g, ln_b)


def _grid_pos_embedding():
    rows = LAT_LEN // GRID_W
    quarter = D_MODEL // 4
    omega = POS_BASE ** (-jnp.arange(quarter, dtype=F32) / quarter)
    ar = jnp.arange(rows, dtype=F32)[:, None] * omega
    ac = jnp.arange(GRID_W, dtype=F32)[:, None] * omega
    row_tab = jnp.concatenate([jnp.sin(ar), jnp.cos(ar)], -1)[:, None, :]
    col_tab = jnp.concatenate([jnp.sin(ac), jnp.cos(ac)], -1)[None, :, :]
    shape = (rows, GRID_W, 2 * quarter)
    pos = jnp.concatenate([jnp.broadcast_to(row_tab, shape), jnp.broadcast_to(col_tab, shape)], -1)
    return pos.reshape(LAT_LEN, D_MODEL)


def kernel(x_prompt, x_sample, state_s5_re, state_s5_im, state_hgrn, c, c_ctx, w_ada, b_ada, ln_g, ln_b,
           ffn_w1, ffn_w3, ffn_w2, w_in, sc_conv, s5_lam_re, s5_lam_im, s5_log_dt, s5_b_re, s5_b_im,
           s5_c_re, s5_c_im, s5_d, s5_glu_w, s5_glu_b, hg_lb, hg_norm_g, pool_w, pool_scale, w_branch, w_out):
    x = (x_prompt.reshape(N_CTX_TOK, D_MODEL), x_sample.reshape(N_LAT_TOK, D_MODEL), _grid_pos_embedding())

    cond = jnp.zeros((COND_PAD, D_MODEL), F32).at[0].set(c_ctx).at[1:N_COND].set(c)
    mod = _ada(cond, w_ada, b_ada).reshape(DEPTH, COND_PAD, N_SUB * 3, D_MODEL)
    ln_g4 = ln_g.reshape(DEPTH, N_SUB, 1, D_MODEL)
    ln_b4 = ln_b.reshape(DEPTH, N_SUB, 1, D_MODEL)

    s5_m, s5_win, s5_wout, s5_pow = _s5_tables(*_s5_param_layout(
        s5_lam_re, s5_lam_im, s5_log_dt, s5_b_re, s5_b_im, s5_c_re, s5_c_im, s5_d))

    w_in_bf = _cast_w_in(w_in)

    ae_np, mask_np, pair_np = _hg_constants()
    ae = jnp.asarray(ae_np, BF16)
    mask = jnp.asarray(mask_np, F32)
    pair_diag = jnp.asarray(pair_np, F32)

    fin_s5, fin_hg = [], []
    for l in range(DEPTH):
        x, h = _ffn(x, mod, ffn_w1, ffn_w3, ffn_w2, ln_g4, ln_b4, l, 0)
        proj, fz, u_rows = _inproj(h, w_in_bf, l)

        h0 = jnp.concatenate([
            state_s5_re[:, l].reshape(N_LAT_SEQ, 2, S5_BLK, S5_SW),
            state_s5_im[:, l].reshape(N_LAT_SEQ, 2, S5_BLK, S5_SW)], axis=-1).transpose(1, 2, 0, 3)
        y_rows, fin = _s5_main(u_rows, s5_m, s5_win, s5_wout, s5_pow, h0, l)
        fin_s5.append(fin)

        h0t = jnp.concatenate([jnp.zeros((1, 2, HG_HEADS, HG_DK, HG_DK), F32),
                               jnp.swapaxes(state_hgrn[:, l], -1, -2)], axis=0)
        o_f, o_b, fin_f, fin_b = _hgrn(proj, fz, hg_lb, ae, mask, pair_diag, h0t, l)
        fin_hg.append(jnp.stack([fin_f, fin_b], axis=1))

        x = _merge(proj, y_rows, o_f, o_b, x, mod, sc_conv, s5_glu_w, s5_glu_b, hg_norm_g,
                   pool_w, pool_scale, w_branch, w_out, ln_g4, ln_b4, l)
        x = _ffn(x, mod, ffn_w1, ffn_w3, ffn_w2, ln_g4, ln_b4, l, 2, split_output=(l == DEPTH - 1))

    y_prompt = x[0].reshape(N_CTX_SEQ, CTX_LEN, D_MODEL)
    y_sample = x[1].reshape(N_LAT_SEQ, LAT_LEN, D_MODEL)
    fin = jnp.stack(fin_s5)
    fin = fin.reshape(DEPTH, 2, S5_BLK, N_CTX_SEQ, 2, S5_GPB, S5_STATE)
    fin = fin.transpose(4, 3, 0, 1, 2, 5, 6).reshape(2, N_CTX_SEQ, DEPTH, 2, S5_GROUPS, S5_STATE)
    new_hgrn = jnp.swapaxes(jnp.stack(fin_hg, axis=1), -1, -2)
    return y_prompt, y_sample, fin[0], fin[1], new_hgrn
```

```python
import functools

import numpy as np
import jax
import jax.numpy as jnp
from jax import lax
from jax.experimental import pallas as pl
from jax.experimental.pallas import tpu as pltpu

F32 = jnp.float32
BF16 = jnp.bfloat16

D_MODEL = 1024
N_CTX_SEQ = 16
CTX_LEN = 256
DEPTH = 2
N_LAT_SEQ = 2
LAT_LEN = 4096
GRID_W = 64
D_BRANCH = 512
N_BRANCH = 4
S5_GROUPS = 32
S5_GROUP = 16
S5_STATE = 64
HG_HEADS = 4
HG_DK = 128
POOL_WINDOWS = (2, 4, 8, 16)
POOL_GROUP = 128
D_FF = 2816
N_SUB = 3
N_IN_PARTS = 10
IN_WIDTH = N_IN_PARTS * D_BRANCH + N_BRANCH * D_MODEL
ADA_WIDTH = N_SUB * 3 * D_MODEL
ALPHA = (2 * DEPTH) ** 0.25
LN_EPS = 1e-5
POS_BASE = 10000.0

N_CTX_TOK = N_CTX_SEQ * CTX_LEN
N_LAT_TOK = N_LAT_SEQ * LAT_LEN
N_TOK = N_CTX_TOK + N_LAT_TOK
N_COND = 1 + N_LAT_SEQ
COND_PAD = 8

LANES = 128
SUBLANES = 8
VMEM_LIMIT = 56 * 1024 * 1024

S5_T = 8
S5_BLK = 4
S5_GPB = S5_GROUPS // S5_BLK
S5_SW = S5_GPB * S5_STATE
S5_CW = S5_T * LANES
CTX_CHUNKS = CTX_LEN // S5_T
LAT_CHUNKS = LAT_LEN // S5_T
S5_ROWS_CTX = N_CTX_TOK // S5_T
S5_ROWS_LAT = N_LAT_TOK // S5_T
S5_ROWS = S5_ROWS_CTX + S5_ROWS_LAT
S5_LAT_GROUPS = LAT_CHUNKS // SUBLANES
S5_POW_ROWS = 3 * SUBLANES
S5_U_COL = 3

HG_C = 128
HG_LEVELS = 7
HG_TM = 256
HG_TILES = N_TOK // HG_TM
HG_CTX_TILES = N_CTX_TOK // HG_TM
HG_TILES_PER_CTX = CTX_LEN // HG_TM
HG_TILES_PER_LAT = LAT_LEN // HG_TM
HG_E_ROWS = (HG_LEVELS + 1) * HG_C + SUBLANES


def _cparams(sem):
    return pltpu.CompilerParams(dimension_semantics=sem, vmem_limit_bytes=VMEM_LIMIT)


def _cond_row(tile, tile_tokens):
    tok = tile * tile_tokens
    return jnp.where(tok < N_CTX_TOK, 0, 1 + (tok - N_CTX_TOK) // LAT_LEN)


def _dot(a, b):
    return jnp.dot(a, b, preferred_element_type=F32)


def _dot_nt(a, b):
    return lax.dot_general(a, b, (((1,), (1,)), ((), ())), preferred_element_type=F32)


def _dot_tn(a, b):
    return lax.dot_general(a, b, (((0,), (0,)), ((), ())), preferred_element_type=F32)


def _split3(x):
    h1 = x.astype(BF16)
    r1 = x - h1.astype(F32)
    h2 = r1.astype(BF16)
    h3 = (r1 - h2.astype(F32)).astype(BF16)
    return h1, h2, h3


def _dot_nt_hi(a3, b3):
    acc = None
    for x in range(2):
        for y in range(2 - x):
            term = _dot_nt(a3[x], b3[y])
            acc = term if acc is None else acc + term
    return acc


def _silu(x):
    return x * jax.nn.sigmoid(x)


def _sigmoid_tanh(x):
    return 0.5 * jnp.tanh(0.5 * x) + 0.5


def _layer_norm(y, g, b):
    mu = jnp.mean(y, axis=-1, keepdims=True)
    yc = y - mu
    var = jnp.mean(yc * yc, axis=-1, keepdims=True)
    return yc * lax.rsqrt(var + LN_EPS) * g + b


def _mod_spec(layer, tile_tokens):
    return pl.BlockSpec((1, 1, N_SUB * 3, D_MODEL),
                        lambda i, *_: (layer, _cond_row(i, tile_tokens), 0, 0))


def _ln_spec(layer, sub):
    return pl.BlockSpec((1, 1, 1, D_MODEL), lambda *_: (layer, sub, 0, 0))


ADA_TN = 1152


def _ada_kernel(c_ref, w_ref, b_ref, o_ref):
    s = _silu(c_ref[...]).astype(BF16)
    o_ref[0] = _dot(s, w_ref[0].astype(BF16)) + b_ref[0]


def _ada(cond, w_ada, b_ada):
    return pl.pallas_call(
        _ada_kernel,
        out_shape=jax.ShapeDtypeStruct((DEPTH, COND_PAD, ADA_WIDTH), F32),
        grid=(DEPTH, ADA_WIDTH // ADA_TN),
        in_specs=[
            pl.BlockSpec((COND_PAD, D_MODEL), lambda l, j: (0, 0)),
            pl.BlockSpec((1, D_MODEL, ADA_TN), lambda l, j: (l, 0, j)),
            pl.BlockSpec((1, 1, ADA_TN), lambda l, j: (l, 0, j)),
        ],
        out_specs=pl.BlockSpec((1, COND_PAD, ADA_TN), lambda l, j: (l, 0, j)),
        compiler_params=_cparams(("parallel", "parallel")),
        name="ada",
    )(cond, w_ada, b_ada.reshape(DEPTH, 1, ADA_WIDTH))


FFN_TM = 1024
FFN_TM_SPLIT = 512
FFN_TC = 256
FFN_CHUNKS = D_FF // FFN_TC


def _ffn_kernel(*refs, sub, tm, split_input, split_output, emit_next):
    step = pl.program_id(0)
    tile = step - FFN_CHUNKS
    is_ctx = tile < N_CTX_TOK // tm
    n_x = 3 if split_input else 1
    x_refs, refs = refs[:n_x], refs[n_x:]
    mod_ref, w1_ref, w3_ref, w2_ref, g_ref, b_ref = refs[:6]
    out_refs = refs[6:-4]
    w1_sc, w3_sc, w2_sc, act_sc = refs[-4:]

    @pl.when(step < FFN_CHUNKS)
    def _():
        w1_sc[step] = w1_ref[0, 0].astype(BF16)
        w3_sc[step] = w3_ref[0, 0].astype(BF16)
        w2_sc[pl.ds(pl.multiple_of(step * FFN_TC, FFN_TC), FFN_TC), :] = w2_ref[0, 0].astype(BF16)

    @pl.when(step >= FFN_CHUNKS)
    def _():
        if split_input:
            ctx_ref, lat_ref, pos_ref = x_refs
            x = jnp.where(is_ctx, ctx_ref[...], lat_ref[...] + pos_ref[...])
        else:
            x = x_refs[0][...]
        shift = mod_ref[0, 0, 3 * sub:3 * sub + 1, :]
        scale = mod_ref[0, 0, 3 * sub + 1:3 * sub + 2, :]
        gate = mod_ref[0, 0, 3 * sub + 2:3 * sub + 3, :]
        h = (x * (1.0 + scale) + shift).astype(BF16)
        for c in range(FFN_CHUNKS):
            a = _dot(h, w1_sc[c])
            b = _dot(h, w3_sc[c])
            act_sc[:, c * FFN_TC:(c + 1) * FFN_TC] = (_silu(a) * b).astype(BF16)
        f = _dot(act_sc[...], w2_sc[...])
        y = _layer_norm(ALPHA * x + gate * (0.5 * f), g_ref[0, 0], b_ref[0, 0])
        if emit_next:
            nxt = 3 * (sub + 1)
            out_refs[1][...] = (y * (1.0 + mod_ref[0, 0, nxt + 1:nxt + 2, :])
                                + mod_ref[0, 0, nxt:nxt + 1, :]).astype(BF16)
        if split_output:
            @pl.when(is_ctx)
            def _():
                out_refs[0][...] = y

            @pl.when(jnp.logical_not(is_ctx))
            def _():
                out_refs[1][...] = y
        else:
            out_refs[0][...] = y


def _ffn(x, mod, w1, w3, w2, ln_g, ln_b, layer, sub, split_output=False):
    which = sub // 2
    split_input = isinstance(x, tuple)
    tm = FFN_TM_SPLIT if split_input else FFN_TM
    ctx_tiles = N_CTX_TOK // tm

    def chunk(s):
        return jnp.minimum(s, FFN_CHUNKS - 1)

    def tile(s):
        return jnp.maximum(s - FFN_CHUNKS, 0)

    def tok_spec(index):
        return pl.BlockSpec((tm, D_MODEL), lambda s: (index(tile(s)), 0))

    def ctx_tile(t):
        return jnp.minimum(t, ctx_tiles - 1)

    def lat_tile(t):
        return jnp.maximum(t - ctx_tiles, 0)

    if split_input:
        pos_tiles = LAT_LEN // tm
        x_specs = [tok_spec(ctx_tile), tok_spec(lat_tile), tok_spec(lambda t: lat_tile(t) % pos_tiles)]
    else:
        x_specs = [tok_spec(lambda t: t)]
        x = (x,)
    emit_next = sub == 0
    if split_output:
        out_shape = (jax.ShapeDtypeStruct((N_CTX_TOK, D_MODEL), F32), jax.ShapeDtypeStruct((N_LAT_TOK, D_MODEL), F32))
        out_specs = (tok_spec(ctx_tile), tok_spec(lat_tile))
    elif emit_next:
        out_shape = (jax.ShapeDtypeStruct((N_TOK, D_MODEL), F32), jax.ShapeDtypeStruct((N_TOK, D_MODEL), BF16))
        out_specs = (tok_spec(lambda t: t), tok_spec(lambda t: t))
    else:
        out_shape = jax.ShapeDtypeStruct((N_TOK, D_MODEL), F32)
        out_specs = tok_spec(lambda t: t)

    return pl.pallas_call(
        functools.partial(_ffn_kernel, sub=sub, tm=tm, split_input=split_input, split_output=split_output,
                          emit_next=emit_next),
        out_shape=out_shape,
        grid=(FFN_CHUNKS + N_TOK // tm,),
        in_specs=x_specs + [
            pl.BlockSpec((1, 1, N_SUB * 3, D_MODEL), lambda s: (layer, _cond_row(tile(s), tm), 0, 0)),
            pl.BlockSpec((1, 1, D_MODEL, FFN_TC), lambda s: (layer, which, 0, chunk(s))),
            pl.BlockSpec((1, 1, D_MODEL, FFN_TC), lambda s: (layer, which, 0, chunk(s))),
            pl.BlockSpec((1, 1, FFN_TC, D_MODEL), lambda s: (layer, which, chunk(s), 0)),
            _ln_spec(layer, sub), _ln_spec(layer, sub),
        ],
        out_specs=out_specs,
        scratch_shapes=[pltpu.VMEM((FFN_CHUNKS, D_MODEL, FFN_TC), BF16), pltpu.VMEM((FFN_CHUNKS, D_MODEL, FFN_TC), BF16),
                        pltpu.VMEM((D_FF, D_MODEL), BF16), pltpu.VMEM((tm, D_FF), BF16)],
        compiler_params=_cparams(("arbitrary",)),
        name=f"ffn{sub}",
    )(*x, mod, w1, w3, w2, ln_g, ln_b)


INP_TM = 2048
INP_PARTS = 2
INP_TN = INP_PARTS * D_BRANCH
INP_STEPS = IN_WIDTH // INP_TN
HG_Q_COL, HG_F_COL, HG_V_COL = 4, 5, 7


def _cast_w_in_kernel(w_ref, o_ref):
    o_ref[0, 0] = w_ref[0].astype(BF16)


def _cast_w_in(w_in):
    return pl.pallas_call(
        _cast_w_in_kernel,
        out_shape=jax.ShapeDtypeStruct((DEPTH, INP_STEPS, D_MODEL, INP_TN), BF16),
        grid=(DEPTH, INP_STEPS),
        in_specs=[pl.BlockSpec((1, D_MODEL, INP_TN), lambda l, j: (l, 0, j))],
        out_specs=pl.BlockSpec((1, 1, D_MODEL, INP_TN), lambda l, j: (l, j, 0, 0)),
        compiler_params=_cparams(("parallel", "parallel")),
        name="cast_w_in",
    )(w_in)


def _inproj_kernel(h_ref, w_ref, o_ref, fz_ref, u_ref, blk_sc):
    j = pl.program_id(1)

    def project(gate=False):
        res = _dot(h_ref[...], w_ref[0, 0])
        o_ref[...] = (_sigmoid_tanh(res) if gate else res).astype(BF16)
        return res

    def part(res, p):
        k = p % INP_PARTS
        return res[:, k * D_BRANCH:(k + 1) * D_BRANCH]

    special = (S5_U_COL, HG_F_COL, HG_F_COL + 1)
    steps = [p // INP_PARTS for p in special]
    assert len(set(steps)) == len(steps)

    first_gate = N_IN_PARTS // INP_PARTS
    assert N_IN_PARTS % INP_PARTS == 0 and max(steps) < first_gate

    @pl.when(functools.reduce(jnp.logical_and, [j != s for s in steps] + [j < first_gate]))
    def _():
        project()

    @pl.when(j >= first_gate)
    def _():
        project(gate=True)

    for p in (HG_F_COL, HG_F_COL + 1):
        @pl.when(j == p // INP_PARTS)
        def _(p=p):
            fz_ref[...] = part(project(), p)

    @pl.when(j == S5_U_COL // INP_PARTS)
    def _():
        s5_u = part(project(), S5_U_COL)
        for b in range(S5_BLK):
            blk_sc[b] = s5_u[:, b * LANES:(b + 1) * LANES]
            for t in range(S5_T):
                rows = blk_sc[b, pl.ds(t, INP_TM // S5_T, stride=S5_T), :]
                u_ref[b, :, t * LANES:(t + 1) * LANES] = rows.astype(BF16)


def _inproj(h, w_in_bf, layer):
    return pl.pallas_call(
        _inproj_kernel,
        out_shape=(jax.ShapeDtypeStruct((N_TOK, IN_WIDTH), BF16),
                   jax.ShapeDtypeStruct((N_TOK, 2 * D_BRANCH), F32),
                   jax.ShapeDtypeStruct((S5_BLK, S5_ROWS, S5_CW), BF16)),
        grid=(N_TOK // INP_TM, INP_STEPS),
        in_specs=[
            pl.BlockSpec((INP_TM, D_MODEL), lambda i, j: (i, 0)),
            pl.BlockSpec((1, 1, D_MODEL, INP_TN), lambda i, j: (layer, j, 0, 0)),
        ],
        out_specs=(pl.BlockSpec((INP_TM, INP_TN), lambda i, j: (i, j)),
                   pl.BlockSpec((INP_TM, D_BRANCH),
                                lambda i, j: (i, jnp.where(j > HG_F_COL // INP_PARTS, 1, 0))),
                   pl.BlockSpec((S5_BLK, INP_TM // S5_T, S5_CW), lambda i, j: (0, i, 0))),
        scratch_shapes=[pltpu.VMEM((S5_BLK, INP_TM, LANES), F32)],
        compiler_params=_cparams(("parallel", "arbitrary")),
        name="inproj",
    )(h, w_in_bf)


def _s5_tab_kernel(lam_ref, bre_ref, bim_ref, cre_ref, cim_ref, dsk_ref, m_ref, win_ref, wout_ref, pow_ref):
    d = pl.program_id(1)
    fwd = d == 0
    lam_re = lam_ref[0, 0, 0, 0:1, :]
    lam_im = lam_ref[0, 0, 0, 1:2, :]
    dt = jnp.exp(lam_ref[0, 0, 0, 2:3, :])

    def apow(j):
        mag = jnp.exp(lam_re * dt * float(j))
        ang = lam_im * dt * float(j)
        return mag * jnp.cos(ang), mag * jnp.sin(ang)

    a_re, a_im = apow(1)
    den = lam_re * lam_re + lam_im * lam_im
    num_re = a_re - 1.0
    coef_re = (num_re * lam_re + a_im * lam_im) / den
    coef_im = (a_im * lam_re - num_re * lam_im) / den
    ri = lax.broadcasted_iota(jnp.int32, (S5_STATE, S5_SW), 0)
    ci = lax.broadcasted_iota(jnp.int32, (S5_STATE, S5_SW), 1)
    repeat = jnp.where(ri == (ci & (S5_STATE - 1)), 1.0, 0.0).astype(BF16)
    own = (jnp.right_shift(lax.broadcasted_iota(jnp.int32, (LANES, S5_SW), 0), S5_GROUP.bit_length() - 1)
           == jnp.right_shift(lax.broadcasted_iota(jnp.int32, (LANES, S5_SW), 1), S5_STATE.bit_length() - 1))

    def block_diag(ref):
        return jnp.where(own, sum(_dot(piece, repeat) for piece in _split3(ref[0, 0, 0])), 0.0)

    b_re = block_diag(bre_ref)
    b_im = block_diag(bim_ref)
    bb_re = coef_re * b_re - coef_im * b_im
    bb_im = coef_re * b_im + coef_im * b_re
    bbr = _split3(bb_re)
    bbi = _split3(bb_im)
    c_re = block_diag(cre_ref)
    c_im = block_diag(cim_ref)

    kmat = []
    for j in range(S5_T + 1):
        aj_re, aj_im = apow(j)
        pc_re = c_re * aj_re - c_im * aj_im
        pc_im = c_re * aj_im + c_im * aj_re
        if j >= 1:
            r = pl.multiple_of(jnp.where(fwd, j - 1, S5_T - j) * LANES, LANES)
            wout_ref[0, 0, 0, pl.ds(r, LANES), :] = jnp.concatenate([pc_re, -pc_im], axis=1).astype(BF16)
        if j < S5_T:
            pb_re = bb_re * aj_re - bb_im * aj_im
            pb_im = bb_re * aj_im + bb_im * aj_re
            r = pl.multiple_of(jnp.where(fwd, S5_T - 1 - j, j) * LANES, LANES)
            win_ref[0, 0, 0, pl.ds(r, LANES), :] = jnp.concatenate([pb_re, pb_im], axis=1).astype(BF16)
            kmat.append(_dot_nt_hi(bbr, _split3(pc_re)) - _dot_nt_hi(bbi, _split3(pc_im)))

    ri = lax.broadcasted_iota(jnp.int32, (LANES, LANES), 0)
    ci = lax.broadcasted_iota(jnp.int32, (LANES, LANES), 1)
    keep_upper = jnp.where(fwd, 1.0, 0.0)
    kmat[0] = kmat[0] + jnp.where(ri == ci, dsk_ref[0, 0], 0.0) * keep_upper
    for s in range(S5_T):
        for t in range(S5_T):
            tile = kmat[abs(t - s)]
            if t > s:
                tile = tile * keep_upper
            elif t < s:
                tile = tile * (1.0 - keep_upper)
            m_ref[0, 0, 0, s * LANES:(s + 1) * LANES, t * LANES:(t + 1) * LANES] = tile.astype(BF16)

    pow_ref[...] = jnp.zeros_like(pow_ref)
    for i in range(SUBLANES):
        for row, j in ((i, S5_T * (i + 1)), (SUBLANES + i, S5_T * (SUBLANES - i))):
            p_re, p_im = apow(j)
            pow_ref[0, 0, 0, 0, row:row + 1, :] = p_re
            pow_ref[0, 0, 0, 1, row:row + 1, :] = p_im
    p_re, p_im = apow(S5_T * 2 * SUBLANES)
    pow_ref[0, 0, 0, 0, 2 * SUBLANES:2 * SUBLANES + 1, :] = p_re
    pow_ref[0, 0, 0, 1, 2 * SUBLANES:2 * SUBLANES + 1, :] = p_im


def _s5_tables(lam, b_re, b_im, c_re, c_im, d_skip):
    lead = (DEPTH, 2, S5_BLK)

    def spec(*tail):
        return pl.BlockSpec((1, 1, 1) + tail, lambda l, d, b: (l, d, b) + (0,) * len(tail))

    return pl.pallas_call(
        _s5_tab_kernel,
        out_shape=(
            jax.ShapeDtypeStruct(lead + (S5_CW, S5_CW), BF16),
            jax.ShapeDtypeStruct(lead + (S5_CW, 2 * S5_SW), BF16),
            jax.ShapeDtypeStruct(lead + (S5_CW, 2 * S5_SW), BF16),
            jax.ShapeDtypeStruct(lead + (2, S5_POW_ROWS, S5_SW), F32),
        ),
        grid=lead,
        in_specs=[spec(3, S5_SW)] + [spec(LANES, S5_STATE)] * 4
        + [pl.BlockSpec((1, 1, 1, LANES), lambda l, d, b: (l, b, 0, 0))],
        out_specs=(spec(S5_CW, S5_CW), spec(S5_CW, 2 * S5_SW), spec(S5_CW, 2 * S5_SW),
                   spec(2, S5_POW_ROWS, S5_SW)),
        compiler_params=_cparams(("parallel", "parallel", "parallel")),
        name="s5_tables",
    )(lam, b_re, b_im, c_re, c_im, d_skip)


def _s5_param_layout(s5_lam_re, s5_lam_im, s5_log_dt, s5_b_re, s5_b_im, s5_c_re, s5_c_im, s5_d):
    lam = jnp.stack([s5_lam_re, s5_lam_im, jnp.broadcast_to(s5_log_dt[..., None], s5_lam_re.shape)], axis=2)
    lam = lam.reshape(DEPTH, 2, 3, S5_BLK, S5_SW).transpose(0, 1, 3, 2, 4)

    def rows_b(b):
        return jnp.swapaxes(b, -1, -2).reshape(DEPTH, 2, S5_BLK, LANES, S5_STATE)

    def rows_c(c):
        return c.reshape(DEPTH, 2, S5_BLK, LANES, S5_STATE)

    return (lam, rows_b(s5_b_re), rows_b(s5_b_im), rows_c(s5_c_re), rows_c(s5_c_im),
            s5_d.reshape(DEPTH, S5_BLK, 1, LANES))


def _s5_kernel(u_ref, m_ref, win_ref, wout_ref, pow_ref, h0_ref, y_ref, fin_ref, x_sc):
    d = pl.program_id(1)
    u = u_ref[0]
    x_sc[...] = _dot(u, win_ref[0, 0, 0])
    p_re = pow_ref[0, 0, 0, 0]
    p_im = pow_ref[0, 0, 0, 1]
    re = slice(0, S5_SW)
    im = slice(S5_SW, 2 * S5_SW)

    def madd(t_re, t_im, w_re, w_im, s_re, s_im):
        return t_re + w_re * s_re - w_im * s_im, t_im + w_re * s_im + w_im * s_re

    def shift_rows(val, k, n, idx, reverse):
        if reverse:
            return jnp.where(idx < n - k, pltpu.roll(val, val.shape[0] - k, axis=0), 0.0)
        return jnp.where(idx >= k, pltpu.roll(val, k, axis=0), 0.0)

    def pow_row(k):
        row = k - 1 if k <= SUBLANES else 2 * SUBLANES
        return p_re[row:row + 1], p_im[row:row + 1]

    def ctx_scan(reverse):
        s_re = x_sc[0:S5_ROWS_CTX, re]
        s_im = x_sc[0:S5_ROWS_CTX, im]
        c = lax.broadcasted_iota(jnp.int32, (S5_ROWS_CTX, 1), 0) & (CTX_CHUNKS - 1)
        def shift_ctx(val, k):
            if k % SUBLANES:
                return shift_rows(val, k, CTX_CHUNKS, c, reverse)
            v3 = val.reshape(N_CTX_SEQ, CTX_CHUNKS, S5_SW)
            pad = jnp.zeros((N_CTX_SEQ, k, S5_SW), F32)
            moved = (jnp.concatenate([v3[:, k:], pad], axis=1) if reverse
                     else jnp.concatenate([pad, v3[:, :CTX_CHUNKS - k]], axis=1))
            return moved.reshape(S5_ROWS_CTX, S5_SW)

        k = 1
        while k < CTX_CHUNKS:
            w_re, w_im = pow_row(k)
            s_re, s_im = madd(s_re, s_im, w_re, w_im, shift_ctx(s_re, k), shift_ctx(s_im, k))
            k *= 2
        last = 0 if reverse else CTX_CHUNKS - 1
        pick = (lax.broadcasted_iota(jnp.int32, (N_CTX_SEQ, S5_ROWS_CTX), 1)
                == lax.broadcasted_iota(jnp.int32, (N_CTX_SEQ, S5_ROWS_CTX), 0) * CTX_CHUNKS + last)
        pick = jnp.where(pick, 1.0, 0.0).astype(BF16)
        fin_ref[0, 0, :, re] = sum(_dot(pick, piece) for piece in _split3(s_re))
        fin_ref[0, 0, :, im] = sum(_dot(pick, piece) for piece in _split3(s_im))
        x_sc[0:S5_ROWS_CTX, re] = shift_rows(s_re, 1, CTX_CHUNKS, c, reverse)
        x_sc[0:S5_ROWS_CTX, im] = shift_rows(s_im, 1, CTX_CHUNKS, c, reverse)

    def lat_group(k, carry, reverse):
        g = (S5_LAT_GROUPS - 1 - k) if reverse else k
        r8 = lax.broadcasted_iota(jnp.int32, (SUBLANES, 1), 0)
        tab = slice(SUBLANES, 2 * SUBLANES) if reverse else slice(0, SUBLANES)
        out = []
        for q in range(N_LAT_SEQ):
            base = pl.multiple_of(S5_ROWS_CTX + q * LAT_CHUNKS + g * SUBLANES, SUBLANES)
            s_re = x_sc[pl.ds(base, SUBLANES), re]
            s_im = x_sc[pl.ds(base, SUBLANES), im]
            j = 1
            while j < SUBLANES:
                w_re, w_im = pow_row(j)
                s_re, s_im = madd(s_re, s_im, w_re, w_im,
                                  shift_rows(s_re, j, SUBLANES, r8, reverse), shift_rows(s_im, j, SUBLANES, r8, reverse))
                j *= 2
            c_re, c_im = carry[2 * q], carry[2 * q + 1]
            s_re, s_im = madd(s_re, s_im, p_re[tab], p_im[tab], c_re, c_im)
            edge = (r8 == SUBLANES - 1) if reverse else (r8 == 0)
            x_sc[pl.ds(base, SUBLANES), re] = jnp.where(edge, c_re, shift_rows(s_re, 1, SUBLANES, r8, reverse))
            x_sc[pl.ds(base, SUBLANES), im] = jnp.where(edge, c_im, shift_rows(s_im, 1, SUBLANES, r8, reverse))
            end = slice(0, 1) if reverse else slice(SUBLANES - 1, SUBLANES)
            out += [s_re[end], s_im[end]]
        return tuple(out)

    def run(reverse):
        for t0 in range(0, S5_T, 2):
            src = slice(t0 * LANES, S5_CW) if reverse else slice(0, (t0 + 2) * LANES)
            dst = slice(t0 * LANES, (t0 + 2) * LANES)
            part = _dot(u[:, src], m_ref[0, 0, 0, src, dst])
            if reverse:
                y_ref[0, :, dst] += part
            else:
                y_ref[0, :, dst] = part
        ctx_scan(reverse)
        h0 = h0_ref[0, 0]
        init = []
        for q in range(N_LAT_SEQ):
            init += [h0[q:q + 1, re], h0[q:q + 1, im]]
        lax.fori_loop(0, S5_LAT_GROUPS, functools.partial(lat_group, reverse=reverse), tuple(init))

    @pl.when(d == 0)
    def _():
        run(False)

    @pl.when(d == 1)
    def _():
        run(True)

    y_ref[0] += _dot_nt(x_sc[...].astype(BF16), wout_ref[0, 0, 0])


def _s5_main(u, m, w_in, w_out, pows, h0, layer):
    def tab(*tail):
        return pl.BlockSpec((1, 1, 1) + tail, lambda b, d: (layer, d, b) + (0,) * len(tail))

    return pl.pallas_call(
        _s5_kernel,
        out_shape=(jax.ShapeDtypeStruct((S5_BLK, S5_ROWS, S5_CW), F32),
                   jax.ShapeDtypeStruct((2, S5_BLK, N_CTX_SEQ, 2 * S5_SW), F32)),
        grid=(S5_BLK, 2),
        in_specs=[
            pl.BlockSpec((1, S5_ROWS, S5_CW), lambda b, d: (b, 0, 0)),
            tab(S5_CW, S5_CW), tab(S5_CW, 2 * S5_SW), tab(S5_CW, 2 * S5_SW), tab(2, S5_POW_ROWS, S5_SW),
            pl.BlockSpec((1, 1, N_LAT_SEQ, 2 * S5_SW), lambda b, d: (d, b, 0, 0)),
        ],
        out_specs=(
            pl.BlockSpec((1, S5_ROWS, S5_CW), lambda b, d: (b, 0, 0)),
            pl.BlockSpec((1, 1, N_CTX_SEQ, 2 * S5_SW), lambda b, d: (d, b, 0, 0)),
        ),
        scratch_shapes=[pltpu.VMEM((S5_ROWS, 2 * S5_SW), F32)],
        compiler_params=_cparams(("parallel", "arbitrary")),
        name="s5",
    )(u, m, w_in, w_out, pows, h0)


def _hg_constants():
    t = np.arange(HG_C)
    ae = np.zeros((2, HG_E_ROWS, HG_C), np.float32)
    mask = np.zeros((2, HG_LEVELS, HG_C, HG_C), np.float32)
    for lvl in range(HG_LEVELS):
        half = 1 << lvl
        pos = t % (2 * half)
        mid = t - pos + half - 1
        upper = pos >= half
        u = t[None, :]
        rows_upper = (u > mid[:, None]) & (u <= t[:, None])
        rows_lower = (u > t[:, None]) & (u <= mid[:, None])
        ae[0, lvl * HG_C:(lvl + 1) * HG_C] = np.where(upper[:, None], rows_upper, rows_lower)
        same = (t[:, None] // (2 * half)) == (t[None, :] // (2 * half))
        mask[0, lvl] = same & upper[:, None] & (~upper)[None, :]
    ae[0, HG_LEVELS * HG_C:(HG_LEVELS + 1) * HG_C] = t[None, :] <= t[:, None]
    ae[0, (HG_LEVELS + 1) * HG_C:] = 1.0
    ae[1] = ae[0][:, ::-1]
    ae[1, :(HG_LEVELS + 1) * HG_C] = ae[1, :(HG_LEVELS + 1) * HG_C].reshape(HG_LEVELS + 1, HG_C, HG_C)[:, ::-1].reshape(-1, HG_C)
    mask[1] = mask[0][:, ::-1, ::-1]
    ae2 = np.concatenate([ae, ae], axis=-1)
    mask2 = np.concatenate([mask, mask], axis=-1)
    pair_diag = np.kron(np.eye(2, dtype=np.float32), np.ones((HG_DK, HG_DK), np.float32))
    return ae2, mask2, pair_diag


def _hg_tile(direction, i):
    return i if direction == 0 else HG_TILES - 1 - i


HG_PAIR = 2 * HG_DK


def _pair_diag(x):
    zero = jnp.zeros((x.shape[0], HG_DK), x.dtype)
    return jnp.concatenate([jnp.concatenate([x[:, :HG_DK], zero], axis=1),
                            jnp.concatenate([zero, x[:, HG_DK:]], axis=1)], axis=0)


def _hg_chunk_prep(direction, rows, q_ref, f_ref, v_ref, lb, ae_ref):
    f = lb + (1.0 - lb) * jax.nn.sigmoid(f_ref[rows, :])
    logf = jnp.log2(f)
    kk = 1.0 - f
    hi = logf.astype(BF16)
    lo = (logf - hi.astype(F32)).astype(BF16)
    ex = _dot(ae_ref[direction], jnp.concatenate([hi, lo], axis=0))
    dec = jnp.exp2(ex[:(HG_LEVELS + 1) * HG_C])
    cum = ex[HG_LEVELS * HG_C:(HG_LEVELS + 1) * HG_C]
    tot = ex[(HG_LEVELS + 1) * HG_C:(HG_LEVELS + 1) * HG_C + 1]
    q = q_ref[rows, :].astype(F32)
    v_bf = v_ref[rows, :]
    v = v_bf.astype(F32)
    q_in = (q * dec[HG_LEVELS * HG_C:]).astype(BF16)
    k_tail = (kk * jnp.exp2(tot - cum)).astype(BF16)
    dec_tot = jnp.exp2(tot)
    return dec, q, kk, v, v_bf, q_in, k_tail, dec_tot


def _hg_chunk_scores(direction, prep, mask_ref):
    dec, q, kk, v, v_bf, q_in, k_tail, dec_tot = prep
    qk = q * kk
    pairs = [slice(p * HG_PAIR, (p + 1) * HG_PAIR) for p in range(HG_HEADS // 2)]
    scores = [jnp.zeros((HG_C, HG_PAIR), F32) for _ in pairs]
    for lvl in range(HG_LEVELS):
        for p, sl in enumerate(pairs):
            g = dec[lvl * HG_C:(lvl + 1) * HG_C, sl]
            scores[p] = scores[p] + mask_ref[direction, lvl] * _dot_nt(
                (q[:, sl] * g).astype(BF16), _pair_diag((kk[:, sl] * g).astype(BF16)))
    local = []
    for p, sl in enumerate(pairs):
        diag = jnp.concatenate(
            [jnp.broadcast_to(jnp.sum(qk[:, (2 * p + j) * HG_DK:(2 * p + j + 1) * HG_DK], axis=-1, keepdims=True),
                              (HG_C, HG_DK)) for j in range(2)], axis=-1)
        local.append(_dot(scores[p].astype(BF16), _pair_diag(v_bf[:, sl])) + diag * v[:, sl])
    return local, q_in, k_tail, dec_tot, v_bf


def _hg_chunk_state(direction, rows, chunk, pd_ref, o_ref, st_sc):
    local, q_in, k_tail, dec_tot, v_bf = chunk
    outs = []
    for p in range(HG_HEADS // 2):
        sl = slice(p * HG_PAIR, (p + 1) * HG_PAIR)
        st = st_sc[direction, p]
        outs.append(local[p] + _dot_nt(q_in[:, sl], st.astype(BF16)))
        st_sc[direction, p] = st * dec_tot[:, sl] + pd_ref[...] * _dot_tn(v_bf[:, sl], k_tail[:, sl])
    o_ref[rows, :] = jnp.concatenate(outs, axis=-1)


def _hg_kernel(qf_ref, ff_ref, vf_ref, qb_ref, fb_ref, vb_ref, lb_ref, ae_ref, mask_ref, pd_ref, h0f_ref, h0b_ref,
               of_ref, ob_ref, finf_ref, finb_ref, st_sc, *, layer):
    i = pl.program_id(0)
    x = lb_ref[:, :, 0, :]
    e = jnp.exp(x - jnp.max(x, axis=0, keepdims=True))
    sm = e / jnp.sum(e, axis=0, keepdims=True)
    lb = jnp.sum(sm[1:layer + 1], axis=0) if layer > 0 else jnp.zeros((2, D_BRANCH), F32)
    for direction, h0_ref in ((0, h0f_ref), (1, h0b_ref)):
        tile = _hg_tile(direction, i)
        first_ctx = 0 if direction == 0 else HG_TILES_PER_CTX - 1
        first_lat = 0 if direction == 0 else HG_TILES_PER_LAT - 1
        is_start = jnp.where(tile < HG_CTX_TILES,
                             tile % HG_TILES_PER_CTX == first_ctx,
                             (tile - HG_CTX_TILES) % HG_TILES_PER_LAT == first_lat)

        @pl.when(is_start)
        def _(direction=direction, h0_ref=h0_ref):
            for p in range(HG_HEADS // 2):
                st_sc[direction, p] = _pair_diag(
                    jnp.concatenate([h0_ref[0, 0, 2 * p], h0_ref[0, 0, 2 * p + 1]], axis=1))

    work = []
    for c in range(HG_TM // HG_C):
        fwd_rows = slice(c * HG_C, (c + 1) * HG_C)
        bwd_rows = slice(HG_TM - (c + 1) * HG_C, HG_TM - c * HG_C)
        for direction, rows, refs, o_ref in ((0, fwd_rows, (qf_ref, ff_ref, vf_ref), of_ref),
                                             (1, bwd_rows, (qb_ref, fb_ref, vb_ref), ob_ref)):
            prep = _hg_chunk_prep(direction, rows, *refs, lb[direction:direction + 1], ae_ref)
            work.append((direction, rows, o_ref, _hg_chunk_scores(direction, prep, mask_ref)))
    for direction, rows, o_ref, chunk in work:
        _hg_chunk_state(direction, rows, chunk, pd_ref, o_ref, st_sc)

    for direction, fin_ref in ((0, finf_ref), (1, finb_ref)):
        @pl.when(_hg_tile(direction, i) < HG_CTX_TILES)
        def _(direction=direction, fin_ref=fin_ref):
            for h in range(HG_HEADS):
                p, j = divmod(h, 2)
                fin_ref[0, h] = st_sc[direction, p, j * HG_DK:(j + 1) * HG_DK, j * HG_DK:(j + 1) * HG_DK]


def _hgrn(proj, fz, hg_lb, ae, mask, pair_diag, h0t, layer):
    def tok_spec(direction, col):
        return pl.BlockSpec((HG_TM, D_BRANCH), lambda i: (_hg_tile(direction, i), col))

    def fin_spec(direction):
        return pl.BlockSpec((1, HG_HEADS, HG_DK, HG_DK),
                            lambda i: (jnp.minimum(_hg_tile(direction, i) // HG_TILES_PER_CTX, N_CTX_SEQ - 1), 0, 0, 0))

    def h0_spec(direction):
        return pl.BlockSpec((1, 1, HG_HEADS, HG_DK, HG_DK),
                            lambda i: (_cond_row(_hg_tile(direction, i), HG_TM), direction, 0, 0, 0))

    fin_shape = jax.ShapeDtypeStruct((N_CTX_SEQ, HG_HEADS, HG_DK, HG_DK), F32)
    return pl.pallas_call(
        functools.partial(_hg_kernel, layer=layer),
        out_shape=(jax.ShapeDtypeStruct((N_TOK, D_BRANCH), F32), jax.ShapeDtypeStruct((N_TOK, D_BRANCH), F32),
                   fin_shape, fin_shape),
        grid=(HG_TILES,),
        in_specs=[
            tok_spec(0, HG_Q_COL), tok_spec(0, 0), tok_spec(0, HG_V_COL),
            tok_spec(1, HG_Q_COL), tok_spec(1, 1), tok_spec(1, HG_V_COL),
            pl.BlockSpec((DEPTH, 2, 1, D_BRANCH), lambda i: (0, 0, 0, 0)),
            pl.BlockSpec((2, HG_E_ROWS, 2 * HG_C), lambda i: (0, 0, 0)),
            pl.BlockSpec((2, HG_LEVELS, HG_C, HG_PAIR), lambda i: (0, 0, 0, 0)),
            pl.BlockSpec((HG_PAIR, HG_PAIR), lambda i: (0, 0)),
            h0_spec(0), h0_spec(1),
        ],
        out_specs=(
            pl.BlockSpec((HG_TM, D_BRANCH), lambda i: (_hg_tile(0, i), 0)),
            pl.BlockSpec((HG_TM, D_BRANCH), lambda i: (_hg_tile(1, i), 0)),
            fin_spec(0), fin_spec(1),
        ),
        scratch_shapes=[pltpu.VMEM((2, HG_HEADS // 2, HG_PAIR, HG_PAIR), F32)],
        compiler_params=_cparams(("arbitrary",)),
        name="hgrn",
    )(proj, fz, proj, proj, fz, proj, hg_lb.reshape(DEPTH, 2, 1, D_BRANCH), ae, mask, pair_diag, h0t, h0t)


MRG_TM = 256


def _gelu_tanh(x):
    return 0.5 * x * (1.0 + jnp.tanh(0.7978845608028654 * (x + 0.044715 * (x * x * x))))


def _merge_kernel(scb_ref, scc_ref, sch_ref, hgg_ref, pool_ref, g0_ref, g1_ref, g2_ref, g3_ref,
                  yrow_ref, of_ref, ob_ref, x_ref, mod_ref, conv_ref, gluw_ref, glub_ref, ng_ref,
                  poolw_ref, pscale_ref, wbr_ref, wout_ref, lng_ref, lnb_ref, o_ref,
                  ys5_sc, gluw_sc, poolw_sc, wbr_sc, wout_sc):
    tile = pl.program_id(0)

    @pl.when(tile == 0)
    def _():
        gluw_sc[...] = gluw_ref[0].astype(BF16)
        poolw_sc[...] = poolw_ref[0].astype(BF16)
        wbr_sc[...] = wbr_ref[0].astype(BF16)
        wout_sc[...] = wout_ref[0].astype(BF16)

    line = jnp.where(tile * MRG_TM < N_CTX_TOK, CTX_LEN, GRID_W)
    pos = lax.broadcasted_iota(jnp.int32, (MRG_TM, 1), 0) & (line - 1)

    def shifted(val, k):
        rolled = pltpu.roll(val, k % MRG_TM, axis=0)
        ok = (pos >= k) if k > 0 else (pos < line + k)
        return jnp.where(ok, rolled, 0.0)

    m = scc_ref[...].astype(F32) * sch_ref[...].astype(F32)
    conv = conv_ref[0, 0:1, :] * shifted(m, 1) + conv_ref[0, 1:2, :] * m + conv_ref[0, 2:3, :] * shifted(m, -1)
    ya = scb_ref[...].astype(F32) * conv

    for b in range(S5_BLK):
        for t in range(S5_T):
            ys5_sc[b, pl.ds(t, MRG_TM // S5_T, stride=S5_T), :] = yrow_ref[b, :, t * LANES:(t + 1) * LANES]
    z = _gelu_tanh(jnp.concatenate([ys5_sc[b] for b in range(S5_BLK)], axis=-1))
    yb = z * _sigmoid_tanh(_dot(z.astype(BF16), gluw_sc[...]) + glub_ref[0])

    o = of_ref[...] + ob_ref[...]
    normed = []
    for h in range(HG_HEADS):
        oh = o[:, h * HG_DK:(h + 1) * HG_DK]
        ms = jnp.mean(oh * oh, axis=-1, keepdims=True)
        normed.append(oh * lax.rsqrt(ms + LN_EPS) * ng_ref[0])
    gate_c = hgg_ref[...].astype(F32)
    yc = jnp.concatenate(normed, axis=-1) * (gate_c * _sigmoid_tanh(gate_c))

    pu = pool_ref[...].astype(F32)
    posf = pos.astype(F32)
    linef = line.astype(F32)
    pooled = []
    for gi, w in enumerate(POOL_WINDOWS):
        vg = pu[:, gi * POOL_GROUP:(gi + 1) * POOL_GROUP]
        back, fwd, span = vg, vg, 1
        while 2 * span <= w // 2:
            back = back + shifted(back, span)
            fwd = fwd + shifted(fwd, -span)
            span *= 2
        s = shifted(back, 1) + fwd
        cnt = jnp.minimum(posf + w // 2, linef) - jnp.maximum(posf - w // 2, 0.0)
        pg = s / cnt - vg
        pooled.append(_dot(pg.astype(BF16), poolw_sc[gi]))
    yd = jnp.concatenate(pooled, axis=-1) * pscale_ref[0]

    merged = jnp.zeros((MRG_TM, D_MODEL), F32)
    for k, (br, gate_ref) in enumerate(((ya, g0_ref), (yb, g1_ref), (yc, g2_ref), (yd, g3_ref))):
        merged = merged + gate_ref[...].astype(F32) * _dot(br.astype(BF16), wbr_sc[k])
    mix = _dot(merged.astype(BF16), wout_sc[...])
    y = ALPHA * x_ref[...] + mod_ref[0, 0, 5:6, :] * mix
    o_ref[...] = _layer_norm(y, lng_ref[0, 0], lnb_ref[0, 0])


def _merge(proj, y_rows, o_f, o_b, x, mod, conv_w, glu_w, glu_b, norm_g, pool_w, pool_scale, w_branch, w_out,
           ln_g, ln_b, layer):
    def part(col):
        return pl.BlockSpec((MRG_TM, D_BRANCH), lambda i: (i, col))

    def gate(k):
        return pl.BlockSpec((MRG_TM, D_MODEL), lambda i: (i, N_IN_PARTS * D_BRANCH // D_MODEL + k))

    def per_layer(*shape, single=False):
        mode = dict(pipeline_mode=pl.Buffered(1)) if single else {}
        return pl.BlockSpec((1,) + shape, lambda i: (layer,) + (0,) * len(shape), **mode)

    return pl.pallas_call(
        _merge_kernel,
        out_shape=jax.ShapeDtypeStruct((N_TOK, D_MODEL), F32),
        grid=(N_TOK // MRG_TM,),
        in_specs=[
            part(0), part(1), part(2), part(8), part(9), gate(0), gate(1), gate(2), gate(3),
            pl.BlockSpec((S5_BLK, MRG_TM // S5_T, S5_CW), lambda i: (0, i, 0)),
            pl.BlockSpec((MRG_TM, D_BRANCH), lambda i: (i, 0)),
            pl.BlockSpec((MRG_TM, D_BRANCH), lambda i: (i, 0)),
            pl.BlockSpec((MRG_TM, D_MODEL), lambda i: (i, 0)),
            _mod_spec(layer, MRG_TM),
            per_layer(3, D_BRANCH), per_layer(D_BRANCH, D_BRANCH, single=True), per_layer(1, D_BRANCH),
            per_layer(1, HG_DK), per_layer(len(POOL_WINDOWS), POOL_GROUP, POOL_GROUP, single=True),
            per_layer(1, D_BRANCH), per_layer(N_BRANCH, D_BRANCH, D_MODEL, single=True),
            per_layer(D_MODEL, D_MODEL, single=True), _ln_spec(layer, 1), _ln_spec(layer, 1),
        ],
        out_specs=pl.BlockSpec((MRG_TM, D_MODEL), lambda i: (i, 0)),
        scratch_shapes=[
            pltpu.VMEM((S5_BLK, MRG_TM, LANES), F32),
            pltpu.VMEM((D_BRANCH, D_BRANCH), BF16),
            pltpu.VMEM((len(POOL_WINDOWS), POOL_GROUP, POOL_GROUP), BF16),
            pltpu.VMEM((N_BRANCH, D_BRANCH, D_MODEL), BF16),
            pltpu.VMEM((D_MODEL, D_MODEL), BF16),
        ],
        compiler_params=_cparams(("arbitrary",)),
        name="merge",
    )(proj, proj, proj, proj, proj, proj, proj, proj, proj, y_rows, o_f, o_b, x, mod,
      conv_w, glu_w, glu_b.reshape(DEPTH, 1, D_BRANCH), norm_g.reshape(DEPTH, 1, HG_DK), pool_w,
      pool_scale.reshape(DEPTH, 1, D_BRANCH), w_branch, w_out, ln_g, ln_b)


def _grid_pos_embedding():
    rows = LAT_LEN // GRID_W
    quarter = D_MODEL // 4
    omega = POS_BASE ** (-jnp.arange(quarter, dtype=F32) / quarter)
    ar = jnp.arange(rows, dtype=F32)[:, None] * omega
    ac = jnp.arange(GRID_W, dtype=F32)[:, None] * omega
    row_tab = jnp.concatenate([jnp.sin(ar), jnp.cos(ar)], -1)[:, None, :]
    col_tab = jnp.concatenate([jnp.sin(ac), jnp.cos(ac)], -1)[None, :, :]
    shape = (rows, GRID_W, 2 * quarter)
    pos = jnp.concatenate([jnp.broadcast_to(row_tab, shape), jnp.broadcast_to(col_tab, shape)], -1)
    return pos.reshape(LAT_LEN, D_MODEL)


def kernel(x_prompt, x_sample, state_s5_re, state_s5_im, state_hgrn, c, c_ctx, w_ada, b_ada, ln_g, ln_b,
           ffn_w1, ffn_w3, ffn_w2, w_in, sc_conv, s5_lam_re, s5_lam_im, s5_log_dt, s5_b_re, s5_b_im,
           s5_c_re, s5_c_im, s5_d, s5_glu_w, s5_glu_b, hg_lb, hg_norm_g, pool_w, pool_scale, w_branch, w_out):
    x = (x_prompt.reshape(N_CTX_TOK, D_MODEL), x_sample.reshape(N_LAT_TOK, D_MODEL), _grid_pos_embedding())

    cond = jnp.zeros((COND_PAD, D_MODEL), F32).at[0].set(c_ctx).at[1:N_COND].set(c)
    mod = _ada(cond, w_ada, b_ada).reshape(DEPTH, COND_PAD, N_SUB * 3, D_MODEL)
    ln_g4 = ln_g.reshape(DEPTH, N_SUB, 1, D_MODEL)
    ln_b4 = ln_b.reshape(DEPTH, N_SUB, 1, D_MODEL)

    s5_m, s5_win, s5_wout, s5_pow = _s5_tables(*_s5_param_layout(
        s5_lam_re, s5_lam_im, s5_log_dt, s5_b_re, s5_b_im, s5_c_re, s5_c_im, s5_d))

    w_in_bf = _cast_w_in(w_in)

    ae_np, mask_np, pair_np = _hg_constants()
    ae = jnp.asarray(ae_np, BF16)
    mask = jnp.asarray(mask_np, F32)
    pair_diag = jnp.asarray(pair_np, F32)

    fin_s5, fin_hg = [], []
    for l in range(DEPTH):
        x, h = _ffn(x, mod, ffn_w1, ffn_w3, ffn_w2, ln_g4, ln_b4, l, 0)
        proj, fz, u_rows = _inproj(h, w_in_bf, l)

        h0 = jnp.concatenate([
            state_s5_re[:, l].reshape(N_LAT_SEQ, 2, S5_BLK, S5_SW),
            state_s5_im[:, l].reshape(N_LAT_SEQ, 2, S5_BLK, S5_SW)], axis=-1).transpose(1, 2, 0, 3)
        y_rows, fin = _s5_main(u_rows, s5_m, s5_win, s5_wout, s5_pow, h0, l)
        fin_s5.append(fin)

        h0t = jnp.concatenate([jnp.zeros((1, 2, HG_HEADS, HG_DK, HG_DK), F32),
                               jnp.swapaxes(state_hgrn[:, l], -1, -2)], axis=0)
        o_f, o_b, fin_f, fin_b = _hgrn(proj, fz, hg_lb, ae, mask, pair_diag, h0t, l)
        fin_hg.append(jnp.stack([fin_f, fin_b], axis=1))

        x = _merge(proj, y_rows, o_f, o_b, x, mod, sc_conv, s5_glu_w, s5_glu_b, hg_norm_g,
                   pool_w, pool_scale, w_branch, w_out, ln_g4, ln_b4, l)
        x = _ffn(x, mod, ffn_w1, ffn_w3, ffn_w2, ln_g4, ln_b4, l, 2, split_output=(l == DEPTH - 1))

    y_prompt = x[0].reshape(N_CTX_SEQ, CTX_LEN, D_MODEL)
    y_sample = x[1].reshape(N_LAT_SEQ, LAT_LEN, D_MODEL)
    fin = jnp.stack(fin_s5)
    fin = fin.reshape(DEPTH, 2, S5_BLK, N_CTX_SEQ, 2, S5_GPB, S5_STATE)
    fin = fin.transpose(4, 3, 0, 1, 2, 5, 6).reshape(2, N_CTX_SEQ, DEPTH, 2, S5_GROUPS, S5_STATE)
    new_hgrn = jnp.swapaxes(jnp.stack(fin_hg, axis=1), -1, -2)
    return y_prompt, y_sample, fin[0], fin[1], new_hgrn
```

```python
import functools

import numpy as np
import jax
import jax.numpy as jnp
from jax import lax
from jax.experimental import pallas as pl
from jax.experimental.pallas import tpu as pltpu

F32 = jnp.float32
BF16 = jnp.bfloat16

D_MODEL = 1024
N_CTX_SEQ = 16
CTX_LEN = 256
DEPTH = 2
N_LAT_SEQ = 2
LAT_LEN = 4096
GRID_W = 64
D_BRANCH = 512
N_BRANCH = 4
S5_GROUPS = 32
S5_GROUP = 16
S5_STATE = 64
HG_HEADS = 4
HG_DK = 128
POOL_WINDOWS = (2, 4, 8, 16)
POOL_GROUP = 128
D_FF = 2816
N_SUB = 3
N_IN_PARTS = 10
IN_WIDTH = N_IN_PARTS * D_BRANCH + N_BRANCH * D_MODEL
ADA_WIDTH = N_SUB * 3 * D_MODEL
ALPHA = (2 * DEPTH) ** 0.25
LN_EPS = 1e-5
POS_BASE = 10000.0

N_CTX_TOK = N_CTX_SEQ * CTX_LEN
N_LAT_TOK = N_LAT_SEQ * LAT_LEN
N_TOK = N_CTX_TOK + N_LAT_TOK
N_COND = 1 + N_LAT_SEQ
COND_PAD = 8

LANES = 128
SUBLANES = 8
VMEM_LIMIT = 56 * 1024 * 1024

S5_T = 8
S5_BLK = 4
S5_GPB = S5_GROUPS // S5_BLK
S5_SW = S5_GPB * S5_STATE
S5_CW = S5_T * LANES
CTX_CHUNKS = CTX_LEN // S5_T
LAT_CHUNKS = LAT_LEN // S5_T
S5_ROWS_CTX = N_CTX_TOK // S5_T
S5_ROWS_LAT = N_LAT_TOK // S5_T
S5_ROWS = S5_ROWS_CTX + S5_ROWS_LAT
S5_LAT_GROUPS = LAT_CHUNKS // SUBLANES
S5_POW_ROWS = 3 * SUBLANES
S5_U_COL = 3

HG_C = 128
HG_LEVELS = 7
HG_TM = 256
HG_TILES = N_TOK // HG_TM
HG_CTX_TILES = N_CTX_TOK // HG_TM
HG_TILES_PER_CTX = CTX_LEN // HG_TM
HG_TILES_PER_LAT = LAT_LEN // HG_TM
HG_E_ROWS = (HG_LEVELS + 1) * HG_C + SUBLANES


def _cparams(sem):
    return pltpu.CompilerParams(dimension_semantics=sem, vmem_limit_bytes=VMEM_LIMIT)


def _cond_row(tile, tile_tokens):
    tok = tile * tile_tokens
    return jnp.where(tok < N_CTX_TOK, 0, 1 + (tok - N_CTX_TOK) // LAT_LEN)


def _dot(a, b):
    return jnp.dot(a, b, preferred_element_type=F32)


def _dot_nt(a, b):
    return lax.dot_general(a, b, (((1,), (1,)), ((), ())), preferred_element_type=F32)


def _dot_tn(a, b):
    return lax.dot_general(a, b, (((0,), (0,)), ((), ())), preferred_element_type=F32)


def _split3(x):
    h1 = x.astype(BF16)
    r1 = x - h1.astype(F32)
    h2 = r1.astype(BF16)
    h3 = (r1 - h2.astype(F32)).astype(BF16)
    return h1, h2, h3


def _dot_nt_hi(a3, b3):
    acc = None
    for x in range(2):
        for y in range(2 - x):
            term = _dot_nt(a3[x], b3[y])
            acc = term if acc is None else acc + term
    return acc


def _silu(x):
    return x * jax.nn.sigmoid(x)


def _sigmoid_tanh(x):
    return 0.5 * jnp.tanh(0.5 * x) + 0.5


def _layer_norm(y, g, b):
    mu = jnp.mean(y, axis=-1, keepdims=True)
    yc = y - mu
    var = jnp.mean(yc * yc, axis=-1, keepdims=True)
    return yc * lax.rsqrt(var + LN_EPS) * g + b


def _mod_spec(layer, tile_tokens):
    return pl.BlockSpec((1, 1, N_SUB * 3, D_MODEL),
                        lambda i, *_: (layer, _cond_row(i, tile_tokens), 0, 0))


def _ln_spec(layer, sub):
    return pl.BlockSpec((1, 1, 1, D_MODEL), lambda *_: (layer, sub, 0, 0))


ADA_TN = 1152


def _ada_kernel(c_ref, w_ref, b_ref, o_ref):
    s = _silu(c_ref[...]).astype(BF16)
    o_ref[0] = _dot(s, w_ref[0].astype(BF16)) + b_ref[0]


def _ada(cond, w_ada, b_ada):
    return pl.pallas_call(
        _ada_kernel,
        out_shape=jax.ShapeDtypeStruct((DEPTH, COND_PAD, ADA_WIDTH), F32),
        grid=(DEPTH, ADA_WIDTH // ADA_TN),
        in_specs=[
            pl.BlockSpec((COND_PAD, D_MODEL), lambda l, j: (0, 0)),
            pl.BlockSpec((1, D_MODEL, ADA_TN), lambda l, j: (l, 0, j)),
            pl.BlockSpec((1, 1, ADA_TN), lambda l, j: (l, 0, j)),
        ],
        out_specs=pl.BlockSpec((1, COND_PAD, ADA_TN), lambda l, j: (l, 0, j)),
        compiler_params=_cparams(("parallel", "parallel")),
        name="ada",
    )(cond, w_ada, b_ada.reshape(DEPTH, 1, ADA_WIDTH))


FFN_TM = 1024
FFN_TM_SPLIT = 512
FFN_TC = 256
FFN_CHUNKS = D_FF // FFN_TC


def _ffn_kernel(*refs, sub, tm, split_input, split_output, emit_next):
    step = pl.program_id(0)
    tile = step - FFN_CHUNKS
    is_ctx = tile < N_CTX_TOK // tm
    n_x = 3 if split_input else 1
    x_refs, refs = refs[:n_x], refs[n_x:]
    mod_ref, w1_ref, w3_ref, w2_ref, g_ref, b_ref = refs[:6]
    out_refs = refs[6:-4]
    w1_sc, w3_sc, w2_sc, act_sc = refs[-4:]

    @pl.when(step < FFN_CHUNKS)
    def _():
        w1_sc[step] = w1_ref[0, 0].astype(BF16)
        w3_sc[step] = w3_ref[0, 0].astype(BF16)
        w2_sc[pl.ds(pl.multiple_of(step * FFN_TC, FFN_TC), FFN_TC), :] = w2_ref[0, 0].astype(BF16)

    @pl.when(step >= FFN_CHUNKS)
    def _():
        if split_input:
            ctx_ref, lat_ref, pos_ref = x_refs
            x = jnp.where(is_ctx, ctx_ref[...], lat_ref[...] + pos_ref[...])
        else:
            x = x_refs[0][...]
        shift = mod_ref[0, 0, 3 * sub:3 * sub + 1, :]
        scale = mod_ref[0, 0, 3 * sub + 1:3 * sub + 2, :]
        gate = mod_ref[0, 0, 3 * sub + 2:3 * sub + 3, :]
        h = (x * (1.0 + scale) + shift).astype(BF16)
        for c in range(FFN_CHUNKS):
            a = _dot(h, w1_sc[c])
            b = _dot(h, w3_sc[c])
            act_sc[:, c * FFN_TC:(c + 1) * FFN_TC] = (_silu(a) * b).astype(BF16)
        f = _dot(act_sc[...], w2_sc[...])
        y = _layer_norm(ALPHA * x + gate * (0.5 * f), g_ref[0, 0], b_ref[0, 0])
        if emit_next:
            nxt = 3 * (sub + 1)
            out_refs[1][...] = (y * (1.0 + mod_ref[0, 0, nxt + 1:nxt + 2, :])
                                + mod_ref[0, 0, nxt:nxt + 1, :]).astype(BF16)
        if split_output:
            @pl.when(is_ctx)
            def _():
                out_refs[0][...] = y

            @pl.when(jnp.logical_not(is_ctx))
            def _():
                out_refs[1][...] = y
        else:
            out_refs[0][...] = y


def _ffn(x, mod, w1, w3, w2, ln_g, ln_b, layer, sub, split_output=False):
    which = sub // 2
    split_input = isinstance(x, tuple)
    tm = FFN_TM_SPLIT if split_input else FFN_TM
    ctx_tiles = N_CTX_TOK // tm

    def chunk(s):
        return jnp.minimum(s, FFN_CHUNKS - 1)

    def tile(s):
        return jnp.maximum(s - FFN_CHUNKS, 0)

    def tok_spec(index):
        return pl.BlockSpec((tm, D_MODEL), lambda s: (index(tile(s)), 0))

    def ctx_tile(t):
        return jnp.minimum(t, ctx_tiles - 1)

    def lat_tile(t):
        return jnp.maximum(t - ctx_tiles, 0)

    if split_input:
        pos_tiles = LAT_LEN // tm
        x_specs = [tok_spec(ctx_tile), tok_spec(lat_tile), tok_spec(lambda t: lat_tile(t) % pos_tiles)]
    else:
        x_specs = [tok_spec(lambda t: t)]
        x = (x,)
    emit_next = sub == 0
    if split_output:
        out_shape = (jax.ShapeDtypeStruct((N_CTX_TOK, D_MODEL), F32), jax.ShapeDtypeStruct((N_LAT_TOK, D_MODEL), F32))
        out_specs = (tok_spec(ctx_tile), tok_spec(lat_tile))
    elif emit_next:
        out_shape = (jax.ShapeDtypeStruct((N_TOK, D_MODEL), F32), jax.ShapeDtypeStruct((N_TOK, D_MODEL), BF16))
        out_specs = (tok_spec(lambda t: t), tok_spec(lambda t: t))
    else:
        out_shape = jax.ShapeDtypeStruct((N_TOK, D_MODEL), F32)
        out_specs = tok_spec(lambda t: t)

    return pl.pallas_call(
        functools.partial(_ffn_kernel, sub=sub, tm=tm, split_input=split_input, split_output=split_output,
                          emit_next=emit_next),
        out_shape=out_shape,
        grid=(FFN_CHUNKS + N_TOK // tm,),
        in_specs=x_specs + [
            pl.BlockSpec((1, 1, N_SUB * 3, D_MODEL), lambda s: (layer, _cond_row(tile(s), tm), 0, 0)),
            pl.BlockSpec((1, 1, D_MODEL, FFN_TC), lambda s: (layer, which, 0, chunk(s))),
            pl.BlockSpec((1, 1, D_MODEL, FFN_TC), lambda s: (layer, which, 0, chunk(s))),
            pl.BlockSpec((1, 1, FFN_TC, D_MODEL), lambda s: (layer, which, chunk(s), 0)),
            _ln_spec(layer, sub), _ln_spec(layer, sub),
        ],
        out_specs=out_specs,
        scratch_shapes=[pltpu.VMEM((FFN_CHUNKS, D_MODEL, FFN_TC), BF16), pltpu.VMEM((FFN_CHUNKS, D_MODEL, FFN_TC), BF16),
                        pltpu.VMEM((D_FF, D_MODEL), BF16), pltpu.VMEM((tm, D_FF), BF16)],
        compiler_params=_cparams(("arbitrary",)),
        name=f"ffn{sub}",
    )(*x, mod, w1, w3, w2, ln_g, ln_b)


INP_TM = 2048
INP_PARTS = 2
INP_TN = INP_PARTS * D_BRANCH
INP_STEPS = IN_WIDTH // INP_TN
HG_Q_COL, HG_F_COL, HG_V_COL = 4, 5, 7


def _cast_w_in_kernel(w_ref, o_ref):
    o_ref[0, 0] = w_ref[0].astype(BF16)


def _cast_w_in(w_in):
    return pl.pallas_call(
        _cast_w_in_kernel,
        out_shape=jax.ShapeDtypeStruct((DEPTH, INP_STEPS, D_MODEL, INP_TN), BF16),
        grid=(DEPTH, INP_STEPS),
        in_specs=[pl.BlockSpec((1, D_MODEL, INP_TN), lambda l, j: (l, 0, j))],
        out_specs=pl.BlockSpec((1, 1, D_MODEL, INP_TN), lambda l, j: (l, j, 0, 0)),
        compiler_params=_cparams(("parallel", "parallel")),
        name="cast_w_in",
    )(w_in)


def _inproj_kernel(h_ref, w_ref, o_ref, fz_ref, u_ref, blk_sc):
    j = pl.program_id(1)

    def project(gate=False):
        res = _dot(h_ref[...], w_ref[0, 0])
        o_ref[...] = (_sigmoid_tanh(res) if gate else res).astype(BF16)
        return res

    def part(res, p):
        k = p % INP_PARTS
        return res[:, k * D_BRANCH:(k + 1) * D_BRANCH]

    special = (S5_U_COL, HG_F_COL, HG_F_COL + 1)
    steps = [p // INP_PARTS for p in special]
    assert len(set(steps)) == len(steps)

    first_gate = N_IN_PARTS // INP_PARTS
    assert N_IN_PARTS % INP_PARTS == 0 and max(steps) < first_gate

    @pl.when(functools.reduce(jnp.logical_and, [j != s for s in steps] + [j < first_gate]))
    def _():
        project()

    @pl.when(j >= first_gate)
    def _():
        project(gate=True)

    for p in (HG_F_COL, HG_F_COL + 1):
        @pl.when(j == p // INP_PARTS)
        def _(p=p):
            fz_ref[...] = part(project(), p)

    @pl.when(j == S5_U_COL // INP_PARTS)
    def _():
        s5_u = part(project(), S5_U_COL)
        for b in range(S5_BLK):
            blk_sc[b] = s5_u[:, b * LANES:(b + 1) * LANES]
            for t in range(S5_T):
                rows = blk_sc[b, pl.ds(t, INP_TM // S5_T, stride=S5_T), :]
                u_ref[b, :, t * LANES:(t + 1) * LANES] = rows.astype(BF16)


def _inproj(h, w_in_bf, layer):
    return pl.pallas_call(
        _inproj_kernel,
        out_shape=(jax.ShapeDtypeStruct((N_TOK, IN_WIDTH), BF16),
                   jax.ShapeDtypeStruct((N_TOK, 2 * D_BRANCH), F32),
                   jax.ShapeDtypeStruct((S5_BLK, S5_ROWS, S5_CW), BF16)),
        grid=(N_TOK // INP_TM, INP_STEPS),
        in_specs=[
            pl.BlockSpec((INP_TM, D_MODEL), lambda i, j: (i, 0)),
            pl.BlockSpec((1, 1, D_MODEL, INP_TN), lambda i, j: (layer, j, 0, 0)),
        ],
        out_specs=(pl.BlockSpec((INP_TM, INP_TN), lambda i, j: (i, j)),
                   pl.BlockSpec((INP_TM, D_BRANCH),
                                lambda i, j: (i, jnp.where(j > HG_F_COL // INP_PARTS, 1, 0))),
                   pl.BlockSpec((S5_BLK, INP_TM // S5_T, S5_CW), lambda i, j: (0, i, 0))),
        scratch_shapes=[pltpu.VMEM((S5_BLK, INP_TM, LANES), F32)],
        compiler_params=_cparams(("parallel", "arbitrary")),
        name="inproj",
    )(h, w_in_bf)


def _s5_tab_kernel(lam_ref, bre_ref, bim_ref, cre_ref, cim_ref, dsk_ref, m_ref, win_ref, wout_ref, pow_ref):
    d = pl.program_id(1)
    fwd = d == 0
    lam_re = lam_ref[0, 0, 0, 0:1, :]
    lam_im = lam_ref[0, 0, 0, 1:2, :]
    dt = jnp.exp(lam_ref[0, 0, 0, 2:3, :])

    def apow(j):
        mag = jnp.exp(lam_re * dt * float(j))
        ang = lam_im * dt * float(j)
        return mag * jnp.cos(ang), mag * jnp.sin(ang)

    a_re, a_im = apow(1)
    den = lam_re * lam_re + lam_im * lam_im
    num_re = a_re - 1.0
    coef_re = (num_re * lam_re + a_im * lam_im) / den
    coef_im = (a_im * lam_re - num_re * lam_im) / den
    ri = lax.broadcasted_iota(jnp.int32, (S5_STATE, S5_SW), 0)
    ci = lax.broadcasted_iota(jnp.int32, (S5_STATE, S5_SW), 1)
    repeat = jnp.where(ri == (ci & (S5_STATE - 1)), 1.0, 0.0).astype(BF16)
    own = (jnp.right_shift(lax.broadcasted_iota(jnp.int32, (LANES, S5_SW), 0), S5_GROUP.bit_length() - 1)
           == jnp.right_shift(lax.broadcasted_iota(jnp.int32, (LANES, S5_SW), 1), S5_STATE.bit_length() - 1))

    def block_diag(ref):
        return jnp.where(own, sum(_dot(piece, repeat) for piece in _split3(ref[0, 0, 0])), 0.0)

    b_re = block_diag(bre_ref)
    b_im = block_diag(bim_ref)
    bb_re = coef_re * b_re - coef_im * b_im
    bb_im = coef_re * b_im + coef_im * b_re
    bbr = _split3(bb_re)
    bbi = _split3(bb_im)
    c_re = block_diag(cre_ref)
    c_im = block_diag(cim_ref)

    kmat = []
    for j in range(S5_T + 1):
        aj_re, aj_im = apow(j)
        pc_re = c_re * aj_re - c_im * aj_im
        pc_im = c_re * aj_im + c_im * aj_re
        if j >= 1:
            r = pl.multiple_of(jnp.where(fwd, j - 1, S5_T - j) * LANES, LANES)
            wout_ref[0, 0, 0, pl.ds(r, LANES), :] = jnp.concatenate([pc_re, -pc_im], axis=1).astype(BF16)
        if j < S5_T:
            pb_re = bb_re * aj_re - bb_im * aj_im
            pb_im = bb_re * aj_im + bb_im * aj_re
            r = pl.multiple_of(jnp.where(fwd, S5_T - 1 - j, j) * LANES, LANES)
            win_ref[0, 0, 0, pl.ds(r, LANES), :] = jnp.concatenate([pb_re, pb_im], axis=1).astype(BF16)
            kmat.append(_dot_nt_hi(bbr, _split3(pc_re)) - _dot_nt_hi(bbi, _split3(pc_im)))

    ri = lax.broadcasted_iota(jnp.int32, (LANES, LANES), 0)
    ci = lax.broadcasted_iota(jnp.int32, (LANES, LANES), 1)
    keep_upper = jnp.where(fwd, 1.0, 0.0)
    kmat[0] = kmat[0] + jnp.where(ri == ci, dsk_ref[0, 0], 0.0) * keep_upper
    for s in range(S5_T):
        for t in range(S5_T):
            tile = kmat[abs(t - s)]
            if t > s:
                tile = tile * keep_upper
            elif t < s:
                tile = tile * (1.0 - keep_upper)
            m_ref[0, 0, 0, s * LANES:(s + 1) * LANES, t * LANES:(t + 1) * LANES] = tile.astype(BF16)

    pow_ref[...] = jnp.zeros_like(pow_ref)
    for i in range(SUBLANES):
        for row, j in ((i, S5_T * (i + 1)), (SUBLANES + i, S5_T * (SUBLANES - i))):
            p_re, p_im = apow(j)
            pow_ref[0, 0, 0, 0, row:row + 1, :] = p_re
            pow_ref[0, 0, 0, 1, row:row + 1, :] = p_im
    p_re, p_im = apow(S5_T * 2 * SUBLANES)
    pow_ref[0, 0, 0, 0, 2 * SUBLANES:2 * SUBLANES + 1, :] = p_re
    pow_ref[0, 0, 0, 1, 2 * SUBLANES:2 * SUBLANES + 1, :] = p_im


def _s5_tables(lam, b_re, b_im, c_re, c_im, d_skip):
    lead = (DEPTH, 2, S5_BLK)

    def spec(*tail):
        return pl.BlockSpec((1, 1, 1) + tail, lambda l, d, b: (l, d, b) + (0,) * len(tail))

    return pl.pallas_call(
        _s5_tab_kernel,
        out_shape=(
            jax.ShapeDtypeStruct(lead + (S5_CW, S5_CW), BF16),
            jax.ShapeDtypeStruct(lead + (S5_CW, 2 * S5_SW), BF16),
            jax.ShapeDtypeStruct(lead + (S5_CW, 2 * S5_SW), BF16),
            jax.ShapeDtypeStruct(lead + (2, S5_POW_ROWS, S5_SW), F32),
        ),
        grid=lead,
        in_specs=[spec(3, S5_SW)] + [spec(LANES, S5_STATE)] * 4
        + [pl.BlockSpec((1, 1, 1, LANES), lambda l, d, b: (l, b, 0, 0))],
        out_specs=(spec(S5_CW, S5_CW), spec(S5_CW, 2 * S5_SW), spec(S5_CW, 2 * S5_SW),
                   spec(2, S5_POW_ROWS, S5_SW)),
        compiler_params=_cparams(("parallel", "parallel", "parallel")),
        name="s5_tables",
    )(lam, b_re, b_im, c_re, c_im, d_skip)


def _s5_param_layout(s5_lam_re, s5_lam_im, s5_log_dt, s5_b_re, s5_b_im, s5_c_re, s5_c_im, s5_d):
    lam = jnp.stack([s5_lam_re, s5_lam_im, jnp.broadcast_to(s5_log_dt[..., None], s5_lam_re.shape)], axis=2)
    lam = lam.reshape(DEPTH, 2, 3, S5_BLK, S5_SW).transpose(0, 1, 3, 2, 4)

    def rows_b(b):
        return jnp.swapaxes(b, -1, -2).reshape(DEPTH, 2, S5_BLK, LANES, S5_STATE)

    def rows_c(c):
        return c.reshape(DEPTH, 2, S5_BLK, LANES, S5_STATE)

    return (lam, rows_b(s5_b_re), rows_b(s5_b_im), rows_c(s5_c_re), rows_c(s5_c_im),
            s5_d.reshape(DEPTH, S5_BLK, 1, LANES))


def _s5_kernel(u_ref, m_ref, win_ref, wout_ref, pow_ref, h0_ref, y_ref, fin_ref, x_sc):
    d = pl.program_id(1)
    u = u_ref[0]
    x_sc[...] = _dot(u, win_ref[0, 0, 0])
    p_re = pow_ref[0, 0, 0, 0]
    p_im = pow_ref[0, 0, 0, 1]
    re = slice(0, S5_SW)
    im = slice(S5_SW, 2 * S5_SW)

    def madd(t_re, t_im, w_re, w_im, s_re, s_im):
        return t_re + w_re * s_re - w_im * s_im, t_im + w_re * s_im + w_im * s_re

    def shift_rows(val, k, n, idx, reverse):
        if reverse:
            return jnp.where(idx < n - k, pltpu.roll(val, val.shape[0] - k, axis=0), 0.0)
        return jnp.where(idx >= k, pltpu.roll(val, k, axis=0), 0.0)

    def pow_row(k):
        row = k - 1 if k <= SUBLANES else 2 * SUBLANES
        return p_re[row:row + 1], p_im[row:row + 1]

    def ctx_scan(reverse):
        s_re = x_sc[0:S5_ROWS_CTX, re]
        s_im = x_sc[0:S5_ROWS_CTX, im]
        c = lax.broadcasted_iota(jnp.int32, (S5_ROWS_CTX, 1), 0) & (CTX_CHUNKS - 1)
        def shift_ctx(val, k):
            if k % SUBLANES:
                return shift_rows(val, k, CTX_CHUNKS, c, reverse)
            v3 = val.reshape(N_CTX_SEQ, CTX_CHUNKS, S5_SW)
            pad = jnp.zeros((N_CTX_SEQ, k, S5_SW), F32)
            moved = (jnp.concatenate([v3[:, k:], pad], axis=1) if reverse
                     else jnp.concatenate([pad, v3[:, :CTX_CHUNKS - k]], axis=1))
            return moved.reshape(S5_ROWS_CTX, S5_SW)

        k = 1
        while k < CTX_CHUNKS:
            w_re, w_im = pow_row(k)
            s_re, s_im = madd(s_re, s_im, w_re, w_im, shift_ctx(s_re, k), shift_ctx(s_im, k))
            k *= 2
        last = 0 if reverse else CTX_CHUNKS - 1
        pick = (lax.broadcasted_iota(jnp.int32, (N_CTX_SEQ, S5_ROWS_CTX), 1)
                == lax.broadcasted_iota(jnp.int32, (N_CTX_SEQ, S5_ROWS_CTX), 0) * CTX_CHUNKS + last)
        pick = jnp.where(pick, 1.0, 0.0).astype(BF16)
        fin_ref[0, 0, :, re] = sum(_dot(pick, piece) for piece in _split3(s_re))
        fin_ref[0, 0, :, im] = sum(_dot(pick, piece) for piece in _split3(s_im))
        x_sc[0:S5_ROWS_CTX, re] = shift_rows(s_re, 1, CTX_CHUNKS, c, reverse)
        x_sc[0:S5_ROWS_CTX, im] = shift_rows(s_im, 1, CTX_CHUNKS, c, reverse)

    def lat_group(k, carry, reverse):
        g = (S5_LAT_GROUPS - 1 - k) if reverse else k
        r8 = lax.broadcasted_iota(jnp.int32, (SUBLANES, 1), 0)
        tab = slice(SUBLANES, 2 * SUBLANES) if reverse else slice(0, SUBLANES)
        out = []
        for q in range(N_LAT_SEQ):
            base = pl.multiple_of(S5_ROWS_CTX + q * LAT_CHUNKS + g * SUBLANES, SUBLANES)
            s_re = x_sc[pl.ds(base, SUBLANES), re]
            s_im = x_sc[pl.ds(base, SUBLANES), im]
            j = 1
            while j < SUBLANES:
                w_re, w_im = pow_row(j)
                s_re, s_im = madd(s_re, s_im, w_re, w_im,
                                  shift_rows(s_re, j, SUBLANES, r8, reverse), shift_rows(s_im, j, SUBLANES, r8, reverse))
                j *= 2
            c_re, c_im = carry[2 * q], carry[2 * q + 1]
            s_re, s_im = madd(s_re, s_im, p_re[tab], p_im[tab], c_re, c_im)
            edge = (r8 == SUBLANES - 1) if reverse else (r8 == 0)
            x_sc[pl.ds(base, SUBLANES), re] = jnp.where(edge, c_re, shift_rows(s_re, 1, SUBLANES, r8, reverse))
            x_sc[pl.ds(base, SUBLANES), im] = jnp.where(edge, c_im, shift_rows(s_im, 1, SUBLANES, r8, reverse))
            end = slice(0, 1) if reverse else slice(SUBLANES - 1, SUBLANES)
            out += [s_re[end], s_im[end]]
        return tuple(out)

    def run(reverse):
        for t0 in range(0, S5_T, 2):
            src = slice(t0 * LANES, S5_CW) if reverse else slice(0, (t0 + 2) * LANES)
            dst = slice(t0 * LANES, (t0 + 2) * LANES)
            part = _dot(u[:, src], m_ref[0, 0, 0, src, dst])
            if reverse:
                y_ref[0, :, dst] += part
            else:
                y_ref[0, :, dst] = part
        ctx_scan(reverse)
        h0 = h0_ref[0, 0]
        init = []
        for q in range(N_LAT_SEQ):
            init += [h0[q:q + 1, re], h0[q:q + 1, im]]
        lax.fori_loop(0, S5_LAT_GROUPS, functools.partial(lat_group, reverse=reverse), tuple(init))

    @pl.when(d == 0)
    def _():
        run(False)

    @pl.when(d == 1)
    def _():
        run(True)

    y_ref[0] += _dot_nt(x_sc[...].astype(BF16), wout_ref[0, 0, 0])


def _s5_main(u, m, w_in, w_out, pows, h0, layer):
    def tab(*tail):
        return pl.BlockSpec((1, 1, 1) + tail, lambda b, d: (layer, d, b) + (0,) * len(tail))

    return pl.pallas_call(
        _s5_kernel,
        out_shape=(jax.ShapeDtypeStruct((S5_BLK, S5_ROWS, S5_CW), F32),
                   jax.ShapeDtypeStruct((2, S5_BLK, N_CTX_SEQ, 2 * S5_SW), F32)),
        grid=(S5_BLK, 2),
        in_specs=[
            pl.BlockSpec((1, S5_ROWS, S5_CW), lambda b, d: (b, 0, 0)),
            tab(S5_CW, S5_CW), tab(S5_CW, 2 * S5_SW), tab(S5_CW, 2 * S5_SW), tab(2, S5_POW_ROWS, S5_SW),
            pl.BlockSpec((1, 1, N_LAT_SEQ, 2 * S5_SW), lambda b, d: (d, b, 0, 0)),
        ],
        out_specs=(
            pl.BlockSpec((1, S5_ROWS, S5_CW), lambda b, d: (b, 0, 0)),
            pl.BlockSpec((1, 1, N_CTX_SEQ, 2 * S5_SW), lambda b, d: (d, b, 0, 0)),
        ),
        scratch_shapes=[pltpu.VMEM((S5_ROWS, 2 * S5_SW), F32)],
        compiler_params=_cparams(("parallel", "arbitrary")),
        name="s5",
    )(u, m, w_in, w_out, pows, h0)


def _hg_constants():
    t = np.arange(HG_C)
    ae = np.zeros((2, HG_E_ROWS, HG_C), np.float32)
    mask = np.zeros((2, HG_LEVELS, HG_C, HG_C), np.float32)
    for lvl in range(HG_LEVELS):
        half = 1 << lvl
        pos = t % (2 * half)
        mid = t - pos + half - 1
        upper = pos >= half
        u = t[None, :]
        rows_upper = (u > mid[:, None]) & (u <= t[:, None])
        rows_lower = (u > t[:, None]) & (u <= mid[:, None])
        ae[0, lvl * HG_C:(lvl + 1) * HG_C] = np.where(upper[:, None], rows_upper, rows_lower)
        same = (t[:, None] // (2 * half)) == (t[None, :] // (2 * half))
        mask[0, lvl] = same & upper[:, None] & (~upper)[None, :]
    ae[0, HG_LEVELS * HG_C:(HG_LEVELS + 1) * HG_C] = t[None, :] <= t[:, None]
    ae[0, (HG_LEVELS + 1) * HG_C:] = 1.0
    ae[1] = ae[0][:, ::-1]
    ae[1, :(HG_LEVELS + 1) * HG_C] = ae[1, :(HG_LEVELS + 1) * HG_C].reshape(HG_LEVELS + 1, HG_C, HG_C)[:, ::-1].reshape(-1, HG_C)
    mask[1] = mask[0][:, ::-1, ::-1]
    ae2 = np.concatenate([ae, ae], axis=-1)
    mask2 = np.concatenate([mask, mask], axis=-1)
    pair_diag = np.kron(np.eye(2, dtype=np.float32), np.ones((HG_DK, HG_DK), np.float32))
    return ae2, mask2, pair_diag


def _hg_tile(direction, i):
    return i if direction == 0 else HG_TILES - 1 - i


HG_PAIR = 2 * HG_DK


def _pair_diag(x):
    zero = jnp.zeros((x.shape[0], HG_DK), x.dtype)
    return jnp.concatenate([jnp.concatenate([x[:, :HG_DK], zero], axis=1),
                            jnp.concatenate([zero, x[:, HG_DK:]], axis=1)], axis=0)


def _hg_chunk_prep(direction, rows, q_ref, f_ref, v_ref, lb, ae_ref):
    f = lb + (1.0 - lb) * jax.nn.sigmoid(f_ref[rows, :])
    logf = jnp.log2(f)
    kk = 1.0 - f
    hi = logf.astype(BF16)
    lo = (logf - hi.astype(F32)).astype(BF16)
    ex = _dot(ae_ref[direction], jnp.concatenate([hi, lo], axis=0))
    dec = jnp.exp2(ex[:(HG_LEVELS + 1) * HG_C])
    cum = ex[HG_LEVELS * HG_C:(HG_LEVELS + 1) * HG_C]
    tot = ex[(HG_LEVELS + 1) * HG_C:(HG_LEVELS + 1) * HG_C + 1]
    q = q_ref[rows, :].astype(F32)
    v_bf = v_ref[rows, :]
    v = v_bf.astype(F32)
    q_in = (q * dec[HG_LEVELS * HG_C:]).astype(BF16)
    k_tail = (kk * jnp.exp2(tot - cum)).astype(BF16)
    dec_tot = jnp.exp2(tot)
    return dec, q, kk, v, v_bf, q_in, k_tail, dec_tot


def _hg_chunk_scores(direction, prep, mask_ref):
    dec, q, kk, v, v_bf, q_in, k_tail, dec_tot = prep
    qk = q * kk
    pairs = [slice(p * HG_PAIR, (p + 1) * HG_PAIR) for p in range(HG_HEADS // 2)]
    scores = [jnp.zeros((HG_C, HG_PAIR), F32) for _ in pairs]
    for lvl in range(HG_LEVELS):
        for p, sl in enumerate(pairs):
            g = dec[lvl * HG_C:(lvl + 1) * HG_C, sl]
            scores[p] = scores[p] + mask_ref[direction, lvl] * _dot_nt(
                (q[:, sl] * g).astype(BF16), _pair_diag((kk[:, sl] * g).astype(BF16)))
    local = []
    for p, sl in enumerate(pairs):
        diag = jnp.concatenate(
            [jnp.broadcast_to(jnp.sum(qk[:, (2 * p + j) * HG_DK:(2 * p + j + 1) * HG_DK], axis=-1, keepdims=True),
                              (HG_C, HG_DK)) for j in range(2)], axis=-1)
        local.append(_dot(scores[p].astype(BF16), _pair_diag(v_bf[:, sl])) + diag * v[:, sl])
    return local, q_in, k_tail, dec_tot, v_bf


def _hg_chunk_state(direction, rows, chunk, pd_ref, o_ref, st_sc):
    local, q_in, k_tail, dec_tot, v_bf = chunk
    outs = []
    for p in range(HG_HEADS // 2):
        sl = slice(p * HG_PAIR, (p + 1) * HG_PAIR)
        st = st_sc[direction, p]
        outs.append(local[p] + _dot_nt(q_in[:, sl], st.astype(BF16)))
        st_sc[direction, p] = st * dec_tot[:, sl] + pd_ref[...] * _dot_tn(v_bf[:, sl], k_tail[:, sl])
    o_ref[rows, :] = jnp.concatenate(outs, axis=-1)


def _hg_kernel(qf_ref, ff_ref, vf_ref, qb_ref, fb_ref, vb_ref, lb_ref, ae_ref, mask_ref, pd_ref, h0f_ref, h0b_ref,
               of_ref, ob_ref, finf_ref, finb_ref, st_sc, *, layer):
    i = pl.program_id(0)
    x = lb_ref[:, :, 0, :]
    e = jnp.exp(x - jnp.max(x, axis=0, keepdims=True))
    sm = e / jnp.sum(e, axis=0, keepdims=True)
    lb = jnp.sum(sm[1:layer + 1], axis=0) if layer > 0 else jnp.zeros((2, D_BRANCH), F32)
    for direction, h0_ref in ((0, h0f_ref), (1, h0b_ref)):
        tile = _hg_tile(direction, i)
        first_ctx = 0 if direction == 0 else HG_TILES_PER_CTX - 1
        first_lat = 0 if direction == 0 else HG_TILES_PER_LAT - 1
        is_start = jnp.where(tile < HG_CTX_TILES,
                             tile % HG_TILES_PER_CTX == first_ctx,
                             (tile - HG_CTX_TILES) % HG_TILES_PER_LAT == first_lat)

        @pl.when(is_start)
        def _(direction=direction, h0_ref=h0_ref):
            for p in range(HG_HEADS // 2):
                st_sc[direction, p] = _pair_diag(
                    jnp.concatenate([h0_ref[0, 0, 2 * p], h0_ref[0, 0, 2 * p + 1]], axis=1))

    work = []
    for c in range(HG_TM // HG_C):
        fwd_rows = slice(c * HG_C, (c + 1) * HG_C)
        bwd_rows = slice(HG_TM - (c + 1) * HG_C, HG_TM - c * HG_C)
        for direction, rows, refs, o_ref in ((0, fwd_rows, (qf_ref, ff_ref, vf_ref), of_ref),
                                             (1, bwd_rows, (qb_ref, fb_ref, vb_ref), ob_ref)):
            prep = _hg_chunk_prep(direction, rows, *refs, lb[direction:direction + 1], ae_ref)
            work.append((direction, rows, o_ref, _hg_chunk_scores(direction, prep, mask_ref)))
    for direction, rows, o_ref, chunk in work:
        _hg_chunk_state(direction, rows, chunk, pd_ref, o_ref, st_sc)

    for direction, fin_ref in ((0, finf_ref), (1, finb_ref)):
        @pl.when(_hg_tile(direction, i) < HG_CTX_TILES)
        def _(direction=direction, fin_ref=fin_ref):
            for h in range(HG_HEADS):
                p, j = divmod(h, 2)
                fin_ref[0, h] = st_sc[direction, p, j * HG_DK:(j + 1) * HG_DK, j * HG_DK:(j + 1) * HG_DK]


def _hgrn(proj, fz, hg_lb, ae, mask, pair_diag, h0t, layer):
    def tok_spec(direction, col):
        return pl.BlockSpec((HG_TM, D_BRANCH), lambda i: (_hg_tile(direction, i), col))

    def fin_spec(direction):
        return pl.BlockSpec((1, HG_HEADS, HG_DK, HG_DK),
                            lambda i: (jnp.minimum(_hg_tile(direction, i) // HG_TILES_PER_CTX, N_CTX_SEQ - 1), 0, 0, 0))

    def h0_spec(direction):
        return pl.BlockSpec((1, 1, HG_HEADS, HG_DK, HG_DK),
                            lambda i: (_cond_row(_hg_tile(direction, i), HG_TM), direction, 0, 0, 0))

    fin_shape = jax.ShapeDtypeStruct((N_CTX_SEQ, HG_HEADS, HG_DK, HG_DK), F32)
    return pl.pallas_call(
        functools.partial(_hg_kernel, layer=layer),
        out_shape=(jax.ShapeDtypeStruct((N_TOK, D_BRANCH), F32), jax.ShapeDtypeStruct((N_TOK, D_BRANCH), F32),
                   fin_shape, fin_shape),
        grid=(HG_TILES,),
        in_specs=[
            tok_spec(0, HG_Q_COL), tok_spec(0, 0), tok_spec(0, HG_V_COL),
            tok_spec(1, HG_Q_COL), tok_spec(1, 1), tok_spec(1, HG_V_COL),
            pl.BlockSpec((DEPTH, 2, 1, D_BRANCH), lambda i: (0, 0, 0, 0)),
            pl.BlockSpec((2, HG_E_ROWS, 2 * HG_C), lambda i: (0, 0, 0)),
            pl.BlockSpec((2, HG_LEVELS, HG_C, HG_PAIR), lambda i: (0, 0, 0, 0)),
            pl.BlockSpec((HG_PAIR, HG_PAIR), lambda i: (0, 0)),
            h0_spec(0), h0_spec(1),
        ],
        out_specs=(
            pl.BlockSpec((HG_TM, D_BRANCH), lambda i: (_hg_tile(0, i), 0)),
            pl.BlockSpec((HG_TM, D_BRANCH), lambda i: (_hg_tile(1, i), 0)),
            fin_spec(0), fin_spec(1),
        ),
        scratch_shapes=[pltpu.VMEM((2, HG_HEADS // 2, HG_PAIR, HG_PAIR), F32)],
        compiler_params=_cparams(("arbitrary",)),
        name="hgrn",
    )(proj, fz, proj, proj, fz, proj, hg_lb.reshape(DEPTH, 2, 1, D_BRANCH), ae, mask, pair_diag, h0t, h0t)


MRG_TM = 256


def _gelu_tanh(x):
    return 0.5 * x * (1.0 + jnp.tanh(0.7978845608028654 * (x + 0.044715 * (x * x * x))))


def _merge_kernel(scb_ref, scc_ref, sch_ref, hgg_ref, pool_ref, g0_ref, g1_ref, g2_ref, g3_ref,
                  yrow_ref, of_ref, ob_ref, x_ref, mod_ref, conv_ref, gluw_ref, glub_ref, ng_ref,
                  poolw_ref, pscale_ref, wbr_ref, wout_ref, lng_ref, lnb_ref, o_ref,
                  ys5_sc, gluw_sc, poolw_sc, wbr_sc, wout_sc):
    tile = pl.program_id(0)

    @pl.when(tile == 0)
    def _():
        gluw_sc[...] = gluw_ref[0].astype(BF16)
        poolw_sc[...] = poolw_ref[0].astype(BF16)
        wbr_sc[...] = wbr_ref[0].astype(BF16)
        wout_sc[...] = wout_ref[0].astype(BF16)

    line = jnp.where(tile * MRG_TM < N_CTX_TOK, CTX_LEN, GRID_W)
    pos = lax.broadcasted_iota(jnp.int32, (MRG_TM, 1), 0) & (line - 1)

    def shifted(val, k):
        rolled = pltpu.roll(val, k % MRG_TM, axis=0)
        ok = (pos >= k) if k > 0 else (pos < line + k)
        return jnp.where(ok, rolled, 0.0)

    for b in range(S5_BLK):
        for t in range(S5_T):
            ys5_sc[b, pl.ds(t, MRG_TM // S5_T, stride=S5_T), :] = yrow_ref[b, :, t * LANES:(t + 1) * LANES]

    m = scc_ref[...].astype(F32) * sch_ref[...].astype(F32)
    conv = conv_ref[0, 0:1, :] * shifted(m, 1) + conv_ref[0, 1:2, :] * m + conv_ref[0, 2:3, :] * shifted(m, -1)
    ya = scb_ref[...].astype(F32) * conv

    def gated_projection(k, branch, gate_ref):
        return gate_ref[...].astype(F32) * _dot(branch.astype(BF16), wbr_sc[k])

    merged = gated_projection(0, ya, g0_ref)

    z = _gelu_tanh(jnp.concatenate([ys5_sc[b] for b in range(S5_BLK)], axis=-1))
    yb = z * _sigmoid_tanh(_dot(z.astype(BF16), gluw_sc[...]) + glub_ref[0])
    merged = merged + gated_projection(1, yb, g1_ref)

    o = of_ref[...] + ob_ref[...]
    normed = []
    for h in range(HG_HEADS):
        oh = o[:, h * HG_DK:(h + 1) * HG_DK]
        ms = jnp.mean(oh * oh, axis=-1, keepdims=True)
        normed.append(oh * lax.rsqrt(ms + LN_EPS) * ng_ref[0])
    gate_c = hgg_ref[...].astype(F32)
    yc = jnp.concatenate(normed, axis=-1) * (gate_c * _sigmoid_tanh(gate_c))
    merged = merged + gated_projection(2, yc, g2_ref)

    pu = pool_ref[...].astype(F32)
    posf = pos.astype(F32)
    linef = line.astype(F32)
    pooled = []
    for gi, w in enumerate(POOL_WINDOWS):
        vg = pu[:, gi * POOL_GROUP:(gi + 1) * POOL_GROUP]
        back, fwd, span = vg, vg, 1
        while 2 * span <= w // 2:
            back = back + shifted(back, span)
            fwd = fwd + shifted(fwd, -span)
            span *= 2
        s = shifted(back, 1) + fwd
        cnt = jnp.minimum(posf + w // 2, linef) - jnp.maximum(posf - w // 2, 0.0)
        pg = s / cnt - vg
        pooled.append(_dot(pg.astype(BF16), poolw_sc[gi]))
    yd = jnp.concatenate(pooled, axis=-1) * pscale_ref[0]

    merged = merged + gated_projection(3, yd, g3_ref)
    mix = _dot(merged.astype(BF16), wout_sc[...])
    y = ALPHA * x_ref[...] + mod_ref[0, 0, 5:6, :] * mix
    o_ref[...] = _layer_norm(y, lng_ref[0, 0], lnb_ref[0, 0])


def _merge(proj, y_rows, o_f, o_b, x, mod, conv_w, glu_w, glu_b, norm_g, pool_w, pool_scale, w_branch, w_out,
           ln_g, ln_b, layer):
    def part(col):
        return pl.BlockSpec((MRG_TM, D_BRANCH), lambda i: (i, col))

    def gate(k):
        return pl.BlockSpec((MRG_TM, D_MODEL), lambda i: (i, N_IN_PARTS * D_BRANCH // D_MODEL + k))

    def per_layer(*shape, single=False):
        mode = dict(pipeline_mode=pl.Buffered(1)) if single else {}
        return pl.BlockSpec((1,) + shape, lambda i: (layer,) + (0,) * len(shape), **mode)

    return pl.pallas_call(
        _merge_kernel,
        out_shape=jax.ShapeDtypeStruct((N_TOK, D_MODEL), F32),
        grid=(N_TOK // MRG_TM,),
        in_specs=[
            part(0), part(1), part(2), part(8), part(9), gate(0), gate(1), gate(2), gate(3),
            pl.BlockSpec((S5_BLK, MRG_TM // S5_T, S5_CW), lambda i: (0, i, 0)),
            pl.BlockSpec((MRG_TM, D_BRANCH), lambda i: (i, 0)),
            pl.BlockSpec((MRG_TM, D_BRANCH), lambda i: (i, 0)),
            pl.BlockSpec((MRG_TM, D_MODEL), lambda i: (i, 0)),
            _mod_spec(layer, MRG_TM),
            per_layer(3, D_BRANCH), per_layer(D_BRANCH, D_BRANCH, single=True), per_layer(1, D_BRANCH),
            per_layer(1, HG_DK), per_layer(len(POOL_WINDOWS), POOL_GROUP, POOL_GROUP, single=True),
            per_layer(1, D_BRANCH), per_layer(N_BRANCH, D_BRANCH, D_MODEL, single=True),
            per_layer(D_MODEL, D_MODEL, single=True), _ln_spec(layer, 1), _ln_spec(layer, 1),
        ],
        out_specs=pl.BlockSpec((MRG_TM, D_MODEL), lambda i: (i, 0)),
        scratch_shapes=[
            pltpu.VMEM((S5_BLK, MRG_TM, LANES), F32),
            pltpu.VMEM((D_BRANCH, D_BRANCH), BF16),
            pltpu.VMEM((len(POOL_WINDOWS), POOL_GROUP, POOL_GROUP), BF16),
            pltpu.VMEM((N_BRANCH, D_BRANCH, D_MODEL), BF16),
            pltpu.VMEM((D_MODEL, D_MODEL), BF16),
        ],
        compiler_params=_cparams(("arbitrary",)),
        name="merge",
    )(proj, proj, proj, proj, proj, proj, proj, proj, proj, y_rows, o_f, o_b, x, mod,
      conv_w, glu_w, glu_b.reshape(DEPTH, 1, D_BRANCH), norm_g.reshape(DEPTH, 1, HG_DK), pool_w,
      pool_scale.reshape(DEPTH, 1, D_BRANCH), w_branch, w_out, ln_g, ln_b)


def _grid_pos_embedding():
    rows = LAT_LEN // GRID_W
    quarter = D_MODEL // 4
    omega = POS_BASE ** (-jnp.arange(quarter, dtype=F32) / quarter)
    ar = jnp.arange(rows, dtype=F32)[:, None] * omega
    ac = jnp.arange(GRID_W, dtype=F32)[:, None] * omega
    row_tab = jnp.concatenate([jnp.sin(ar), jnp.cos(ar)], -1)[:, None, :]
    col_tab = jnp.concatenate([jnp.sin(ac), jnp.cos(ac)], -1)[None, :, :]
    shape = (rows, GRID_W, 2 * quarter)
    pos = jnp.concatenate([jnp.broadcast_to(row_tab, shape), jnp.broadcast_to(col_tab, shape)], -1)
    return pos.reshape(LAT_LEN, D_MODEL)


def kernel(x_prompt, x_sample, state_s5_re, state_s5_im, state_hgrn, c, c_ctx, w_ada, b_ada, ln_g, ln_b,
           ffn_w1, ffn_w3, ffn_w2, w_in, sc_conv, s5_lam_re, s5_lam_im, s5_log_dt, s5_b_re, s5_b_im,
           s5_c_re, s5_c_im, s5_d, s5_glu_w, s5_glu_b, hg_lb, hg_norm_g, pool_w, pool_scale, w_branch, w_out):
    x = (x_prompt.reshape(N_CTX_TOK, D_MODEL), x_sample.reshape(N_LAT_TOK, D_MODEL), _grid_pos_embedding())

    cond = jnp.zeros((COND_PAD, D_MODEL), F32).at[0].set(c_ctx).at[1:N_COND].set(c)
    mod = _ada(cond, w_ada, b_ada).reshape(DEPTH, COND_PAD, N_SUB * 3, D_MODEL)
    ln_g4 = ln_g.reshape(DEPTH, N_SUB, 1, D_MODEL)
    ln_b4 = ln_b.reshape(DEPTH, N_SUB, 1, D_MODEL)

    s5_m, s5_win, s5_wout, s5_pow = _s5_tables(*_s5_param_layout(
        s5_lam_re, s5_lam_im, s5_log_dt, s5_b_re, s5_b_im, s5_c_re, s5_c_im, s5_d))

    w_in_bf = _cast_w_in(w_in)

    ae_np, mask_np, pair_np = _hg_constants()
    ae = jnp.asarray(ae_np, BF16)
    mask = jnp.asarray(mask_np, F32)
    pair_diag = jnp.asarray(pair_np, F32)

    fin_s5, fin_hg = [], []
    for l in range(DEPTH):
        x, h = _ffn(x, mod, ffn_w1, ffn_w3, ffn_w2, ln_g4, ln_b4, l, 0)
        proj, fz, u_rows = _inproj(h, w_in_bf, l)

        h0 = jnp.concatenate([
            state_s5_re[:, l].reshape(N_LAT_SEQ, 2, S5_BLK, S5_SW),
            state_s5_im[:, l].reshape(N_LAT_SEQ, 2, S5_BLK, S5_SW)], axis=-1).transpose(1, 2, 0, 3)
        y_rows, fin = _s5_main(u_rows, s5_m, s5_win, s5_wout, s5_pow, h0, l)
        fin_s5.append(fin)

        h0t = jnp.concatenate([jnp.zeros((1, 2, HG_HEADS, HG_DK, HG_DK), F32),
                               jnp.swapaxes(state_hgrn[:, l], -1, -2)], axis=0)
        o_f, o_b, fin_f, fin_b = _hgrn(proj, fz, hg_lb, ae, mask, pair_diag, h0t, l)
        fin_hg.append(jnp.stack([fin_f, fin_b], axis=1))

        x = _merge(proj, y_rows, o_f, o_b, x, mod, sc_conv, s5_glu_w, s5_glu_b, hg_norm_g,
                   pool_w, pool_scale, w_branch, w_out, ln_g4, ln_b4, l)
        x = _ffn(x, mod, ffn_w1, ffn_w3, ffn_w2, ln_g4, ln_b4, l, 2, split_output=(l == DEPTH - 1))

    y_prompt = x[0].reshape(N_CTX_SEQ, CTX_LEN, D_MODEL)
    y_sample = x[1].reshape(N_LAT_SEQ, LAT_LEN, D_MODEL)
    fin = jnp.stack(fin_s5)
    fin = fin.reshape(DEPTH, 2, S5_BLK, N_CTX_SEQ, 2, S5_GPB, S5_STATE)
    fin = fin.transpose(4, 3, 0, 1, 2, 5, 6).reshape(2, N_CTX_SEQ, DEPTH, 2, S5_GROUPS, S5_STATE)
    new_hgrn = jnp.swapaxes(jnp.stack(fin_hg, axis=1), -1, -2)
    return y_prompt, y_sample, fin[0], fin[1], new_hgrn
```

```python
import functools

import numpy as np
import jax
import jax.numpy as jnp
from jax import lax
from jax.experimental import pallas as pl
from jax.experimental.pallas import tpu as pltpu

F32 = jnp.float32
BF16 = jnp.bfloat16

D_MODEL = 1024
N_CTX_SEQ = 16
CTX_LEN = 256
DEPTH = 2
N_LAT_SEQ = 2
LAT_LEN = 4096
GRID_W = 64
D_BRANCH = 512
N_BRANCH = 4
S5_GROUPS = 32
S5_GROUP = 16
S5_STATE = 64
HG_HEADS = 4
HG_DK = 128
POOL_WINDOWS = (2, 4, 8, 16)
POOL_GROUP = 128
D_FF = 2816
N_SUB = 3
N_IN_PARTS = 10
IN_WIDTH = N_IN_PARTS * D_BRANCH + N_BRANCH * D_MODEL
ADA_WIDTH = N_SUB * 3 * D_MODEL
ALPHA = (2 * DEPTH) ** 0.25
LN_EPS = 1e-5
POS_BASE = 10000.0

N_CTX_TOK = N_CTX_SEQ * CTX_LEN
N_LAT_TOK = N_LAT_SEQ * LAT_LEN
N_TOK = N_CTX_TOK + N_LAT_TOK
N_COND = 1 + N_LAT_SEQ
COND_PAD = 8

LANES = 128
SUBLANES = 8
VMEM_LIMIT = 58 * 1024 * 1024

S5_T = 8
S5_BLK = 4
S5_GPB = S5_GROUPS // S5_BLK
S5_SW = S5_GPB * S5_STATE
S5_CW = S5_T * LANES
CTX_CHUNKS = CTX_LEN // S5_T
LAT_CHUNKS = LAT_LEN // S5_T
S5_ROWS_CTX = N_CTX_TOK // S5_T
S5_ROWS_LAT = N_LAT_TOK // S5_T
S5_ROWS = S5_ROWS_CTX + S5_ROWS_LAT
S5_LAT_GROUPS = LAT_CHUNKS // SUBLANES
S5_POW_ROWS = 3 * SUBLANES
S5_U_COL = 3

HG_C = 128
HG_LEVELS = 7
HG_TM = 256
HG_TILES = N_TOK // HG_TM
HG_CTX_TILES = N_CTX_TOK // HG_TM
HG_TILES_PER_CTX = CTX_LEN // HG_TM
HG_TILES_PER_LAT = LAT_LEN // HG_TM
HG_E_ROWS = (HG_LEVELS + 1) * HG_C + SUBLANES


def _cparams(sem):
    return pltpu.CompilerParams(dimension_semantics=sem, vmem_limit_bytes=VMEM_LIMIT)


def _cond_row(tile, tile_tokens):
    tok = tile * tile_tokens
    return jnp.where(tok < N_CTX_TOK, 0, 1 + (tok - N_CTX_TOK) // LAT_LEN)


def _dot(a, b):
    return jnp.dot(a, b, preferred_element_type=F32)


def _dot_nt(a, b):
    return lax.dot_general(a, b, (((1,), (1,)), ((), ())), preferred_element_type=F32)


def _dot_tn(a, b):
    return lax.dot_general(a, b, (((0,), (0,)), ((), ())), preferred_element_type=F32)


def _split3(x):
    h1 = x.astype(BF16)
    r1 = x - h1.astype(F32)
    h2 = r1.astype(BF16)
    h3 = (r1 - h2.astype(F32)).astype(BF16)
    return h1, h2, h3


def _dot_nt_hi(a3, b3):
    acc = None
    for x in range(2):
        for y in range(2 - x):
            term = _dot_nt(a3[x], b3[y])
            acc = term if acc is None else acc + term
    return acc


def _silu(x):
    return x * jax.nn.sigmoid(x)


def _sigmoid_tanh(x):
    return 0.5 * jnp.tanh(0.5 * x) + 0.5


def _layer_norm(y, g, b):
    mu = jnp.mean(y, axis=-1, keepdims=True)
    yc = y - mu
    var = jnp.mean(yc * yc, axis=-1, keepdims=True)
    return yc * lax.rsqrt(var + LN_EPS) * g + b


def _mod_spec(layer, tile_tokens):
    return pl.BlockSpec((1, 1, N_SUB * 3, D_MODEL),
                        lambda i, *_: (layer, _cond_row(i, tile_tokens), 0, 0))


def _ln_spec(layer, sub):
    return pl.BlockSpec((1, 1, 1, D_MODEL), lambda *_: (layer, sub, 0, 0))


ADA_TN = 1152


def _ada_kernel(c_ref, w_ref, b_ref, o_ref):
    s = _silu(c_ref[...]).astype(BF16)
    o_ref[0] = _dot(s, w_ref[0].astype(BF16)) + b_ref[0]


def _ada(cond, w_ada, b_ada):
    return pl.pallas_call(
        _ada_kernel,
        out_shape=jax.ShapeDtypeStruct((DEPTH, COND_PAD, ADA_WIDTH), F32),
        grid=(DEPTH, ADA_WIDTH // ADA_TN),
        in_specs=[
            pl.BlockSpec((COND_PAD, D_MODEL), lambda l, j: (0, 0)),
            pl.BlockSpec((1, D_MODEL, ADA_TN), lambda l, j: (l, 0, j)),
            pl.BlockSpec((1, 1, ADA_TN), lambda l, j: (l, 0, j)),
        ],
        out_specs=pl.BlockSpec((1, COND_PAD, ADA_TN), lambda l, j: (l, 0, j)),
        compiler_params=_cparams(("parallel", "parallel")),
        name="ada",
    )(cond, w_ada, b_ada.reshape(DEPTH, 1, ADA_WIDTH))


FFN_TM = 1024
FFN_TM_SPLIT = 512
FFN_TC = 256
FFN_CHUNKS = D_FF // FFN_TC


def _ffn_kernel(*refs, sub, tm, split_input, split_output, emit_next):
    step = pl.program_id(0)
    tile = step - FFN_CHUNKS
    is_ctx = tile < N_CTX_TOK // tm
    n_x = 3 if split_input else 1
    x_refs, refs = refs[:n_x], refs[n_x:]
    mod_ref, w1_ref, w3_ref, w2_ref, g_ref, b_ref = refs[:6]
    out_refs = refs[6:-4]
    w1_sc, w3_sc, w2_sc, act_sc = refs[-4:]

    def load_x():
        if split_input:
            ctx_ref, lat_ref, pos_ref = x_refs
            return jnp.where(is_ctx, ctx_ref[...], lat_ref[...] + pos_ref[...])
        return x_refs[0][...]

    def modulated(x):
        shift = mod_ref[0, 0, 3 * sub:3 * sub + 1, :]
        scale = mod_ref[0, 0, 3 * sub + 1:3 * sub + 2, :]
        return (x * (1.0 + scale) + shift).astype(BF16)

    @pl.when(step < FFN_CHUNKS)
    def _():
        w1 = w1_ref[0, 0].astype(BF16)
        w3 = w3_ref[0, 0].astype(BF16)
        w1_sc[step] = w1
        w3_sc[step] = w3
        w2_sc[pl.ds(pl.multiple_of(step * FFN_TC, FFN_TC), FFN_TC), :] = w2_ref[0, 0].astype(BF16)
        h = modulated(load_x())
        act_sc[:, pl.ds(pl.multiple_of(step * FFN_TC, FFN_TC), FFN_TC)] = (
            _silu(_dot(h, w1)) * _dot(h, w3)).astype(BF16)

    @pl.when(step > FFN_CHUNKS)
    def _():
        h = modulated(load_x())
        for c in range(FFN_CHUNKS):
            a = _dot(h, w1_sc[c])
            b = _dot(h, w3_sc[c])
            act_sc[:, c * FFN_TC:(c + 1) * FFN_TC] = (_silu(a) * b).astype(BF16)

    @pl.when(step >= FFN_CHUNKS)
    def _():
        x = load_x()
        gate = mod_ref[0, 0, 3 * sub + 2:3 * sub + 3, :]
        f = _dot(act_sc[...], w2_sc[...])
        y = _layer_norm(ALPHA * x + gate * (0.5 * f), g_ref[0, 0], b_ref[0, 0])
        if emit_next:
            nxt = 3 * (sub + 1)
            out_refs[1][...] = (y * (1.0 + mod_ref[0, 0, nxt + 1:nxt + 2, :])
                                + mod_ref[0, 0, nxt:nxt + 1, :]).astype(BF16)
        if split_output:
            @pl.when(is_ctx)
            def _():
                out_refs[0][...] = y

            @pl.when(jnp.logical_not(is_ctx))
            def _():
                out_refs[1][...] = y
        else:
            out_refs[0][...] = y


def _ffn(x, mod, w1, w3, w2, ln_g, ln_b, layer, sub, split_output=False):
    which = sub // 2
    split_input = isinstance(x, tuple)
    tm = FFN_TM_SPLIT if split_input else FFN_TM
    ctx_tiles = N_CTX_TOK // tm

    def chunk(s):
        return jnp.minimum(s, FFN_CHUNKS - 1)

    def tile(s):
        return jnp.maximum(s - FFN_CHUNKS, 0)

    def tok_spec(index):
        return pl.BlockSpec((tm, D_MODEL), lambda s: (index(tile(s)), 0))

    def ctx_tile(t):
        return jnp.minimum(t, ctx_tiles - 1)

    def lat_tile(t):
        return jnp.maximum(t - ctx_tiles, 0)

    if split_input:
        pos_tiles = LAT_LEN // tm
        x_specs = [tok_spec(ctx_tile), tok_spec(lat_tile), tok_spec(lambda t: lat_tile(t) % pos_tiles)]
    else:
        x_specs = [tok_spec(lambda t: t)]
        x = (x,)
    emit_next = sub == 0
    if split_output:
        out_shape = (jax.ShapeDtypeStruct((N_CTX_TOK, D_MODEL), F32), jax.ShapeDtypeStruct((N_LAT_TOK, D_MODEL), F32))
        out_specs = (tok_spec(ctx_tile), tok_spec(lat_tile))
    elif emit_next:
        out_shape = (jax.ShapeDtypeStruct((N_TOK, D_MODEL), F32), jax.ShapeDtypeStruct((N_TOK, D_MODEL), BF16))
        out_specs = (tok_spec(lambda t: t), tok_spec(lambda t: t))
    else:
        out_shape = jax.ShapeDtypeStruct((N_TOK, D_MODEL), F32)
        out_specs = tok_spec(lambda t: t)

    return pl.pallas_call(
        functools.partial(_ffn_kernel, sub=sub, tm=tm, split_input=split_input, split_output=split_output,
                          emit_next=emit_next),
        out_shape=out_shape,
        grid=(FFN_CHUNKS + N_TOK // tm,),
        in_specs=x_specs + [
            pl.BlockSpec((1, 1, N_SUB * 3, D_MODEL), lambda s: (layer, _cond_row(tile(s), tm), 0, 0)),
            pl.BlockSpec((1, 1, D_MODEL, FFN_TC), lambda s: (layer, which, 0, chunk(s))),
            pl.BlockSpec((1, 1, D_MODEL, FFN_TC), lambda s: (layer, which, 0, chunk(s))),
            pl.BlockSpec((1, 1, FFN_TC, D_MODEL), lambda s: (layer, which, chunk(s), 0)),
            _ln_spec(layer, sub), _ln_spec(layer, sub),
        ],
        out_specs=out_specs,
        scratch_shapes=[pltpu.VMEM((FFN_CHUNKS, D_MODEL, FFN_TC), BF16), pltpu.VMEM((FFN_CHUNKS, D_MODEL, FFN_TC), BF16),
                        pltpu.VMEM((D_FF, D_MODEL), BF16), pltpu.VMEM((tm, D_FF), BF16)],
        compiler_params=_cparams(("arbitrary",)),
        name=f"ffn{sub}",
    )(*x, mod, w1, w3, w2, ln_g, ln_b)


INP_TM = 2048
INP_PARTS = 2
INP_TN = INP_PARTS * D_BRANCH
INP_STEPS = IN_WIDTH // INP_TN
HG_Q_COL, HG_F_COL, HG_V_COL = 4, 5, 7


def _cast_w_in_kernel(w_ref, o_ref):
    o_ref[0, 0] = w_ref[0].astype(BF16)


def _cast_w_in(w_in):
    return pl.pallas_call(
        _cast_w_in_kernel,
        out_shape=jax.ShapeDtypeStruct((DEPTH, INP_STEPS, D_MODEL, INP_TN), BF16),
        grid=(DEPTH, INP_STEPS),
        in_specs=[pl.BlockSpec((1, D_MODEL, INP_TN), lambda l, j: (l, 0, j))],
        out_specs=pl.BlockSpec((1, 1, D_MODEL, INP_TN), lambda l, j: (l, j, 0, 0)),
        compiler_params=_cparams(("parallel", "parallel")),
        name="cast_w_in",
    )(w_in)


def _inproj_kernel(h_ref, w_ref, o_ref, fz_ref, u_ref, blk_sc):
    j = pl.program_id(1)

    def project(gate=False):
        res = _dot(h_ref[...], w_ref[0, 0])
        o_ref[...] = (_sigmoid_tanh(res) if gate else res).astype(BF16)
        return res

    def part(res, p):
        k = p % INP_PARTS
        return res[:, k * D_BRANCH:(k + 1) * D_BRANCH]

    special = (S5_U_COL, HG_F_COL, HG_F_COL + 1)
    steps = [p // INP_PARTS for p in special]
    assert len(set(steps)) == len(steps)

    first_gate = N_IN_PARTS // INP_PARTS
    assert N_IN_PARTS % INP_PARTS == 0 and max(steps) < first_gate

    @pl.when(functools.reduce(jnp.logical_and, [j != s for s in steps] + [j < first_gate]))
    def _():
        project()

    @pl.when(j >= first_gate)
    def _():
        project(gate=True)

    for p in (HG_F_COL, HG_F_COL + 1):
        @pl.when(j == p // INP_PARTS)
        def _(p=p):
            fz_ref[...] = part(project(), p)

    @pl.when(j == S5_U_COL // INP_PARTS)
    def _():
        s5_u = part(project(), S5_U_COL)
        for b in range(S5_BLK):
            blk_sc[b] = s5_u[:, b * LANES:(b + 1) * LANES]
            for t in range(S5_T):
                rows = blk_sc[b, pl.ds(t, INP_TM // S5_T, stride=S5_T), :]
                u_ref[b, :, t * LANES:(t + 1) * LANES] = rows.astype(BF16)


def _inproj(h, w_in_bf, layer):
    return pl.pallas_call(
        _inproj_kernel,
        out_shape=(jax.ShapeDtypeStruct((N_TOK, IN_WIDTH), BF16),
                   jax.ShapeDtypeStruct((N_TOK, 2 * D_BRANCH), F32),
                   jax.ShapeDtypeStruct((S5_BLK, S5_ROWS, S5_CW), BF16)),
        grid=(N_TOK // INP_TM, INP_STEPS),
        in_specs=[
            pl.BlockSpec((INP_TM, D_MODEL), lambda i, j: (i, 0)),
            pl.BlockSpec((1, 1, D_MODEL, INP_TN), lambda i, j: (layer, j, 0, 0)),
        ],
        out_specs=(pl.BlockSpec((INP_TM, INP_TN), lambda i, j: (i, j)),
                   pl.BlockSpec((INP_TM, D_BRANCH),
                                lambda i, j: (i, jnp.where(j > HG_F_COL // INP_PARTS, 1, 0))),
                   pl.BlockSpec((S5_BLK, INP_TM // S5_T, S5_CW), lambda i, j: (0, i, 0))),
        scratch_shapes=[pltpu.VMEM((S5_BLK, INP_TM, LANES), F32)],
        compiler_params=_cparams(("parallel", "arbitrary")),
        name="inproj",
    )(h, w_in_bf)


def _s5_tab_kernel(lam_ref, bre_ref, bim_ref, cre_ref, cim_ref, dsk_ref, m_ref, win_ref, wout_ref, pow_ref):
    d = pl.program_id(1)
    fwd = d == 0
    lam_re = lam_ref[0, 0, 0, 0:1, :]
    lam_im = lam_ref[0, 0, 0, 1:2, :]
    dt = jnp.exp(lam_ref[0, 0, 0, 2:3, :])

    def apow(j):
        mag = jnp.exp(lam_re * dt * float(j))
        ang = lam_im * dt * float(j)
        return mag * jnp.cos(ang), mag * jnp.sin(ang)

    a_re, a_im = apow(1)
    den = lam_re * lam_re + lam_im * lam_im
    num_re = a_re - 1.0
    coef_re = (num_re * lam_re + a_im * lam_im) / den
    coef_im = (a_im * lam_re - num_re * lam_im) / den
    ri = lax.broadcasted_iota(jnp.int32, (S5_STATE, S5_SW), 0)
    ci = lax.broadcasted_iota(jnp.int32, (S5_STATE, S5_SW), 1)
    repeat = jnp.where(ri == (ci & (S5_STATE - 1)), 1.0, 0.0).astype(BF16)
    own = (jnp.right_shift(lax.broadcasted_iota(jnp.int32, (LANES, S5_SW), 0), S5_GROUP.bit_length() - 1)
           == jnp.right_shift(lax.broadcasted_iota(jnp.int32, (LANES, S5_SW), 1), S5_STATE.bit_length() - 1))

    def block_diag(ref):
        return jnp.where(own, sum(_dot(piece, repeat) for piece in _split3(ref[0, 0, 0])), 0.0)

    b_re = block_diag(bre_ref)
    b_im = block_diag(bim_ref)
    bb_re = coef_re * b_re - coef_im * b_im
    bb_im = coef_re * b_im + coef_im * b_re
    bbr = _split3(bb_re)
    bbi = _split3(bb_im)
    c_re = block_diag(cre_ref)
    c_im = block_diag(cim_ref)

    kmat = []
    for j in range(S5_T + 1):
        aj_re, aj_im = apow(j)
        pc_re = c_re * aj_re - c_im * aj_im
        pc_im = c_re * aj_im + c_im * aj_re
        if j >= 1:
            r = pl.multiple_of(jnp.where(fwd, j - 1, S5_T - j) * LANES, LANES)
            wout_ref[0, 0, 0, pl.ds(r, LANES), :] = jnp.concatenate([pc_re, -pc_im], axis=1).astype(BF16)
        if j < S5_T:
            pb_re = bb_re * aj_re - bb_im * aj_im
            pb_im = bb_re * aj_im + bb_im * aj_re
            r = pl.multiple_of(jnp.where(fwd, S5_T - 1 - j, j) * LANES, LANES)
            win_ref[0, 0, 0, pl.ds(r, LANES), :] = jnp.concatenate([pb_re, pb_im], axis=1).astype(BF16)
            kmat.append(_dot_nt_hi(bbr, _split3(pc_re)) - _dot_nt_hi(bbi, _split3(pc_im)))

    ri = lax.broadcasted_iota(jnp.int32, (LANES, LANES), 0)
    ci = lax.broadcasted_iota(jnp.int32, (LANES, LANES), 1)
    keep_upper = jnp.where(fwd, 1.0, 0.0)
    kmat[0] = kmat[0] + jnp.where(ri == ci, dsk_ref[0, 0], 0.0) * keep_upper
    for s in range(S5_T):
        for t in range(S5_T):
            tile = kmat[abs(t - s)]
            if t > s:
                tile = tile * keep_upper
            elif t < s:
                tile = tile * (1.0 - keep_upper)
            m_ref[0, 0, 0, s * LANES:(s + 1) * LANES, t * LANES:(t + 1) * LANES] = tile.astype(BF16)

    pow_ref[...] = jnp.zeros_like(pow_ref)
    for i in range(SUBLANES):
        for row, j in ((i, S5_T * (i + 1)), (SUBLANES + i, S5_T * (SUBLANES - i))):
            p_re, p_im = apow(j)
            pow_ref[0, 0, 0, 0, row:row + 1, :] = p_re
            pow_ref[0, 0, 0, 1, row:row + 1, :] = p_im
    p_re, p_im = apow(S5_T * 2 * SUBLANES)
    pow_ref[0, 0, 0, 0, 2 * SUBLANES:2 * SUBLANES + 1, :] = p_re
    pow_ref[0, 0, 0, 1, 2 * SUBLANES:2 * SUBLANES + 1, :] = p_im


def _s5_tables(lam, b_re, b_im, c_re, c_im, d_skip):
    lead = (DEPTH, 2, S5_BLK)

    def spec(*tail):
        return pl.BlockSpec((1, 1, 1) + tail, lambda l, d, b: (l, d, b) + (0,) * len(tail))

    return pl.pallas_call(
        _s5_tab_kernel,
        out_shape=(
            jax.ShapeDtypeStruct(lead + (S5_CW, S5_CW), BF16),
            jax.ShapeDtypeStruct(lead + (S5_CW, 2 * S5_SW), BF16),
            jax.ShapeDtypeStruct(lead + (S5_CW, 2 * S5_SW), BF16),
            jax.ShapeDtypeStruct(lead + (2, S5_POW_ROWS, S5_SW), F32),
        ),
        grid=lead,
        in_specs=[spec(3, S5_SW)] + [spec(LANES, S5_STATE)] * 4
        + [pl.BlockSpec((1, 1, 1, LANES), lambda l, d, b: (l, b, 0, 0))],
        out_specs=(spec(S5_CW, S5_CW), spec(S5_CW, 2 * S5_SW), spec(S5_CW, 2 * S5_SW),
                   spec(2, S5_POW_ROWS, S5_SW)),
        compiler_params=_cparams(("parallel", "parallel", "parallel")),
        name="s5_tables",
    )(lam, b_re, b_im, c_re, c_im, d_skip)


def _s5_param_layout(s5_lam_re, s5_lam_im, s5_log_dt, s5_b_re, s5_b_im, s5_c_re, s5_c_im, s5_d):
    lam = jnp.stack([s5_lam_re, s5_lam_im, jnp.broadcast_to(s5_log_dt[..., None], s5_lam_re.shape)], axis=2)
    lam = lam.reshape(DEPTH, 2, 3, S5_BLK, S5_SW).transpose(0, 1, 3, 2, 4)

    def rows_b(b):
        return jnp.swapaxes(b, -1, -2).reshape(DEPTH, 2, S5_BLK, LANES, S5_STATE)

    def rows_c(c):
        return c.reshape(DEPTH, 2, S5_BLK, LANES, S5_STATE)

    return (lam, rows_b(s5_b_re), rows_b(s5_b_im), rows_c(s5_c_re), rows_c(s5_c_im),
            s5_d.reshape(DEPTH, S5_BLK, 1, LANES))


def _s5_kernel(u_ref, m_ref, win_ref, wout_ref, pow_ref, h0_ref, y_ref, fin_ref, x_sc):
    d = pl.program_id(1)
    u = u_ref[0]
    x_sc[...] = _dot(u, win_ref[0, 0, 0])
    p_re = pow_ref[0, 0, 0, 0]
    p_im = pow_ref[0, 0, 0, 1]
    re = slice(0, S5_SW)
    im = slice(S5_SW, 2 * S5_SW)

    def madd(t_re, t_im, w_re, w_im, s_re, s_im):
        return t_re + w_re * s_re - w_im * s_im, t_im + w_re * s_im + w_im * s_re

    def shift_rows(val, k, n, idx, reverse):
        if reverse:
            return jnp.where(idx < n - k, pltpu.roll(val, val.shape[0] - k, axis=0), 0.0)
        return jnp.where(idx >= k, pltpu.roll(val, k, axis=0), 0.0)

    def pow_row(k):
        row = k - 1 if k <= SUBLANES else 2 * SUBLANES
        return p_re[row:row + 1], p_im[row:row + 1]

    def ctx_scan(reverse):
        s_re = x_sc[0:S5_ROWS_CTX, re]
        s_im = x_sc[0:S5_ROWS_CTX, im]
        c = lax.broadcasted_iota(jnp.int32, (S5_ROWS_CTX, 1), 0) & (CTX_CHUNKS - 1)
        def shift_ctx(val, k):
            if k % SUBLANES:
                return shift_rows(val, k, CTX_CHUNKS, c, reverse)
            v3 = val.reshape(N_CTX_SEQ, CTX_CHUNKS, S5_SW)
            pad = jnp.zeros((N_CTX_SEQ, k, S5_SW), F32)
            moved = (jnp.concatenate([v3[:, k:], pad], axis=1) if reverse
                     else jnp.concatenate([pad, v3[:, :CTX_CHUNKS - k]], axis=1))
            return moved.reshape(S5_ROWS_CTX, S5_SW)

        k = 1
        while k < CTX_CHUNKS:
            w_re, w_im = pow_row(k)
            s_re, s_im = madd(s_re, s_im, w_re, w_im, shift_ctx(s_re, k), shift_ctx(s_im, k))
            k *= 2
        last = 0 if reverse else CTX_CHUNKS - 1
        pick = (lax.broadcasted_iota(jnp.int32, (N_CTX_SEQ, S5_ROWS_CTX), 1)
                == lax.broadcasted_iota(jnp.int32, (N_CTX_SEQ, S5_ROWS_CTX), 0) * CTX_CHUNKS + last)
        pick = jnp.where(pick, 1.0, 0.0).astype(BF16)
        fin_ref[0, 0, :, re] = sum(_dot(pick, piece) for piece in _split3(s_re))
        fin_ref[0, 0, :, im] = sum(_dot(pick, piece) for piece in _split3(s_im))
        x_sc[0:S5_ROWS_CTX, re] = shift_rows(s_re, 1, CTX_CHUNKS, c, reverse)
        x_sc[0:S5_ROWS_CTX, im] = shift_rows(s_im, 1, CTX_CHUNKS, c, reverse)

    def lat_group(k, carry, reverse):
        g = (S5_LAT_GROUPS - 1 - k) if reverse else k
        r8 = lax.broadcasted_iota(jnp.int32, (SUBLANES, 1), 0)
        tab = slice(SUBLANES, 2 * SUBLANES) if reverse else slice(0, SUBLANES)
        out = []
        for q in range(N_LAT_SEQ):
            base = pl.multiple_of(S5_ROWS_CTX + q * LAT_CHUNKS + g * SUBLANES, SUBLANES)
            s_re = x_sc[pl.ds(base, SUBLANES), re]
            s_im = x_sc[pl.ds(base, SUBLANES), im]
            j = 1
            while j < SUBLANES:
                w_re, w_im = pow_row(j)
                s_re, s_im = madd(s_re, s_im, w_re, w_im,
                                  shift_rows(s_re, j, SUBLANES, r8, reverse), shift_rows(s_im, j, SUBLANES, r8, reverse))
                j *= 2
            c_re, c_im = carry[2 * q], carry[2 * q + 1]
            s_re, s_im = madd(s_re, s_im, p_re[tab], p_im[tab], c_re, c_im)
            edge = (r8 == SUBLANES - 1) if reverse else (r8 == 0)
            x_sc[pl.ds(base, SUBLANES), re] = jnp.where(edge, c_re, shift_rows(s_re, 1, SUBLANES, r8, reverse))
            x_sc[pl.ds(base, SUBLANES), im] = jnp.where(edge, c_im, shift_rows(s_im, 1, SUBLANES, r8, reverse))
            end = slice(0, 1) if reverse else slice(SUBLANES - 1, SUBLANES)
            out += [s_re[end], s_im[end]]
        return tuple(out)

    def run(reverse):
        for t0 in range(0, S5_T, 2):
            src = slice(t0 * LANES, S5_CW) if reverse else slice(0, (t0 + 2) * LANES)
            dst = slice(t0 * LANES, (t0 + 2) * LANES)
            part = _dot(u[:, src], m_ref[0, 0, 0, src, dst])
            if reverse:
                y_ref[0, :, dst] += part
            else:
                y_ref[0, :, dst] = part
        ctx_scan(reverse)
        h0 = h0_ref[0, 0]
        init = []
        for q in range(N_LAT_SEQ):
            init += [h0[q:q + 1, re], h0[q:q + 1, im]]
        lax.fori_loop(0, S5_LAT_GROUPS, functools.partial(lat_group, reverse=reverse), tuple(init))

    @pl.when(d == 0)
    def _():
        run(False)

    @pl.when(d == 1)
    def _():
        run(True)

    y_ref[0] += _dot_nt(x_sc[...].astype(BF16), wout_ref[0, 0, 0])


def _s5_main(u, m, w_in, w_out, pows, h0, layer):
    def tab(*tail):
        return pl.BlockSpec((1, 1, 1) + tail, lambda b, d: (layer, d, b) + (0,) * len(tail))

    return pl.pallas_call(
        _s5_kernel,
        out_shape=(jax.ShapeDtypeStruct((S5_BLK, S5_ROWS, S5_CW), F32),
                   jax.ShapeDtypeStruct((2, S5_BLK, N_CTX_SEQ, 2 * S5_SW), F32)),
        grid=(S5_BLK, 2),
        in_specs=[
            pl.BlockSpec((1, S5_ROWS, S5_CW), lambda b, d: (b, 0, 0)),
            tab(S5_CW, S5_CW), tab(S5_CW, 2 * S5_SW), tab(S5_CW, 2 * S5_SW), tab(2, S5_POW_ROWS, S5_SW),
            pl.BlockSpec((1, 1, N_LAT_SEQ, 2 * S5_SW), lambda b, d: (d, b, 0, 0)),
        ],
        out_specs=(
            pl.BlockSpec((1, S5_ROWS, S5_CW), lambda b, d: (b, 0, 0)),
            pl.BlockSpec((1, 1, N_CTX_SEQ, 2 * S5_SW), lambda b, d: (d, b, 0, 0)),
        ),
        scratch_shapes=[pltpu.VMEM((S5_ROWS, 2 * S5_SW), F32)],
        compiler_params=_cparams(("parallel", "arbitrary")),
        name="s5",
    )(u, m, w_in, w_out, pows, h0)


def _hg_constants():
    t = np.arange(HG_C)
    ae = np.zeros((2, HG_E_ROWS, HG_C), np.float32)
    mask = np.zeros((2, HG_LEVELS, HG_C, HG_C), np.float32)
    for lvl in range(HG_LEVELS):
        half = 1 << lvl
        pos = t % (2 * half)
        mid = t - pos + half - 1
        upper = pos >= half
        u = t[None, :]
        rows_upper = (u > mid[:, None]) & (u <= t[:, None])
        rows_lower = (u > t[:, None]) & (u <= mid[:, None])
        ae[0, lvl * HG_C:(lvl + 1) * HG_C] = np.where(upper[:, None], rows_upper, rows_lower)
        same = (t[:, None] // (2 * half)) == (t[None, :] // (2 * half))
        mask[0, lvl] = same & upper[:, None] & (~upper)[None, :]
    ae[0, HG_LEVELS * HG_C:(HG_LEVELS + 1) * HG_C] = t[None, :] <= t[:, None]
    ae[0, (HG_LEVELS + 1) * HG_C:] = 1.0
    ae[1] = ae[0][:, ::-1]
    ae[1, :(HG_LEVELS + 1) * HG_C] = ae[1, :(HG_LEVELS + 1) * HG_C].reshape(HG_LEVELS + 1, HG_C, HG_C)[:, ::-1].reshape(-1, HG_C)
    mask[1] = mask[0][:, ::-1, ::-1]
    ae2 = np.concatenate([ae, ae], axis=-1)
    mask2 = np.concatenate([mask, mask], axis=-1)
    pair_diag = np.kron(np.eye(2, dtype=np.float32), np.ones((HG_DK, HG_DK), np.float32))
    return ae2, mask2, pair_diag


def _hg_tile(direction, i):
    return i if direction == 0 else HG_TILES - 1 - i


HG_PAIR = 2 * HG_DK


def _pair_diag(x):
    zero = jnp.zeros((x.shape[0], HG_DK), x.dtype)
    return jnp.concatenate([jnp.concatenate([x[:, :HG_DK], zero], axis=1),
                            jnp.concatenate([zero, x[:, HG_DK:]], axis=1)], axis=0)


def _hg_chunk_prep(direction, rows, q_ref, f_ref, v_ref, lb, ae_ref):
    f = lb + (1.0 - lb) * jax.nn.sigmoid(f_ref[rows, :])
    logf = jnp.log2(f)
    kk = 1.0 - f
    hi = logf.astype(BF16)
    lo = (logf - hi.astype(F32)).astype(BF16)
    ex = _dot(ae_ref[direction], jnp.concatenate([hi, lo], axis=0))
    dec = jnp.exp2(ex[:(HG_LEVELS + 1) * HG_C])
    cum = ex[HG_LEVELS * HG_C:(HG_LEVELS + 1) * HG_C]
    tot = ex[(HG_LEVELS + 1) * HG_C:(HG_LEVELS + 1) * HG_C + 1]
    q = q_ref[rows, :].astype(F32)
    v_bf = v_ref[rows, :]
    v = v_bf.astype(F32)
    q_in = (q * dec[HG_LEVELS * HG_C:]).astype(BF16)
    k_tail = (kk * jnp.exp2(tot - cum)).astype(BF16)
    dec_tot = jnp.exp2(tot)
    return dec, q, kk, v, v_bf, q_in, k_tail, dec_tot


def _hg_chunk_scores(direction, prep, mask_ref):
    dec, q, kk, v, v_bf, q_in, k_tail, dec_tot = prep
    qk = q * kk
    pairs = [slice(p * HG_PAIR, (p + 1) * HG_PAIR) for p in range(HG_HEADS // 2)]
    scores = [jnp.zeros((HG_C, HG_PAIR), F32) for _ in pairs]
    for lvl in range(HG_LEVELS):
        for p, sl in enumerate(pairs):
            g = dec[lvl * HG_C:(lvl + 1) * HG_C, sl]
            scores[p] = scores[p] + mask_ref[direction, lvl] * _dot_nt(
                (q[:, sl] * g).astype(BF16), _pair_diag((kk[:, sl] * g).astype(BF16)))
    local = []
    for p, sl in enumerate(pairs):
        diag = jnp.concatenate(
            [jnp.broadcast_to(jnp.sum(qk[:, (2 * p + j) * HG_DK:(2 * p + j + 1) * HG_DK], axis=-1, keepdims=True),
                              (HG_C, HG_DK)) for j in range(2)], axis=-1)
        local.append(_dot(scores[p].astype(BF16), _pair_diag(v_bf[:, sl])) + diag * v[:, sl])
    return local, q_in, k_tail, dec_tot, v_bf


def _hg_chunk_state(direction, rows, chunk, pd_ref, o_ref, st_sc):
    local, q_in, k_tail, dec_tot, v_bf = chunk
    outs = []
    for p in range(HG_HEADS // 2):
        sl = slice(p * HG_PAIR, (p + 1) * HG_PAIR)
        st = st_sc[direction, p]
        outs.append(local[p] + _dot_nt(q_in[:, sl], st.astype(BF16)))
        st_sc[direction, p] = st * dec_tot[:, sl] + pd_ref[...] * _dot_tn(v_bf[:, sl], k_tail[:, sl])
    o_ref[rows, :] = jnp.concatenate(outs, axis=-1)


def _hg_kernel(qf_ref, ff_ref, vf_ref, qb_ref, fb_ref, vb_ref, lb_ref, ae_ref, mask_ref, pd_ref, h0f_ref, h0b_ref,
               of_ref, ob_ref, finf_ref, finb_ref, st_sc, *, layer):
    i = pl.program_id(0)
    x = lb_ref[:, :, 0, :]
    e = jnp.exp(x - jnp.max(x, axis=0, keepdims=True))
    sm = e / jnp.sum(e, axis=0, keepdims=True)
    lb = jnp.sum(sm[1:layer + 1], axis=0) if layer > 0 else jnp.zeros((2, D_BRANCH), F32)
    for direction, h0_ref in ((0, h0f_ref), (1, h0b_ref)):
        tile = _hg_tile(direction, i)
        first_ctx = 0 if direction == 0 else HG_TILES_PER_CTX - 1
        first_lat = 0 if direction == 0 else HG_TILES_PER_LAT - 1
        is_start = jnp.where(tile < HG_CTX_TILES,
                             tile % HG_TILES_PER_CTX == first_ctx,
                             (tile - HG_CTX_TILES) % HG_TILES_PER_LAT == first_lat)

        @pl.when(is_start)
        def _(direction=direction, h0_ref=h0_ref):
            for p in range(HG_HEADS // 2):
                st_sc[direction, p] = _pair_diag(
                    jnp.concatenate([h0_ref[0, 0, 2 * p], h0_ref[0, 0, 2 * p + 1]], axis=1))

    work = []
    for c in range(HG_TM // HG_C):
        fwd_rows = slice(c * HG_C, (c + 1) * HG_C)
        bwd_rows = slice(HG_TM - (c + 1) * HG_C, HG_TM - c * HG_C)
        for direction, rows, refs, o_ref in ((0, fwd_rows, (qf_ref, ff_ref, vf_ref), of_ref),
                                             (1, bwd_rows, (qb_ref, fb_ref, vb_ref), ob_ref)):
            prep = _hg_chunk_prep(direction, rows, *refs, lb[direction:direction + 1], ae_ref)
            work.append((direction, rows, o_ref, _hg_chunk_scores(direction, prep, mask_ref)))
    for direction, rows, o_ref, chunk in work:
        _hg_chunk_state(direction, rows, chunk, pd_ref, o_ref, st_sc)

    for direction, fin_ref in ((0, finf_ref), (1, finb_ref)):
        @pl.when(_hg_tile(direction, i) < HG_CTX_TILES)
        def _(direction=direction, fin_ref=fin_ref):
            for h in range(HG_HEADS):
                p, j = divmod(h, 2)
                fin_ref[0, h] = st_sc[direction, p, j * HG_DK:(j + 1) * HG_DK, j * HG_DK:(j + 1) * HG_DK]


def _hgrn(proj, fz, hg_lb, ae, mask, pair_diag, h0t, layer):
    def tok_spec(direction, col):
        return pl.BlockSpec((HG_TM, D_BRANCH), lambda i: (_hg_tile(direction, i), col))

    def fin_spec(direction):
        return pl.BlockSpec((1, HG_HEADS, HG_DK, HG_DK),
                            lambda i: (jnp.minimum(_hg_tile(direction, i) // HG_TILES_PER_CTX, N_CTX_SEQ - 1), 0, 0, 0))

    def h0_spec(direction):
        return pl.BlockSpec((1, 1, HG_HEADS, HG_DK, HG_DK),
                            lambda i: (_cond_row(_hg_tile(direction, i), HG_TM), direction, 0, 0, 0))

    fin_shape = jax.ShapeDtypeStruct((N_CTX_SEQ, HG_HEADS, HG_DK, HG_DK), F32)
    return pl.pallas_call(
        functools.partial(_hg_kernel, layer=layer),
        out_shape=(jax.ShapeDtypeStruct((N_TOK, D_BRANCH), F32), jax.ShapeDtypeStruct((N_TOK, D_BRANCH), F32),
                   fin_shape, fin_shape),
        grid=(HG_TILES,),
        in_specs=[
            tok_spec(0, HG_Q_COL), tok_spec(0, 0), tok_spec(0, HG_V_COL),
            tok_spec(1, HG_Q_COL), tok_spec(1, 1), tok_spec(1, HG_V_COL),
            pl.BlockSpec((DEPTH, 2, 1, D_BRANCH), lambda i: (0, 0, 0, 0)),
            pl.BlockSpec((2, HG_E_ROWS, 2 * HG_C), lambda i: (0, 0, 0)),
            pl.BlockSpec((2, HG_LEVELS, HG_C, HG_PAIR), lambda i: (0, 0, 0, 0)),
            pl.BlockSpec((HG_PAIR, HG_PAIR), lambda i: (0, 0)),
            h0_spec(0), h0_spec(1),
        ],
        out_specs=(
            pl.BlockSpec((HG_TM, D_BRANCH), lambda i: (_hg_tile(0, i), 0)),
            pl.BlockSpec((HG_TM, D_BRANCH), lambda i: (_hg_tile(1, i), 0)),
            fin_spec(0), fin_spec(1),
        ),
        scratch_shapes=[pltpu.VMEM((2, HG_HEADS // 2, HG_PAIR, HG_PAIR), F32)],
        compiler_params=_cparams(("arbitrary",)),
        name="hgrn",
    )(proj, fz, proj, proj, fz, proj, hg_lb.reshape(DEPTH, 2, 1, D_BRANCH), ae, mask, pair_diag, h0t, h0t)


MRG_TM = 256


def _gelu_tanh(x):
    return 0.5 * x * (1.0 + jnp.tanh(0.7978845608028654 * (x + 0.044715 * (x * x * x))))


def _merge_kernel(scb_ref, scc_ref, sch_ref, hgg_ref, pool_ref, g0_ref, g1_ref, g2_ref, g3_ref,
                  yrow_ref, of_ref, ob_ref, x_ref, mod_ref, conv_ref, gluw_ref, glub_ref, ng_ref,
                  poolw_ref, pscale_ref, wbr_ref, wout_ref, lng_ref, lnb_ref, o_ref,
                  ys5_sc, gluw_sc, poolw_sc, wbr_sc, wout_sc):
    tile = pl.program_id(0)

    @pl.when(tile == 0)
    def _():
        gluw_sc[...] = gluw_ref[0].astype(BF16)
        poolw_sc[...] = poolw_ref[0].astype(BF16)
        wbr_sc[...] = wbr_ref[0].astype(BF16)
        wout_sc[...] = wout_ref[0].astype(BF16)

    line = jnp.where(tile * MRG_TM < N_CTX_TOK, CTX_LEN, GRID_W)
    pos = lax.broadcasted_iota(jnp.int32, (MRG_TM, 1), 0) & (line - 1)

    def shifted(val, k):
        rolled = pltpu.roll(val, k % MRG_TM, axis=0)
        ok = (pos >= k) if k > 0 else (pos < line + k)
        return jnp.where(ok, rolled, 0.0)

    m = scc_ref[...].astype(F32) * sch_ref[...].astype(F32)
    conv = conv_ref[0, 0:1, :] * shifted(m, 1) + conv_ref[0, 1:2, :] * m + conv_ref[0, 2:3, :] * shifted(m, -1)
    ya = scb_ref[...].astype(F32) * conv

    for b in range(S5_BLK):
        for t in range(S5_T):
            ys5_sc[b, pl.ds(t, MRG_TM // S5_T, stride=S5_T), :] = yrow_ref[b, :, t * LANES:(t + 1) * LANES]
    z = _gelu_tanh(jnp.concatenate([ys5_sc[b] for b in range(S5_BLK)], axis=-1))
    yb = z * _sigmoid_tanh(_dot(z.astype(BF16), gluw_sc[...]) + glub_ref[0])

    o = of_ref[...] + ob_ref[...]
    normed = []
    for h in range(HG_HEADS):
        oh = o[:, h * HG_DK:(h + 1) * HG_DK]
        ms = jnp.mean(oh * oh, axis=-1, keepdims=True)
        normed.append(oh * lax.rsqrt(ms + LN_EPS) * ng_ref[0])
    gate_c = hgg_ref[...].astype(F32)
    yc = jnp.concatenate(normed, axis=-1) * (gate_c * _sigmoid_tanh(gate_c))

    pu = pool_ref[...].astype(F32)
    posf = pos.astype(F32)
    linef = line.astype(F32)
    pooled = []
    for gi, w in enumerate(POOL_WINDOWS):
        vg = pu[:, gi * POOL_GROUP:(gi + 1) * POOL_GROUP]
        back, fwd, span = vg, vg, 1
        while 2 * span <= w // 2:
            back = back + shifted(back, span)
            fwd = fwd + shifted(fwd, -span)
            span *= 2
        s = shifted(back, 1) + fwd
        cnt = jnp.minimum(posf + w // 2, linef) - jnp.maximum(posf - w // 2, 0.0)
        pg = s / cnt - vg
        pooled.append(_dot(pg.astype(BF16), poolw_sc[gi]))
    yd = jnp.concatenate(pooled, axis=-1) * pscale_ref[0]

    merged = jnp.zeros((MRG_TM, D_MODEL), F32)
    for k, (br, gate_ref) in enumerate(((ya, g0_ref), (yb, g1_ref), (yc, g2_ref), (yd, g3_ref))):
        merged = merged + gate_ref[...].astype(F32) * _dot(br.astype(BF16), wbr_sc[k])
    mix = _dot(merged.astype(BF16), wout_sc[...])
    y = ALPHA * x_ref[...] + mod_ref[0, 0, 5:6, :] * mix
    o_ref[...] = _layer_norm(y, lng_ref[0, 0], lnb_ref[0, 0])


def _merge(proj, y_rows, o_f, o_b, x, mod, conv_w, glu_w, glu_b, norm_g, pool_w, pool_scale, w_branch, w_out,
           ln_g, ln_b, layer):
    def part(col):
        return pl.BlockSpec((MRG_TM, D_BRANCH), lambda i: (i, col))

    def gate(k):
        return pl.BlockSpec((MRG_TM, D_MODEL), lambda i: (i, N_IN_PARTS * D_BRANCH // D_MODEL + k))

    def per_layer(*shape, single=False):
        mode = dict(pipeline_mode=pl.Buffered(1)) if single else {}
        return pl.BlockSpec((1,) + shape, lambda i: (layer,) + (0,) * len(shape), **mode)

    return pl.pallas_call(
        _merge_kernel,
        out_shape=jax.ShapeDtypeStruct((N_TOK, D_MODEL), F32),
        grid=(N_TOK // MRG_TM,),
        in_specs=[
            part(0), part(1), part(2), part(8), part(9), gate(0), gate(1), gate(2), gate(3),
            pl.BlockSpec((S5_BLK, MRG_TM // S5_T, S5_CW), lambda i: (0, i, 0)),
            pl.BlockSpec((MRG_TM, D_BRANCH), lambda i: (i, 0)),
            pl.BlockSpec((MRG_TM, D_BRANCH), lambda i: (i, 0)),
            pl.BlockSpec((MRG_TM, D_MODEL), lambda i: (i, 0)),
            _mod_spec(layer, MRG_TM),
            per_layer(3, D_BRANCH), per_layer(D_BRANCH, D_BRANCH, single=True), per_layer(1, D_BRANCH),
            per_layer(1, HG_DK), per_layer(len(POOL_WINDOWS), POOL_GROUP, POOL_GROUP, single=True),
            per_layer(1, D_BRANCH), per_layer(N_BRANCH, D_BRANCH, D_MODEL, single=True),
            per_layer(D_MODEL, D_MODEL, single=True), _ln_spec(layer, 1), _ln_spec(layer, 1),
        ],
        out_specs=pl.BlockSpec((MRG_TM, D_MODEL), lambda i: (i, 0)),
        scratch_shapes=[
            pltpu.VMEM((S5_BLK, MRG_TM, LANES), F32),
            pltpu.VMEM((D_BRANCH, D_BRANCH), BF16),
            pltpu.VMEM((len(POOL_WINDOWS), POOL_GROUP, POOL_GROUP), BF16),
            pltpu.VMEM((N_BRANCH, D_BRANCH, D_MODEL), BF16),
            pltpu.VMEM((D_MODEL, D_MODEL), BF16),
        ],
        compiler_params=_cparams(("arbitrary",)),
        name="merge",
    )(proj, proj, proj, proj, proj, proj, proj, proj, proj, y_rows, o_f, o_b, x, mod,
      conv_w, glu_w, glu_b.reshape(DEPTH, 1, D_BRANCH), norm_g.reshape(DEPTH, 1, HG_DK), pool_w,
      pool_scale.reshape(DEPTH, 1, D_BRANCH), w_branch, w_out, ln_g, ln_b)


def _grid_pos_embedding():
    rows = LAT_LEN // GRID_W
    quarter = D_MODEL // 4
    omega = POS_BASE ** (-jnp.arange(quarter, dtype=F32) / quarter)
    ar = jnp.arange(rows, dtype=F32)[:, None] * omega
    ac = jnp.arange(GRID_W, dtype=F32)[:, None] * omega
    row_tab = jnp.concatenate([jnp.sin(ar), jnp.cos(ar)], -1)[:, None, :]
    col_tab = jnp.concatenate([jnp.sin(ac), jnp.cos(ac)], -1)[None, :, :]
    shape = (rows, GRID_W, 2 * quarter)
    pos = jnp.concatenate([jnp.broadcast_to(row_tab, shape), jnp.broadcast_to(col_tab, shape)], -1)
    return pos.reshape(LAT_LEN, D_MODEL)


def kernel(x_prompt, x_sample, state_s5_re, state_s5_im, state_hgrn, c, c_ctx, w_ada, b_ada, ln_g, ln_b,
           ffn_w1, ffn_w3, ffn_w2, w_in, sc_conv, s5_lam_re, s5_lam_im, s5_log_dt, s5_b_re, s5_b_im,
           s5_c_re, s5_c_im, s5_d, s5_glu_w, s5_glu_b, hg_lb, hg_norm_g, pool_w, pool_scale, w_branch, w_out):
    x = (x_prompt.reshape(N_CTX_TOK, D_MODEL), x_sample.reshape(N_LAT_TOK, D_MODEL), _grid_pos_embedding())

    cond = jnp.zeros((COND_PAD, D_MODEL), F32).at[0].set(c_ctx).at[1:N_COND].set(c)
    mod = _ada(cond, w_ada, b_ada).reshape(DEPTH, COND_PAD, N_SUB * 3, D_MODEL)
    ln_g4 = ln_g.reshape(DEPTH, N_SUB, 1, D_MODEL)
    ln_b4 = ln_b.reshape(DEPTH, N_SUB, 1, D_MODEL)

    s5_m, s5_win, s5_wout, s5_pow = _s5_tables(*_s5_param_layout(
        s5_lam_re, s5_lam_im, s5_log_dt, s5_b_re, s5_b_im, s5_c_re, s5_c_im, s5_d))

    w_in_bf = _cast_w_in(w_in)

    ae_np, mask_np, pair_np = _hg_constants()
    ae = jnp.asarray(ae_np, BF16)
    mask = jnp.asarray(mask_np, F32)
    pair_diag = jnp.asarray(pair_np, F32)

    fin_s5, fin_hg = [], []
    for l in range(DEPTH):
        x, h = _ffn(x, mod, ffn_w1, ffn_w3, ffn_w2, ln_g4, ln_b4, l, 0)
        proj, fz, u_rows = _inproj(h, w_in_bf, l)

        h0 = jnp.concatenate([
            state_s5_re[:, l].reshape(N_LAT_SEQ, 2, S5_BLK, S5_SW),
            state_s5_im[:, l].reshape(N_LAT_SEQ, 2, S5_BLK, S5_SW)], axis=-1).transpose(1, 2, 0, 3)
        y_rows, fin = _s5_main(u_rows, s5_m, s5_win, s5_wout, s5_pow, h0, l)
        fin_s5.append(fin)

        h0t = jnp.concatenate([jnp.zeros((1, 2, HG_HEADS, HG_DK, HG_DK), F32),
                               jnp.swapaxes(state_hgrn[:, l], -1, -2)], axis=0)
        o_f, o_b, fin_f, fin_b = _hgrn(proj, fz, hg_lb, ae, mask, pair_diag, h0t, l)
        fin_hg.append(jnp.stack([fin_f, fin_b], axis=1))

        x = _merge(proj, y_rows, o_f, o_b, x, mod, sc_conv, s5_glu_w, s5_glu_b, hg_norm_g,
                   pool_w, pool_scale, w_branch, w_out, ln_g4, ln_b4, l)
        x = _ffn(x, mod, ffn_w1, ffn_w3, ffn_w2, ln_g4, ln_b4, l, 2, split_output=(l == DEPTH - 1))

    y_prompt = x[0].reshape(N_CTX_SEQ, CTX_LEN, D_MODEL)
    y_sample = x[1].reshape(N_LAT_SEQ, LAT_LEN, D_MODEL)
    fin = jnp.stack(fin_s5)
    fin = fin.reshape(DEPTH, 2, S5_BLK, N_CTX_SEQ, 2, S5_GPB, S5_STATE)
    fin = fin.transpose(4, 3, 0, 1, 2, 5, 6).reshape(2, N_CTX_SEQ, DEPTH, 2, S5_GROUPS, S5_STATE)
    new_hgrn = jnp.swapaxes(jnp.stack(fin_hg, axis=1), -1, -2)
    return y_prompt, y_sample, fin[0], fin[1], new_hgrn
```

```python
import functools

import numpy as np
import jax
import jax.numpy as jnp
from jax import lax
from jax.experimental import pallas as pl
from jax.experimental.pallas import tpu as pltpu

F32 = jnp.float32
BF16 = jnp.bfloat16

D_MODEL = 1024
N_CTX_SEQ = 16
CTX_LEN = 256
DEPTH = 2
N_LAT_SEQ = 2
LAT_LEN = 4096
GRID_W = 64
D_BRANCH = 512
N_BRANCH = 4
S5_GROUPS = 32
S5_GROUP = 16
S5_STATE = 64
HG_HEADS = 4
HG_DK = 128
POOL_WINDOWS = (2, 4, 8, 16)
POOL_GROUP = 128
D_FF = 2816
N_SUB = 3
N_IN_PARTS = 10
IN_WIDTH = N_IN_PARTS * D_BRANCH + N_BRANCH * D_MODEL
ADA_WIDTH = N_SUB * 3 * D_MODEL
ALPHA = (2 * DEPTH) ** 0.25
LN_EPS = 1e-5
POS_BASE = 10000.0

N_CTX_TOK = N_CTX_SEQ * CTX_LEN
N_LAT_TOK = N_LAT_SEQ * LAT_LEN
N_TOK = N_CTX_TOK + N_LAT_TOK
N_COND = 1 + N_LAT_SEQ
COND_PAD = 8

LANES = 128
SUBLANES = 8
VMEM_LIMIT = 58 * 1024 * 1024

S5_T = 8
S5_BLK = 4
S5_GPB = S5_GROUPS // S5_BLK
S5_SW = S5_GPB * S5_STATE
S5_CW = S5_T * LANES
CTX_CHUNKS = CTX_LEN // S5_T
LAT_CHUNKS = LAT_LEN // S5_T
S5_ROWS_CTX = N_CTX_TOK // S5_T
S5_ROWS_LAT = N_LAT_TOK // S5_T
S5_ROWS = S5_ROWS_CTX + S5_ROWS_LAT
S5_LAT_GROUPS = LAT_CHUNKS // SUBLANES
S5_POW_ROWS = 3 * SUBLANES
S5_U_COL = 3

HG_C = 128
HG_LEVELS = 7
HG_TM = 256
HG_TILES = N_TOK // HG_TM
HG_CTX_TILES = N_CTX_TOK // HG_TM
HG_TILES_PER_CTX = CTX_LEN // HG_TM
HG_TILES_PER_LAT = LAT_LEN // HG_TM
HG_E_ROWS = (HG_LEVELS + 1) * HG_C + SUBLANES


def _cparams(sem):
    return pltpu.CompilerParams(dimension_semantics=sem, vmem_limit_bytes=VMEM_LIMIT)


def _cond_row(tile, tile_tokens):
    tok = tile * tile_tokens
    return jnp.where(tok < N_CTX_TOK, 0, 1 + (tok - N_CTX_TOK) // LAT_LEN)


def _dot(a, b):
    return jnp.dot(a, b, preferred_element_type=F32)


def _dot_nt(a, b):
    return lax.dot_general(a, b, (((1,), (1,)), ((), ())), preferred_element_type=F32)


def _dot_tn(a, b):
    return lax.dot_general(a, b, (((0,), (0,)), ((), ())), preferred_element_type=F32)


def _split3(x):
    h1 = x.astype(BF16)
    r1 = x - h1.astype(F32)
    h2 = r1.astype(BF16)
    h3 = (r1 - h2.astype(F32)).astype(BF16)
    return h1, h2, h3


def _dot_nt_hi(a3, b3):
    acc = None
    for x in range(2):
        for y in range(2 - x):
            term = _dot_nt(a3[x], b3[y])
            acc = term if acc is None else acc + term
    return acc


def _silu(x):
    return x * jax.nn.sigmoid(x)


def _sigmoid_tanh(x):
    return 0.5 * jnp.tanh(0.5 * x) + 0.5


def _layer_norm(y, g, b):
    mu = jnp.mean(y, axis=-1, keepdims=True)
    yc = y - mu
    var = jnp.mean(yc * yc, axis=-1, keepdims=True)
    return yc * lax.rsqrt(var + LN_EPS) * g + b


def _mod_spec(layer, tile_tokens):
    return pl.BlockSpec((1, 1, N_SUB * 3, D_MODEL),
                        lambda i, *_: (layer, _cond_row(i, tile_tokens), 0, 0))


def _ln_spec(layer, sub):
    return pl.BlockSpec((1, 1, 1, D_MODEL), lambda *_: (layer, sub, 0, 0))


ADA_TN = 1152


def _ada_kernel(c_ref, w_ref, b_ref, o_ref):
    s = _silu(c_ref[...]).astype(BF16)
    o_ref[0] = _dot(s, w_ref[0].astype(BF16)) + b_ref[0]


def _ada(cond, w_ada, b_ada):
    return pl.pallas_call(
        _ada_kernel,
        out_shape=jax.ShapeDtypeStruct((DEPTH, COND_PAD, ADA_WIDTH), F32),
        grid=(DEPTH, ADA_WIDTH // ADA_TN),
        in_specs=[
            pl.BlockSpec((COND_PAD, D_MODEL), lambda l, j: (0, 0)),
            pl.BlockSpec((1, D_MODEL, ADA_TN), lambda l, j: (l, 0, j)),
            pl.BlockSpec((1, 1, ADA_TN), lambda l, j: (l, 0, j)),
        ],
        out_specs=pl.BlockSpec((1, COND_PAD, ADA_TN), lambda l, j: (l, 0, j)),
        compiler_params=_cparams(("parallel", "parallel")),
        name="ada",
    )(cond, w_ada, b_ada.reshape(DEPTH, 1, ADA_WIDTH))


FFN_TM = 1024
FFN_TM_SPLIT = 512
FFN_TC = 256
FFN_CHUNKS = D_FF // FFN_TC


def _ffn_kernel(*refs, sub, tm, split_input, split_output, emit_next):
    step = pl.program_id(0)
    tile = step - FFN_CHUNKS
    is_ctx = tile < N_CTX_TOK // tm
    n_x = 3 if split_input else 1
    x_refs, refs = refs[:n_x], refs[n_x:]
    mod_ref, w1_ref, w3_ref, w2_ref, g_ref, b_ref = refs[:6]
    out_refs = refs[6:-4]
    w1_sc, w3_sc, w2_sc, act_sc = refs[-4:]

    def load_x():
        if split_input:
            ctx_ref, lat_ref, pos_ref = x_refs
            return jnp.where(is_ctx, ctx_ref[...], lat_ref[...] + pos_ref[...])
        return x_refs[0][...]

    def modulated(x):
        shift = mod_ref[0, 0, 3 * sub:3 * sub + 1, :]
        scale = mod_ref[0, 0, 3 * sub + 1:3 * sub + 2, :]
        return (x * (1.0 + scale) + shift).astype(BF16)

    @pl.when(step < FFN_CHUNKS)
    def _():
        w1 = w1_ref[0, 0].astype(BF16)
        w3 = w3_ref[0, 0].astype(BF16)
        w1_sc[step] = w1
        w3_sc[step] = w3
        w2_sc[pl.ds(pl.multiple_of(step * FFN_TC, FFN_TC), FFN_TC), :] = w2_ref[0, 0].astype(BF16)
        h = modulated(load_x())
        act_sc[:, pl.ds(pl.multiple_of(step * FFN_TC, FFN_TC), FFN_TC)] = (
            _silu(_dot(h, w1)) * _dot(h, w3)).astype(BF16)

    @pl.when(step > FFN_CHUNKS)
    def _():
        h = modulated(load_x())
        for c in range(FFN_CHUNKS):
            a = _dot(h, w1_sc[c])
            b = _dot(h, w3_sc[c])
            act_sc[:, c * FFN_TC:(c + 1) * FFN_TC] = (_silu(a) * b).astype(BF16)

    @pl.when(step >= FFN_CHUNKS)
    def _():
        x = load_x()
        gate = mod_ref[0, 0, 3 * sub + 2:3 * sub + 3, :]
        f = _dot(act_sc[...], w2_sc[...])
        y = _layer_norm(ALPHA * x + gate * (0.5 * f), g_ref[0, 0], b_ref[0, 0])
        if emit_next:
            nxt = 3 * (sub + 1)
            out_refs[1][...] = (y * (1.0 + mod_ref[0, 0, nxt + 1:nxt + 2, :])
                                + mod_ref[0, 0, nxt:nxt + 1, :]).astype(BF16)
        if split_output:
            @pl.when(is_ctx)
            def _():
                out_refs[0][...] = y

            @pl.when(jnp.logical_not(is_ctx))
            def _():
                out_refs[1][...] = y
        else:
            out_refs[0][...] = y


def _ffn(x, mod, w1, w3, w2, ln_g, ln_b, layer, sub, split_output=False):
    which = sub // 2
    split_input = isinstance(x, tuple)
    tm = FFN_TM_SPLIT if split_input else FFN_TM
    ctx_tiles = N_CTX_TOK // tm

    def chunk(s):
        return jnp.minimum(s, FFN_CHUNKS - 1)

    def tile(s):
        return jnp.maximum(s - FFN_CHUNKS, 0)

    def tok_spec(index):
        return pl.BlockSpec((tm, D_MODEL), lambda s: (index(tile(s)), 0))

    def ctx_tile(t):
        return jnp.minimum(t, ctx_tiles - 1)

    def lat_tile(t):
        return jnp.maximum(t - ctx_tiles, 0)

    if split_input:
        pos_tiles = LAT_LEN // tm
        x_specs = [tok_spec(ctx_tile), tok_spec(lat_tile), tok_spec(lambda t: lat_tile(t) % pos_tiles)]
    else:
        x_specs = [tok_spec(lambda t: t)]
        x = (x,)
    emit_next = sub == 0
    if split_output:
        out_shape = (jax.ShapeDtypeStruct((N_CTX_TOK, D_MODEL), F32), jax.ShapeDtypeStruct((N_LAT_TOK, D_MODEL), F32))
        out_specs = (tok_spec(ctx_tile), tok_spec(lat_tile))
    elif emit_next:
        out_shape = (jax.ShapeDtypeStruct((N_TOK, D_MODEL), F32), jax.ShapeDtypeStruct((N_TOK, D_MODEL), BF16))
        out_specs = (tok_spec(lambda t: t), tok_spec(lambda t: t))
    else:
        out_shape = jax.ShapeDtypeStruct((N_TOK, D_MODEL), F32)
        out_specs = tok_spec(lambda t: t)

    return pl.pallas_call(
        functools.partial(_ffn_kernel, sub=sub, tm=tm, split_input=split_input, split_output=split_output,
                          emit_next=emit_next),
        out_shape=out_shape,
        grid=(FFN_CHUNKS + N_TOK // tm,),
        in_specs=x_specs + [
            pl.BlockSpec((1, 1, N_SUB * 3, D_MODEL), lambda s: (layer, _cond_row(tile(s), tm), 0, 0)),
            pl.BlockSpec((1, 1, D_MODEL, FFN_TC), lambda s: (layer, which, 0, chunk(s))),
            pl.BlockSpec((1, 1, D_MODEL, FFN_TC), lambda s: (layer, which, 0, chunk(s))),
            pl.BlockSpec((1, 1, FFN_TC, D_MODEL), lambda s: (layer, which, chunk(s), 0)),
            _ln_spec(layer, sub), _ln_spec(layer, sub),
        ],
        out_specs=out_specs,
        scratch_shapes=[pltpu.VMEM((FFN_CHUNKS, D_MODEL, FFN_TC), BF16), pltpu.VMEM((FFN_CHUNKS, D_MODEL, FFN_TC), BF16),
                        pltpu.VMEM((D_FF, D_MODEL), BF16), pltpu.VMEM((tm, D_FF), BF16)],
        compiler_params=_cparams(("arbitrary",)),
        name=f"ffn{sub}",
    )(*x, mod, w1, w3, w2, ln_g, ln_b)


INP_TM = 2048
INP_PARTS = 2
INP_TN = INP_PARTS * D_BRANCH
INP_STEPS = IN_WIDTH // INP_TN
HG_Q_COL, HG_F_COL, HG_V_COL = 4, 5, 7


def _cast_w_in_kernel(w_ref, o_ref):
    o_ref[0, 0] = w_ref[0].astype(BF16)


def _cast_w_in(w_in):
    return pl.pallas_call(
        _cast_w_in_kernel,
        out_shape=jax.ShapeDtypeStruct((DEPTH, INP_STEPS, D_MODEL, INP_TN), BF16),
        grid=(DEPTH, INP_STEPS),
        in_specs=[pl.BlockSpec((1, D_MODEL, INP_TN), lambda l, j: (l, 0, j))],
        out_specs=pl.BlockSpec((1, 1, D_MODEL, INP_TN), lambda l, j: (l, j, 0, 0)),
        compiler_params=_cparams(("parallel", "parallel")),
        name="cast_w_in",
    )(w_in)


def _inproj_kernel(h_ref, w_ref, o_ref, fz_ref, u_ref, blk_sc):
    j = pl.program_id(1)

    def project(gate=False):
        res = _dot(h_ref[...], w_ref[0, 0])
        o_ref[...] = (_sigmoid_tanh(res) if gate else res).astype(BF16)
        return res

    def part(res, p):
        k = p % INP_PARTS
        return res[:, k * D_BRANCH:(k + 1) * D_BRANCH]

    special = (S5_U_COL, HG_F_COL, HG_F_COL + 1)
    steps = [p // INP_PARTS for p in special]
    assert len(set(steps)) == len(steps)

    first_gate = N_IN_PARTS // INP_PARTS
    assert N_IN_PARTS % INP_PARTS == 0 and max(steps) < first_gate

    @pl.when(functools.reduce(jnp.logical_and, [j != s for s in steps] + [j < first_gate]))
    def _():
        project()

    @pl.when(j >= first_gate)
    def _():
        project(gate=True)

    for p in (HG_F_COL, HG_F_COL + 1):
        @pl.when(j == p // INP_PARTS)
        def _(p=p):
            fz_ref[...] = part(project(), p)

    @pl.when(j == S5_U_COL // INP_PARTS)
    def _():
        s5_u = part(project(), S5_U_COL)
        for b in range(S5_BLK):
            blk_sc[b] = s5_u[:, b * LANES:(b + 1) * LANES]
            for t in range(S5_T):
                rows = blk_sc[b, pl.ds(t, INP_TM // S5_T, stride=S5_T), :]
                u_ref[b, :, t * LANES:(t + 1) * LANES] = rows.astype(BF16)


def _inproj(h, w_in_bf, layer):
    return pl.pallas_call(
        _inproj_kernel,
        out_shape=(jax.ShapeDtypeStruct((N_TOK, IN_WIDTH), BF16),
                   jax.ShapeDtypeStruct((N_TOK, 2 * D_BRANCH), F32),
                   jax.ShapeDtypeStruct((S5_BLK, S5_ROWS, S5_CW), BF16)),
        grid=(N_TOK // INP_TM, INP_STEPS),
        in_specs=[
            pl.BlockSpec((INP_TM, D_MODEL), lambda i, j: (i, 0)),
            pl.BlockSpec((1, 1, D_MODEL, INP_TN), lambda i, j: (layer, j, 0, 0)),
        ],
        out_specs=(pl.BlockSpec((INP_TM, INP_TN), lambda i, j: (i, j)),
                   pl.BlockSpec((INP_TM, D_BRANCH),
                                lambda i, j: (i, jnp.where(j > HG_F_COL // INP_PARTS, 1, 0))),
                   pl.BlockSpec((S5_BLK, INP_TM // S5_T, S5_CW), lambda i, j: (0, i, 0))),
        scratch_shapes=[pltpu.VMEM((S5_BLK, INP_TM, LANES), F32)],
        compiler_params=_cparams(("parallel", "arbitrary")),
        name="inproj",
    )(h, w_in_bf)


def _s5_tab_kernel(lam_ref, bre_ref, bim_ref, cre_ref, cim_ref, dsk_ref, m_ref, win_ref, wout_ref, pow_ref):
    d = pl.program_id(1)
    fwd = d == 0
    lam_re = lam_ref[0, 0, 0, 0:1, :]
    lam_im = lam_ref[0, 0, 0, 1:2, :]
    dt = jnp.exp(lam_ref[0, 0, 0, 2:3, :])

    def apow(j):
        mag = jnp.exp(lam_re * dt * float(j))
        ang = lam_im * dt * float(j)
        return mag * jnp.cos(ang), mag * jnp.sin(ang)

    a_re, a_im = apow(1)
    den = lam_re * lam_re + lam_im * lam_im
    num_re = a_re - 1.0
    coef_re = (num_re * lam_re + a_im * lam_im) / den
    coef_im = (a_im * lam_re - num_re * lam_im) / den
    ri = lax.broadcasted_iota(jnp.int32, (S5_STATE, S5_SW), 0)
    ci = lax.broadcasted_iota(jnp.int32, (S5_STATE, S5_SW), 1)
    repeat = jnp.where(ri == (ci & (S5_STATE - 1)), 1.0, 0.0).astype(BF16)
    own = (jnp.right_shift(lax.broadcasted_iota(jnp.int32, (LANES, S5_SW), 0), S5_GROUP.bit_length() - 1)
           == jnp.right_shift(lax.broadcasted_iota(jnp.int32, (LANES, S5_SW), 1), S5_STATE.bit_length() - 1))

    def block_diag(ref):
        return jnp.where(own, sum(_dot(piece, repeat) for piece in _split3(ref[0, 0, 0])), 0.0)

    b_re = block_diag(bre_ref)
    b_im = block_diag(bim_ref)
    bb_re = coef_re * b_re - coef_im * b_im
    bb_im = coef_re * b_im + coef_im * b_re
    bbr = _split3(bb_re)
    bbi = _split3(bb_im)
    c_re = block_diag(cre_ref)
    c_im = block_diag(cim_ref)

    kmat = []
    for j in range(S5_T + 1):
        aj_re, aj_im = apow(j)
        pc_re = c_re * aj_re - c_im * aj_im
        pc_im = c_re * aj_im + c_im * aj_re
        if j >= 1:
            r = pl.multiple_of(jnp.where(fwd, j - 1, S5_T - j) * LANES, LANES)
            wout_ref[0, 0, 0, pl.ds(r, LANES), :] = jnp.concatenate([pc_re, -pc_im], axis=1).astype(BF16)
        if j < S5_T:
            pb_re = bb_re * aj_re - bb_im * aj_im
            pb_im = bb_re * aj_im + bb_im * aj_re
            r = pl.multiple_of(jnp.where(fwd, S5_T - 1 - j, j) * LANES, LANES)
            win_ref[0, 0, 0, pl.ds(r, LANES), :] = jnp.concatenate([pb_re, pb_im], axis=1).astype(BF16)
            kmat.append(_dot_nt_hi(bbr, _split3(pc_re)) - _dot_nt_hi(bbi, _split3(pc_im)))

    ri = lax.broadcasted_iota(jnp.int32, (LANES, LANES), 0)
    ci = lax.broadcasted_iota(jnp.int32, (LANES, LANES), 1)
    keep_upper = jnp.where(fwd, 1.0, 0.0)
    kmat[0] = kmat[0] + jnp.where(ri == ci, dsk_ref[0, 0], 0.0) * keep_upper
    for s in range(S5_T):
        for t in range(S5_T):
            tile = kmat[abs(t - s)]
            if t > s:
                tile = tile * keep_upper
            elif t < s:
                tile = tile * (1.0 - keep_upper)
            m_ref[0, 0, 0, s * LANES:(s + 1) * LANES, t * LANES:(t + 1) * LANES] = tile.astype(BF16)

    pow_ref[...] = jnp.zeros_like(pow_ref)
    for i in range(SUBLANES):
        for row, j in ((i, S5_T * (i + 1)), (SUBLANES + i, S5_T * (SUBLANES - i))):
            p_re, p_im = apow(j)
            pow_ref[0, 0, 0, 0, row:row + 1, :] = p_re
            pow_ref[0, 0, 0, 1, row:row + 1, :] = p_im
    p_re, p_im = apow(S5_T * 2 * SUBLANES)
    pow_ref[0, 0, 0, 0, 2 * SUBLANES:2 * SUBLANES + 1, :] = p_re
    pow_ref[0, 0, 0, 1, 2 * SUBLANES:2 * SUBLANES + 1, :] = p_im


def _s5_tables(lam, b_re, b_im, c_re, c_im, d_skip):
    lead = (DEPTH, 2, S5_BLK)

    def spec(*tail):
        return pl.BlockSpec((1, 1, 1) + tail, lambda l, d, b: (l, d, b) + (0,) * len(tail))

    return pl.pallas_call(
        _s5_tab_kernel,
        out_shape=(
            jax.ShapeDtypeStruct(lead + (S5_CW, S5_CW), BF16),
            jax.ShapeDtypeStruct(lead + (S5_CW, 2 * S5_SW), BF16),
            jax.ShapeDtypeStruct(lead + (S5_CW, 2 * S5_SW), BF16),
            jax.ShapeDtypeStruct(lead + (2, S5_POW_ROWS, S5_SW), F32),
        ),
        grid=lead,
        in_specs=[spec(3, S5_SW)] + [spec(LANES, S5_STATE)] * 4
        + [pl.BlockSpec((1, 1, 1, LANES), lambda l, d, b: (l, b, 0, 0))],
        out_specs=(spec(S5_CW, S5_CW), spec(S5_CW, 2 * S5_SW), spec(S5_CW, 2 * S5_SW),
                   spec(2, S5_POW_ROWS, S5_SW)),
        compiler_params=_cparams(("parallel", "parallel", "parallel")),
        name="s5_tables",
    )(lam, b_re, b_im, c_re, c_im, d_skip)


def _s5_param_layout(s5_lam_re, s5_lam_im, s5_log_dt, s5_b_re, s5_b_im, s5_c_re, s5_c_im, s5_d):
    lam = jnp.stack([s5_lam_re, s5_lam_im, jnp.broadcast_to(s5_log_dt[..., None], s5_lam_re.shape)], axis=2)
    lam = lam.reshape(DEPTH, 2, 3, S5_BLK, S5_SW).transpose(0, 1, 3, 2, 4)

    def rows_b(b):
        return jnp.swapaxes(b, -1, -2).reshape(DEPTH, 2, S5_BLK, LANES, S5_STATE)

    def rows_c(c):
        return c.reshape(DEPTH, 2, S5_BLK, LANES, S5_STATE)

    return (lam, rows_b(s5_b_re), rows_b(s5_b_im), rows_c(s5_c_re), rows_c(s5_c_im),
            s5_d.reshape(DEPTH, S5_BLK, 1, LANES))


def _s5_kernel(u_ref, m_ref, win_ref, wout_ref, pow_ref, h0_ref, y_ref, fin_ref, x_sc):
    d = pl.program_id(1)
    u = u_ref[0]
    x_sc[...] = _dot(u, win_ref[0, 0, 0])
    p_re = pow_ref[0, 0, 0, 0]
    p_im = pow_ref[0, 0, 0, 1]
    re = slice(0, S5_SW)
    im = slice(S5_SW, 2 * S5_SW)

    def madd(t_re, t_im, w_re, w_im, s_re, s_im):
        return t_re + w_re * s_re - w_im * s_im, t_im + w_re * s_im + w_im * s_re

    def shift_rows(val, k, n, idx, reverse):
        if reverse:
            return jnp.where(idx < n - k, pltpu.roll(val, val.shape[0] - k, axis=0), 0.0)
        return jnp.where(idx >= k, pltpu.roll(val, k, axis=0), 0.0)

    def pow_row(k):
        row = k - 1 if k <= SUBLANES else 2 * SUBLANES
        return p_re[row:row + 1], p_im[row:row + 1]

    def ctx_scan(reverse):
        s_re = x_sc[0:S5_ROWS_CTX, re]
        s_im = x_sc[0:S5_ROWS_CTX, im]
        c = lax.broadcasted_iota(jnp.int32, (S5_ROWS_CTX, 1), 0) & (CTX_CHUNKS - 1)
        def shift_ctx(val, k):
            if k % SUBLANES:
                return shift_rows(val, k, CTX_CHUNKS, c, reverse)
            v3 = val.reshape(N_CTX_SEQ, CTX_CHUNKS, S5_SW)
            pad = jnp.zeros((N_CTX_SEQ, k, S5_SW), F32)
            moved = (jnp.concatenate([v3[:, k:], pad], axis=1) if reverse
                     else jnp.concatenate([pad, v3[:, :CTX_CHUNKS - k]], axis=1))
            return moved.reshape(S5_ROWS_CTX, S5_SW)

        k = 1
        while k < CTX_CHUNKS:
            w_re, w_im = pow_row(k)
            s_re, s_im = madd(s_re, s_im, w_re, w_im, shift_ctx(s_re, k), shift_ctx(s_im, k))
            k *= 2
        last = 0 if reverse else CTX_CHUNKS - 1
        pick = (lax.broadcasted_iota(jnp.int32, (N_CTX_SEQ, S5_ROWS_CTX), 1)
                == lax.broadcasted_iota(jnp.int32, (N_CTX_SEQ, S5_ROWS_CTX), 0) * CTX_CHUNKS + last)
        pick = jnp.where(pick, 1.0, 0.0).astype(BF16)
        fin_ref[0, 0, :, re] = sum(_dot(pick, piece) for piece in _split3(s_re))
        fin_ref[0, 0, :, im] = sum(_dot(pick, piece) for piece in _split3(s_im))
        x_sc[0:S5_ROWS_CTX, re] = shift_rows(s_re, 1, CTX_CHUNKS, c, reverse)
        x_sc[0:S5_ROWS_CTX, im] = shift_rows(s_im, 1, CTX_CHUNKS, c, reverse)

    def lat_group(k, carry, reverse):
        g = (S5_LAT_GROUPS - 1 - k) if reverse else k
        r8 = lax.broadcasted_iota(jnp.int32, (SUBLANES, 1), 0)
        tab = slice(SUBLANES, 2 * SUBLANES) if reverse else slice(0, SUBLANES)
        out = []
        for q in range(N_LAT_SEQ):
            base = pl.multiple_of(S5_ROWS_CTX + q * LAT_CHUNKS + g * SUBLANES, SUBLANES)
            s_re = x_sc[pl.ds(base, SUBLANES), re]
            s_im = x_sc[pl.ds(base, SUBLANES), im]
            j = 1
            while j < SUBLANES:
                w_re, w_im = pow_row(j)
                s_re, s_im = madd(s_re, s_im, w_re, w_im,
                                  shift_rows(s_re, j, SUBLANES, r8, reverse), shift_rows(s_im, j, SUBLANES, r8, reverse))
                j *= 2
            c_re, c_im = carry[2 * q], carry[2 * q + 1]
            s_re, s_im = madd(s_re, s_im, p_re[tab], p_im[tab], c_re, c_im)
            edge = (r8 == SUBLANES - 1) if reverse else (r8 == 0)
            x_sc[pl.ds(base, SUBLANES), re] = jnp.where(edge, c_re, shift_rows(s_re, 1, SUBLANES, r8, reverse))
            x_sc[pl.ds(base, SUBLANES), im] = jnp.where(edge, c_im, shift_rows(s_im, 1, SUBLANES, r8, reverse))
            end = slice(0, 1) if reverse else slice(SUBLANES - 1, SUBLANES)
            out += [s_re[end], s_im[end]]
        return tuple(out)

    def run(reverse):
        for t0 in range(0, S5_T, 2):
            src = slice(t0 * LANES, S5_CW) if reverse else slice(0, (t0 + 2) * LANES)
            dst = slice(t0 * LANES, (t0 + 2) * LANES)
            part = _dot(u[:, src], m_ref[0, 0, 0, src, dst])
            if reverse:
                y_ref[0, :, dst] += part
            else:
                y_ref[0, :, dst] = part
        ctx_scan(reverse)
        h0 = h0_ref[0, 0]
        init = []
        for q in range(N_LAT_SEQ):
            init += [h0[q:q + 1, re], h0[q:q + 1, im]]
        lax.fori_loop(0, S5_LAT_GROUPS, functools.partial(lat_group, reverse=reverse), tuple(init))

    @pl.when(d == 0)
    def _():
        run(False)

    @pl.when(d == 1)
    def _():
        run(True)

    y_ref[0] += _dot_nt(x_sc[...].astype(BF16), wout_ref[0, 0, 0])


def _s5_main(u, m, w_in, w_out, pows, h0, layer):
    def tab(*tail):
        return pl.BlockSpec((1, 1, 1) + tail, lambda b, d: (layer, d, b) + (0,) * len(tail))

    return pl.pallas_call(
        _s5_kernel,
        out_shape=(jax.ShapeDtypeStruct((S5_BLK, S5_ROWS, S5_CW), F32),
                   jax.ShapeDtypeStruct((2, S5_BLK, N_CTX_SEQ, 2 * S5_SW), F32)),
        grid=(S5_BLK, 2),
        in_specs=[
            pl.BlockSpec((1, S5_ROWS, S5_CW), lambda b, d: (b, 0, 0)),
            tab(S5_CW, S5_CW), tab(S5_CW, 2 * S5_SW), tab(S5_CW, 2 * S5_SW), tab(2, S5_POW_ROWS, S5_SW),
            pl.BlockSpec((1, 1, N_LAT_SEQ, 2 * S5_SW), lambda b, d: (d, b, 0, 0)),
        ],
        out_specs=(
            pl.BlockSpec((1, S5_ROWS, S5_CW), lambda b, d: (b, 0, 0)),
            pl.BlockSpec((1, 1, N_CTX_SEQ, 2 * S5_SW), lambda b, d: (d, b, 0, 0)),
        ),
        scratch_shapes=[pltpu.VMEM((S5_ROWS, 2 * S5_SW), F32)],
        compiler_params=_cparams(("parallel", "arbitrary")),
        name="s5",
    )(u, m, w_in, w_out, pows, h0)


def _hg_constants():
    t = np.arange(HG_C)
    ae = np.zeros((2, HG_E_ROWS, HG_C), np.float32)
    mask = np.zeros((2, HG_LEVELS, HG_C, HG_C), np.float32)
    for lvl in range(HG_LEVELS):
        half = 1 << lvl
        pos = t % (2 * half)
        mid = t - pos + half - 1
        upper = pos >= half
        u = t[None, :]
        rows_upper = (u > mid[:, None]) & (u <= t[:, None])
        rows_lower = (u > t[:, None]) & (u <= mid[:, None])
        ae[0, lvl * HG_C:(lvl + 1) * HG_C] = np.where(upper[:, None], rows_upper, rows_lower)
        same = (t[:, None] // (2 * half)) == (t[None, :] // (2 * half))
        mask[0, lvl] = same & upper[:, None] & (~upper)[None, :]
    ae[0, HG_LEVELS * HG_C:(HG_LEVELS + 1) * HG_C] = t[None, :] <= t[:, None]
    ae[0, (HG_LEVELS + 1) * HG_C:] = 1.0
    ae[1] = ae[0][:, ::-1]
    ae[1, :(HG_LEVELS + 1) * HG_C] = ae[1, :(HG_LEVELS + 1) * HG_C].reshape(HG_LEVELS + 1, HG_C, HG_C)[:, ::-1].reshape(-1, HG_C)
    mask[1] = mask[0][:, ::-1, ::-1]
    ae2 = np.concatenate([ae, ae], axis=-1)
    mask2 = np.concatenate([mask, mask], axis=-1)
    pair_diag = np.kron(np.eye(2, dtype=np.float32), np.ones((HG_DK, HG_DK), np.float32))
    return ae2, mask2, pair_diag


def _hg_tile(direction, i):
    return i if direction == 0 else HG_TILES - 1 - i


HG_PAIR = 2 * HG_DK


def _pair_diag(x):
    zero = jnp.zeros((x.shape[0], HG_DK), x.dtype)
    return jnp.concatenate([jnp.concatenate([x[:, :HG_DK], zero], axis=1),
                            jnp.concatenate([zero, x[:, HG_DK:]], axis=1)], axis=0)


def _hg_chunk_prep(direction, rows, q_ref, f_ref, v_ref, lb, ae_ref):
    f = lb + (1.0 - lb) * jax.nn.sigmoid(f_ref[rows, :])
    logf = jnp.log2(f)
    kk = 1.0 - f
    hi = logf.astype(BF16)
    lo = (logf - hi.astype(F32)).astype(BF16)
    ex = _dot(ae_ref[direction], jnp.concatenate([hi, lo], axis=0))
    dec = jnp.exp2(ex[:(HG_LEVELS + 1) * HG_C])
    cum = ex[HG_LEVELS * HG_C:(HG_LEVELS + 1) * HG_C]
    tot = ex[(HG_LEVELS + 1) * HG_C:(HG_LEVELS + 1) * HG_C + 1]
    q = q_ref[rows, :].astype(F32)
    v_bf = v_ref[rows, :]
    v = v_bf.astype(F32)
    q_in = (q * dec[HG_LEVELS * HG_C:]).astype(BF16)
    k_tail = (kk * jnp.exp2(tot - cum)).astype(BF16)
    dec_tot = jnp.exp2(tot)
    return dec, q, kk, v, v_bf, q_in, k_tail, dec_tot


def _hg_chunk_scores(direction, prep, mask_ref):
    dec, q, kk, v, v_bf, q_in, k_tail, dec_tot = prep
    qk = q * kk
    pairs = [slice(p * HG_PAIR, (p + 1) * HG_PAIR) for p in range(HG_HEADS // 2)]
    scores = [jnp.zeros((HG_C, HG_PAIR), F32) for _ in pairs]
    for lvl in range(HG_LEVELS):
        for p, sl in enumerate(pairs):
            g = dec[lvl * HG_C:(lvl + 1) * HG_C, sl]
            scores[p] = scores[p] + mask_ref[direction, lvl] * _dot_nt(
                (q[:, sl] * g).astype(BF16), _pair_diag((kk[:, sl] * g).astype(BF16)))
    local = []
    for p, sl in enumerate(pairs):
        diag = jnp.concatenate(
            [jnp.broadcast_to(jnp.sum(qk[:, (2 * p + j) * HG_DK:(2 * p + j + 1) * HG_DK], axis=-1, keepdims=True),
                              (HG_C, HG_DK)) for j in range(2)], axis=-1)
        local.append(_dot(scores[p].astype(BF16), _pair_diag(v_bf[:, sl])) + diag * v[:, sl])
    return local, q_in, k_tail, dec_tot, v_bf


def _hg_chunk_state(direction, rows, chunk, pd_ref, o_ref, st_sc):
    local, q_in, k_tail, dec_tot, v_bf = chunk
    outs = []
    for p in range(HG_HEADS // 2):
        sl = slice(p * HG_PAIR, (p + 1) * HG_PAIR)
        st = st_sc[direction, p]
        outs.append(local[p] + _dot_nt(q_in[:, sl], st.astype(BF16)))
        st_sc[direction, p] = st * dec_tot[:, sl] + pd_ref[...] * _dot_tn(v_bf[:, sl], k_tail[:, sl])
    o_ref[rows, :] = jnp.concatenate(outs, axis=-1)


def _hg_kernel(qf_ref, ff_ref, vf_ref, qb_ref, fb_ref, vb_ref, lb_ref, ae_ref, mask_ref, pd_ref, h0f_ref, h0b_ref,
               of_ref, ob_ref, finf_ref, finb_ref, st_sc, *, layer):
    i = pl.program_id(0)
    x = lb_ref[:, :, 0, :]
    e = jnp.exp(x - jnp.max(x, axis=0, keepdims=True))
    sm = e / jnp.sum(e, axis=0, keepdims=True)
    lb = jnp.sum(sm[1:layer + 1], axis=0) if layer > 0 else jnp.zeros((2, D_BRANCH), F32)
    for direction, h0_ref in ((0, h0f_ref), (1, h0b_ref)):
        tile = _hg_tile(direction, i)
        first_ctx = 0 if direction == 0 else HG_TILES_PER_CTX - 1
        first_lat = 0 if direction == 0 else HG_TILES_PER_LAT - 1
        is_start = jnp.where(tile < HG_CTX_TILES,
                             tile % HG_TILES_PER_CTX == first_ctx,
                             (tile - HG_CTX_TILES) % HG_TILES_PER_LAT == first_lat)

        @pl.when(is_start)
        def _(direction=direction, h0_ref=h0_ref):
            for p in range(HG_HEADS // 2):
                st_sc[direction, p] = _pair_diag(
                    jnp.concatenate([h0_ref[0, 0, 2 * p], h0_ref[0, 0, 2 * p + 1]], axis=1))

    work = []
    for c in range(HG_TM // HG_C):
        fwd_rows = slice(c * HG_C, (c + 1) * HG_C)
        bwd_rows = slice(HG_TM - (c + 1) * HG_C, HG_TM - c * HG_C)
        for direction, rows, refs, o_ref in ((0, fwd_rows, (qf_ref, ff_ref, vf_ref), of_ref),
                                             (1, bwd_rows, (qb_ref, fb_ref, vb_ref), ob_ref)):
            prep = _hg_chunk_prep(direction, rows, *refs, lb[direction:direction + 1], ae_ref)
            work.append((direction, rows, o_ref, _hg_chunk_scores(direction, prep, mask_ref)))
    for direction, rows, o_ref, chunk in work:
        _hg_chunk_state(direction, rows, chunk, pd_ref, o_ref, st_sc)

    for direction, fin_ref in ((0, finf_ref), (1, finb_ref)):
        @pl.when(_hg_tile(direction, i) < HG_CTX_TILES)
        def _(direction=direction, fin_ref=fin_ref):
            for h in range(HG_HEADS):
                p, j = divmod(h, 2)
                fin_ref[0, h] = st_sc[direction, p, j * HG_DK:(j + 1) * HG_DK, j * HG_DK:(j + 1) * HG_DK]


def _hgrn(proj, fz, hg_lb, ae, mask, pair_diag, h0t, layer):
    def tok_spec(direction, col):
        return pl.BlockSpec((HG_TM, D_BRANCH), lambda i: (_hg_tile(direction, i), col))

    def fin_spec(direction):
        return pl.BlockSpec((1, HG_HEADS, HG_DK, HG_DK),
                            lambda i: (jnp.minimum(_hg_tile(direction, i) // HG_TILES_PER_CTX, N_CTX_SEQ - 1), 0, 0, 0))

    def h0_spec(direction):
        return pl.BlockSpec((1, 1, HG_HEADS, HG_DK, HG_DK),
                            lambda i: (_cond_row(_hg_tile(direction, i), HG_TM), direction, 0, 0, 0))

    fin_shape = jax.ShapeDtypeStruct((N_CTX_SEQ, HG_HEADS, HG_DK, HG_DK), F32)
    return pl.pallas_call(
        functools.partial(_hg_kernel, layer=layer),
        out_shape=(jax.ShapeDtypeStruct((N_TOK, D_BRANCH), F32), jax.ShapeDtypeStruct((N_TOK, D_BRANCH), F32),
                   fin_shape, fin_shape),
        grid=(HG_TILES,),
        in_specs=[
            tok_spec(0, HG_Q_COL), tok_spec(0, 0), tok_spec(0, HG_V_COL),
            tok_spec(1, HG_Q_COL), tok_spec(1, 1), tok_spec(1, HG_V_COL),
            pl.BlockSpec((DEPTH, 2, 1, D_BRANCH), lambda i: (0, 0, 0, 0)),
            pl.BlockSpec((2, HG_E_ROWS, 2 * HG_C), lambda i: (0, 0, 0)),
            pl.BlockSpec((2, HG_LEVELS, HG_C, HG_PAIR), lambda i: (0, 0, 0, 0)),
            pl.BlockSpec((HG_PAIR, HG_PAIR), lambda i: (0, 0)),
            h0_spec(0), h0_spec(1),
        ],
        out_specs=(
            pl.BlockSpec((HG_TM, D_BRANCH), lambda i: (_hg_tile(0, i), 0)),
            pl.BlockSpec((HG_TM, D_BRANCH), lambda i: (_hg_tile(1, i), 0)),
            fin_spec(0), fin_spec(1),
        ),
        scratch_shapes=[pltpu.VMEM((2, HG_HEADS // 2, HG_PAIR, HG_PAIR), F32)],
        compiler_params=_cparams(("arbitrary",)),
        name="hgrn",
    )(proj, fz, proj, proj, fz, proj, hg_lb.reshape(DEPTH, 2, 1, D_BRANCH), ae, mask, pair_diag, h0t, h0t)


MRG_TM = 256


def _gelu_tanh(x):
    return 0.5 * x * (1.0 + jnp.tanh(0.7978845608028654 * (x + 0.044715 * (x * x * x))))


def _merge_kernel(scb_ref, scc_ref, sch_ref, hgg_ref, pool_ref, g0_ref, g1_ref, g2_ref, g3_ref,
                  yrow_ref, of_ref, ob_ref, x_ref, mod_ref, conv_ref, gluw_ref, glub_ref, ng_ref,
                  poolw_ref, pscale_ref, wbr_ref, wout_ref, lng_ref, lnb_ref, o_ref,
                  ys5_sc, gluw_sc, poolw_sc, wbr_sc, wout_sc):
    tile = pl.program_id(0)

    @pl.when(tile == 0)
    def _():
        gluw_sc[...] = gluw_ref[0].astype(BF16)
        poolw_sc[...] = poolw_ref[0].astype(BF16)
        wbr_sc[...] = wbr_ref[0].astype(BF16)
        wout_sc[...] = wout_ref[0].astype(BF16)

    line = jnp.where(tile * MRG_TM < N_CTX_TOK, CTX_LEN, GRID_W)
    pos = lax.broadcasted_iota(jnp.int32, (MRG_TM, 1), 0) & (line - 1)

    def shifted(val, k):
        rolled = pltpu.roll(val, k % MRG_TM, axis=0)
        ok = (pos >= k) if k > 0 else (pos < line + k)
        return jnp.where(ok, rolled, 0.0)

    m = scc_ref[...].astype(F32) * sch_ref[...].astype(F32)
    conv = conv_ref[0, 0:1, :] * shifted(m, 1) + conv_ref[0, 1:2, :] * m + conv_ref[0, 2:3, :] * shifted(m, -1)
    ya = scb_ref[...].astype(F32) * conv

    for b in range(S5_BLK):
        for t in range(S5_T):
            ys5_sc[b, pl.ds(t, MRG_TM // S5_T, stride=S5_T), :] = yrow_ref[b, :, t * LANES:(t + 1) * LANES]
    z = _gelu_tanh(jnp.concatenate([ys5_sc[b] for b in range(S5_BLK)], axis=-1))
    yb = z * _sigmoid_tanh(_dot(z.astype(BF16), gluw_sc[...]) + glub_ref[0])

    o = of_ref[...] + ob_ref[...]
    normed = []
    for h in range(HG_HEADS):
        oh = o[:, h * HG_DK:(h + 1) * HG_DK]
        ms = jnp.mean(oh * oh, axis=-1, keepdims=True)
        normed.append(oh * lax.rsqrt(ms + LN_EPS) * ng_ref[0])
    gate_c = hgg_ref[...].astype(F32)
    yc = jnp.concatenate(normed, axis=-1) * (gate_c * _sigmoid_tanh(gate_c))

    pu = pool_ref[...].astype(F32)
    posf = pos.astype(F32)
    linef = line.astype(F32)
    pooled = []
    for gi, w in enumerate(POOL_WINDOWS):
        vg = pu[:, gi * POOL_GROUP:(gi + 1) * POOL_GROUP]
        back, fwd, span = vg, vg, 1
        while 2 * span <= w // 2:
            back = back + shifted(back, span)
            fwd = fwd + shifted(fwd, -span)
            span *= 2
        s = shifted(back, 1) + fwd
        cnt = jnp.minimum(posf + w // 2, linef) - jnp.maximum(posf - w // 2, 0.0)
        pg = s / cnt - vg
        pooled.append(_dot(pg.astype(BF16), poolw_sc[gi]))
    yd = jnp.concatenate(pooled, axis=-1) * pscale_ref[0]

    branches = [(br.astype(BF16), gate_ref) for br, gate_ref in ((ya, g0_ref), (yb, g1_ref), (yc, g2_ref), (yd, g3_ref))]
    half = D_MODEL // 2
    mix = None
    for c in range(2):
        cols = slice(c * half, (c + 1) * half)
        merged = None
        for k, (br, gate_ref) in enumerate(branches):
            term = gate_ref[:, cols].astype(F32) * _dot(br, wbr_sc[k, :, cols])
            merged = term if merged is None else merged + term
        part = _dot(merged.astype(BF16), wout_sc[cols, :])
        mix = part if mix is None else mix + part
    y = ALPHA * x_ref[...] + mod_ref[0, 0, 5:6, :] * mix
    o_ref[...] = _layer_norm(y, lng_ref[0, 0], lnb_ref[0, 0])


def _merge(proj, y_rows, o_f, o_b, x, mod, conv_w, glu_w, glu_b, norm_g, pool_w, pool_scale, w_branch, w_out,
           ln_g, ln_b, layer):
    def part(col):
        return pl.BlockSpec((MRG_TM, D_BRANCH), lambda i: (i, col))

    def gate(k):
        return pl.BlockSpec((MRG_TM, D_MODEL), lambda i: (i, N_IN_PARTS * D_BRANCH // D_MODEL + k))

    def per_layer(*shape, single=False):
        mode = dict(pipeline_mode=pl.Buffered(1)) if single else {}
        return pl.BlockSpec((1,) + shape, lambda i: (layer,) + (0,) * len(shape), **mode)

    return pl.pallas_call(
        _merge_kernel,
        out_shape=jax.ShapeDtypeStruct((N_TOK, D_MODEL), F32),
        grid=(N_TOK // MRG_TM,),
        in_specs=[
            part(0), part(1), part(2), part(8), part(9), gate(0), gate(1), gate(2), gate(3),
            pl.BlockSpec((S5_BLK, MRG_TM // S5_T, S5_CW), lambda i: (0, i, 0)),
            pl.BlockSpec((MRG_TM, D_BRANCH), lambda i: (i, 0)),
            pl.BlockSpec((MRG_TM, D_BRANCH), lambda i: (i, 0)),
            pl.BlockSpec((MRG_TM, D_MODEL), lambda i: (i, 0)),
            _mod_spec(layer, MRG_TM),
            per_layer(3, D_BRANCH), per_layer(D_BRANCH, D_BRANCH, single=True), per_layer(1, D_BRANCH),
            per_layer(1, HG_DK), per_layer(len(POOL_WINDOWS), POOL_GROUP, POOL_GROUP, single=True),
            per_layer(1, D_BRANCH), per_layer(N_BRANCH, D_BRANCH, D_MODEL, single=True),
            per_layer(D_MODEL, D_MODEL, single=True), _ln_spec(layer, 1), _ln_spec(layer, 1),
        ],
        out_specs=pl.BlockSpec((MRG_TM, D_MODEL), lambda i: (i, 0)),
        scratch_shapes=[
            pltpu.VMEM((S5_BLK, MRG_TM, LANES), F32),
            pltpu.VMEM((D_BRANCH, D_BRANCH), BF16),
            pltpu.VMEM((len(POOL_WINDOWS), POOL_GROUP, POOL_GROUP), BF16),
            pltpu.VMEM((N_BRANCH, D_BRANCH, D_MODEL), BF16),
            pltpu.VMEM((D_MODEL, D_MODEL), BF16),
        ],
        compiler_params=_cparams(("arbitrary",)),
        name="merge",
    )(proj, proj, proj, proj, proj, proj, proj, proj, proj, y_rows, o_f, o_b, x, mod,
      conv_w, glu_w, glu_b.reshape(DEPTH, 1, D_BRANCH), norm_g.reshape(DEPTH, 1, HG_DK), pool_w,
      pool_scale.reshape(DEPTH, 1, D_BRANCH), w_branch, w_out, ln_g, ln_b)


def _grid_pos_embedding():
    rows = LAT_LEN // GRID_W
    quarter = D_MODEL // 4
    omega = POS_BASE ** (-jnp.arange(quarter, dtype=F32) / quarter)
    ar = jnp.arange(rows, dtype=F32)[:, None] * omega
    ac = jnp.arange(GRID_W, dtype=F32)[:, None] * omega
    row_tab = jnp.concatenate([jnp.sin(ar), jnp.cos(ar)], -1)[:, None, :]
    col_tab = jnp.concatenate([jnp.sin(ac), jnp.cos(ac)], -1)[None, :, :]
    shape = (rows, GRID_W, 2 * quarter)
    pos = jnp.concatenate([jnp.broadcast_to(row_tab, shape), jnp.broadcast_to(col_tab, shape)], -1)
    return pos.reshape(LAT_LEN, D_MODEL)


def kernel(x_prompt, x_sample, state_s5_re, state_s5_im, state_hgrn, c, c_ctx, w_ada, b_ada, ln_g, ln_b,
           ffn_w1, ffn_w3, ffn_w2, w_in, sc_conv, s5_lam_re, s5_lam_im, s5_log_dt, s5_b_re, s5_b_im,
           s5_c_re, s5_c_im, s5_d, s5_glu_w, s5_glu_b, hg_lb, hg_norm_g, pool_w, pool_scale, w_branch, w_out):
    x = (x_prompt.reshape(N_CTX_TOK, D_MODEL), x_sample.reshape(N_LAT_TOK, D_MODEL), _grid_pos_embedding())

    cond = jnp.zeros((COND_PAD, D_MODEL), F32).at[0].set(c_ctx).at[1:N_COND].set(c)
    mod = _ada(cond, w_ada, b_ada).reshape(DEPTH, COND_PAD, N_SUB * 3, D_MODEL)
    ln_g4 = ln_g.reshape(DEPTH, N_SUB, 1, D_MODEL)
    ln_b4 = ln_b.reshape(DEPTH, N_SUB, 1, D_MODEL)

    s5_m, s5_win, s5_wout, s5_pow = _s5_tables(*_s5_param_layout(
        s5_lam_re, s5_lam_im, s5_log_dt, s5_b_re, s5_b_im, s5_c_re, s5_c_im, s5_d))

    w_in_bf = _cast_w_in(w_in)

    ae_np, mask_np, pair_np = _hg_constants()
    ae = jnp.asarray(ae_np, BF16)
    mask = jnp.asarray(mask_np, F32)
    pair_diag = jnp.asarray(pair_np, F32)

    fin_s5, fin_hg = [], []
    for l in range(DEPTH):
        x, h = _ffn(x, mod, ffn_w1, ffn_w3, ffn_w2, ln_g4, ln_b4, l, 0)
        proj, fz, u_rows = _inproj(h, w_in_bf, l)

        h0 = jnp.concatenate([
            state_s5_re[:, l].reshape(N_LAT_SEQ, 2, S5_BLK, S5_SW),
            state_s5_im[:, l].reshape(N_LAT_SEQ, 2, S5_BLK, S5_SW)], axis=-1).transpose(1, 2, 0, 3)
        y_rows, fin = _s5_main(u_rows, s5_m, s5_win, s5_wout, s5_pow, h0, l)
        fin_s5.append(fin)

        h0t = jnp.concatenate([jnp.zeros((1, 2, HG_HEADS, HG_DK, HG_DK), F32),
                               jnp.swapaxes(state_hgrn[:, l], -1, -2)], axis=0)
        o_f, o_b, fin_f, fin_b = _hgrn(proj, fz, hg_lb, ae, mask, pair_diag, h0t, l)
        fin_hg.append(jnp.stack([fin_f, fin_b], axis=1))

        x = _merge(proj, y_rows, o_f, o_b, x, mod, sc_conv, s5_glu_w, s5_glu_b, hg_norm_g,
                   pool_w, pool_scale, w_branch, w_out, ln_g4, ln_b4, l)
        x = _ffn(x, mod, ffn_w1, ffn_w3, ffn_w2, ln_g4, ln_b4, l, 2, split_output=(l == DEPTH - 1))

    y_prompt = x[0].reshape(N_CTX_SEQ, CTX_LEN, D_MODEL)
    y_sample = x[1].reshape(N_LAT_SEQ, LAT_LEN, D_MODEL)
    fin = jnp.stack(fin_s5)
    fin = fin.reshape(DEPTH, 2, S5_BLK, N_CTX_SEQ, 2, S5_GPB, S5_STATE)
    fin = fin.transpose(4, 3, 0, 1, 2, 5, 6).reshape(2, N_CTX_SEQ, DEPTH, 2, S5_GROUPS, S5_STATE)
    new_hgrn = jnp.swapaxes(jnp.stack(fin_hg, axis=1), -1, -2)
    return y_prompt, y_sample, fin[0], fin[1], new_hgrn
```
